```python
import math
import jax, jax.numpy as jnp
from jax import lax
import numpy as np

D_MODEL = 1024
BATCH = 8
SEQ = 2048
DEPTH = 1
DEC_BATCH = 128
DEC_SEQ = 8
PAST_LEN = 16384
PAGE_SIZE = 128

PLE_DIM = 256
M_HEADS = 4
D_MLSTM = D_MODEL
M_HEAD_DIM = D_MLSTM // M_HEADS
M_CONV_W = 4
MLSTM_CHUNK = 64
D_SCONV = D_MODEL
S_CONV_W = 3
N_GROUPS = 4
EXPERTS_PER_GROUP = 8
N_EXPERTS = N_GROUPS * EXPERTS_PER_GROUP
TOP_K_IN_GROUP = 2
D_EXPERT = D_MODEL // 4
RMS_EPS = 1e-6
D_IN = 3 * D_MLSTM + 2 * M_HEADS + 3 * D_SCONV + 2 * D_MODEL

kernel_name = 'hybrid_mlstm_shortconv_hmoe_step'


def rms_norm(x, g):
    xf = x.astype(jnp.float32)
    xf = xf * lax.rsqrt(jnp.mean(xf * xf, axis=-1, keepdims=True) + RMS_EPS)
    return (xf * g.astype(jnp.float32)).astype(x.dtype)


def causal_depthwise_conv(x, buf, w):
    t = x.shape[1]
    xp = jnp.concatenate([buf.astype(x.dtype), x], axis=1)
    y = sum(xp[:, j:j + t] * w[j] for j in range(w.shape[0]))
    return y, xp[:, t:]


def mlstm_chunkwise(q, k, v, log_i, log_f, c0, n0, m0):
    b, t, h, dh = q.shape
    L = math.gcd(t, MLSTM_CHUNK)
    nc = t // L

    def blocks(a):
        a = a.reshape((b, nc, L, h) + a.shape[3:])
        return jnp.moveaxis(jnp.moveaxis(a, 1, 0), 2, 3)

    causal = jnp.tril(jnp.ones((L, L), dtype=bool))

    def step(carry, inp):
        c, n, m = carry
        qc, kc, vc, lic, lfc = inp
        bcum = jnp.cumsum(lfc, axis=-1)
        d = jnp.where(causal, bcum[..., :, None] - bcum[..., None, :] + lic[..., None, :], -jnp.inf)
        inter = m[..., None] + bcum
        m_t = jnp.maximum(inter, jnp.max(d, axis=-1))
        w_inter = jnp.exp(inter - m_t)
        s = jnp.exp(d - m_t[..., None]) * jnp.einsum('bhtd,bhsd->bhts', qc, kc)
        num = w_inter[..., None] * jnp.einsum('bhtd,bhde->bhte', qc, c) + jnp.einsum('bhts,bhse->bhte', s, vc)
        den = w_inter * jnp.einsum('bhtd,bhd->bht', qc, n) + jnp.sum(s, axis=-1)
        out = num / jnp.maximum(jnp.abs(den), jnp.exp(-m_t))[..., None]
        b_last = bcum[..., -1]
        lw = b_last[..., None] - bcum + lic
        m_new = jnp.maximum(m + b_last, jnp.max(lw, axis=-1))
        decay = jnp.exp(m + b_last - m_new)
        wk = jnp.exp(lw - m_new[..., None])[..., None] * kc
        c_new = decay[..., None, None] * c + jnp.einsum('bhsd,bhse->bhde', wk, vc)
        n_new = decay[..., None] * n + jnp.sum(wk, axis=2)
        return (c_new, n_new, m_new), out

    xs = (blocks(q), blocks(k), blocks(v), blocks(log_i), blocks(log_f))
    (c1, n1, m1), hs = lax.scan(step, (c0, n0, m0), xs)
    hs = jnp.moveaxis(jnp.moveaxis(hs, 3, 2), 0, 1).reshape(b, t, h, dh)
    return hs, c1, n1, m1


def hierarchical_moe(h, w_rg, b_rg, w_re, b_re, w_gate, w_up, w_down):
    shp = h.shape
    hf = h.reshape(-1, shp[-1])
    g_logits = (hf @ w_rg + b_rg).astype(jnp.float32)
    g_sel = jnp.argmax(g_logits, axis=-1)
    p_grp = jnp.take_along_axis(jax.nn.softmax(g_logits, axis=-1), g_sel[:, None], axis=-1)
    e_logits = (hf @ w_re + b_re).astype(jnp.float32).reshape(-1, N_GROUPS, EXPERTS_PER_GROUP)
    e_in = jnp.take_along_axis(e_logits, g_sel[:, None, None], axis=1)[:, 0]
    top_v, top_i = lax.top_k(e_in, TOP_K_IN_GROUP)
    top_w = jax.nn.softmax(top_v, axis=-1) * p_grp
    expert_id = g_sel[:, None] * EXPERTS_PER_GROUP + top_i
    combine = jnp.sum(jax.nn.one_hot(expert_id, N_EXPERTS, dtype=jnp.float32) * top_w[..., None], axis=1).astype(h.dtype)
    out = jnp.zeros_like(hf)
    for e in range(N_EXPERTS):
        he = jax.nn.silu(hf @ w_gate[e]) * (hf @ w_up[e])
        out = out + combine[:, e:e + 1] * (he @ w_down[e])
    return out.reshape(shp)


def hybrid_layer(x, p, mconv_buf, sconv_buf, c0, n0, m0,
                 g_mix, w_in, w_mconv, w_q, w_k, b_i, b_f, g_head, w_a, w_sconv, w_sout, w_o,
                 g_ffn, w_rg, b_rg, w_re, b_re, w_gate, w_up, w_down, g_ple, w_ple_gate, w_ple_proj):
    bsz, t, _ = x.shape
    f32 = jnp.float32
    h = rms_norm(x, g_mix)
    sizes = (D_MLSTM, D_MLSTM, D_MLSTM, M_HEADS, M_HEADS, D_SCONV, D_SCONV, D_SCONV, D_MODEL, D_MODEL)
    points = [int(s) for s in np.cumsum(sizes)[:-1]]
    xm, v, o_pre, i_pre, f_pre, s_b, s_c, s_x, gate_a, gate_b = jnp.split(h @ w_in, points, axis=-1)
    xq, mconv_new = causal_depthwise_conv(xm, mconv_buf, w_mconv)
    xq = jax.nn.silu(xq).reshape(bsz, t, M_HEADS, M_HEAD_DIM)
    q = jnp.einsum('bthd,hde->bthe', xq, w_q).astype(f32) * (M_HEAD_DIM ** -0.5)
    k = jnp.einsum('bthd,hde->bthe', xq, w_k).astype(f32)
    vh = v.reshape(bsz, t, M_HEADS, M_HEAD_DIM).astype(f32)
    log_i = (i_pre + b_i).astype(f32)
    log_f = jax.nn.log_sigmoid((f_pre + b_f).astype(f32))
    hm, c1, n1, m1 = mlstm_chunkwise(q, k, vh, log_i, log_f, c0.astype(f32), n0.astype(f32), m0.astype(f32))
    hm = hm * lax.rsqrt(jnp.mean(hm * hm, axis=-1, keepdims=True) + RMS_EPS) * g_head.astype(f32)
    y_a = (jax.nn.sigmoid(o_pre) * hm.reshape(bsz, t, D_MLSTM).astype(x.dtype)) @ w_a
    u, sconv_new = causal_depthwise_conv(s_c * s_x, sconv_buf, w_sconv)
    y_b = (s_b * u) @ w_sout
    merged = jax.nn.sigmoid(gate_a) * y_a + jax.nn.sigmoid(gate_b) * y_b
    x = x + merged @ w_o
    x = x + hierarchical_moe(rms_norm(x, g_ffn), w_rg, b_rg, w_re, b_re, w_gate, w_up, w_down)
    x = x + jax.nn.sigmoid(rms_norm(x, g_ple) @ w_ple_gate) * (p.astype(x.dtype) @ w_ple_proj)
    return x, c1, n1, m1, mconv_new, sconv_new


def setup_inputs(seed: int = 0) -> dict:
    key = jax.random.key(seed)
    ks = iter(jax.random.split(key, 48))
    f32 = jnp.float32
    L = DEPTH

    def nrm(shape, scale):
        return jax.random.normal(next(ks), shape, f32) * scale

    def gain(shape):
        return 1.0 + nrm(shape, 0.05)

    return {
        'x_prompt': nrm((BATCH, SEQ, D_MODEL), 1.0),
        'x_sample': nrm((DEC_BATCH, DEC_SEQ, D_MODEL), 1.0),
        'p_prompt': nrm((DEPTH, BATCH, SEQ, PLE_DIM), 1.0),
        'p_sample': nrm((DEPTH, DEC_BATCH, DEC_SEQ, PLE_DIM), 1.0),
        'state_mlstm_C': nrm((DEPTH, DEC_BATCH, M_HEADS, M_HEAD_DIM, M_HEAD_DIM), M_HEAD_DIM ** -0.5),
        'state_mlstm_n': nrm((DEPTH, DEC_BATCH, M_HEADS, M_HEAD_DIM), 1.0),
        'state_mlstm_m': nrm((DEPTH, DEC_BATCH, M_HEADS), 1.0),
        'state_mlstm_conv': nrm((DEPTH, DEC_BATCH, M_CONV_W - 1, D_MLSTM), 1.0),
        'state_sconv': nrm((DEPTH, DEC_BATCH, S_CONV_W - 1, D_SCONV), 1.0),
        'g_mix': gain((L, D_MODEL)),
        'w_in': nrm((L, D_MODEL, D_IN), D_MODEL ** -0.5),
        'w_mconv': nrm((L, M_CONV_W, D_MLSTM), M_CONV_W ** -0.5),
        'w_q': nrm((L, M_HEADS, M_HEAD_DIM, M_HEAD_DIM), M_HEAD_DIM ** -0.5),
        'w_k': nrm((L, M_HEADS, M_HEAD_DIM, M_HEAD_DIM), M_HEAD_DIM ** -0.5),
        'b_i': nrm((L, M_HEADS), 0.1),
        'b_f': jnp.linspace(3.0, 6.0, M_HEADS, dtype=f32)[None, :] + nrm((L, M_HEADS), 0.1),
        'g_head': gain((L, M_HEADS, M_HEAD_DIM)),
        'w_a': nrm((L, D_MLSTM, D_MODEL), D_MLSTM ** -0.5),
        'w_sconv': nrm((L, S_CONV_W, D_SCONV), S_CONV_W ** -0.5),
        'w_sout': nrm((L, D_SCONV, D_MODEL), D_SCONV ** -0.5),
        'w_o': nrm((L, D_MODEL, D_MODEL), D_MODEL ** -0.5),
        'g_ffn': gain((L, D_MODEL)),
        'w_rg': nrm((L, D_MODEL, N_GROUPS), D_MODEL ** -0.5),
        'b_rg': nrm((L, N_GROUPS), 0.01),
        'w_re': nrm((L, D_MODEL, N_EXPERTS), D_MODEL ** -0.5),
        'b_re': nrm((L, N_EXPERTS), 0.01),
        'w_gate': nrm((L, N_EXPERTS, D_MODEL, D_EXPERT), D_MODEL ** -0.5),
        'w_up': nrm((L, N_EXPERTS, D_MODEL, D_EXPERT), D_MODEL ** -0.5),
        'w_down': nrm((L, N_EXPERTS, D_EXPERT, D_MODEL), D_EXPERT ** -0.5),
        'g_ple': gain((L, D_MODEL)),
        'w_ple_gate': nrm((L, D_MODEL, D_MODEL), D_MODEL ** -0.5),
        'w_ple_proj': nrm((L, PLE_DIM, D_MODEL), PLE_DIM ** -0.5),
        'g_final': gain((D_MODEL,)),
    }


def reference(x_prompt, x_sample, p_prompt, p_sample, state_mlstm_C, state_mlstm_n, state_mlstm_m,
              state_mlstm_conv, state_sconv, g_mix, w_in, w_mconv, w_q, w_k, b_i, b_f, g_head, w_a,
              w_sconv, w_sout, w_o, g_ffn, w_rg, b_rg, w_re, b_re, w_gate, w_up, w_down,
              g_ple, w_ple_gate, w_ple_proj, g_final):
    f32 = jnp.float32
    bp = x_prompt.shape[0]
    yp, ys = x_prompt, x_sample
    new_p, new_s = [], []
    for i in range(DEPTH):
        lw = [a[i] for a in (g_mix, w_in, w_mconv, w_q, w_k, b_i, b_f, g_head, w_a, w_sconv, w_sout, w_o,
                             g_ffn, w_rg, b_rg, w_re, b_re, w_gate, w_up, w_down, g_ple, w_ple_gate, w_ple_proj)]
        yp, *st_p = hybrid_layer(
            yp, p_prompt[i],
            jnp.zeros((bp, M_CONV_W - 1, D_MLSTM), yp.dtype),
            jnp.zeros((bp, S_CONV_W - 1, D_SCONV), yp.dtype),
            jnp.zeros((bp, M_HEADS, M_HEAD_DIM, M_HEAD_DIM), f32),
            jnp.zeros((bp, M_HEADS, M_HEAD_DIM), f32),
            jnp.zeros((bp, M_HEADS), f32),
            *lw)
        ys, *st_s = hybrid_layer(
            ys, p_sample[i], state_mlstm_conv[i], state_sconv[i],
            state_mlstm_C[i], state_mlstm_n[i], state_mlstm_m[i], *lw)
        new_p.append(st_p)
        new_s.append(st_s)

    def stack(sts, j):
        return jnp.stack([s[j] for s in sts])

    y_prompt = rms_norm(yp, g_final)
    y_sample = rms_norm(ys, g_final)
    return (y_prompt, y_sample,
            stack(new_p, 0), stack(new_p, 1), stack(new_p, 2), stack(new_p, 3), stack(new_p, 4),
            stack(new_s, 0), stack(new_s, 1), stack(new_s, 2), stack(new_s, 3), stack(new_s, 4))
```

```python
import functools

import jax
import jax.numpy as jnp
from jax import lax
from jax.experimental import pallas as pl
from jax.experimental.pallas import tpu as pltpu

D_MODEL = 1024
M_HEADS = 4
M_HEAD_DIM = 256
M_CONV_W = 4
S_CONV_W = 3
N_GROUPS = 4
EXPERTS_PER_GROUP = 8
N_EXPERTS = N_GROUPS * EXPERTS_PER_GROUP
D_EXPERT = 256
PLE_DIM = 256
RMS_EPS = 1e-6

LANES = 128
SUBLANES = 8
VMEM_LIMIT = 56 * 1024 * 1024

BF16 = jnp.bfloat16
F32 = jnp.float32


def _rms(x, g):
    return x * lax.rsqrt(jnp.mean(x * x, axis=-1, keepdims=True) + RMS_EPS) * g


def _dot(a, b):
    return jnp.dot(a.astype(BF16), b.astype(BF16), preferred_element_type=F32)


def _sigmoid(x):
    return 1.0 / (1.0 + jnp.exp(-x))


def _silu(x):
    return x * _sigmoid(x)


def _resident(shape):
    nd = len(shape)
    return pl.BlockSpec(shape, lambda *_: (0,) * nd, pipeline_mode=pl.Buffered(1))


def _in_proj_kernel(x_ref, g_ref, w_main_ref, w_if_ref, w_rest_ref,
                    xm_ref, v_ref, o_ref, if_ref, sb_ref, cx_ref, ga_ref, gb_ref):
    h = _rms(x_ref[...], g_ref[...]).astype(BF16)
    d = D_MODEL
    xm_ref[...] = jnp.dot(h, w_main_ref[:, 0:d], preferred_element_type=F32)
    v_ref[...] = jnp.dot(h, w_main_ref[:, d:2 * d], preferred_element_type=F32)
    o_ref[...] = jnp.dot(h, w_main_ref[:, 2 * d:3 * d], preferred_element_type=F32)
    if_ref[...] = jnp.dot(h, w_if_ref[...], preferred_element_type=F32)
    sb_ref[...] = jnp.dot(h, w_rest_ref[:, 0:d], preferred_element_type=F32)
    s_c = jnp.dot(h, w_rest_ref[:, d:2 * d], preferred_element_type=F32)
    s_x = jnp.dot(h, w_rest_ref[:, 2 * d:3 * d], preferred_element_type=F32)
    cx_ref[...] = s_c * s_x
    ga_ref[...] = jnp.dot(h, w_rest_ref[:, 3 * d:4 * d], preferred_element_type=F32)
    gb_ref[...] = jnp.dot(h, w_rest_ref[:, 4 * d:5 * d], preferred_element_type=F32)


def _in_proj(x, g_mix, w_main, w_if, w_rest, tm):
    n = x.shape[0]
    row = lambda i: (i, 0)
    big = pl.BlockSpec((tm, D_MODEL), row)
    outs = [jax.ShapeDtypeStruct((n, D_MODEL), F32)] * 3 + [jax.ShapeDtypeStruct((n, LANES), F32)] \
        + [jax.ShapeDtypeStruct((n, D_MODEL), F32)] * 4
    return pl.pallas_call(
        _in_proj_kernel,
        grid=(n // tm,),
        in_specs=[big, _resident(g_mix.shape), _resident(w_main.shape), _resident(w_if.shape),
                  _resident(w_rest.shape)],
        out_specs=[big, big, big, pl.BlockSpec((tm, LANES), row), big, big, big, big],
        out_shape=outs,
        compiler_params=pltpu.CompilerParams(dimension_semantics=("arbitrary",),
                                             vmem_limit_bytes=VMEM_LIMIT),
        name="in_proj",
    )(x, g_mix, w_main, w_if, w_rest)


def _cumsum_rows(x):
    rows = x.shape[0]
    idx = lax.broadcasted_iota(jnp.int32, x.shape, 0)
    shift = 1
    while shift < rows:
        x = x + jnp.where(idx >= shift, pltpu.roll(x, shift, axis=0), 0.0)
        shift *= 2
    return x


def _mlstm_kernel(xm_ref, v_ref, cx_ref, if_ref, mconv0_ref, sconv0_ref, c0_ref, n0_ref, m0_ref,
                  w_mconv_ref, w_sconv_ref, wq_ref, wk_ref, bias_ref, ghead_ref,
                  hm_ref, u_ref, c1_ref, n1_ref, m1_ref, mconv1_ref, sconv1_ref,
                  xp_sc, cp_sc, *, tt):
    t = pl.program_id(1)

    @pl.when(t == 0)
    def _():
        xp_sc[0:SUBLANES, :] = mconv0_ref[...]
        cp_sc[0:SUBLANES, :] = sconv0_ref[...]
        c1_ref[...] = c0_ref[...]
        n1_ref[...] = n0_ref[...]
        m1_ref[...] = m0_ref[...]

    xp_sc[SUBLANES:SUBLANES + tt, :] = xm_ref[...]
    cp_sc[SUBLANES:SUBLANES + tt, :] = cx_ref[...]
    conv = jnp.zeros((tt, D_MODEL), F32)
    for j in range(M_CONV_W):
        off = SUBLANES - (M_CONV_W - 1) + j
        conv = conv + xp_sc[pl.ds(off, tt), :] * w_mconv_ref[j:j + 1, :]
    u = jnp.zeros((tt, D_MODEL), F32)
    for j in range(S_CONV_W):
        off = SUBLANES - (S_CONV_W - 1) + j
        u = u + cp_sc[pl.ds(off, tt), :] * w_sconv_ref[j:j + 1, :]
    u_ref[...] = u
    tail_x = xp_sc[tt:tt + SUBLANES, :]
    tail_c = cp_sc[tt:tt + SUBLANES, :]
    xp_sc[0:SUBLANES, :] = tail_x
    cp_sc[0:SUBLANES, :] = tail_c
    mconv1_ref[...] = tail_x
    sconv1_ref[...] = tail_c

    xq = _silu(conv)

    g = if_ref[...] + bias_ref[...]
    log_f = jnp.minimum(g, 0.0) - jnp.log1p(jnp.exp(-jnp.abs(g)))
    lane = lax.broadcasted_iota(jnp.int32, g.shape, 1)
    gates = jnp.where(lane < M_HEADS, g, _cumsum_rows(log_f))
    gates_t = gates.T

    r_idx = lax.broadcasted_iota(jnp.int32, (tt, tt), 0)
    c_idx = lax.broadcasted_iota(jnp.int32, (tt, tt), 1)
    causal = c_idx <= r_idx

    for h in range(M_HEADS):
        sl = slice(h * M_HEAD_DIM, (h + 1) * M_HEAD_DIM)
        xq_h = xq[:, sl]
        q = _dot(xq_h, wq_ref[h]) * (M_HEAD_DIM ** -0.5)
        k = _dot(xq_h, wk_ref[h])
        v = v_ref[:, sl]
        c_prev = c1_ref[h]
        n_prev = n1_ref[h:h + 1, :]
        m_prev = m1_ref[h:h + 1, 0:1]

        li_col = gates[:, h:h + 1]
        b_col = gates[:, M_HEADS + h:M_HEADS + h + 1]
        li_row = gates_t[h:h + 1, :]
        b_row = gates_t[M_HEADS + h:M_HEADS + h + 1, :]
        b_last = b_row[:, tt - 1:tt]

        dmat = jnp.where(causal, b_col - b_row + li_row, -jnp.inf)
        inter = m_prev + b_col
        m_t = jnp.maximum(inter, jnp.max(dmat, axis=-1, keepdims=True))
        w_inter = jnp.exp(inter - m_t)
        qk = lax.dot_general(q.astype(BF16), k.astype(BF16), (((1,), (1,)), ((), ())),
                             preferred_element_type=F32)
        s = jnp.exp(dmat - m_t) * qk
        num = w_inter * _dot(q, c_prev) + _dot(s, v)
        den = w_inter * jnp.sum(q * n_prev, axis=-1, keepdims=True) + jnp.sum(s, axis=-1, keepdims=True)
        out = num * (1.0 / jnp.maximum(jnp.abs(den), jnp.exp(-m_t)))
        out = out * lax.rsqrt(jnp.mean(out * out, axis=-1, keepdims=True) + RMS_EPS) * ghead_ref[:, sl]
        hm_ref[:, sl] = out

        lw_col = b_last - b_col + li_col
        lw_row = b_last - b_row + li_row
        m_new = jnp.maximum(m_prev + b_last, jnp.max(lw_row, axis=-1, keepdims=True))
        decay = jnp.exp(m_prev + b_last - m_new)
        wk = jnp.exp(lw_col - m_new) * k
        c1_ref[h] = decay * c_prev + lax.dot_general(
            wk.astype(BF16), v.astype(BF16), (((0,), (0,)), ((), ())), preferred_element_type=F32)
        n1_ref[h:h + 1, :] = decay * n_prev + jnp.sum(wk, axis=0, keepdims=True)
        m1_ref[h:h + 1, :] = jnp.broadcast_to(m_new, (1, LANES))


def _mlstm(xm, v, cx, ifp, mconv0, sconv0, c0, n0, m0, w_mconv, w_sconv, wq, wk, bias, ghead, *, bsz, t, tt):
    nt = t // tt
    n = bsz * t
    tok = lambda b, i: (b * nt + i, 0)
    seq3 = lambda b, i: (b, 0, 0)
    seq4 = lambda b, i: (b, 0, 0, 0)
    big = pl.BlockSpec((tt, D_MODEL), tok)
    tail = pl.BlockSpec((None, SUBLANES, D_MODEL), seq3)
    c_spec = pl.BlockSpec((None, M_HEADS, M_HEAD_DIM, M_HEAD_DIM), seq4)
    n_spec = pl.BlockSpec((None, M_HEADS, M_HEAD_DIM), seq3)
    m_spec = pl.BlockSpec((None, SUBLANES, LANES), seq3)
    return pl.pallas_call(
        functools.partial(_mlstm_kernel, tt=tt),
        grid=(bsz, nt),
        in_specs=[big, big, big, pl.BlockSpec((tt, LANES), tok), tail, tail, c_spec, n_spec, m_spec,
                  _resident(w_mconv.shape), _resident(w_sconv.shape), _resident(wq.shape),
                  _resident(wk.shape), _resident(bias.shape), _resident(ghead.shape)],
        out_specs=[big, big, c_spec, n_spec, m_spec, tail, tail],
        out_shape=[jax.ShapeDtypeStruct((n, D_MODEL), F32), jax.ShapeDtypeStruct((n, D_MODEL), F32),
                   jax.ShapeDtypeStruct(c0.shape, F32), jax.ShapeDtypeStruct(n0.shape, F32),
                   jax.ShapeDtypeStruct(m0.shape, F32),
                   jax.ShapeDtypeStruct(mconv0.shape, F32), jax.ShapeDtypeStruct(sconv0.shape, F32)],
        scratch_shapes=[pltpu.VMEM((tt + SUBLANES, D_MODEL), F32), pltpu.VMEM((tt + SUBLANES, D_MODEL), F32)],
        compiler_params=pltpu.CompilerParams(dimension_semantics=("arbitrary", "arbitrary"),
                                             vmem_limit_bytes=VMEM_LIMIT),
        name="mlstm",
    )(xm, v, cx, ifp, mconv0, sconv0, c0, n0, m0, w_mconv, w_sconv, wq, wk, bias, ghead)


def _merge_kernel(x_ref, hm_ref, o_ref, ga_ref, gb_ref, sb_ref, u_ref, wa_ref, wsout_ref, wo_ref, x1_ref):
    y_a = _dot(_sigmoid(o_ref[...]) * hm_ref[...], wa_ref[...])
    y_b = _dot(sb_ref[...] * u_ref[...], wsout_ref[...])
    merged = _sigmoid(ga_ref[...]) * y_a + _sigmoid(gb_ref[...]) * y_b
    x1_ref[...] = x_ref[...] + _dot(merged, wo_ref[...])


def _merge(x, hm, o, ga, gb, sb, u, w_a, w_sout, w_o, tm):
    n = x.shape[0]
    big = pl.BlockSpec((tm, D_MODEL), lambda i: (i, 0))
    return pl.pallas_call(
        _merge_kernel,
        grid=(n // tm,),
        in_specs=[big] * 7 + [_resident(w_a.shape), _resident(w_sout.shape), _resident(w_o.shape)],
        out_specs=big,
        out_shape=jax.ShapeDtypeStruct((n, D_MODEL), F32),
        compiler_params=pltpu.CompilerParams(dimension_semantics=("arbitrary",),
                                             vmem_limit_bytes=VMEM_LIMIT),
        name="merge",
    )(x, hm, o, ga, gb, sb, u, w_a, w_sout, w_o)


def _split_bf16(x):
    hi = x.astype(BF16)
    lo = (x - hi.astype(F32)).astype(BF16)
    return hi, lo


def _route(logits):
    lane = lax.broadcasted_iota(jnp.int32, logits.shape, 1)
    neg = -jnp.inf
    big = jnp.int32(LANES)
    is_grp = lane < N_GROUPS
    g_max = jnp.max(jnp.where(is_grp, logits, neg), axis=-1, keepdims=True)
    g_sel = jnp.min(jnp.where(is_grp & (logits == g_max), lane, big), axis=-1, keepdims=True)
    p_grp = 1.0 / jnp.sum(jnp.where(is_grp, jnp.exp(logits - g_max), 0.0), axis=-1, keepdims=True)
    lo = N_GROUPS + g_sel * EXPERTS_PER_GROUP
    in_grp = (lane >= lo) & (lane < lo + EXPERTS_PER_GROUP)
    v1 = jnp.max(jnp.where(in_grp, logits, neg), axis=-1, keepdims=True)
    e1 = jnp.min(jnp.where(in_grp & (logits == v1), lane, big), axis=-1, keepdims=True)
    rest = in_grp & (lane != e1)
    v2 = jnp.max(jnp.where(rest, logits, neg), axis=-1, keepdims=True)
    e2 = jnp.min(jnp.where(rest & (logits == v2), lane, big), axis=-1, keepdims=True)
    z = jnp.exp(v2 - v1)
    w1 = p_grp / (1.0 + z)
    w2 = p_grp * z / (1.0 + z)
    return jnp.where(lane == e1, w1, jnp.where(lane == e2, w2, 0.0))


def _moe_tail_kernel(x1_ref, p_ref, gffn_ref, wr_hi_ref, wr_lo_ref, br_ref, wg_ref, wu_ref, wd_ref,
                     gple_ref, wpg_ref, wpp_ref, gfin_ref, y_ref, hn_sc, cw_sc):
    g = pl.program_id(1)

    @pl.when(g == 0)
    def _():
        x1 = x1_ref[...]
        hn = _rms(x1, gffn_ref[...])
        hn_hi, hn_lo = _split_bf16(hn)
        hn_sc[...] = hn_hi
        logits = (jnp.dot(hn_hi, wr_hi_ref[...], preferred_element_type=F32)
                  + jnp.dot(hn_hi, wr_lo_ref[...], preferred_element_type=F32)
                  + jnp.dot(hn_lo, wr_hi_ref[...], preferred_element_type=F32)) + br_ref[...]
        cw_sc[...] = _route(logits)
        y_ref[...] = x1

    hn = hn_sc[...]
    cw = cw_sc[...]
    lane = lax.broadcasted_iota(jnp.int32, cw.shape, 1)
    acc = y_ref[...]
    for e in range(EXPERTS_PER_GROUP):
        sl = slice(e * D_EXPERT, (e + 1) * D_EXPERT)
        col = jnp.sum(jnp.where(lane == N_GROUPS + g * EXPERTS_PER_GROUP + e, cw, 0.0), axis=-1, keepdims=True)
        he = _silu(jnp.dot(hn, wg_ref[:, sl], preferred_element_type=F32)) \
            * jnp.dot(hn, wu_ref[:, sl], preferred_element_type=F32)
        acc = acc + col * _dot(he, wd_ref[sl, :])
    y_ref[...] = acc

    @pl.when(g == N_GROUPS - 1)
    def _():
        x2 = y_ref[...]
        gate = _sigmoid(_dot(_rms(x2, gple_ref[...]), wpg_ref[...]))
        x3 = x2 + gate * _dot(p_ref[...], wpp_ref[...])
        y_ref[...] = _rms(x3, gfin_ref[...])


def _moe_tail(x1, p, g_ffn, wr_hi, wr_lo, b_r, wg, wu, wd, g_ple, w_pg, w_pp, g_fin, tm):
    n = x1.shape[0]
    row = lambda i, g: (i, 0)
    grp = lambda i, g: (g, 0, 0)
    big = pl.BlockSpec((tm, D_MODEL), row)
    gw = EXPERTS_PER_GROUP * D_EXPERT
    return pl.pallas_call(
        _moe_tail_kernel,
        grid=(n // tm, N_GROUPS),
        in_specs=[big, pl.BlockSpec((tm, PLE_DIM), row), _resident(g_ffn.shape), _resident(wr_hi.shape),
                  _resident(wr_lo.shape), _resident(b_r.shape),
                  pl.BlockSpec((None, D_MODEL, gw), grp), pl.BlockSpec((None, D_MODEL, gw), grp),
                  pl.BlockSpec((None, gw, D_MODEL), grp),
                  _resident(g_ple.shape), _resident(w_pg.shape), _resident(w_pp.shape), _resident(g_fin.shape)],
        out_specs=big,
        out_shape=jax.ShapeDtypeStruct((n, D_MODEL), F32),
        scratch_shapes=[pltpu.VMEM((tm, D_MODEL), BF16), pltpu.VMEM((tm, LANES), F32)],
        compiler_params=pltpu.CompilerParams(dimension_semantics=("arbitrary", "arbitrary"),
                                             vmem_limit_bytes=VMEM_LIMIT),
        name="moe_tail",
    )(x1, p, g_ffn, wr_hi, wr_lo, b_r, wg, wu, wd, g_ple, w_pg, w_pp, g_fin)


def _pad_tail(buf):
    return jnp.pad(buf, ((0, 0), (SUBLANES - buf.shape[1], 0), (0, 0)))


def _group(x, p, mconv0, sconv0, c0, n0, m0, wts, *, tt, tm):
    bsz, t, _ = x.shape
    xf = x.reshape(bsz * t, D_MODEL)
    pf = p.reshape(bsz * t, PLE_DIM)
    xm, v, o, ifp, sb, cx, ga, gb = _in_proj(xf, wts["g_mix"], wts["w_main"], wts["w_if"], wts["w_rest"], tm)
    m0p = jnp.broadcast_to(jnp.pad(m0, ((0, 0), (0, SUBLANES - M_HEADS)))[:, :, None], (bsz, SUBLANES, LANES))
    hm, u, c1, n1, m1p, mconv1, sconv1 = _mlstm(
        xm, v, cx, ifp, _pad_tail(mconv0), _pad_tail(sconv0), c0, n0, m0p,
        wts["w_mconv"], wts["w_sconv"], wts["w_q"], wts["w_k"], wts["gate_bias"], wts["g_head"],
        bsz=bsz, t=t, tt=tt)
    x1 = _merge(xf, hm, o, ga, gb, sb, u, wts["w_a"], wts["w_sout"], wts["w_o"], tm)
    y = _moe_tail(x1, pf, wts["g_ffn"], wts["wr_hi"], wts["wr_lo"], wts["b_r"], wts["w_gate"], wts["w_up"],
                  wts["w_down"], wts["g_ple"], wts["w_ple_gate"], wts["w_ple_proj"], wts["g_final"], tm)
    return (y.reshape(bsz, t, D_MODEL), c1[None], n1[None], m1p[None, :, :M_HEADS, 0],
            mconv1[None, :, SUBLANES - (M_CONV_W - 1):, :], sconv1[None, :, SUBLANES - (S_CONV_W - 1):, :])


def kernel(x_prompt, x_sample, p_prompt, p_sample, state_mlstm_C, state_mlstm_n, state_mlstm_m, state_mlstm_conv, state_sconv, g_mix, w_in, w_mconv, w_q, w_k, b_i, b_f, g_head, w_a, w_sconv, w_sout, w_o, g_ffn, w_rg, b_rg, w_re, b_re, w_gate, w_up, w_down, g_ple, w_ple_gate, w_ple_proj, g_final):
    assert g_mix.shape[0] == 1, "single-layer trunk"
    d = D_MODEL
    w_in0 = w_in[0]
    n_if = 2 * M_HEADS
    w_router = jnp.pad(jnp.concatenate([w_rg[0], w_re[0]], axis=1), ((0, 0), (0, LANES - N_GROUPS - N_EXPERTS)))
    wr_hi = w_router.astype(BF16)
    wr_lo = (w_router - wr_hi.astype(F32)).astype(BF16)
    gw = EXPERTS_PER_GROUP * D_EXPERT
    wts = {
        "g_mix": g_mix,
        "w_main": w_in0[:, :3 * d].astype(BF16),
        "w_if": jnp.pad(w_in0[:, 3 * d:3 * d + n_if], ((0, 0), (0, LANES - n_if))).astype(BF16),
        "w_rest": w_in0[:, 3 * d + n_if:].astype(BF16),
        "w_mconv": w_mconv[0], "w_sconv": w_sconv[0],
        "w_q": w_q[0].astype(BF16), "w_k": w_k[0].astype(BF16),
        "gate_bias": jnp.pad(jnp.concatenate([b_i[0], b_f[0]])[None, :], ((0, 0), (0, LANES - n_if))),
        "g_head": g_head[0].reshape(1, d),
        "w_a": w_a[0].astype(BF16), "w_sout": w_sout[0].astype(BF16), "w_o": w_o[0].astype(BF16),
        "g_ffn": g_ffn, "wr_hi": wr_hi, "wr_lo": wr_lo,
        "b_r": jnp.pad(jnp.concatenate([b_rg[0], b_re[0]])[None, :], ((0, 0), (0, LANES - N_GROUPS - N_EXPERTS))),
        "w_gate": w_gate[0].astype(BF16).reshape(N_GROUPS, EXPERTS_PER_GROUP, d, D_EXPERT)
                  .transpose(0, 2, 1, 3).reshape(N_GROUPS, d, gw),
        "w_up": w_up[0].astype(BF16).reshape(N_GROUPS, EXPERTS_PER_GROUP, d, D_EXPERT)
                .transpose(0, 2, 1, 3).reshape(N_GROUPS, d, gw),
        "w_down": w_down[0].astype(BF16).reshape(N_GROUPS, gw, d),
        "g_ple": g_ple, "w_ple_gate": w_ple_gate[0].astype(BF16), "w_ple_proj": w_ple_proj[0].astype(BF16),
        "g_final": g_final[None, :],
    }
    bp = x_prompt.shape[0]
    zeros = lambda *s: jnp.zeros(s, F32)
    yp, *st_p = _group(
        x_prompt, p_prompt[0], zeros(bp, M_CONV_W - 1, d), zeros(bp, S_CONV_W - 1, d),
        zeros(bp, M_HEADS, M_HEAD_DIM, M_HEAD_DIM), zeros(bp, M_HEADS, M_HEAD_DIM), zeros(bp, M_HEADS),
        wts, tt=256, tm=512)
    ys, *st_s = _group(
        x_sample, p_sample[0], state_mlstm_conv[0], state_sconv[0],
        state_mlstm_C[0], state_mlstm_n[0], state_mlstm_m[0],
        wts, tt=x_sample.shape[1], tm=512)
    return (yp, ys, *st_p, *st_s)
```

```python
import functools

import jax
import jax.numpy as jnp
from jax import lax
from jax.experimental import pallas as pl
from jax.experimental.pallas import tpu as pltpu

D_MODEL = 1024
M_HEADS = 4
M_HEAD_DIM = 256
M_CONV_W = 4
S_CONV_W = 3
N_GROUPS = 4
EXPERTS_PER_GROUP = 8
N_EXPERTS = N_GROUPS * EXPERTS_PER_GROUP
D_EXPERT = 256
PLE_DIM = 256
RMS_EPS = 1e-6

LANES = 128
SUBLANES = 8
VMEM_LIMIT = 56 * 1024 * 1024

BF16 = jnp.bfloat16
F32 = jnp.float32


def _rms(x, g):
    return x * lax.rsqrt(jnp.mean(x * x, axis=-1, keepdims=True) + RMS_EPS) * g


def _dot(a, b):
    return jnp.dot(a.astype(BF16), b.astype(BF16), preferred_element_type=F32)


def _sigmoid(x):
    return 1.0 / (1.0 + jnp.exp(-x))


def _silu(x):
    return x * _sigmoid(x)


def _resident(shape):
    nd = len(shape)
    return pl.BlockSpec(shape, lambda *_: (0,) * nd, pipeline_mode=pl.Buffered(1))


def _in_proj_kernel(x_ref, g_ref, w_main_ref, w_if_ref, w_rest_ref,
                    xm_ref, v_ref, o_ref, if_ref, sb_ref, cx_ref, ga_ref, gb_ref):
    h = _rms(x_ref[...], g_ref[...]).astype(BF16)
    d = D_MODEL
    xm_ref[...] = jnp.dot(h, w_main_ref[:, 0:d], preferred_element_type=F32)
    v_ref[...] = jnp.dot(h, w_main_ref[:, d:2 * d], preferred_element_type=F32)
    o_ref[...] = jnp.dot(h, w_main_ref[:, 2 * d:3 * d], preferred_element_type=F32)
    if_ref[...] = jnp.dot(h, w_if_ref[...], preferred_element_type=F32)
    sb_ref[...] = jnp.dot(h, w_rest_ref[:, 0:d], preferred_element_type=F32)
    s_c = jnp.dot(h, w_rest_ref[:, d:2 * d], preferred_element_type=F32)
    s_x = jnp.dot(h, w_rest_ref[:, 2 * d:3 * d], preferred_element_type=F32)
    cx_ref[...] = s_c * s_x
    ga_ref[...] = jnp.dot(h, w_rest_ref[:, 3 * d:4 * d], preferred_element_type=F32)
    gb_ref[...] = jnp.dot(h, w_rest_ref[:, 4 * d:5 * d], preferred_element_type=F32)


def _in_proj(x, g_mix, w_main, w_if, w_rest, tm):
    n = x.shape[0]
    row = lambda i: (i, 0)
    big = pl.BlockSpec((tm, D_MODEL), row)
    outs = [jax.ShapeDtypeStruct((n, D_MODEL), F32)] * 3 + [jax.ShapeDtypeStruct((n, LANES), F32)] \
        + [jax.ShapeDtypeStruct((n, D_MODEL), F32)] * 4
    return pl.pallas_call(
        _in_proj_kernel,
        grid=(n // tm,),
        in_specs=[big, _resident(g_mix.shape), _resident(w_main.shape), _resident(w_if.shape),
                  _resident(w_rest.shape)],
        out_specs=[big, big, big, pl.BlockSpec((tm, LANES), row), big, big, big, big],
        out_shape=outs,
        compiler_params=pltpu.CompilerParams(dimension_semantics=("arbitrary",),
                                             vmem_limit_bytes=VMEM_LIMIT),
        name="in_proj",
    )(x, g_mix, w_main, w_if, w_rest)


def _cumsum_rows(x):
    rows = x.shape[0]
    idx = lax.broadcasted_iota(jnp.int32, x.shape, 0)
    shift = 1
    while shift < rows:
        x = x + jnp.where(idx >= shift, pltpu.roll(x, shift, axis=0), 0.0)
        shift *= 2
    return x


def _mlstm_kernel(xm_ref, v_ref, cx_ref, if_ref, mconv0_ref, sconv0_ref, c0_ref, n0_ref, m0_ref,
                  w_mconv_ref, w_sconv_ref, wq_ref, wk_ref, bias_ref, ghead_ref,
                  hm_ref, u_ref, c1_ref, n1_ref, m1_ref, mconv1_ref, sconv1_ref,
                  xp_sc, cp_sc, *, tt):
    t = pl.program_id(1)

    @pl.when(t == 0)
    def _():
        xp_sc[0:SUBLANES, :] = mconv0_ref[...]
        cp_sc[0:SUBLANES, :] = sconv0_ref[...]
        c1_ref[...] = c0_ref[...]
        n1_ref[...] = n0_ref[...]
        m1_ref[...] = m0_ref[...]

    xp_sc[SUBLANES:SUBLANES + tt, :] = xm_ref[...]
    cp_sc[SUBLANES:SUBLANES + tt, :] = cx_ref[...]
    conv = jnp.zeros((tt, D_MODEL), F32)
    for j in range(M_CONV_W):
        off = SUBLANES - (M_CONV_W - 1) + j
        conv = conv + xp_sc[pl.ds(off, tt), :] * w_mconv_ref[j:j + 1, :]
    u = jnp.zeros((tt, D_MODEL), F32)
    for j in range(S_CONV_W):
        off = SUBLANES - (S_CONV_W - 1) + j
        u = u + cp_sc[pl.ds(off, tt), :] * w_sconv_ref[j:j + 1, :]
    u_ref[...] = u
    tail_x = xp_sc[tt:tt + SUBLANES, :]
    tail_c = cp_sc[tt:tt + SUBLANES, :]
    xp_sc[0:SUBLANES, :] = tail_x
    cp_sc[0:SUBLANES, :] = tail_c
    mconv1_ref[...] = tail_x
    sconv1_ref[...] = tail_c

    xq = _silu(conv)

    g = if_ref[...] + bias_ref[...]
    log_f = jnp.minimum(g, 0.0) - jnp.log1p(jnp.exp(-jnp.abs(g)))
    lane = lax.broadcasted_iota(jnp.int32, g.shape, 1)
    gates = jnp.where(lane < M_HEADS, g, _cumsum_rows(log_f))
    gates_t = gates.T

    r_idx = lax.broadcasted_iota(jnp.int32, (tt, tt), 0)
    c_idx = lax.broadcasted_iota(jnp.int32, (tt, tt), 1)
    causal = c_idx <= r_idx

    for h in range(M_HEADS):
        sl = slice(h * M_HEAD_DIM, (h + 1) * M_HEAD_DIM)
        xq_h = xq[:, sl]
        q = _dot(xq_h, wq_ref[h]) * (M_HEAD_DIM ** -0.5)
        k = _dot(xq_h, wk_ref[h])
        v = v_ref[:, sl]
        c_prev = c1_ref[h]
        n_prev = n1_ref[h:h + 1, :]
        m_prev = m1_ref[h:h + 1, 0:1]

        li_col = gates[:, h:h + 1]
        b_col = gates[:, M_HEADS + h:M_HEADS + h + 1]
        li_row = gates_t[h:h + 1, :]
        b_row = gates_t[M_HEADS + h:M_HEADS + h + 1, :]
        b_last = b_row[:, tt - 1:tt]

        dmat = jnp.where(causal, b_col - b_row + li_row, -jnp.inf)
        inter = m_prev + b_col
        m_t = jnp.maximum(inter, jnp.max(dmat, axis=-1, keepdims=True))
        w_inter = jnp.exp(inter - m_t)
        qk = lax.dot_general(q.astype(BF16), k.astype(BF16), (((1,), (1,)), ((), ())),
                             preferred_element_type=F32)
        s = jnp.exp(dmat - m_t) * qk
        num = w_inter * _dot(q, c_prev) + _dot(s, v)
        den = w_inter * jnp.sum(q * n_prev, axis=-1, keepdims=True) + jnp.sum(s, axis=-1, keepdims=True)
        out = num * (1.0 / jnp.maximum(jnp.abs(den), jnp.exp(-m_t)))
        out = out * lax.rsqrt(jnp.mean(out * out, axis=-1, keepdims=True) + RMS_EPS) * ghead_ref[:, sl]
        hm_ref[:, sl] = out

        lw_col = b_last - b_col + li_col
        lw_row = b_last - b_row + li_row
        m_new = jnp.maximum(m_prev + b_last, jnp.max(lw_row, axis=-1, keepdims=True))
        decay = jnp.exp(m_prev + b_last - m_new)
        wk = jnp.exp(lw_col - m_new) * k
        c1_ref[h] = decay * c_prev + lax.dot_general(
            wk.astype(BF16), v.astype(BF16), (((0,), (0,)), ((), ())), preferred_element_type=F32)
        n1_ref[h:h + 1, :] = decay * n_prev + jnp.sum(wk, axis=0, keepdims=True)
        m1_ref[h:h + 1, :] = jnp.broadcast_to(m_new, (1, LANES))


def _mlstm(xm, v, cx, ifp, mconv0, sconv0, c0, n0, m0, w_mconv, w_sconv, wq, wk, bias, ghead, *, bsz, t, tt):
    nt = t // tt
    n = bsz * t
    tok = lambda b, i: (b * nt + i, 0)
    seq3 = lambda b, i: (b, 0, 0)
    seq4 = lambda b, i: (b, 0, 0, 0)
    big = pl.BlockSpec((tt, D_MODEL), tok)
    tail = pl.BlockSpec((None, SUBLANES, D_MODEL), seq3)
    c_spec = pl.BlockSpec((None, M_HEADS, M_HEAD_DIM, M_HEAD_DIM), seq4)
    n_spec = pl.BlockSpec((None, M_HEADS, M_HEAD_DIM), seq3)
    m_spec = pl.BlockSpec((None, SUBLANES, LANES), seq3)
    return pl.pallas_call(
        functools.partial(_mlstm_kernel, tt=tt),
        grid=(bsz, nt),
        in_specs=[big, big, big, pl.BlockSpec((tt, LANES), tok), tail, tail, c_spec, n_spec, m_spec,
                  _resident(w_mconv.shape), _resident(w_sconv.shape), _resident(wq.shape),
                  _resident(wk.shape), _resident(bias.shape), _resident(ghead.shape)],
        out_specs=[big, big, c_spec, n_spec, m_spec, tail, tail],
        out_shape=[jax.ShapeDtypeStruct((n, D_MODEL), F32), jax.ShapeDtypeStruct((n, D_MODEL), F32),
                   jax.ShapeDtypeStruct(c0.shape, F32), jax.ShapeDtypeStruct(n0.shape, F32),
                   jax.ShapeDtypeStruct(m0.shape, F32),
                   jax.ShapeDtypeStruct(mconv0.shape, F32), jax.ShapeDtypeStruct(sconv0.shape, F32)],
        scratch_shapes=[pltpu.VMEM((tt + SUBLANES, D_MODEL), F32), pltpu.VMEM((tt + SUBLANES, D_MODEL), F32)],
        compiler_params=pltpu.CompilerParams(dimension_semantics=("arbitrary", "arbitrary"),
                                             vmem_limit_bytes=VMEM_LIMIT),
        name="mlstm",
    )(xm, v, cx, ifp, mconv0, sconv0, c0, n0, m0, w_mconv, w_sconv, wq, wk, bias, ghead)


def _merge_kernel(x_ref, hm_ref, o_ref, ga_ref, gb_ref, sb_ref, u_ref, wa_ref, wsout_ref, wo_ref, x1_ref):
    y_a = _dot(_sigmoid(o_ref[...]) * hm_ref[...], wa_ref[...])
    y_b = _dot(sb_ref[...] * u_ref[...], wsout_ref[...])
    merged = _sigmoid(ga_ref[...]) * y_a + _sigmoid(gb_ref[...]) * y_b
    x1_ref[...] = x_ref[...] + _dot(merged, wo_ref[...])


def _merge(x, hm, o, ga, gb, sb, u, w_a, w_sout, w_o, tm):
    n = x.shape[0]
    big = pl.BlockSpec((tm, D_MODEL), lambda i: (i, 0))
    return pl.pallas_call(
        _merge_kernel,
        grid=(n // tm,),
        in_specs=[big] * 7 + [_resident(w_a.shape), _resident(w_sout.shape), _resident(w_o.shape)],
        out_specs=big,
        out_shape=jax.ShapeDtypeStruct((n, D_MODEL), F32),
        compiler_params=pltpu.CompilerParams(dimension_semantics=("arbitrary",),
                                             vmem_limit_bytes=VMEM_LIMIT),
        name="merge",
    )(x, hm, o, ga, gb, sb, u, w_a, w_sout, w_o)


MOE_WINDOW = 1024
MOE_ROW_TILE = 128
MOE_ALIGN = 16
EXPERTS_PER_STEP = 4
MOE_STEPS = N_EXPERTS // EXPERTS_PER_STEP
MOE_CHUNK = 256
MOE_SORTED_ROWS = -(-(MOE_WINDOW + N_GROUPS * (MOE_ALIGN - 1) + MOE_ROW_TILE) // MOE_CHUNK) * MOE_CHUNK


def _split_bf16(x):
    hi = x.astype(BF16)
    lo = (x - hi.astype(F32)).astype(BF16)
    return hi, lo


def _split3_bf16(x):
    hi = x.astype(BF16)
    r1 = x - hi.astype(F32)
    mid = r1.astype(BF16)
    lo = (r1 - mid.astype(F32)).astype(BF16)
    return hi, mid, lo


def _route(logits):
    lane = lax.broadcasted_iota(jnp.int32, logits.shape, 1)
    neg = -jnp.inf
    big = jnp.int32(LANES)
    is_grp = lane < N_GROUPS
    g_max = jnp.max(jnp.where(is_grp, logits, neg), axis=-1, keepdims=True)
    g_sel = jnp.min(jnp.where(is_grp & (logits == g_max), lane, big), axis=-1, keepdims=True)
    p_grp = 1.0 / jnp.sum(jnp.where(is_grp, jnp.exp(logits - g_max), 0.0), axis=-1, keepdims=True)
    lo = N_GROUPS + g_sel * EXPERTS_PER_GROUP
    in_grp = (lane >= lo) & (lane < lo + EXPERTS_PER_GROUP)
    v1 = jnp.max(jnp.where(in_grp, logits, neg), axis=-1, keepdims=True)
    e1 = jnp.min(jnp.where(in_grp & (logits == v1), lane, big), axis=-1, keepdims=True)
    rest = in_grp & (lane != e1)
    v2 = jnp.max(jnp.where(rest, logits, neg), axis=-1, keepdims=True)
    e2 = jnp.min(jnp.where(rest & (logits == v2), lane, big), axis=-1, keepdims=True)
    z = jnp.exp(v2 - v1)
    w1 = p_grp / (1.0 + z)
    w2 = p_grp * z / (1.0 + z)
    return g_sel, jnp.where(lane == e1, w1, jnp.where(lane == e2, w2, 0.0))


def _moe_tail_kernel(x1_ref, p_ref, gffn_ref, wr_hi_ref, wr_lo_ref, br_ref, wg_ref, wu_ref, wd_ref,
                     gple_ref, wpg_ref, wpp_ref, gfin_ref, y_ref,
                     xs_sc, cws_sc, osort_sc, pos_sc, off_sm, nt_sm):
    s = pl.program_id(1)
    w = MOE_WINDOW

    @pl.when(s == 0)
    def _():
        hn = _rms(x1_ref[...], gffn_ref[...])
        hn_hi, hn_lo = _split_bf16(hn)
        logits = (jnp.dot(hn_hi, wr_hi_ref[...], preferred_element_type=F32)
                  + jnp.dot(hn_hi, wr_lo_ref[...], preferred_element_type=F32)
                  + jnp.dot(hn_lo, wr_hi_ref[...], preferred_element_type=F32)) + br_ref[...]
        g_sel, cw = _route(logits)
        lane = lax.broadcasted_iota(jnp.int32, (w, LANES), 1)
        onehot = jnp.where(lane == g_sel, 1.0, 0.0)
        cum = _cumsum_rows(onehot)
        cnt = cum[w - 1:w, :].astype(jnp.int32)
        cnt_pad = ((cnt + (MOE_ALIGN - 1)) // MOE_ALIGN) * MOE_ALIGN
        lane1 = lax.broadcasted_iota(jnp.int32, (1, LANES), 1)
        off = jnp.zeros((1, LANES), jnp.int32)
        for gi in range(N_GROUPS - 1):
            off = off + jnp.where(lane1 > gi, cnt_pad[:, gi:gi + 1], 0)
        n_tiles = (cnt + (MOE_ROW_TILE - 1)) // MOE_ROW_TILE
        for gi in range(N_GROUPS):
            off_sm[gi] = off[0, gi]
            nt_sm[gi] = n_tiles[0, gi]
        pos = jnp.sum(onehot * (off.astype(F32) + cum - 1.0), axis=-1, keepdims=True)
        pos_b = jnp.broadcast_to(pos, (w, LANES))
        pos_sc[...] = pos_b
        pos_row = pos_b.T[0:1, :].astype(jnp.int32)
        cw_hi, cw_mid, cw_lo = _split3_bf16(cw)
        for c in range(MOE_SORTED_ROWS // MOE_CHUNK):
            rows = lax.broadcasted_iota(jnp.int32, (MOE_CHUNK, w), 0) + c * MOE_CHUNK
            sel = jnp.where(rows == pos_row, 1.0, 0.0).astype(BF16)
            sl = slice(c * MOE_CHUNK, (c + 1) * MOE_CHUNK)
            xs_sc[sl, :] = jnp.dot(sel, hn_hi, preferred_element_type=F32).astype(BF16)
            cws_sc[sl, :] = (jnp.dot(sel, cw_hi, preferred_element_type=F32)
                             + jnp.dot(sel, cw_mid, preferred_element_type=F32)
                             + jnp.dot(sel, cw_lo, preferred_element_type=F32))
        osort_sc[...] = jnp.zeros(osort_sc.shape, F32)

    grp = s // (EXPERTS_PER_GROUP // EXPERTS_PER_STEP)
    row0 = off_sm[grp]
    lane_t = lax.broadcasted_iota(jnp.int32, (MOE_ROW_TILE, LANES), 1)
    first_lane = N_GROUPS + s * EXPERTS_PER_STEP

    def tile_body(i, carry):
        r0 = pl.multiple_of(row0 + i * MOE_ROW_TILE, MOE_ALIGN)
        xt = xs_sc[pl.ds(r0, MOE_ROW_TILE), :]
        cwt = cws_sc[pl.ds(r0, MOE_ROW_TILE), :]
        hg = jnp.dot(xt, wg_ref[...], preferred_element_type=F32)
        hu = jnp.dot(xt, wu_ref[...], preferred_element_type=F32)
        parts = []
        for e in range(EXPERTS_PER_STEP):
            sl = slice(e * D_EXPERT, (e + 1) * D_EXPERT)
            col = jnp.sum(jnp.where(lane_t == first_lane + e, cwt, 0.0), axis=-1, keepdims=True)
            parts.append((_silu(hg[:, sl]) * hu[:, sl] * col).astype(BF16))
        he = jnp.concatenate(parts, axis=-1)
        out = jnp.dot(he, wd_ref[...], preferred_element_type=F32)
        osort_sc[pl.ds(r0, MOE_ROW_TILE), :] = osort_sc[pl.ds(r0, MOE_ROW_TILE), :] + out
        return carry

    lax.fori_loop(0, nt_sm[grp], tile_body, 0)

    @pl.when(s == MOE_STEPS - 1)
    def _():
        osort = osort_sc[...].astype(BF16)
        for c in range(w // MOE_CHUNK):
            sl = slice(c * MOE_CHUNK, (c + 1) * MOE_CHUNK)
            pos_col = pos_sc[sl, 0:1].astype(jnp.int32)
            cols = lax.broadcasted_iota(jnp.int32, (MOE_CHUNK, MOE_SORTED_ROWS), 1)
            sel = jnp.where(cols == pos_col, 1.0, 0.0).astype(BF16)
            x2 = x1_ref[sl, :] + jnp.dot(sel, osort, preferred_element_type=F32)
            gate = _sigmoid(_dot(_rms(x2, gple_ref[...]), wpg_ref[...]))
            x3 = x2 + gate * _dot(p_ref[sl, :], wpp_ref[...])
            y_ref[sl, :] = _rms(x3, gfin_ref[...])


def _moe_tail(x1, p, g_ffn, wr_hi, wr_lo, b_r, wg, wu, wd, g_ple, w_pg, w_pp, g_fin):
    n = x1.shape[0]
    tm = MOE_WINDOW
    row = lambda i, s: (i, 0)
    step = lambda i, s: (s, 0, 0)
    big = pl.BlockSpec((tm, D_MODEL), row)
    sw = EXPERTS_PER_STEP * D_EXPERT
    return pl.pallas_call(
        _moe_tail_kernel,
        grid=(n // tm, MOE_STEPS),
        in_specs=[big, pl.BlockSpec((tm, PLE_DIM), row), _resident(g_ffn.shape), _resident(wr_hi.shape),
                  _resident(wr_lo.shape), _resident(b_r.shape),
                  pl.BlockSpec((None, D_MODEL, sw), step), pl.BlockSpec((None, D_MODEL, sw), step),
                  pl.BlockSpec((None, sw, D_MODEL), step),
                  _resident(g_ple.shape), _resident(w_pg.shape), _resident(w_pp.shape), _resident(g_fin.shape)],
        out_specs=big,
        out_shape=jax.ShapeDtypeStruct((n, D_MODEL), F32),
        scratch_shapes=[pltpu.VMEM((MOE_SORTED_ROWS, D_MODEL), BF16), pltpu.VMEM((MOE_SORTED_ROWS, LANES), F32),
                        pltpu.VMEM((MOE_SORTED_ROWS, D_MODEL), F32), pltpu.VMEM((tm, LANES), F32),
                        pltpu.SMEM((N_GROUPS,), jnp.int32), pltpu.SMEM((N_GROUPS,), jnp.int32)],
        compiler_params=pltpu.CompilerParams(dimension_semantics=("arbitrary", "arbitrary"),
                                             vmem_limit_bytes=VMEM_LIMIT),
        name="moe_tail",
    )(x1, p, g_ffn, wr_hi, wr_lo, b_r, wg, wu, wd, g_ple, w_pg, w_pp, g_fin)


def _pad_tail(buf):
    return jnp.pad(buf, ((0, 0), (SUBLANES - buf.shape[1], 0), (0, 0)))


def _group(x, p, mconv0, sconv0, c0, n0, m0, wts, *, tt, tm):
    bsz, t, _ = x.shape
    xf = x.reshape(bsz * t, D_MODEL)
    pf = p.reshape(bsz * t, PLE_DIM)
    xm, v, o, ifp, sb, cx, ga, gb = _in_proj(xf, wts["g_mix"], wts["w_main"], wts["w_if"], wts["w_rest"], tm)
    m0p = jnp.broadcast_to(jnp.pad(m0, ((0, 0), (0, SUBLANES - M_HEADS)))[:, :, None], (bsz, SUBLANES, LANES))
    hm, u, c1, n1, m1p, mconv1, sconv1 = _mlstm(
        xm, v, cx, ifp, _pad_tail(mconv0), _pad_tail(sconv0), c0, n0, m0p,
        wts["w_mconv"], wts["w_sconv"], wts["w_q"], wts["w_k"], wts["gate_bias"], wts["g_head"],
        bsz=bsz, t=t, tt=tt)
    x1 = _merge(xf, hm, o, ga, gb, sb, u, wts["w_a"], wts["w_sout"], wts["w_o"], tm)
    y = _moe_tail(x1, pf, wts["g_ffn"], wts["wr_hi"], wts["wr_lo"], wts["b_r"], wts["w_gate"], wts["w_up"],
                  wts["w_down"], wts["g_ple"], wts["w_ple_gate"], wts["w_ple_proj"], wts["g_final"])
    return (y.reshape(bsz, t, D_MODEL), c1[None], n1[None], m1p[None, :, :M_HEADS, 0],
            mconv1[None, :, SUBLANES - (M_CONV_W - 1):, :], sconv1[None, :, SUBLANES - (S_CONV_W - 1):, :])


def kernel(x_prompt, x_sample, p_prompt, p_sample, state_mlstm_C, state_mlstm_n, state_mlstm_m, state_mlstm_conv, state_sconv, g_mix, w_in, w_mconv, w_q, w_k, b_i, b_f, g_head, w_a, w_sconv, w_sout, w_o, g_ffn, w_rg, b_rg, w_re, b_re, w_gate, w_up, w_down, g_ple, w_ple_gate, w_ple_proj, g_final):
    assert g_mix.shape[0] == 1, "single-layer trunk"
    d = D_MODEL
    w_in0 = w_in[0]
    n_if = 2 * M_HEADS
    w_router = jnp.pad(jnp.concatenate([w_rg[0], w_re[0]], axis=1), ((0, 0), (0, LANES - N_GROUPS - N_EXPERTS)))
    wr_hi = w_router.astype(BF16)
    wr_lo = (w_router - wr_hi.astype(F32)).astype(BF16)
    sw = EXPERTS_PER_STEP * D_EXPERT
    wts = {
        "g_mix": g_mix,
        "w_main": w_in0[:, :3 * d].astype(BF16),
        "w_if": jnp.pad(w_in0[:, 3 * d:3 * d + n_if], ((0, 0), (0, LANES - n_if))).astype(BF16),
        "w_rest": w_in0[:, 3 * d + n_if:].astype(BF16),
        "w_mconv": w_mconv[0], "w_sconv": w_sconv[0],
        "w_q": w_q[0].astype(BF16), "w_k": w_k[0].astype(BF16),
        "gate_bias": jnp.pad(jnp.concatenate([b_i[0], b_f[0]])[None, :], ((0, 0), (0, LANES - n_if))),
        "g_head": g_head[0].reshape(1, d),
        "w_a": w_a[0].astype(BF16), "w_sout": w_sout[0].astype(BF16), "w_o": w_o[0].astype(BF16),
        "g_ffn": g_ffn, "wr_hi": wr_hi, "wr_lo": wr_lo,
        "b_r": jnp.pad(jnp.concatenate([b_rg[0], b_re[0]])[None, :], ((0, 0), (0, LANES - N_GROUPS - N_EXPERTS))),
        "w_gate": w_gate[0].astype(BF16).reshape(MOE_STEPS, EXPERTS_PER_STEP, d, D_EXPERT)
                  .transpose(0, 2, 1, 3).reshape(MOE_STEPS, d, sw),
        "w_up": w_up[0].astype(BF16).reshape(MOE_STEPS, EXPERTS_PER_STEP, d, D_EXPERT)
                .transpose(0, 2, 1, 3).reshape(MOE_STEPS, d, sw),
        "w_down": w_down[0].astype(BF16).reshape(MOE_STEPS, sw, d),
        "g_ple": g_ple, "w_ple_gate": w_ple_gate[0].astype(BF16), "w_ple_proj": w_ple_proj[0].astype(BF16),
        "g_final": g_final[None, :],
    }
    bp = x_prompt.shape[0]
    zeros = lambda *s: jnp.zeros(s, F32)
    yp, *st_p = _group(
        x_prompt, p_prompt[0], zeros(bp, M_CONV_W - 1, d), zeros(bp, S_CONV_W - 1, d),
        zeros(bp, M_HEADS, M_HEAD_DIM, M_HEAD_DIM), zeros(bp, M_HEADS, M_HEAD_DIM), zeros(bp, M_HEADS),
        wts, tt=256, tm=512)
    ys, *st_s = _group(
        x_sample, p_sample[0], state_mlstm_conv[0], state_sconv[0],
        state_mlstm_C[0], state_mlstm_n[0], state_mlstm_m[0],
        wts, tt=x_sample.shape[1], tm=512)
    return (yp, ys, *st_p, *st_s)
```

```python
import functools

import jax
import jax.numpy as jnp
from jax import lax
from jax.experimental import pallas as pl
from jax.experimental.pallas import tpu as pltpu

D_MODEL = 1024
M_HEADS = 4
M_HEAD_DIM = 256
M_CONV_W = 4
S_CONV_W = 3
N_GROUPS = 4
EXPERTS_PER_GROUP = 8
N_EXPERTS = N_GROUPS * EXPERTS_PER_GROUP
D_EXPERT = 256
PLE_DIM = 256
RMS_EPS = 1e-6

LANES = 128
SUBLANES = 8
VMEM_LIMIT = 56 * 1024 * 1024

BF16 = jnp.bfloat16
F32 = jnp.float32


def _rms(x, g):
    return x * lax.rsqrt(jnp.mean(x * x, axis=-1, keepdims=True) + RMS_EPS) * g


def _dot(a, b):
    return jnp.dot(a.astype(BF16), b.astype(BF16), preferred_element_type=F32)


def _sigmoid(x):
    return 0.5 * jnp.tanh(0.5 * x) + 0.5


def _silu(x):
    return x * _sigmoid(x)


def _resident(shape):
    nd = len(shape)
    return pl.BlockSpec(shape, lambda *_: (0,) * nd, pipeline_mode=pl.Buffered(1))


def _in_proj_kernel(x_ref, g_ref, w_main_ref, w_if_ref, w_rest_ref,
                    xm_ref, v_ref, o_ref, if_ref, sb_ref, cx_ref, ga_ref, gb_ref):
    h = _rms(x_ref[...], g_ref[...]).astype(BF16)
    d = D_MODEL
    xm_ref[...] = jnp.dot(h, w_main_ref[:, 0:d], preferred_element_type=F32)
    v_ref[...] = jnp.dot(h, w_main_ref[:, d:2 * d], preferred_element_type=F32)
    o_ref[...] = jnp.dot(h, w_main_ref[:, 2 * d:3 * d], preferred_element_type=F32)
    if_ref[...] = jnp.dot(h, w_if_ref[...], preferred_element_type=F32)
    sb_ref[...] = jnp.dot(h, w_rest_ref[:, 0:d], preferred_element_type=F32)
    s_c = jnp.dot(h, w_rest_ref[:, d:2 * d], preferred_element_type=F32)
    s_x = jnp.dot(h, w_rest_ref[:, 2 * d:3 * d], preferred_element_type=F32)
    cx_ref[...] = s_c * s_x
    ga_ref[...] = jnp.dot(h, w_rest_ref[:, 3 * d:4 * d], preferred_element_type=F32)
    gb_ref[...] = jnp.dot(h, w_rest_ref[:, 4 * d:5 * d], preferred_element_type=F32)


def _in_proj(x, g_mix, w_main, w_if, w_rest, tm):
    n = x.shape[0]
    row = lambda i: (i, 0)
    big = pl.BlockSpec((tm, D_MODEL), row)
    outs = [jax.ShapeDtypeStruct((n, D_MODEL), F32)] * 3 + [jax.ShapeDtypeStruct((n, LANES), F32)] \
        + [jax.ShapeDtypeStruct((n, D_MODEL), F32)] * 4
    return pl.pallas_call(
        _in_proj_kernel,
        grid=(n // tm,),
        in_specs=[big, _resident(g_mix.shape), _resident(w_main.shape), _resident(w_if.shape),
                  _resident(w_rest.shape)],
        out_specs=[big, big, big, pl.BlockSpec((tm, LANES), row), big, big, big, big],
        out_shape=outs,
        compiler_params=pltpu.CompilerParams(dimension_semantics=("arbitrary",),
                                             vmem_limit_bytes=VMEM_LIMIT),
        name="in_proj",
    )(x, g_mix, w_main, w_if, w_rest)


def _cumsum(x, axis):
    idx = lax.broadcasted_iota(jnp.int32, x.shape, axis)
    shift = 1
    while shift < x.shape[axis]:
        x = x + jnp.where(idx >= shift, pltpu.roll(x, shift, axis=axis), 0.0)
        shift *= 2
    return x


def _cumsum_rows(x):
    return _cumsum(x, 0)


def _log_sigmoid(x):
    return jnp.minimum(x, 0.0) - jnp.log1p(jnp.exp(-jnp.abs(x)))


def _mlstm_kernel(xm_ref, v_ref, cx_ref, if_ref, mconv0_ref, sconv0_ref, c0_ref, n0_ref, m0_ref,
                  w_mconv_ref, w_sconv_ref, wq_ref, wk_ref, bias_ref, ghead_ref,
                  hm_ref, u_ref, c1_ref, n1_ref, m1_ref, mconv1_ref, sconv1_ref,
                  xp_sc, cp_sc, *, tt):
    t = pl.program_id(1)

    @pl.when(t == 0)
    def _():
        xp_sc[0:SUBLANES, :] = mconv0_ref[...]
        cp_sc[0:SUBLANES, :] = sconv0_ref[...]
        c1_ref[...] = c0_ref[...]
        n1_ref[...] = n0_ref[...]
        m1_ref[...] = m0_ref[...]

    xp_sc[SUBLANES:SUBLANES + tt, :] = xm_ref[...]
    cp_sc[SUBLANES:SUBLANES + tt, :] = cx_ref[...]
    conv = jnp.zeros((tt, D_MODEL), F32)
    for j in range(M_CONV_W):
        off = SUBLANES - (M_CONV_W - 1) + j
        conv = conv + xp_sc[pl.ds(off, tt), :] * w_mconv_ref[j:j + 1, :]
    u = jnp.zeros((tt, D_MODEL), F32)
    for j in range(S_CONV_W):
        off = SUBLANES - (S_CONV_W - 1) + j
        u = u + cp_sc[pl.ds(off, tt), :] * w_sconv_ref[j:j + 1, :]
    u_ref[...] = u
    tail_x = xp_sc[tt:tt + SUBLANES, :]
    tail_c = cp_sc[tt:tt + SUBLANES, :]
    xp_sc[0:SUBLANES, :] = tail_x
    cp_sc[0:SUBLANES, :] = tail_c
    mconv1_ref[...] = tail_x
    sconv1_ref[...] = tail_c

    xq = _silu(conv)

    g = if_ref[...] + bias_ref[...]
    if tt % LANES == 0:
        g_t = g.T[0:SUBLANES, :]
        sub = lax.broadcasted_iota(jnp.int32, g_t.shape, 0)
        gates_t = jnp.where(sub < M_HEADS, g_t, _cumsum(_log_sigmoid(g_t), 1))
        gates = jnp.concatenate([gates_t, jnp.zeros((LANES - SUBLANES, tt), F32)], axis=0).T
    else:
        lane = lax.broadcasted_iota(jnp.int32, g.shape, 1)
        gates = jnp.where(lane < M_HEADS, g, _cumsum_rows(_log_sigmoid(g)))
        gates_t = gates.T

    r_idx = lax.broadcasted_iota(jnp.int32, (tt, tt), 0)
    c_idx = lax.broadcasted_iota(jnp.int32, (tt, tt), 1)
    causal = c_idx <= r_idx

    for h in range(M_HEADS):
        sl = slice(h * M_HEAD_DIM, (h + 1) * M_HEAD_DIM)
        xq_h = xq[:, sl]
        q = _dot(xq_h, wq_ref[h]) * (M_HEAD_DIM ** -0.5)
        k = _dot(xq_h, wk_ref[h])
        v = v_ref[:, sl]
        c_prev = c1_ref[h]
        n_prev = n1_ref[h:h + 1, :]
        m_prev = m1_ref[h:h + 1, 0:1]

        li_col = gates[:, h:h + 1]
        b_col = gates[:, M_HEADS + h:M_HEADS + h + 1]
        li_row = gates_t[h:h + 1, :]
        b_row = gates_t[M_HEADS + h:M_HEADS + h + 1, :]
        b_last = b_row[:, tt - 1:tt]

        dmat = jnp.where(causal, b_col - b_row + li_row, -jnp.inf)
        inter = m_prev + b_col
        m_t = jnp.maximum(inter, jnp.max(dmat, axis=-1, keepdims=True))
        w_inter = jnp.exp(inter - m_t)
        qk = lax.dot_general(q.astype(BF16), k.astype(BF16), (((1,), (1,)), ((), ())),
                             preferred_element_type=F32)
        s = jnp.exp(dmat - m_t) * qk
        num = w_inter * _dot(q, c_prev) + _dot(s, v)
        den = w_inter * jnp.sum(q * n_prev, axis=-1, keepdims=True) + jnp.sum(s, axis=-1, keepdims=True)
        out = num * (1.0 / jnp.maximum(jnp.abs(den), jnp.exp(-m_t)))
        out = out * lax.rsqrt(jnp.mean(out * out, axis=-1, keepdims=True) + RMS_EPS) * ghead_ref[:, sl]
        hm_ref[:, sl] = out

        lw_col = b_last - b_col + li_col
        lw_row = b_last - b_row + li_row
        m_new = jnp.maximum(m_prev + b_last, jnp.max(lw_row, axis=-1, keepdims=True))
        decay = jnp.exp(m_prev + b_last - m_new)
        wk = jnp.exp(lw_col - m_new) * k
        c1_ref[h] = decay * c_prev + lax.dot_general(
            wk.astype(BF16), v.astype(BF16), (((0,), (0,)), ((), ())), preferred_element_type=F32)
        n1_ref[h:h + 1, :] = decay * n_prev + jnp.sum(wk, axis=0, keepdims=True)
        m1_ref[h:h + 1, :] = jnp.broadcast_to(m_new, (1, LANES))


def _mlstm(xm, v, cx, ifp, mconv0, sconv0, c0, n0, m0, w_mconv, w_sconv, wq, wk, bias, ghead, *, bsz, t, tt):
    nt = t // tt
    n = bsz * t
    tok = lambda b, i: (b * nt + i, 0)
    seq3 = lambda b, i: (b, 0, 0)
    seq4 = lambda b, i: (b, 0, 0, 0)
    big = pl.BlockSpec((tt, D_MODEL), tok)
    tail = pl.BlockSpec((None, SUBLANES, D_MODEL), seq3)
    c_spec = pl.BlockSpec((None, M_HEADS, M_HEAD_DIM, M_HEAD_DIM), seq4)
    n_spec = pl.BlockSpec((None, M_HEADS, M_HEAD_DIM), seq3)
    m_spec = pl.BlockSpec((None, SUBLANES, LANES), seq3)
    return pl.pallas_call(
        functools.partial(_mlstm_kernel, tt=tt),
        grid=(bsz, nt),
        in_specs=[big, big, big, pl.BlockSpec((tt, LANES), tok), tail, tail, c_spec, n_spec, m_spec,
                  _resident(w_mconv.shape), _resident(w_sconv.shape), _resident(wq.shape),
                  _resident(wk.shape), _resident(bias.shape), _resident(ghead.shape)],
        out_specs=[big, big, c_spec, n_spec, m_spec, tail, tail],
        out_shape=[jax.ShapeDtypeStruct((n, D_MODEL), F32), jax.ShapeDtypeStruct((n, D_MODEL), F32),
                   jax.ShapeDtypeStruct(c0.shape, F32), jax.ShapeDtypeStruct(n0.shape, F32),
                   jax.ShapeDtypeStruct(m0.shape, F32),
                   jax.ShapeDtypeStruct(mconv0.shape, F32), jax.ShapeDtypeStruct(sconv0.shape, F32)],
        scratch_shapes=[pltpu.VMEM((tt + SUBLANES, D_MODEL), F32), pltpu.VMEM((tt + SUBLANES, D_MODEL), F32)],
        compiler_params=pltpu.CompilerParams(dimension_semantics=("arbitrary", "arbitrary"),
                                             vmem_limit_bytes=VMEM_LIMIT),
        name="mlstm",
    )(xm, v, cx, ifp, mconv0, sconv0, c0, n0, m0, w_mconv, w_sconv, wq, wk, bias, ghead)


def _merge_kernel(x_ref, hm_ref, o_ref, ga_ref, gb_ref, sb_ref, u_ref, wa_ref, wsout_ref, wo_ref, x1_ref):
    y_a = _dot(_sigmoid(o_ref[...]) * hm_ref[...], wa_ref[...])
    y_b = _dot(sb_ref[...] * u_ref[...], wsout_ref[...])
    merged = _sigmoid(ga_ref[...]) * y_a + _sigmoid(gb_ref[...]) * y_b
    x1_ref[...] = x_ref[...] + _dot(merged, wo_ref[...])


def _merge(x, hm, o, ga, gb, sb, u, w_a, w_sout, w_o, tm):
    n = x.shape[0]
    big = pl.BlockSpec((tm, D_MODEL), lambda i: (i, 0))
    return pl.pallas_call(
        _merge_kernel,
        grid=(n // tm,),
        in_specs=[big] * 7 + [_resident(w_a.shape), _resident(w_sout.shape), _resident(w_o.shape)],
        out_specs=big,
        out_shape=jax.ShapeDtypeStruct((n, D_MODEL), F32),
        compiler_params=pltpu.CompilerParams(dimension_semantics=("arbitrary",),
                                             vmem_limit_bytes=VMEM_LIMIT),
        name="merge",
    )(x, hm, o, ga, gb, sb, u, w_a, w_sout, w_o)


MOE_WINDOW = 1024
MOE_ROW_TILE = 128
MOE_ALIGN = 16
EXPERTS_PER_STEP = 4
MOE_STEPS = N_EXPERTS // EXPERTS_PER_STEP
MOE_CHUNK = 256
MOE_SORTED_ROWS = -(-(MOE_WINDOW + N_GROUPS * (MOE_ALIGN - 1) + MOE_ROW_TILE) // MOE_CHUNK) * MOE_CHUNK


def _split_bf16(x):
    hi = x.astype(BF16)
    lo = (x - hi.astype(F32)).astype(BF16)
    return hi, lo


def _split3_bf16(x):
    hi = x.astype(BF16)
    r1 = x - hi.astype(F32)
    mid = r1.astype(BF16)
    lo = (r1 - mid.astype(F32)).astype(BF16)
    return hi, mid, lo


def _route(logits):
    lane = lax.broadcasted_iota(jnp.int32, logits.shape, 1)
    neg = -jnp.inf
    big = jnp.int32(LANES)
    is_grp = lane < N_GROUPS
    g_max = jnp.max(jnp.where(is_grp, logits, neg), axis=-1, keepdims=True)
    g_sel = jnp.min(jnp.where(is_grp & (logits == g_max), lane, big), axis=-1, keepdims=True)
    p_grp = 1.0 / jnp.sum(jnp.where(is_grp, jnp.exp(logits - g_max), 0.0), axis=-1, keepdims=True)
    lo = N_GROUPS + g_sel * EXPERTS_PER_GROUP
    in_grp = (lane >= lo) & (lane < lo + EXPERTS_PER_GROUP)
    v1 = jnp.max(jnp.where(in_grp, logits, neg), axis=-1, keepdims=True)
    e1 = jnp.min(jnp.where(in_grp & (logits == v1), lane, big), axis=-1, keepdims=True)
    rest = in_grp & (lane != e1)
    v2 = jnp.max(jnp.where(rest, logits, neg), axis=-1, keepdims=True)
    e2 = jnp.min(jnp.where(rest & (logits == v2), lane, big), axis=-1, keepdims=True)
    z = jnp.exp(v2 - v1)
    w1 = p_grp / (1.0 + z)
    w2 = p_grp * z / (1.0 + z)
    return g_sel, jnp.where(lane == e1, w1, jnp.where(lane == e2, w2, 0.0))


def _moe_tail_kernel(x1_ref, p_ref, gffn_ref, wr_hi_ref, wr_lo_ref, br_ref, wg_ref, wu_ref, wd_ref,
                     gple_ref, wpg_ref, wpp_ref, gfin_ref, y_ref,
                     xs_sc, cws_sc, osort_sc, pos_sc, off_sm, nt_sm):
    s = pl.program_id(1)
    w = MOE_WINDOW

    @pl.when(s == 0)
    def _():
        hn = _rms(x1_ref[...], gffn_ref[...])
        hn_hi, hn_lo = _split_bf16(hn)
        logits = (jnp.dot(hn_hi, wr_hi_ref[...], preferred_element_type=F32)
                  + jnp.dot(hn_hi, wr_lo_ref[...], preferred_element_type=F32)
                  + jnp.dot(hn_lo, wr_hi_ref[...], preferred_element_type=F32)) + br_ref[...]
        g_sel, cw = _route(logits)
        lane = lax.broadcasted_iota(jnp.int32, (w, LANES), 1)
        onehot = jnp.where(lane == g_sel, 1.0, 0.0)
        cum = _cumsum_rows(onehot)
        cnt = cum[w - 1:w, :].astype(jnp.int32)
        cnt_pad = ((cnt + (MOE_ALIGN - 1)) // MOE_ALIGN) * MOE_ALIGN
        lane1 = lax.broadcasted_iota(jnp.int32, (1, LANES), 1)
        off = jnp.zeros((1, LANES), jnp.int32)
        for gi in range(N_GROUPS - 1):
            off = off + jnp.where(lane1 > gi, cnt_pad[:, gi:gi + 1], 0)
        n_tiles = (cnt + (MOE_ROW_TILE - 1)) // MOE_ROW_TILE
        for gi in range(N_GROUPS):
            off_sm[gi] = off[0, gi]
            nt_sm[gi] = n_tiles[0, gi]
        pos = jnp.sum(onehot * (off.astype(F32) + cum - 1.0), axis=-1, keepdims=True)
        pos_b = jnp.broadcast_to(pos, (w, LANES))
        pos_sc[...] = pos_b
        pos_row = pos_b.T[0:1, :].astype(jnp.int32)
        cw_hi, cw_mid, cw_lo = _split3_bf16(cw)
        for c in range(MOE_SORTED_ROWS // MOE_CHUNK):
            rows = lax.broadcasted_iota(jnp.int32, (MOE_CHUNK, w), 0) + c * MOE_CHUNK
            sel = jnp.where(rows == pos_row, 1.0, 0.0).astype(BF16)
            sl = slice(c * MOE_CHUNK, (c + 1) * MOE_CHUNK)
            xs_sc[sl, :] = jnp.dot(sel, hn_hi, preferred_element_type=F32).astype(BF16)
            cws_sc[sl, :] = (jnp.dot(sel, cw_hi, preferred_element_type=F32)
                             + jnp.dot(sel, cw_mid, preferred_element_type=F32)
                             + jnp.dot(sel, cw_lo, preferred_element_type=F32))
        osort_sc[...] = jnp.zeros(osort_sc.shape, F32)

    grp = s // (EXPERTS_PER_GROUP // EXPERTS_PER_STEP)
    row0 = off_sm[grp]
    lane_t = lax.broadcasted_iota(jnp.int32, (MOE_ROW_TILE, LANES), 1)
    first_lane = N_GROUPS + s * EXPERTS_PER_STEP

    def tile_body(i, carry):
        r0 = pl.multiple_of(row0 + i * MOE_ROW_TILE, MOE_ALIGN)
        xt = xs_sc[pl.ds(r0, MOE_ROW_TILE), :]
        cwt = cws_sc[pl.ds(r0, MOE_ROW_TILE), :]
        parts = []
        for e in range(EXPERTS_PER_STEP):
            hg = jnp.dot(xt, wg_ref[e], preferred_element_type=F32)
            hu = jnp.dot(xt, wu_ref[e], preferred_element_type=F32)
            col = jnp.sum(jnp.where(lane_t == first_lane + e, cwt, 0.0), axis=-1, keepdims=True)
            parts.append((_silu(hg) * hu * col).astype(BF16))
        he = jnp.concatenate(parts, axis=-1)
        out = jnp.dot(he, wd_ref[...], preferred_element_type=F32)
        osort_sc[pl.ds(r0, MOE_ROW_TILE), :] = osort_sc[pl.ds(r0, MOE_ROW_TILE), :] + out
        return carry

    lax.fori_loop(0, nt_sm[grp], tile_body, 0)

    @pl.when(s == MOE_STEPS - 1)
    def _():
        osort = osort_sc[...].astype(BF16)
        for c in range(w // MOE_CHUNK):
            sl = slice(c * MOE_CHUNK, (c + 1) * MOE_CHUNK)
            pos_col = pos_sc[sl, 0:1].astype(jnp.int32)
            cols = lax.broadcasted_iota(jnp.int32, (MOE_CHUNK, MOE_SORTED_ROWS), 1)
            sel = jnp.where(cols == pos_col, 1.0, 0.0).astype(BF16)
            x2 = x1_ref[sl, :] + jnp.dot(sel, osort, preferred_element_type=F32)
            gate = _sigmoid(_dot(_rms(x2, gple_ref[...]), wpg_ref[...]))
            x3 = x2 + gate * _dot(p_ref[sl, :], wpp_ref[...])
            y_ref[sl, :] = _rms(x3, gfin_ref[...])


def _moe_tail(x1, p, g_ffn, wr_hi, wr_lo, b_r, wg, wu, wd, g_ple, w_pg, w_pp, g_fin):
    n = x1.shape[0]
    tm = MOE_WINDOW
    row = lambda i, s: (i, 0)
    step = lambda i, s: (s, 0, 0)
    big = pl.BlockSpec((tm, D_MODEL), row)
    sw = EXPERTS_PER_STEP * D_EXPERT
    return pl.pallas_call(
        _moe_tail_kernel,
        grid=(n // tm, MOE_STEPS),
        in_specs=[big, pl.BlockSpec((tm, PLE_DIM), row), _resident(g_ffn.shape), _resident(wr_hi.shape),
                  _resident(wr_lo.shape), _resident(b_r.shape),
                  pl.BlockSpec((EXPERTS_PER_STEP, D_MODEL, D_EXPERT), step),
                  pl.BlockSpec((EXPERTS_PER_STEP, D_MODEL, D_EXPERT), step),
                  pl.BlockSpec((None, sw, D_MODEL), step),
                  _resident(g_ple.shape), _resident(w_pg.shape), _resident(w_pp.shape), _resident(g_fin.shape)],
        out_specs=big,
        out_shape=jax.ShapeDtypeStruct((n, D_MODEL), F32),
        scratch_shapes=[pltpu.VMEM((MOE_SORTED_ROWS, D_MODEL), BF16), pltpu.VMEM((MOE_SORTED_ROWS, LANES), F32),
                        pltpu.VMEM((MOE_SORTED_ROWS, D_MODEL), F32), pltpu.VMEM((tm, LANES), F32),
                        pltpu.SMEM((N_GROUPS,), jnp.int32), pltpu.SMEM((N_GROUPS,), jnp.int32)],
        compiler_params=pltpu.CompilerParams(dimension_semantics=("arbitrary", "arbitrary"),
                                             vmem_limit_bytes=VMEM_LIMIT),
        name="moe_tail",
    )(x1, p, g_ffn, wr_hi, wr_lo, b_r, wg, wu, wd, g_ple, w_pg, w_pp, g_fin)


def _pad_tail(buf):
    return jnp.pad(buf, ((0, 0), (SUBLANES - buf.shape[1], 0), (0, 0)))


def _group(x, p, mconv0, sconv0, c0, n0, m0, wts, *, tt, tm):
    bsz, t, _ = x.shape
    xf = x.reshape(bsz * t, D_MODEL)
    pf = p.reshape(bsz * t, PLE_DIM)
    xm, v, o, ifp, sb, cx, ga, gb = _in_proj(xf, wts["g_mix"], wts["w_main"], wts["w_if"], wts["w_rest"], tm)
    m0p = jnp.broadcast_to(jnp.pad(m0, ((0, 0), (0, SUBLANES - M_HEADS)))[:, :, None], (bsz, SUBLANES, LANES))
    hm, u, c1, n1, m1p, mconv1, sconv1 = _mlstm(
        xm, v, cx, ifp, _pad_tail(mconv0), _pad_tail(sconv0), c0, n0, m0p,
        wts["w_mconv"], wts["w_sconv"], wts["w_q"], wts["w_k"], wts["gate_bias"], wts["g_head"],
        bsz=bsz, t=t, tt=tt)
    x1 = _merge(xf, hm, o, ga, gb, sb, u, wts["w_a"], wts["w_sout"], wts["w_o"], tm)
    y = _moe_tail(x1, pf, wts["g_ffn"], wts["wr_hi"], wts["wr_lo"], wts["b_r"], wts["w_gate"], wts["w_up"],
                  wts["w_down"], wts["g_ple"], wts["w_ple_gate"], wts["w_ple_proj"], wts["g_final"])
    return (y.reshape(bsz, t, D_MODEL), c1[None], n1[None], m1p[None, :, :M_HEADS, 0],
            mconv1[None, :, SUBLANES - (M_CONV_W - 1):, :], sconv1[None, :, SUBLANES - (S_CONV_W - 1):, :])


def kernel(x_prompt, x_sample, p_prompt, p_sample, state_mlstm_C, state_mlstm_n, state_mlstm_m, state_mlstm_conv, state_sconv, g_mix, w_in, w_mconv, w_q, w_k, b_i, b_f, g_head, w_a, w_sconv, w_sout, w_o, g_ffn, w_rg, b_rg, w_re, b_re, w_gate, w_up, w_down, g_ple, w_ple_gate, w_ple_proj, g_final):
    assert g_mix.shape[0] == 1, "single-layer trunk"
    d = D_MODEL
    w_in0 = w_in[0]
    n_if = 2 * M_HEADS
    w_router = jnp.pad(jnp.concatenate([w_rg[0], w_re[0]], axis=1), ((0, 0), (0, LANES - N_GROUPS - N_EXPERTS)))
    wr_hi = w_router.astype(BF16)
    wr_lo = (w_router - wr_hi.astype(F32)).astype(BF16)
    sw = EXPERTS_PER_STEP * D_EXPERT
    wts = {
        "g_mix": g_mix,
        "w_main": w_in0[:, :3 * d].astype(BF16),
        "w_if": jnp.pad(w_in0[:, 3 * d:3 * d + n_if], ((0, 0), (0, LANES - n_if))).astype(BF16),
        "w_rest": w_in0[:, 3 * d + n_if:].astype(BF16),
        "w_mconv": w_mconv[0], "w_sconv": w_sconv[0],
        "w_q": w_q[0].astype(BF16), "w_k": w_k[0].astype(BF16),
        "gate_bias": jnp.pad(jnp.concatenate([b_i[0], b_f[0]])[None, :], ((0, 0), (0, LANES - n_if))),
        "g_head": g_head[0].reshape(1, d),
        "w_a": w_a[0].astype(BF16), "w_sout": w_sout[0].astype(BF16), "w_o": w_o[0].astype(BF16),
        "g_ffn": g_ffn, "wr_hi": wr_hi, "wr_lo": wr_lo,
        "b_r": jnp.pad(jnp.concatenate([b_rg[0], b_re[0]])[None, :], ((0, 0), (0, LANES - N_GROUPS - N_EXPERTS))),
        "w_gate": w_gate[0].astype(BF16), "w_up": w_up[0].astype(BF16),
        "w_down": w_down[0].astype(BF16).reshape(MOE_STEPS, sw, d),
        "g_ple": g_ple, "w_ple_gate": w_ple_gate[0].astype(BF16), "w_ple_proj": w_ple_proj[0].astype(BF16),
        "g_final": g_final[None, :],
    }
    bp = x_prompt.shape[0]
    zeros = lambda *s: jnp.zeros(s, F32)
    yp, *st_p = _group(
        x_prompt, p_prompt[0], zeros(bp, M_CONV_W - 1, d), zeros(bp, S_CONV_W - 1, d),
        zeros(bp, M_HEADS, M_HEAD_DIM, M_HEAD_DIM), zeros(bp, M_HEADS, M_HEAD_DIM), zeros(bp, M_HEADS),
        wts, tt=256, tm=512)
    ys, *st_s = _group(
        x_sample, p_sample[0], state_mlstm_conv[0], state_sconv[0],
        state_mlstm_C[0], state_mlstm_n[0], state_mlstm_m[0],
        wts, tt=x_sample.shape[1], tm=512)
    return (yp, ys, *st_p, *st_s)
```

```python
import functools

import jax
import jax.numpy as jnp
from jax import lax
from jax.experimental import pallas as pl
from jax.experimental.pallas import tpu as pltpu

D_MODEL = 1024
M_HEADS = 4
M_HEAD_DIM = 256
M_CONV_W = 4
S_CONV_W = 3
N_GROUPS = 4
EXPERTS_PER_GROUP = 8
N_EXPERTS = N_GROUPS * EXPERTS_PER_GROUP
D_EXPERT = 256
PLE_DIM = 256
RMS_EPS = 1e-6

LANES = 128
SUBLANES = 8
VMEM_LIMIT = 56 * 1024 * 1024

BF16 = jnp.bfloat16
F32 = jnp.float32


def _rms(x, g):
    return x * lax.rsqrt(jnp.mean(x * x, axis=-1, keepdims=True) + RMS_EPS) * g


def _dot(a, b):
    return jnp.dot(a.astype(BF16), b.astype(BF16), preferred_element_type=F32)


def _sigmoid(x):
    return 0.5 * jnp.tanh(0.5 * x) + 0.5


def _silu(x):
    return x * _sigmoid(x)


def _resident(shape):
    nd = len(shape)
    return pl.BlockSpec(shape, lambda *_: (0,) * nd, pipeline_mode=pl.Buffered(1))


def _in_proj_kernel(x_ref, g_ref, w_main_ref, w_if_ref, w_rest_ref,
                    xm_ref, v_ref, o_ref, if_ref, sb_ref, cx_ref, ga_ref, gb_ref):
    h = _rms(x_ref[...], g_ref[...]).astype(BF16)
    d = D_MODEL
    xm_ref[...] = jnp.dot(h, w_main_ref[:, 0:d], preferred_element_type=F32)
    v_ref[...] = jnp.dot(h, w_main_ref[:, d:2 * d], preferred_element_type=F32)
    o_ref[...] = jnp.dot(h, w_main_ref[:, 2 * d:3 * d], preferred_element_type=F32)
    if_ref[...] = jnp.dot(h, w_if_ref[...], preferred_element_type=F32)
    sb_ref[...] = jnp.dot(h, w_rest_ref[:, 0:d], preferred_element_type=F32)
    s_c = jnp.dot(h, w_rest_ref[:, d:2 * d], preferred_element_type=F32)
    s_x = jnp.dot(h, w_rest_ref[:, 2 * d:3 * d], preferred_element_type=F32)
    cx_ref[...] = s_c * s_x
    ga_ref[...] = jnp.dot(h, w_rest_ref[:, 3 * d:4 * d], preferred_element_type=F32)
    gb_ref[...] = jnp.dot(h, w_rest_ref[:, 4 * d:5 * d], preferred_element_type=F32)


def _in_proj(x, g_mix, w_main, w_if, w_rest, tm):
    n = x.shape[0]
    row = lambda i: (i, 0)
    big = pl.BlockSpec((tm, D_MODEL), row)
    outs = [jax.ShapeDtypeStruct((n, D_MODEL), F32)] * 3 + [jax.ShapeDtypeStruct((n, LANES), F32)] \
        + [jax.ShapeDtypeStruct((n, D_MODEL), F32)] * 4
    return pl.pallas_call(
        _in_proj_kernel,
        grid=(n // tm,),
        in_specs=[big, _resident(g_mix.shape), _resident(w_main.shape), _resident(w_if.shape),
                  _resident(w_rest.shape)],
        out_specs=[big, big, big, pl.BlockSpec((tm, LANES), row), big, big, big, big],
        out_shape=outs,
        compiler_params=pltpu.CompilerParams(dimension_semantics=("arbitrary",),
                                             vmem_limit_bytes=VMEM_LIMIT),
        name="in_proj",
    )(x, g_mix, w_main, w_if, w_rest)


def _scan(x, axis, op, identity):
    idx = lax.broadcasted_iota(jnp.int32, x.shape, axis)
    shift = 1
    while shift < x.shape[axis]:
        x = op(x, jnp.where(idx >= shift, pltpu.roll(x, shift, axis=axis), identity))
        shift *= 2
    return x


def _cumsum_rows(x):
    return _scan(x, 0, jnp.add, 0.0)


def _scan_rows(x, op, identity):
    sub = lax.broadcasted_iota(jnp.int32, (SUBLANES, x.shape[1]), 0)
    blocks, carry = [], None
    for i in range(x.shape[0] // SUBLANES):
        blk = x[i * SUBLANES:(i + 1) * SUBLANES, :]
        for shift in (1, 2, 4):
            blk = op(blk, jnp.where(sub >= shift, pltpu.roll(blk, shift, axis=0), identity))
        if carry is not None:
            blk = op(blk, carry)
        carry = jnp.broadcast_to(blk[SUBLANES - 1:SUBLANES, :], blk.shape)
        blocks.append(blk)
    return jnp.concatenate(blocks, axis=0)


def _log_sigmoid(x):
    return jnp.minimum(x, 0.0) - jnp.log1p(jnp.exp(-jnp.abs(x)))


def _mlstm_kernel(xm_ref, v_ref, cx_ref, if_ref, mconv0_ref, sconv0_ref, c0_ref, n0_ref, m0_ref,
                  w_mconv_ref, w_sconv_ref, wq_ref, wk_ref, bias_ref, ghead_ref,
                  hm_ref, u_ref, c1_ref, n1_ref, m1_ref, mconv1_ref, sconv1_ref,
                  xp_sc, cp_sc, *, tt):
    t = pl.program_id(1)

    @pl.when(t == 0)
    def _():
        xp_sc[0:SUBLANES, :] = mconv0_ref[...]
        cp_sc[0:SUBLANES, :] = sconv0_ref[...]
        c1_ref[...] = c0_ref[...]
        n1_ref[...] = n0_ref[...]
        m1_ref[...] = m0_ref[...]

    xp_sc[SUBLANES:SUBLANES + tt, :] = xm_ref[...]
    cp_sc[SUBLANES:SUBLANES + tt, :] = cx_ref[...]
    conv = jnp.zeros((tt, D_MODEL), F32)
    for j in range(M_CONV_W):
        off = SUBLANES - (M_CONV_W - 1) + j
        conv = conv + xp_sc[pl.ds(off, tt), :] * w_mconv_ref[j:j + 1, :]
    u = jnp.zeros((tt, D_MODEL), F32)
    for j in range(S_CONV_W):
        off = SUBLANES - (S_CONV_W - 1) + j
        u = u + cp_sc[pl.ds(off, tt), :] * w_sconv_ref[j:j + 1, :]
    u_ref[...] = u
    tail_x = xp_sc[tt:tt + SUBLANES, :]
    tail_c = cp_sc[tt:tt + SUBLANES, :]
    xp_sc[0:SUBLANES, :] = tail_x
    cp_sc[0:SUBLANES, :] = tail_c
    mconv1_ref[...] = tail_x
    sconv1_ref[...] = tail_c

    xq = _silu(conv)

    g = if_ref[...] + bias_ref[...]
    nh = M_HEADS
    b_c = _scan_rows(_log_sigmoid(g), jnp.add, 0.0)
    a_c = g - pltpu.roll(b_c, LANES - nh, axis=1)
    amax_c = _scan_rows(a_c, jnp.maximum, -jnp.inf)
    a_t = a_c.T

    r_idx = lax.broadcasted_iota(jnp.int32, (tt, tt), 0)
    c_idx = lax.broadcasted_iota(jnp.int32, (tt, tt), 1)
    causal = c_idx <= r_idx

    heads = range(M_HEADS)
    hs = [slice(h * M_HEAD_DIM, (h + 1) * M_HEAD_DIM) for h in heads]
    q = [_dot(xq[:, hs[h]], wq_ref[h]) * (M_HEAD_DIM ** -0.5) for h in heads]
    k = [_dot(xq[:, hs[h]], wk_ref[h]) for h in heads]
    v = [v_ref[:, hs[h]] for h in heads]
    c_prev = [c1_ref[h] for h in heads]
    n_prev = [n1_ref[h:h + 1, :] for h in heads]
    m_prev = [m1_ref[h:h + 1, 0:1] for h in heads]
    b_col = [b_c[:, nh + h:nh + h + 1] for h in heads]
    a_col = [a_c[:, h:h + 1] for h in heads]
    a_row = [a_t[h:h + 1, :] for h in heads]
    b_last = [b_c[tt - 1:tt, nh + h:nh + h + 1] for h in heads]
    mm_col = [jnp.maximum(m_prev[h], amax_c[:, h:h + 1]) for h in heads]
    mm_last = [jnp.maximum(m_prev[h], amax_c[tt - 1:tt, h:h + 1]) for h in heads]

    w_inter = [jnp.exp(m_prev[h] - mm_col[h]) for h in heads]
    qk = [lax.dot_general(q[h].astype(BF16), k[h].astype(BF16), (((1,), (1,)), ((), ())),
                          preferred_element_type=F32) for h in heads]
    s = [jnp.exp(jnp.where(causal, a_row[h] - mm_col[h], -jnp.inf)) * qk[h] for h in heads]
    num = [w_inter[h] * _dot(q[h], c_prev[h]) + _dot(s[h], v[h]) for h in heads]
    den = [w_inter[h] * jnp.sum(q[h] * n_prev[h], axis=-1, keepdims=True)
           + jnp.sum(s[h], axis=-1, keepdims=True) for h in heads]
    out = [num[h] * (1.0 / jnp.maximum(jnp.abs(den[h]), jnp.exp(-(b_col[h] + mm_col[h])))) for h in heads]
    for h in heads:
        hm_ref[:, hs[h]] = out[h] * lax.rsqrt(jnp.mean(out[h] * out[h], axis=-1, keepdims=True) + RMS_EPS) \
            * ghead_ref[:, hs[h]]

    decay = [jnp.exp(m_prev[h] - mm_last[h]) for h in heads]
    wk = [jnp.exp(a_col[h] - mm_last[h]) * k[h] for h in heads]
    for h in heads:
        c1_ref[h] = decay[h] * c_prev[h] + lax.dot_general(
            wk[h].astype(BF16), v[h].astype(BF16), (((0,), (0,)), ((), ())), preferred_element_type=F32)
        n1_ref[h:h + 1, :] = decay[h] * n_prev[h] + jnp.sum(wk[h], axis=0, keepdims=True)
        m1_ref[h:h + 1, :] = jnp.broadcast_to(b_last[h] + mm_last[h], (1, LANES))


def _mlstm(xm, v, cx, ifp, mconv0, sconv0, c0, n0, m0, w_mconv, w_sconv, wq, wk, bias, ghead, *, bsz, t, tt):
    nt = t // tt
    n = bsz * t
    tok = lambda b, i: (b * nt + i, 0)
    seq3 = lambda b, i: (b, 0, 0)
    seq4 = lambda b, i: (b, 0, 0, 0)
    big = pl.BlockSpec((tt, D_MODEL), tok)
    tail = pl.BlockSpec((None, SUBLANES, D_MODEL), seq3)
    c_spec = pl.BlockSpec((None, M_HEADS, M_HEAD_DIM, M_HEAD_DIM), seq4)
    n_spec = pl.BlockSpec((None, M_HEADS, M_HEAD_DIM), seq3)
    m_spec = pl.BlockSpec((None, SUBLANES, LANES), seq3)
    return pl.pallas_call(
        functools.partial(_mlstm_kernel, tt=tt),
        grid=(bsz, nt),
        in_specs=[big, big, big, pl.BlockSpec((tt, LANES), tok), tail, tail, c_spec, n_spec, m_spec,
                  _resident(w_mconv.shape), _resident(w_sconv.shape), _resident(wq.shape),
                  _resident(wk.shape), _resident(bias.shape), _resident(ghead.shape)],
        out_specs=[big, big, c_spec, n_spec, m_spec, tail, tail],
        out_shape=[jax.ShapeDtypeStruct((n, D_MODEL), F32), jax.ShapeDtypeStruct((n, D_MODEL), F32),
                   jax.ShapeDtypeStruct(c0.shape, F32), jax.ShapeDtypeStruct(n0.shape, F32),
                   jax.ShapeDtypeStruct(m0.shape, F32),
                   jax.ShapeDtypeStruct(mconv0.shape, F32), jax.ShapeDtypeStruct(sconv0.shape, F32)],
        scratch_shapes=[pltpu.VMEM((tt + SUBLANES, D_MODEL), F32), pltpu.VMEM((tt + SUBLANES, D_MODEL), F32)],
        compiler_params=pltpu.CompilerParams(dimension_semantics=("arbitrary", "arbitrary"),
                                             vmem_limit_bytes=VMEM_LIMIT),
        name="mlstm",
    )(xm, v, cx, ifp, mconv0, sconv0, c0, n0, m0, w_mconv, w_sconv, wq, wk, bias, ghead)


def _merge_kernel(x_ref, hm_ref, o_ref, ga_ref, gb_ref, sb_ref, u_ref, wa_ref, wsout_ref, wo_ref, x1_ref):
    y_a = _dot(_sigmoid(o_ref[...]) * hm_ref[...], wa_ref[...])
    y_b = _dot(sb_ref[...] * u_ref[...], wsout_ref[...])
    merged = _sigmoid(ga_ref[...]) * y_a + _sigmoid(gb_ref[...]) * y_b
    x1_ref[...] = x_ref[...] + _dot(merged, wo_ref[...])


def _merge(x, hm, o, ga, gb, sb, u, w_a, w_sout, w_o, tm):
    n = x.shape[0]
    big = pl.BlockSpec((tm, D_MODEL), lambda i: (i, 0))
    return pl.pallas_call(
        _merge_kernel,
        grid=(n // tm,),
        in_specs=[big] * 7 + [_resident(w_a.shape), _resident(w_sout.shape), _resident(w_o.shape)],
        out_specs=big,
        out_shape=jax.ShapeDtypeStruct((n, D_MODEL), F32),
        compiler_params=pltpu.CompilerParams(dimension_semantics=("arbitrary",),
                                             vmem_limit_bytes=VMEM_LIMIT),
        name="merge",
    )(x, hm, o, ga, gb, sb, u, w_a, w_sout, w_o)


MOE_WINDOW = 1024
MOE_ROW_TILE = 128
MOE_ALIGN = 16
EXPERTS_PER_STEP = 4
MOE_STEPS = N_EXPERTS // EXPERTS_PER_STEP
MOE_CHUNK = 256
MOE_SORTED_ROWS = -(-(MOE_WINDOW + N_GROUPS * (MOE_ALIGN - 1) + MOE_ROW_TILE) // MOE_CHUNK) * MOE_CHUNK


def _split_bf16(x):
    hi = x.astype(BF16)
    lo = (x - hi.astype(F32)).astype(BF16)
    return hi, lo


def _split3_bf16(x):
    hi = x.astype(BF16)
    r1 = x - hi.astype(F32)
    mid = r1.astype(BF16)
    lo = (r1 - mid.astype(F32)).astype(BF16)
    return hi, mid, lo


def _route(logits):
    lane = lax.broadcasted_iota(jnp.int32, logits.shape, 1)
    neg = -jnp.inf
    big = jnp.int32(LANES)
    is_grp = lane < N_GROUPS
    g_max = jnp.max(jnp.where(is_grp, logits, neg), axis=-1, keepdims=True)
    g_sel = jnp.min(jnp.where(is_grp & (logits == g_max), lane, big), axis=-1, keepdims=True)
    p_grp = 1.0 / jnp.sum(jnp.where(is_grp, jnp.exp(logits - g_max), 0.0), axis=-1, keepdims=True)
    lo = N_GROUPS + g_sel * EXPERTS_PER_GROUP
    in_grp = (lane >= lo) & (lane < lo + EXPERTS_PER_GROUP)
    v1 = jnp.max(jnp.where(in_grp, logits, neg), axis=-1, keepdims=True)
    e1 = jnp.min(jnp.where(in_grp & (logits == v1), lane, big), axis=-1, keepdims=True)
    rest = in_grp & (lane != e1)
    v2 = jnp.max(jnp.where(rest, logits, neg), axis=-1, keepdims=True)
    e2 = jnp.min(jnp.where(rest & (logits == v2), lane, big), axis=-1, keepdims=True)
    z = jnp.exp(v2 - v1)
    w1 = p_grp / (1.0 + z)
    w2 = p_grp * z / (1.0 + z)
    return g_sel, jnp.where(lane == e1, w1, jnp.where(lane == e2, w2, 0.0))


def _moe_tail_kernel(x1_ref, p_ref, gffn_ref, wr_hi_ref, wr_lo_ref, br_ref, wg_ref, wu_ref, wd_ref,
                     gple_ref, wpg_ref, wpp_ref, gfin_ref, y_ref,
                     xs_sc, cws_sc, osort_sc, pos_sc, off_sm, nt_sm):
    s = pl.program_id(1)
    w = MOE_WINDOW

    @pl.when(s == 0)
    def _():
        hn = _rms(x1_ref[...], gffn_ref[...])
        hn_hi, hn_lo = _split_bf16(hn)
        logits = (jnp.dot(hn_hi, wr_hi_ref[...], preferred_element_type=F32)
                  + jnp.dot(hn_hi, wr_lo_ref[...], preferred_element_type=F32)
                  + jnp.dot(hn_lo, wr_hi_ref[...], preferred_element_type=F32)) + br_ref[...]
        g_sel, cw = _route(logits)
        lane = lax.broadcasted_iota(jnp.int32, (w, LANES), 1)
        onehot = jnp.where(lane == g_sel, 1.0, 0.0)
        cum = _cumsum_rows(onehot)
        cnt = cum[w - 1:w, :].astype(jnp.int32)
        cnt_pad = ((cnt + (MOE_ALIGN - 1)) // MOE_ALIGN) * MOE_ALIGN
        lane1 = lax.broadcasted_iota(jnp.int32, (1, LANES), 1)
        off = jnp.zeros((1, LANES), jnp.int32)
        for gi in range(N_GROUPS - 1):
            off = off + jnp.where(lane1 > gi, cnt_pad[:, gi:gi + 1], 0)
        n_tiles = (cnt + (MOE_ROW_TILE - 1)) // MOE_ROW_TILE
        for gi in range(N_GROUPS):
            off_sm[gi] = off[0, gi]
            nt_sm[gi] = n_tiles[0, gi]
        pos = jnp.sum(onehot * (off.astype(F32) + cum - 1.0), axis=-1, keepdims=True)
        pos_b = jnp.broadcast_to(pos, (w, LANES))
        pos_sc[...] = pos_b
        pos_row = pos_b.T[0:1, :].astype(jnp.int32)
        cw_hi, cw_mid, cw_lo = _split3_bf16(cw)
        for c in range(MOE_SORTED_ROWS // MOE_CHUNK):
            rows = lax.broadcasted_iota(jnp.int32, (MOE_CHUNK, w), 0) + c * MOE_CHUNK
            sel = jnp.where(rows == pos_row, 1.0, 0.0).astype(BF16)
            sl = slice(c * MOE_CHUNK, (c + 1) * MOE_CHUNK)
            xs_sc[sl, :] = jnp.dot(sel, hn_hi, preferred_element_type=F32).astype(BF16)
            cws_sc[sl, :] = (jnp.dot(sel, cw_hi, preferred_element_type=F32)
                             + jnp.dot(sel, cw_mid, preferred_element_type=F32)
                             + jnp.dot(sel, cw_lo, preferred_element_type=F32))
        osort_sc[...] = jnp.zeros(osort_sc.shape, F32)

    grp = s // (EXPERTS_PER_GROUP // EXPERTS_PER_STEP)
    row0 = off_sm[grp]
    lane_t = lax.broadcasted_iota(jnp.int32, (MOE_ROW_TILE, LANES), 1)
    first_lane = N_GROUPS + s * EXPERTS_PER_STEP

    def tile_body(i, carry):
        r0 = pl.multiple_of(row0 + i * MOE_ROW_TILE, MOE_ALIGN)
        xt = xs_sc[pl.ds(r0, MOE_ROW_TILE), :]
        cwt = cws_sc[pl.ds(r0, MOE_ROW_TILE), :]
        parts = []
        for e in range(EXPERTS_PER_STEP):
            hg = jnp.dot(xt, wg_ref[e], preferred_element_type=F32)
            hu = jnp.dot(xt, wu_ref[e], preferred_element_type=F32)
            col = jnp.sum(jnp.where(lane_t == first_lane + e, cwt, 0.0), axis=-1, keepdims=True)
            parts.append((_silu(hg) * hu * col).astype(BF16))
        he = jnp.concatenate(parts, axis=-1)
        out = jnp.dot(he, wd_ref[...], preferred_element_type=F32)
        osort_sc[pl.ds(r0, MOE_ROW_TILE), :] = osort_sc[pl.ds(r0, MOE_ROW_TILE), :] + out
        return carry

    lax.fori_loop(0, nt_sm[grp], tile_body, 0)

    @pl.when(s == MOE_STEPS - 1)
    def _():
        osort = osort_sc[...].astype(BF16)
        for c in range(w // MOE_CHUNK):
            sl = slice(c * MOE_CHUNK, (c + 1) * MOE_CHUNK)
            pos_col = pos_sc[sl, 0:1].astype(jnp.int32)
            cols = lax.broadcasted_iota(jnp.int32, (MOE_CHUNK, MOE_SORTED_ROWS), 1)
            sel = jnp.where(cols == pos_col, 1.0, 0.0).astype(BF16)
            x2 = x1_ref[sl, :] + jnp.dot(sel, osort, preferred_element_type=F32)
            gate = _sigmoid(_dot(_rms(x2, gple_ref[...]), wpg_ref[...]))
            x3 = x2 + gate * _dot(p_ref[sl, :], wpp_ref[...])
            y_ref[sl, :] = _rms(x3, gfin_ref[...])


def _moe_tail(x1, p, g_ffn, wr_hi, wr_lo, b_r, wg, wu, wd, g_ple, w_pg, w_pp, g_fin):
    n = x1.shape[0]
    tm = MOE_WINDOW
    row = lambda i, s: (i, 0)
    step = lambda i, s: (s, 0, 0)
    big = pl.BlockSpec((tm, D_MODEL), row)
    sw = EXPERTS_PER_STEP * D_EXPERT
    return pl.pallas_call(
        _moe_tail_kernel,
        grid=(n // tm, MOE_STEPS),
        in_specs=[big, pl.BlockSpec((tm, PLE_DIM), row), _resident(g_ffn.shape), _resident(wr_hi.shape),
                  _resident(wr_lo.shape), _resident(b_r.shape),
                  pl.BlockSpec((EXPERTS_PER_STEP, D_MODEL, D_EXPERT), step),
                  pl.BlockSpec((EXPERTS_PER_STEP, D_MODEL, D_EXPERT), step),
                  pl.BlockSpec((None, sw, D_MODEL), step),
                  _resident(g_ple.shape), _resident(w_pg.shape), _resident(w_pp.shape), _resident(g_fin.shape)],
        out_specs=big,
        out_shape=jax.ShapeDtypeStruct((n, D_MODEL), F32),
        scratch_shapes=[pltpu.VMEM((MOE_SORTED_ROWS, D_MODEL), BF16), pltpu.VMEM((MOE_SORTED_ROWS, LANES), F32),
                        pltpu.VMEM((MOE_SORTED_ROWS, D_MODEL), F32), pltpu.VMEM((tm, LANES), F32),
                        pltpu.SMEM((N_GROUPS,), jnp.int32), pltpu.SMEM((N_GROUPS,), jnp.int32)],
        compiler_params=pltpu.CompilerParams(dimension_semantics=("arbitrary", "arbitrary"),
                                             vmem_limit_bytes=VMEM_LIMIT),
        name="moe_tail",
    )(x1, p, g_ffn, wr_hi, wr_lo, b_r, wg, wu, wd, g_ple, w_pg, w_pp, g_fin)


def _pad_tail(buf):
    return jnp.pad(buf, ((0, 0), (SUBLANES - buf.shape[1], 0), (0, 0)))


def _group(x, p, mconv0, sconv0, c0, n0, m0, wts, *, tt, tm):
    bsz, t, _ = x.shape
    xf = x.reshape(bsz * t, D_MODEL)
    pf = p.reshape(bsz * t, PLE_DIM)
    xm, v, o, ifp, sb, cx, ga, gb = _in_proj(xf, wts["g_mix"], wts["w_main"], wts["w_if"], wts["w_rest"], tm)
    m0p = jnp.broadcast_to(jnp.pad(m0, ((0, 0), (0, SUBLANES - M_HEADS)))[:, :, None], (bsz, SUBLANES, LANES))
    hm, u, c1, n1, m1p, mconv1, sconv1 = _mlstm(
        xm, v, cx, ifp, _pad_tail(mconv0), _pad_tail(sconv0), c0, n0, m0p,
        wts["w_mconv"], wts["w_sconv"], wts["w_q"], wts["w_k"], wts["gate_bias"], wts["g_head"],
        bsz=bsz, t=t, tt=tt)
    x1 = _merge(xf, hm, o, ga, gb, sb, u, wts["w_a"], wts["w_sout"], wts["w_o"], tm)
    y = _moe_tail(x1, pf, wts["g_ffn"], wts["wr_hi"], wts["wr_lo"], wts["b_r"], wts["w_gate"], wts["w_up"],
                  wts["w_down"], wts["g_ple"], wts["w_ple_gate"], wts["w_ple_proj"], wts["g_final"])
    return (y.reshape(bsz, t, D_MODEL), c1[None], n1[None], m1p[None, :, :M_HEADS, 0],
            mconv1[None, :, SUBLANES - (M_CONV_W - 1):, :], sconv1[None, :, SUBLANES - (S_CONV_W - 1):, :])


def kernel(x_prompt, x_sample, p_prompt, p_sample, state_mlstm_C, state_mlstm_n, state_mlstm_m, state_mlstm_conv, state_sconv, g_mix, w_in, w_mconv, w_q, w_k, b_i, b_f, g_head, w_a, w_sconv, w_sout, w_o, g_ffn, w_rg, b_rg, w_re, b_re, w_gate, w_up, w_down, g_ple, w_ple_gate, w_ple_proj, g_final):
    assert g_mix.shape[0] == 1, "single-layer trunk"
    d = D_MODEL
    w_in0 = w_in[0]
    n_if = 2 * M_HEADS
    w_router = jnp.pad(jnp.concatenate([w_rg[0], w_re[0]], axis=1), ((0, 0), (0, LANES - N_GROUPS - N_EXPERTS)))
    wr_hi = w_router.astype(BF16)
    wr_lo = (w_router - wr_hi.astype(F32)).astype(BF16)
    sw = EXPERTS_PER_STEP * D_EXPERT
    wts = {
        "g_mix": g_mix,
        "w_main": w_in0[:, :3 * d].astype(BF16),
        "w_if": jnp.pad(w_in0[:, 3 * d:3 * d + n_if], ((0, 0), (0, LANES - n_if))).astype(BF16),
        "w_rest": w_in0[:, 3 * d + n_if:].astype(BF16),
        "w_mconv": w_mconv[0], "w_sconv": w_sconv[0],
        "w_q": w_q[0].astype(BF16), "w_k": w_k[0].astype(BF16),
        "gate_bias": jnp.pad(jnp.concatenate([b_i[0], b_f[0]])[None, :], ((0, 0), (0, LANES - n_if))),
        "g_head": g_head[0].reshape(1, d),
        "w_a": w_a[0].astype(BF16), "w_sout": w_sout[0].astype(BF16), "w_o": w_o[0].astype(BF16),
        "g_ffn": g_ffn, "wr_hi": wr_hi, "wr_lo": wr_lo,
        "b_r": jnp.pad(jnp.concatenate([b_rg[0], b_re[0]])[None, :], ((0, 0), (0, LANES - N_GROUPS - N_EXPERTS))),
        "w_gate": w_gate[0].astype(BF16), "w_up": w_up[0].astype(BF16),
        "w_down": w_down[0].astype(BF16).reshape(MOE_STEPS, sw, d),
        "g_ple": g_ple, "w_ple_gate": w_ple_gate[0].astype(BF16), "w_ple_proj": w_ple_proj[0].astype(BF16),
        "g_final": g_final[None, :],
    }
    bp = x_prompt.shape[0]
    zeros = lambda *s: jnp.zeros(s, F32)
    yp, *st_p = _group(
        x_prompt, p_prompt[0], zeros(bp, M_CONV_W - 1, d), zeros(bp, S_CONV_W - 1, d),
        zeros(bp, M_HEADS, M_HEAD_DIM, M_HEAD_DIM), zeros(bp, M_HEADS, M_HEAD_DIM), zeros(bp, M_HEADS),
        wts, tt=256, tm=512)
    ys, *st_s = _group(
        x_sample, p_sample[0], state_mlstm_conv[0], state_sconv[0],
        state_mlstm_C[0], state_mlstm_n[0], state_mlstm_m[0],
        wts, tt=x_sample.shape[1], tm=512)
    return (yp, ys, *st_p, *st_s)
```

```python
import functools

import jax
import jax.numpy as jnp
from jax import lax
from jax.experimental import pallas as pl
from jax.experimental.pallas import tpu as pltpu

D_MODEL = 1024
M_HEADS = 4
M_HEAD_DIM = 256
M_CONV_W = 4
S_CONV_W = 3
N_GROUPS = 4
EXPERTS_PER_GROUP = 8
N_EXPERTS = N_GROUPS * EXPERTS_PER_GROUP
D_EXPERT = 256
PLE_DIM = 256
RMS_EPS = 1e-6

LANES = 128
SUBLANES = 8
VMEM_LIMIT = 56 * 1024 * 1024

BF16 = jnp.bfloat16
F32 = jnp.float32


def _rms(x, g):
    return x * lax.rsqrt(jnp.mean(x * x, axis=-1, keepdims=True) + RMS_EPS) * g


def _dot(a, b):
    return jnp.dot(a.astype(BF16), b.astype(BF16), preferred_element_type=F32)


def _sigmoid(x):
    return 0.5 * jnp.tanh(0.5 * x) + 0.5


def _silu(x):
    return x * _sigmoid(x)


def _resident(shape):
    nd = len(shape)
    return pl.BlockSpec(shape, lambda *_: (0,) * nd, pipeline_mode=pl.Buffered(1))


def _in_proj_kernel(x_ref, g_ref, w_main_ref, w_if_ref, w_rest_ref,
                    xm_ref, v_ref, o_ref, if_ref, sb_ref, cx_ref, ga_ref, gb_ref):
    h = _rms(x_ref[...], g_ref[...]).astype(BF16)
    d = D_MODEL
    xm_ref[...] = jnp.dot(h, w_main_ref[:, 0:d], preferred_element_type=F32)
    v_ref[...] = jnp.dot(h, w_main_ref[:, d:2 * d], preferred_element_type=F32)
    o_ref[...] = jnp.dot(h, w_main_ref[:, 2 * d:3 * d], preferred_element_type=F32)
    if_ref[...] = jnp.dot(h, w_if_ref[...], preferred_element_type=F32)
    sb_ref[...] = jnp.dot(h, w_rest_ref[:, 0:d], preferred_element_type=F32)
    s_c = jnp.dot(h, w_rest_ref[:, d:2 * d], preferred_element_type=F32)
    s_x = jnp.dot(h, w_rest_ref[:, 2 * d:3 * d], preferred_element_type=F32)
    cx_ref[...] = s_c * s_x
    ga_ref[...] = jnp.dot(h, w_rest_ref[:, 3 * d:4 * d], preferred_element_type=F32)
    gb_ref[...] = jnp.dot(h, w_rest_ref[:, 4 * d:5 * d], preferred_element_type=F32)


def _in_proj(x, g_mix, w_main, w_if, w_rest, tm):
    n = x.shape[0]
    row = lambda i: (i, 0)
    big = pl.BlockSpec((tm, D_MODEL), row)
    outs = [jax.ShapeDtypeStruct((n, D_MODEL), F32)] * 3 + [jax.ShapeDtypeStruct((n, LANES), F32)] \
        + [jax.ShapeDtypeStruct((n, D_MODEL), F32)] * 4
    return pl.pallas_call(
        _in_proj_kernel,
        grid=(n // tm,),
        in_specs=[big, _resident(g_mix.shape), _resident(w_main.shape), _resident(w_if.shape),
                  _resident(w_rest.shape)],
        out_specs=[big, big, big, pl.BlockSpec((tm, LANES), row), big, big, big, big],
        out_shape=outs,
        compiler_params=pltpu.CompilerParams(dimension_semantics=("arbitrary",),
                                             vmem_limit_bytes=VMEM_LIMIT),
        name="in_proj",
    )(x, g_mix, w_main, w_if, w_rest)


def _scan(x, axis, op, identity):
    idx = lax.broadcasted_iota(jnp.int32, x.shape, axis)
    shift = 1
    while shift < x.shape[axis]:
        x = op(x, jnp.where(idx >= shift, pltpu.roll(x, shift, axis=axis), identity))
        shift *= 2
    return x


def _cumsum_rows(x):
    return _scan(x, 0, jnp.add, 0.0)


def _scan_rows(x, op, identity):
    sub = lax.broadcasted_iota(jnp.int32, (SUBLANES, x.shape[1]), 0)
    blocks, carry = [], None
    for i in range(x.shape[0] // SUBLANES):
        blk = x[i * SUBLANES:(i + 1) * SUBLANES, :]
        for shift in (1, 2, 4):
            blk = op(blk, jnp.where(sub >= shift, pltpu.roll(blk, shift, axis=0), identity))
        if carry is not None:
            blk = op(blk, carry)
        carry = jnp.broadcast_to(blk[SUBLANES - 1:SUBLANES, :], blk.shape)
        blocks.append(blk)
    return jnp.concatenate(blocks, axis=0)


def _log_sigmoid(x):
    return jnp.minimum(x, 0.0) - jnp.log1p(jnp.exp(-jnp.abs(x)))


def _seq_init(mconv0_ref, sconv0_ref, c0_ref, n0_ref, m0_ref, xp_sc, cp_sc, c1_ref, n1_ref, m1_ref):
    @pl.when(pl.program_id(1) == 0)
    def _():
        xp_sc[0:SUBLANES, :] = mconv0_ref[...]
        cp_sc[0:SUBLANES, :] = sconv0_ref[...]
        c1_ref[...] = c0_ref[...]
        n1_ref[...] = n0_ref[...]
        m1_ref[...] = m0_ref[...]


def _causal_conv(src_sc, w_ref, width, tt):
    acc = None
    for j in range(width):
        term = src_sc[pl.ds(SUBLANES - (width - 1) + j, tt), :] * w_ref[j:j + 1, :]
        acc = term if acc is None else acc + term
    return acc


def _carry_tail(src_sc, tail_ref, tt):
    tail = src_sc[tt:tt + SUBLANES, :]
    src_sc[0:SUBLANES, :] = tail
    tail_ref[...] = tail


def _head_slices():
    return [slice(h * M_HEAD_DIM, (h + 1) * M_HEAD_DIM) for h in range(M_HEADS)]


def _mlstm_tile(xq, v, ifp, c1_ref, n1_ref, m1_ref, wq_ref, wk_ref, bias_ref, ghead_ref, tt, between=()):
    g = ifp + bias_ref[...]
    nh = M_HEADS
    b_c = _scan_rows(_log_sigmoid(g), jnp.add, 0.0)
    a_c = g - pltpu.roll(b_c, LANES - nh, axis=1)
    amax_c = _scan_rows(a_c, jnp.maximum, -jnp.inf)
    a_t = a_c.T

    r_idx = lax.broadcasted_iota(jnp.int32, (tt, tt), 0)
    c_idx = lax.broadcasted_iota(jnp.int32, (tt, tt), 1)
    causal = c_idx <= r_idx

    heads = range(M_HEADS)
    hs = _head_slices()
    q = [_dot(xq[:, hs[h]], wq_ref[h]) * (M_HEAD_DIM ** -0.5) for h in heads]
    k = [_dot(xq[:, hs[h]], wk_ref[h]) for h in heads]
    c_prev = [c1_ref[h] for h in heads]
    n_prev = [n1_ref[h:h + 1, :] for h in heads]
    m_prev = [m1_ref[h:h + 1, 0:1] for h in heads]
    b_col = [b_c[:, nh + h:nh + h + 1] for h in heads]
    a_col = [a_c[:, h:h + 1] for h in heads]
    a_row = [a_t[h:h + 1, :] for h in heads]
    b_last = [b_c[tt - 1:tt, nh + h:nh + h + 1] for h in heads]
    mm_col = [jnp.maximum(m_prev[h], amax_c[:, h:h + 1]) for h in heads]
    mm_last = [jnp.maximum(m_prev[h], amax_c[tt - 1:tt, h:h + 1]) for h in heads]

    extra = []
    pending = list(between)

    def run_one():
        if pending:
            extra.append(pending.pop(0)())

    w_inter = [jnp.exp(m_prev[h] - mm_col[h]) for h in heads]
    qk = [lax.dot_general(q[h].astype(BF16), k[h].astype(BF16), (((1,), (1,)), ((), ())),
                          preferred_element_type=F32) for h in heads]
    run_one()
    s = [jnp.exp(jnp.where(causal, a_row[h] - mm_col[h], -jnp.inf)) * qk[h] for h in heads]
    run_one()
    num = [w_inter[h] * _dot(q[h], c_prev[h]) + _dot(s[h], v[h]) for h in heads]
    den = [w_inter[h] * jnp.sum(q[h] * n_prev[h], axis=-1, keepdims=True)
           + jnp.sum(s[h], axis=-1, keepdims=True) for h in heads]
    run_one()
    out = [num[h] * (1.0 / jnp.maximum(jnp.abs(den[h]), jnp.exp(-(b_col[h] + mm_col[h])))) for h in heads]
    out = [out[h] * lax.rsqrt(jnp.mean(out[h] * out[h], axis=-1, keepdims=True) + RMS_EPS) * ghead_ref[:, hs[h]]
           for h in heads]
    while pending:
        run_one()

    decay = [jnp.exp(m_prev[h] - mm_last[h]) for h in heads]
    wk = [jnp.exp(a_col[h] - mm_last[h]) * k[h] for h in heads]
    for h in heads:
        c1_ref[h] = decay[h] * c_prev[h] + lax.dot_general(
            wk[h].astype(BF16), v[h].astype(BF16), (((0,), (0,)), ((), ())), preferred_element_type=F32)
        n1_ref[h:h + 1, :] = decay[h] * n_prev[h] + jnp.sum(wk[h], axis=0, keepdims=True)
        m1_ref[h:h + 1, :] = jnp.broadcast_to(b_last[h] + mm_last[h], (1, LANES))
    return out, extra


def _mlstm_kernel(xm_ref, v_ref, cx_ref, if_ref, mconv0_ref, sconv0_ref, c0_ref, n0_ref, m0_ref,
                  w_mconv_ref, w_sconv_ref, wq_ref, wk_ref, bias_ref, ghead_ref,
                  hm_ref, u_ref, c1_ref, n1_ref, m1_ref, mconv1_ref, sconv1_ref,
                  xp_sc, cp_sc, *, tt):
    _seq_init(mconv0_ref, sconv0_ref, c0_ref, n0_ref, m0_ref, xp_sc, cp_sc, c1_ref, n1_ref, m1_ref)
    xp_sc[SUBLANES:SUBLANES + tt, :] = xm_ref[...]
    cp_sc[SUBLANES:SUBLANES + tt, :] = cx_ref[...]
    xq = _silu(_causal_conv(xp_sc, w_mconv_ref, M_CONV_W, tt))
    u_ref[...] = _causal_conv(cp_sc, w_sconv_ref, S_CONV_W, tt)
    _carry_tail(xp_sc, mconv1_ref, tt)
    _carry_tail(cp_sc, sconv1_ref, tt)
    hs = _head_slices()
    out, _ = _mlstm_tile(xq, [v_ref[:, sl] for sl in hs], if_ref[...], c1_ref, n1_ref, m1_ref,
                         wq_ref, wk_ref, bias_ref, ghead_ref, tt)
    for h, sl in enumerate(hs):
        hm_ref[:, sl] = out[h]


def _mlstm(xm, v, cx, ifp, mconv0, sconv0, c0, n0, m0, w_mconv, w_sconv, wq, wk, bias, ghead, *, bsz, t, tt):
    nt = t // tt
    n = bsz * t
    tok = lambda b, i: (b * nt + i, 0)
    seq3 = lambda b, i: (b, 0, 0)
    seq4 = lambda b, i: (b, 0, 0, 0)
    big = pl.BlockSpec((tt, D_MODEL), tok)
    tail = pl.BlockSpec((None, SUBLANES, D_MODEL), seq3)
    c_spec = pl.BlockSpec((None, M_HEADS, M_HEAD_DIM, M_HEAD_DIM), seq4)
    n_spec = pl.BlockSpec((None, M_HEADS, M_HEAD_DIM), seq3)
    m_spec = pl.BlockSpec((None, SUBLANES, LANES), seq3)
    return pl.pallas_call(
        functools.partial(_mlstm_kernel, tt=tt),
        grid=(bsz, nt),
        in_specs=[big, big, big, pl.BlockSpec((tt, LANES), tok), tail, tail, c_spec, n_spec, m_spec,
                  _resident(w_mconv.shape), _resident(w_sconv.shape), _resident(wq.shape),
                  _resident(wk.shape), _resident(bias.shape), _resident(ghead.shape)],
        out_specs=[big, big, c_spec, n_spec, m_spec, tail, tail],
        out_shape=[jax.ShapeDtypeStruct((n, D_MODEL), F32), jax.ShapeDtypeStruct((n, D_MODEL), F32),
                   jax.ShapeDtypeStruct(c0.shape, F32), jax.ShapeDtypeStruct(n0.shape, F32),
                   jax.ShapeDtypeStruct(m0.shape, F32),
                   jax.ShapeDtypeStruct(mconv0.shape, F32), jax.ShapeDtypeStruct(sconv0.shape, F32)],
        scratch_shapes=[pltpu.VMEM((tt + SUBLANES, D_MODEL), F32), pltpu.VMEM((tt + SUBLANES, D_MODEL), F32)],
        compiler_params=pltpu.CompilerParams(dimension_semantics=("arbitrary", "arbitrary"),
                                             vmem_limit_bytes=VMEM_LIMIT),
        name="mlstm",
    )(xm, v, cx, ifp, mconv0, sconv0, c0, n0, m0, w_mconv, w_sconv, wq, wk, bias, ghead)


def _merge_kernel(x_ref, hm_ref, o_ref, ga_ref, gb_ref, sb_ref, u_ref, wa_ref, wsout_ref, wo_ref, x1_ref):
    y_a = _dot(_sigmoid(o_ref[...]) * hm_ref[...], wa_ref[...])
    y_b = _dot(sb_ref[...] * u_ref[...], wsout_ref[...])
    merged = _sigmoid(ga_ref[...]) * y_a + _sigmoid(gb_ref[...]) * y_b
    x1_ref[...] = x_ref[...] + _dot(merged, wo_ref[...])


def _merge(x, hm, o, ga, gb, sb, u, w_a, w_sout, w_o, tm):
    n = x.shape[0]
    big = pl.BlockSpec((tm, D_MODEL), lambda i: (i, 0))
    return pl.pallas_call(
        _merge_kernel,
        grid=(n // tm,),
        in_specs=[big] * 7 + [_resident(w_a.shape), _resident(w_sout.shape), _resident(w_o.shape)],
        out_specs=big,
        out_shape=jax.ShapeDtypeStruct((n, D_MODEL), F32),
        compiler_params=pltpu.CompilerParams(dimension_semantics=("arbitrary",),
                                             vmem_limit_bytes=VMEM_LIMIT),
        name="merge",
    )(x, hm, o, ga, gb, sb, u, w_a, w_sout, w_o)


def _mixer_kernel(x_ref, mconv0_ref, sconv0_ref, c0_ref, n0_ref, m0_ref,
                  g_ref, w_main_ref, w_if_ref, w_rest_ref, w_mconv_ref, w_sconv_ref, wq_ref, wk_ref,
                  bias_ref, ghead_ref, wa_ref, wsout_ref, wo_ref,
                  x1_ref, c1_ref, n1_ref, m1_ref, mconv1_ref, sconv1_ref,
                  xp_sc, cp_sc, hm_sc, *, tt):
    _seq_init(mconv0_ref, sconv0_ref, c0_ref, n0_ref, m0_ref, xp_sc, cp_sc, c1_ref, n1_ref, m1_ref)
    d = D_MODEL
    h = _rms(x_ref[...], g_ref[...]).astype(BF16)

    def proj(w_ref, j):
        return jnp.dot(h, w_ref[:, j * d:(j + 1) * d], preferred_element_type=F32)

    xp_sc[SUBLANES:SUBLANES + tt, :] = proj(w_main_ref, 0)
    ifp = jnp.dot(h, w_if_ref[...], preferred_element_type=F32)
    cp_sc[SUBLANES:SUBLANES + tt, :] = proj(w_rest_ref, 1) * proj(w_rest_ref, 2)
    xq = _silu(_causal_conv(xp_sc, w_mconv_ref, M_CONV_W, tt))
    u = _causal_conv(cp_sc, w_sconv_ref, S_CONV_W, tt)
    _carry_tail(xp_sc, mconv1_ref, tt)
    _carry_tail(cp_sc, sconv1_ref, tt)
    v = proj(w_main_ref, 1)
    hs = _head_slices()
    out, (y_b, sig_o, sig_ga, sig_gb) = _mlstm_tile(
        xq, [v[:, sl] for sl in hs], ifp, c1_ref, n1_ref, m1_ref, wq_ref, wk_ref, bias_ref, ghead_ref, tt,
        between=(lambda: _dot(proj(w_rest_ref, 0) * u, wsout_ref[...]),
                 lambda: _sigmoid(proj(w_main_ref, 2)),
                 lambda: _sigmoid(proj(w_rest_ref, 3)),
                 lambda: _sigmoid(proj(w_rest_ref, 4))))
    for hd, sl in enumerate(hs):
        hm_sc[:, sl] = (sig_o[:, sl] * out[hd]).astype(BF16)
    y_a = jnp.dot(hm_sc[...], wa_ref[...], preferred_element_type=F32)
    merged = sig_ga * y_a + sig_gb * y_b
    x1_ref[...] = x_ref[...] + _dot(merged, wo_ref[...])


def _mixer(x, mconv0, sconv0, c0, n0, m0, wts, *, bsz, t, tt):
    nt = t // tt
    tok = lambda b, i: (b * nt + i, 0)
    seq3 = lambda b, i: (b, 0, 0)
    seq4 = lambda b, i: (b, 0, 0, 0)
    big = pl.BlockSpec((tt, D_MODEL), tok)
    tail = pl.BlockSpec((None, SUBLANES, D_MODEL), seq3)
    c_spec = pl.BlockSpec((None, M_HEADS, M_HEAD_DIM, M_HEAD_DIM), seq4)
    n_spec = pl.BlockSpec((None, M_HEADS, M_HEAD_DIM), seq3)
    m_spec = pl.BlockSpec((None, SUBLANES, LANES), seq3)
    names = ("g_mix", "w_main", "w_if", "w_rest", "w_mconv", "w_sconv", "w_q", "w_k", "gate_bias", "g_head",
             "w_a", "w_sout", "w_o")
    weights = [wts[k] for k in names]
    return pl.pallas_call(
        functools.partial(_mixer_kernel, tt=tt),
        grid=(bsz, nt),
        in_specs=[big, tail, tail, c_spec, n_spec, m_spec] + [_resident(w.shape) for w in weights],
        out_specs=[big, c_spec, n_spec, m_spec, tail, tail],
        out_shape=[jax.ShapeDtypeStruct(x.shape, F32), jax.ShapeDtypeStruct(c0.shape, F32),
                   jax.ShapeDtypeStruct(n0.shape, F32), jax.ShapeDtypeStruct(m0.shape, F32),
                   jax.ShapeDtypeStruct(mconv0.shape, F32), jax.ShapeDtypeStruct(sconv0.shape, F32)],
        scratch_shapes=[pltpu.VMEM((tt + SUBLANES, D_MODEL), F32), pltpu.VMEM((tt + SUBLANES, D_MODEL), F32),
                        pltpu.VMEM((tt, D_MODEL), BF16)],
        compiler_params=pltpu.CompilerParams(dimension_semantics=("arbitrary", "arbitrary"),
                                             vmem_limit_bytes=VMEM_LIMIT),
        name="mixer",
    )(x, mconv0, sconv0, c0, n0, m0, *weights)


MOE_WINDOW = 1024
MOE_ROW_TILE = 128
MOE_ALIGN = 16
EXPERTS_PER_STEP = 4
MOE_STEPS = N_EXPERTS // EXPERTS_PER_STEP
MOE_CHUNK = 256
MOE_SORTED_ROWS = -(-(MOE_WINDOW + N_GROUPS * (MOE_ALIGN - 1) + MOE_ROW_TILE) // MOE_CHUNK) * MOE_CHUNK


def _split_bf16(x):
    hi = x.astype(BF16)
    lo = (x - hi.astype(F32)).astype(BF16)
    return hi, lo


def _split3_bf16(x):
    hi = x.astype(BF16)
    r1 = x - hi.astype(F32)
    mid = r1.astype(BF16)
    lo = (r1 - mid.astype(F32)).astype(BF16)
    return hi, mid, lo


def _route(logits):
    lane = lax.broadcasted_iota(jnp.int32, logits.shape, 1)
    neg = -jnp.inf
    big = jnp.int32(LANES)
    is_grp = lane < N_GROUPS
    g_max = jnp.max(jnp.where(is_grp, logits, neg), axis=-1, keepdims=True)
    g_sel = jnp.min(jnp.where(is_grp & (logits == g_max), lane, big), axis=-1, keepdims=True)
    p_grp = 1.0 / jnp.sum(jnp.where(is_grp, jnp.exp(logits - g_max), 0.0), axis=-1, keepdims=True)
    lo = N_GROUPS + g_sel * EXPERTS_PER_GROUP
    in_grp = (lane >= lo) & (lane < lo + EXPERTS_PER_GROUP)
    v1 = jnp.max(jnp.where(in_grp, logits, neg), axis=-1, keepdims=True)
    e1 = jnp.min(jnp.where(in_grp & (logits == v1), lane, big), axis=-1, keepdims=True)
    rest = in_grp & (lane != e1)
    v2 = jnp.max(jnp.where(rest, logits, neg), axis=-1, keepdims=True)
    e2 = jnp.min(jnp.where(rest & (logits == v2), lane, big), axis=-1, keepdims=True)
    z = jnp.exp(v2 - v1)
    w1 = p_grp / (1.0 + z)
    w2 = p_grp * z / (1.0 + z)
    return g_sel, jnp.where(lane == e1, w1, jnp.where(lane == e2, w2, 0.0))


def _moe_tail_kernel(x1_ref, p_ref, gffn_ref, wr_hi_ref, wr_lo_ref, br_ref, wg_ref, wu_ref, wd_ref,
                     gple_ref, wpg_ref, wpp_ref, gfin_ref, y_ref,
                     xs_sc, cws_sc, osort_sc, pos_sc, off_sm, nt_sm):
    s = pl.program_id(1)
    w = MOE_WINDOW

    @pl.when(s == 0)
    def _():
        hn = _rms(x1_ref[...], gffn_ref[...])
        hn_hi, hn_lo = _split_bf16(hn)
        logits = (jnp.dot(hn_hi, wr_hi_ref[...], preferred_element_type=F32)
                  + jnp.dot(hn_hi, wr_lo_ref[...], preferred_element_type=F32)
                  + jnp.dot(hn_lo, wr_hi_ref[...], preferred_element_type=F32)) + br_ref[...]
        g_sel, cw = _route(logits)
        lane = lax.broadcasted_iota(jnp.int32, (w, LANES), 1)
        onehot = jnp.where(lane == g_sel, 1.0, 0.0)
        cum = _cumsum_rows(onehot)
        cnt = cum[w - 1:w, :].astype(jnp.int32)
        cnt_pad = ((cnt + (MOE_ALIGN - 1)) // MOE_ALIGN) * MOE_ALIGN
        lane1 = lax.broadcasted_iota(jnp.int32, (1, LANES), 1)
        off = jnp.zeros((1, LANES), jnp.int32)
        for gi in range(N_GROUPS - 1):
            off = off + jnp.where(lane1 > gi, cnt_pad[:, gi:gi + 1], 0)
        n_tiles = (cnt + (MOE_ROW_TILE - 1)) // MOE_ROW_TILE
        for gi in range(N_GROUPS):
            off_sm[gi] = off[0, gi]
            nt_sm[gi] = n_tiles[0, gi]
        pos = jnp.sum(onehot * (off.astype(F32) + cum - 1.0), axis=-1, keepdims=True)
        pos_b = jnp.broadcast_to(pos, (w, LANES))
        pos_sc[...] = pos_b
        pos_row = pos_b.T[0:1, :].astype(jnp.int32)
        cw_hi, cw_mid, cw_lo = _split3_bf16(cw)
        for c in range(MOE_SORTED_ROWS // MOE_CHUNK):
            rows = lax.broadcasted_iota(jnp.int32, (MOE_CHUNK, w), 0) + c * MOE_CHUNK
            sel = jnp.where(rows == pos_row, 1.0, 0.0).astype(BF16)
            sl = slice(c * MOE_CHUNK, (c + 1) * MOE_CHUNK)
            xs_sc[sl, :] = jnp.dot(sel, hn_hi, preferred_element_type=F32).astype(BF16)
            cws_sc[sl, :] = (jnp.dot(sel, cw_hi, preferred_element_type=F32)
                             + jnp.dot(sel, cw_mid, preferred_element_type=F32)
                             + jnp.dot(sel, cw_lo, preferred_element_type=F32))
        osort_sc[...] = jnp.zeros(osort_sc.shape, F32)

    grp = s // (EXPERTS_PER_GROUP // EXPERTS_PER_STEP)
    row0 = off_sm[grp]
    lane_t = lax.broadcasted_iota(jnp.int32, (MOE_ROW_TILE, LANES), 1)
    first_lane = N_GROUPS + s * EXPERTS_PER_STEP

    def tile_body(i, carry):
        r0 = pl.multiple_of(row0 + i * MOE_ROW_TILE, MOE_ALIGN)
        xt = xs_sc[pl.ds(r0, MOE_ROW_TILE), :]
        cwt = cws_sc[pl.ds(r0, MOE_ROW_TILE), :]
        parts = []
        for e in range(EXPERTS_PER_STEP):
            hg = jnp.dot(xt, wg_ref[e], preferred_element_type=F32)
            hu = jnp.dot(xt, wu_ref[e], preferred_element_type=F32)
            col = jnp.sum(jnp.where(lane_t == first_lane + e, cwt, 0.0), axis=-1, keepdims=True)
            parts.append((_silu(hg) * hu * col).astype(BF16))
        he = jnp.concatenate(parts, axis=-1)
        out = jnp.dot(he, wd_ref[...], preferred_element_type=F32)
        osort_sc[pl.ds(r0, MOE_ROW_TILE), :] = osort_sc[pl.ds(r0, MOE_ROW_TILE), :] + out
        return carry

    lax.fori_loop(0, nt_sm[grp], tile_body, 0)

    @pl.when(s == MOE_STEPS - 1)
    def _():
        osort = osort_sc[...].astype(BF16)
        for c in range(w // MOE_CHUNK):
            sl = slice(c * MOE_CHUNK, (c + 1) * MOE_CHUNK)
            pos_col = pos_sc[sl, 0:1].astype(jnp.int32)
            cols = lax.broadcasted_iota(jnp.int32, (MOE_CHUNK, MOE_SORTED_ROWS), 1)
            sel = jnp.where(cols == pos_col, 1.0, 0.0).astype(BF16)
            x2 = x1_ref[sl, :] + jnp.dot(sel, osort, preferred_element_type=F32)
            gate = _sigmoid(_dot(_rms(x2, gple_ref[...]), wpg_ref[...]))
            x3 = x2 + gate * _dot(p_ref[sl, :], wpp_ref[...])
            y_ref[sl, :] = _rms(x3, gfin_ref[...])


def _moe_tail(x1, p, g_ffn, wr_hi, wr_lo, b_r, wg, wu, wd, g_ple, w_pg, w_pp, g_fin):
    n = x1.shape[0]
    tm = MOE_WINDOW
    row = lambda i, s: (i, 0)
    step = lambda i, s: (s, 0, 0)
    big = pl.BlockSpec((tm, D_MODEL), row)
    sw = EXPERTS_PER_STEP * D_EXPERT
    return pl.pallas_call(
        _moe_tail_kernel,
        grid=(n // tm, MOE_STEPS),
        in_specs=[big, pl.BlockSpec((tm, PLE_DIM), row), _resident(g_ffn.shape), _resident(wr_hi.shape),
                  _resident(wr_lo.shape), _resident(b_r.shape),
                  pl.BlockSpec((EXPERTS_PER_STEP, D_MODEL, D_EXPERT), step),
                  pl.BlockSpec((EXPERTS_PER_STEP, D_MODEL, D_EXPERT), step),
                  pl.BlockSpec((None, sw, D_MODEL), step),
                  _resident(g_ple.shape), _resident(w_pg.shape), _resident(w_pp.shape), _resident(g_fin.shape)],
        out_specs=big,
        out_shape=jax.ShapeDtypeStruct((n, D_MODEL), F32),
        scratch_shapes=[pltpu.VMEM((MOE_SORTED_ROWS, D_MODEL), BF16), pltpu.VMEM((MOE_SORTED_ROWS, LANES), F32),
                        pltpu.VMEM((MOE_SORTED_ROWS, D_MODEL), F32), pltpu.VMEM((tm, LANES), F32),
                        pltpu.SMEM((N_GROUPS,), jnp.int32), pltpu.SMEM((N_GROUPS,), jnp.int32)],
        compiler_params=pltpu.CompilerParams(dimension_semantics=("arbitrary", "arbitrary"),
                                             vmem_limit_bytes=VMEM_LIMIT),
        name="moe_tail",
    )(x1, p, g_ffn, wr_hi, wr_lo, b_r, wg, wu, wd, g_ple, w_pg, w_pp, g_fin)


def _pad_tail(buf):
    return jnp.pad(buf, ((0, 0), (SUBLANES - buf.shape[1], 0), (0, 0)))


def _group(x, p, mconv0, sconv0, c0, n0, m0, wts, *, tt, tm, fused):
    bsz, t, _ = x.shape
    xf = x.reshape(bsz * t, D_MODEL)
    pf = p.reshape(bsz * t, PLE_DIM)
    m0p = jnp.broadcast_to(jnp.pad(m0, ((0, 0), (0, SUBLANES - M_HEADS)))[:, :, None], (bsz, SUBLANES, LANES))
    if fused:
        x1, c1, n1, m1p, mconv1, sconv1 = _mixer(xf, _pad_tail(mconv0), _pad_tail(sconv0), c0, n0, m0p, wts,
                                                 bsz=bsz, t=t, tt=tt)
    else:
        xm, v, o, ifp, sb, cx, ga, gb = _in_proj(xf, wts["g_mix"], wts["w_main"], wts["w_if"], wts["w_rest"], tm)
        hm, u, c1, n1, m1p, mconv1, sconv1 = _mlstm(
            xm, v, cx, ifp, _pad_tail(mconv0), _pad_tail(sconv0), c0, n0, m0p,
            wts["w_mconv"], wts["w_sconv"], wts["w_q"], wts["w_k"], wts["gate_bias"], wts["g_head"],
            bsz=bsz, t=t, tt=tt)
        x1 = _merge(xf, hm, o, ga, gb, sb, u, wts["w_a"], wts["w_sout"], wts["w_o"], tm)
    y = _moe_tail(x1, pf, wts["g_ffn"], wts["wr_hi"], wts["wr_lo"], wts["b_r"], wts["w_gate"], wts["w_up"],
                  wts["w_down"], wts["g_ple"], wts["w_ple_gate"], wts["w_ple_proj"], wts["g_final"])
    return (y.reshape(bsz, t, D_MODEL), c1[None], n1[None], m1p[None, :, :M_HEADS, 0],
            mconv1[None, :, SUBLANES - (M_CONV_W - 1):, :], sconv1[None, :, SUBLANES - (S_CONV_W - 1):, :])


def kernel(x_prompt, x_sample, p_prompt, p_sample, state_mlstm_C, state_mlstm_n, state_mlstm_m, state_mlstm_conv, state_sconv, g_mix, w_in, w_mconv, w_q, w_k, b_i, b_f, g_head, w_a, w_sconv, w_sout, w_o, g_ffn, w_rg, b_rg, w_re, b_re, w_gate, w_up, w_down, g_ple, w_ple_gate, w_ple_proj, g_final):
    assert g_mix.shape[0] == 1, "single-layer trunk"
    d = D_MODEL
    w_in0 = w_in[0]
    n_if = 2 * M_HEADS
    w_router = jnp.pad(jnp.concatenate([w_rg[0], w_re[0]], axis=1), ((0, 0), (0, LANES - N_GROUPS - N_EXPERTS)))
    wr_hi = w_router.astype(BF16)
    wr_lo = (w_router - wr_hi.astype(F32)).astype(BF16)
    sw = EXPERTS_PER_STEP * D_EXPERT
    wts = {
        "g_mix": g_mix,
        "w_main": w_in0[:, :3 * d].astype(BF16),
        "w_if": jnp.pad(w_in0[:, 3 * d:3 * d + n_if], ((0, 0), (0, LANES - n_if))).astype(BF16),
        "w_rest": w_in0[:, 3 * d + n_if:].astype(BF16),
        "w_mconv": w_mconv[0], "w_sconv": w_sconv[0],
        "w_q": w_q[0].astype(BF16), "w_k": w_k[0].astype(BF16),
        "gate_bias": jnp.pad(jnp.concatenate([b_i[0], b_f[0]])[None, :], ((0, 0), (0, LANES - n_if))),
        "g_head": g_head[0].reshape(1, d),
        "w_a": w_a[0].astype(BF16), "w_sout": w_sout[0].astype(BF16), "w_o": w_o[0].astype(BF16),
        "g_ffn": g_ffn, "wr_hi": wr_hi, "wr_lo": wr_lo,
        "b_r": jnp.pad(jnp.concatenate([b_rg[0], b_re[0]])[None, :], ((0, 0), (0, LANES - N_GROUPS - N_EXPERTS))),
        "w_gate": w_gate[0].astype(BF16), "w_up": w_up[0].astype(BF16),
        "w_down": w_down[0].astype(BF16).reshape(MOE_STEPS, sw, d),
        "g_ple": g_ple, "w_ple_gate": w_ple_gate[0].astype(BF16), "w_ple_proj": w_ple_proj[0].astype(BF16),
        "g_final": g_final[None, :],
    }
    bp = x_prompt.shape[0]
    zeros = lambda *s: jnp.zeros(s, F32)
    yp, *st_p = _group(
        x_prompt, p_prompt[0], zeros(bp, M_CONV_W - 1, d), zeros(bp, S_CONV_W - 1, d),
        zeros(bp, M_HEADS, M_HEAD_DIM, M_HEAD_DIM), zeros(bp, M_HEADS, M_HEAD_DIM), zeros(bp, M_HEADS),
        wts, tt=256, tm=512, fused=True)
    ys, *st_s = _group(
        x_sample, p_sample[0], state_mlstm_conv[0], state_sconv[0],
        state_mlstm_C[0], state_mlstm_n[0], state_mlstm_m[0],
        wts, tt=x_sample.shape[1], tm=512, fused=False)
    return (yp, ys, *st_p, *st_s)
```

```python
import functools

import jax
import jax.numpy as jnp
from jax import lax
from jax.experimental import pallas as pl
from jax.experimental.pallas import tpu as pltpu

D_MODEL = 1024
M_HEADS = 4
M_HEAD_DIM = 256
M_CONV_W = 4
S_CONV_W = 3
N_GROUPS = 4
EXPERTS_PER_GROUP = 8
N_EXPERTS = N_GROUPS * EXPERTS_PER_GROUP
D_EXPERT = 256
PLE_DIM = 256
RMS_EPS = 1e-6

LANES = 128
SUBLANES = 8
VMEM_LIMIT = 56 * 1024 * 1024

BF16 = jnp.bfloat16
F32 = jnp.float32


def _rms(x, g):
    return x * lax.rsqrt(jnp.mean(x * x, axis=-1, keepdims=True) + RMS_EPS) * g


def _dot(a, b):
    return jnp.dot(a.astype(BF16), b.astype(BF16), preferred_element_type=F32)


def _sigmoid(x):
    return 0.5 * jnp.tanh(0.5 * x) + 0.5


def _silu(x):
    return x * _sigmoid(x)


def _resident(shape):
    nd = len(shape)
    return pl.BlockSpec(shape, lambda *_: (0,) * nd, pipeline_mode=pl.Buffered(1))


def _in_proj_kernel(x_ref, g_ref, w_main_ref, w_if_ref, w_rest_ref,
                    xm_ref, v_ref, o_ref, if_ref, sb_ref, cx_ref, ga_ref, gb_ref):
    h = _rms(x_ref[...], g_ref[...]).astype(BF16)
    d = D_MODEL
    xm_ref[...] = jnp.dot(h, w_main_ref[:, 0:d], preferred_element_type=F32)
    v_ref[...] = jnp.dot(h, w_main_ref[:, d:2 * d], preferred_element_type=F32)
    o_ref[...] = jnp.dot(h, w_main_ref[:, 2 * d:3 * d], preferred_element_type=F32)
    if_ref[...] = jnp.dot(h, w_if_ref[...], preferred_element_type=F32)
    sb_ref[...] = jnp.dot(h, w_rest_ref[:, 0:d], preferred_element_type=F32)
    s_c = jnp.dot(h, w_rest_ref[:, d:2 * d], preferred_element_type=F32)
    s_x = jnp.dot(h, w_rest_ref[:, 2 * d:3 * d], preferred_element_type=F32)
    cx_ref[...] = s_c * s_x
    ga_ref[...] = jnp.dot(h, w_rest_ref[:, 3 * d:4 * d], preferred_element_type=F32)
    gb_ref[...] = jnp.dot(h, w_rest_ref[:, 4 * d:5 * d], preferred_element_type=F32)


def _in_proj(x, g_mix, w_main, w_if, w_rest, tm):
    n = x.shape[0]
    row = lambda i: (i, 0)
    big = pl.BlockSpec((tm, D_MODEL), row)
    outs = [jax.ShapeDtypeStruct((n, D_MODEL), F32)] * 3 + [jax.ShapeDtypeStruct((n, LANES), F32)] \
        + [jax.ShapeDtypeStruct((n, D_MODEL), F32)] * 4
    return pl.pallas_call(
        _in_proj_kernel,
        grid=(n // tm,),
        in_specs=[big, _resident(g_mix.shape), _resident(w_main.shape), _resident(w_if.shape),
                  _resident(w_rest.shape)],
        out_specs=[big, big, big, pl.BlockSpec((tm, LANES), row), big, big, big, big],
        out_shape=outs,
        compiler_params=pltpu.CompilerParams(dimension_semantics=("arbitrary",),
                                             vmem_limit_bytes=VMEM_LIMIT),
        name="in_proj",
    )(x, g_mix, w_main, w_if, w_rest)


def _scan_rows(x, op, identity):
    sub = lax.broadcasted_iota(jnp.int32, (SUBLANES, x.shape[1]), 0)
    blocks, carry = [], None
    for i in range(x.shape[0] // SUBLANES):
        blk = x[i * SUBLANES:(i + 1) * SUBLANES, :]
        for shift in (1, 2, 4):
            blk = op(blk, jnp.where(sub >= shift, pltpu.roll(blk, shift, axis=0), identity))
        if carry is not None:
            blk = op(blk, carry)
        carry = jnp.broadcast_to(blk[SUBLANES - 1:SUBLANES, :], blk.shape)
        blocks.append(blk)
    return jnp.concatenate(blocks, axis=0)


def _log_sigmoid(x):
    return jnp.minimum(x, 0.0) - jnp.log1p(jnp.exp(-jnp.abs(x)))


def _seq_init(mconv0_ref, sconv0_ref, c0_ref, n0_ref, m0_ref, xp_sc, cp_sc, c1_ref, n1_ref, m1_ref):
    @pl.when(pl.program_id(1) == 0)
    def _():
        xp_sc[0:SUBLANES, :] = mconv0_ref[...]
        cp_sc[0:SUBLANES, :] = sconv0_ref[...]
        c1_ref[...] = c0_ref[...]
        n1_ref[...] = n0_ref[...]
        m1_ref[...] = m0_ref[...]


def _causal_conv(src_sc, w_ref, width, tt):
    acc = None
    for j in range(width):
        term = src_sc[pl.ds(SUBLANES - (width - 1) + j, tt), :] * w_ref[j:j + 1, :]
        acc = term if acc is None else acc + term
    return acc


def _carry_tail(src_sc, tail_ref, tt):
    tail = src_sc[tt:tt + SUBLANES, :]
    src_sc[0:SUBLANES, :] = tail
    tail_ref[...] = tail


def _head_slices():
    return [slice(h * M_HEAD_DIM, (h + 1) * M_HEAD_DIM) for h in range(M_HEADS)]


def _mlstm_tile(xq, v, ifp, states, wq_ref, wk_ref, bias_ref, ghead_ref, tt, between=()):
    nh = M_HEADS
    hs = _head_slices()
    seq_rows = [slice(j * tt, (j + 1) * tt) for j in range(len(states))]
    g = [ifp[rows, :] + bias_ref[...] for rows in seq_rows]
    b_c = [_scan_rows(_log_sigmoid(gj), jnp.add, 0.0) for gj in g]
    a_c = [gj - pltpu.roll(bj, LANES - nh, axis=1) for gj, bj in zip(g, b_c)]
    amax_c = [_scan_rows(aj, jnp.maximum, -jnp.inf) for aj in a_c]
    a_t = [aj.T for aj in a_c]

    r_idx = lax.broadcasted_iota(jnp.int32, (tt, tt), 0)
    c_idx = lax.broadcasted_iota(jnp.int32, (tt, tt), 1)
    causal = c_idx <= r_idx

    q_all = [_dot(xq[:, hs[h]], wq_ref[h]) * (M_HEAD_DIM ** -0.5) for h in range(nh)]
    k_all = [_dot(xq[:, hs[h]], wk_ref[h]) for h in range(nh)]
    units = [(j, h) for j in range(len(states)) for h in range(nh)]
    ids = range(len(units))
    q = [q_all[h][seq_rows[j], :] for j, h in units]
    k = [k_all[h][seq_rows[j], :] for j, h in units]
    vv = [v[seq_rows[j], hs[h]] for j, h in units]
    c_prev = [states[j][0][h] for j, h in units]
    n_prev = [states[j][1][h:h + 1, :] for j, h in units]
    m_prev = [states[j][2][h:h + 1, 0:1] for j, h in units]
    b_col = [b_c[j][:, nh + h:nh + h + 1] for j, h in units]
    a_col = [a_c[j][:, h:h + 1] for j, h in units]
    a_row = [a_t[j][h:h + 1, :] for j, h in units]
    b_last = [b_c[j][tt - 1:tt, nh + h:nh + h + 1] for j, h in units]
    mm_col = [jnp.maximum(m_prev[i], amax_c[j][:, h:h + 1]) for i, (j, h) in enumerate(units)]
    mm_last = [jnp.maximum(m_prev[i], amax_c[j][tt - 1:tt, h:h + 1]) for i, (j, h) in enumerate(units)]

    extra = []
    pending = list(between)

    def run_one():
        if pending:
            extra.append(pending.pop(0)())

    w_inter = [jnp.exp(m_prev[i] - mm_col[i]) for i in ids]
    qk = [lax.dot_general(q[i].astype(BF16), k[i].astype(BF16), (((1,), (1,)), ((), ())),
                          preferred_element_type=F32) for i in ids]
    run_one()
    s = [jnp.exp(jnp.where(causal, a_row[i] - mm_col[i], -jnp.inf)) * qk[i] for i in ids]
    run_one()
    num = [w_inter[i] * _dot(q[i], c_prev[i]) + _dot(s[i], vv[i]) for i in ids]
    den = [w_inter[i] * jnp.sum(q[i] * n_prev[i], axis=-1, keepdims=True)
           + jnp.sum(s[i], axis=-1, keepdims=True) for i in ids]
    run_one()
    out = [num[i] * (1.0 / jnp.maximum(jnp.abs(den[i]), jnp.exp(-(b_col[i] + mm_col[i])))) for i in ids]
    out = [out[i] * lax.rsqrt(jnp.mean(out[i] * out[i], axis=-1, keepdims=True) + RMS_EPS)
           * ghead_ref[:, hs[units[i][1]]] for i in ids]
    while pending:
        run_one()

    decay = [jnp.exp(m_prev[i] - mm_last[i]) for i in ids]
    wk = [jnp.exp(a_col[i] - mm_last[i]) * k[i] for i in ids]
    for i, (j, h) in enumerate(units):
        c_ref, n_ref, m_ref = states[j]
        c_ref[h] = decay[i] * c_prev[i] + lax.dot_general(
            wk[i].astype(BF16), vv[i].astype(BF16), (((0,), (0,)), ((), ())), preferred_element_type=F32)
        n_ref[h:h + 1, :] = decay[i] * n_prev[i] + jnp.sum(wk[i], axis=0, keepdims=True)
        m_ref[h:h + 1, :] = jnp.broadcast_to(b_last[i] + mm_last[i], (1, LANES))
    return [out[j * nh:(j + 1) * nh] for j in range(len(states))], extra


def _mlstm_kernel(xm_ref, v_ref, cx_ref, if_ref, mconv0_ref, sconv0_ref, c0_ref, n0_ref, m0_ref,
                  w_mconv_ref, w_sconv_ref, wq_ref, wk_ref, bias_ref, ghead_ref,
                  hm_ref, u_ref, c1_ref, n1_ref, m1_ref, mconv1_ref, sconv1_ref,
                  xp_sc, cp_sc, *, tt, nb):
    hs = _head_slices()
    xq, states = [], []
    for j in range(nb):
        rows = slice(j * tt, (j + 1) * tt)
        xp, cp, state = xp_sc.at[j], cp_sc.at[j], (c1_ref.at[j], n1_ref.at[j], m1_ref.at[j])
        _seq_init(mconv0_ref.at[j], sconv0_ref.at[j], c0_ref.at[j], n0_ref.at[j], m0_ref.at[j], xp, cp, *state)
        xp[SUBLANES:SUBLANES + tt, :] = xm_ref[rows, :]
        cp[SUBLANES:SUBLANES + tt, :] = cx_ref[rows, :]
        xq.append(_silu(_causal_conv(xp, w_mconv_ref, M_CONV_W, tt)))
        u_ref[rows, :] = _causal_conv(cp, w_sconv_ref, S_CONV_W, tt)
        _carry_tail(xp, mconv1_ref.at[j], tt)
        _carry_tail(cp, sconv1_ref.at[j], tt)
        states.append(state)
    xq = xq[0] if nb == 1 else jnp.concatenate(xq, axis=0)
    out, _ = _mlstm_tile(xq, v_ref[...], if_ref[...], states, wq_ref, wk_ref, bias_ref, ghead_ref, tt)
    for j in range(nb):
        for h, sl in enumerate(hs):
            hm_ref[j * tt:(j + 1) * tt, sl] = out[j][h]


def _mlstm(xm, v, cx, ifp, mconv0, sconv0, c0, n0, m0, w_mconv, w_sconv, wq, wk, bias, ghead, *, bsz, t, tt, nb):
    nt = t // tt
    n = bsz * t
    assert nb == 1 or nt == 1, "several sequences per step only when a step covers whole sequences"
    tok = lambda b, i: (b * nt + i, 0)
    seq3 = lambda b, i: (b, 0, 0)
    seq4 = lambda b, i: (b, 0, 0, 0)
    big = pl.BlockSpec((nb * tt, D_MODEL), tok)
    tail = pl.BlockSpec((nb, SUBLANES, D_MODEL), seq3)
    c_spec = pl.BlockSpec((nb, M_HEADS, M_HEAD_DIM, M_HEAD_DIM), seq4)
    n_spec = pl.BlockSpec((nb, M_HEADS, M_HEAD_DIM), seq3)
    m_spec = pl.BlockSpec((nb, SUBLANES, LANES), seq3)
    return pl.pallas_call(
        functools.partial(_mlstm_kernel, tt=tt, nb=nb),
        grid=(bsz // nb, nt),
        in_specs=[big, big, big, pl.BlockSpec((nb * tt, LANES), tok), tail, tail, c_spec, n_spec, m_spec,
                  _resident(w_mconv.shape), _resident(w_sconv.shape), _resident(wq.shape),
                  _resident(wk.shape), _resident(bias.shape), _resident(ghead.shape)],
        out_specs=[big, big, c_spec, n_spec, m_spec, tail, tail],
        out_shape=[jax.ShapeDtypeStruct((n, D_MODEL), F32), jax.ShapeDtypeStruct((n, D_MODEL), F32),
                   jax.ShapeDtypeStruct(c0.shape, F32), jax.ShapeDtypeStruct(n0.shape, F32),
                   jax.ShapeDtypeStruct(m0.shape, F32),
                   jax.ShapeDtypeStruct(mconv0.shape, F32), jax.ShapeDtypeStruct(sconv0.shape, F32)],
        scratch_shapes=[pltpu.VMEM((nb, tt + SUBLANES, D_MODEL), F32),
                        pltpu.VMEM((nb, tt + SUBLANES, D_MODEL), F32)],
        compiler_params=pltpu.CompilerParams(dimension_semantics=("arbitrary", "arbitrary"),
                                             vmem_limit_bytes=VMEM_LIMIT),
        name="mlstm",
    )(xm, v, cx, ifp, mconv0, sconv0, c0, n0, m0, w_mconv, w_sconv, wq, wk, bias, ghead)


def _merge_kernel(x_ref, hm_ref, o_ref, ga_ref, gb_ref, sb_ref, u_ref, wa_ref, wsout_ref, wo_ref, x1_ref):
    y_a = _dot(_sigmoid(o_ref[...]) * hm_ref[...], wa_ref[...])
    y_b = _dot(sb_ref[...] * u_ref[...], wsout_ref[...])
    merged = _sigmoid(ga_ref[...]) * y_a + _sigmoid(gb_ref[...]) * y_b
    x1_ref[...] = x_ref[...] + _dot(merged, wo_ref[...])


def _merge(x, hm, o, ga, gb, sb, u, w_a, w_sout, w_o, tm):
    n = x.shape[0]
    big = pl.BlockSpec((tm, D_MODEL), lambda i: (i, 0))
    return pl.pallas_call(
        _merge_kernel,
        grid=(n // tm,),
        in_specs=[big] * 7 + [_resident(w_a.shape), _resident(w_sout.shape), _resident(w_o.shape)],
        out_specs=big,
        out_shape=jax.ShapeDtypeStruct((n, D_MODEL), F32),
        compiler_params=pltpu.CompilerParams(dimension_semantics=("arbitrary",),
                                             vmem_limit_bytes=VMEM_LIMIT),
        name="merge",
    )(x, hm, o, ga, gb, sb, u, w_a, w_sout, w_o)


def _mixer_kernel(x_ref, mconv0_ref, sconv0_ref, c0_ref, n0_ref, m0_ref,
                  g_ref, w_main_ref, w_if_ref, w_rest_ref, w_mconv_ref, w_sconv_ref, wq_ref, wk_ref,
                  bias_ref, ghead_ref, wa_ref, wsout_ref, wo_ref,
                  x1_ref, c1_ref, n1_ref, m1_ref, mconv1_ref, sconv1_ref,
                  xp_sc, cp_sc, hm_sc, *, tt):
    _seq_init(mconv0_ref, sconv0_ref, c0_ref, n0_ref, m0_ref, xp_sc, cp_sc, c1_ref, n1_ref, m1_ref)
    d = D_MODEL
    h = _rms(x_ref[...], g_ref[...]).astype(BF16)

    def proj(w_ref, j):
        return jnp.dot(h, w_ref[:, j * d:(j + 1) * d], preferred_element_type=F32)

    xp_sc[SUBLANES:SUBLANES + tt, :] = proj(w_main_ref, 0)
    ifp = jnp.dot(h, w_if_ref[...], preferred_element_type=F32)
    cp_sc[SUBLANES:SUBLANES + tt, :] = proj(w_rest_ref, 1) * proj(w_rest_ref, 2)
    xq = _silu(_causal_conv(xp_sc, w_mconv_ref, M_CONV_W, tt))
    u = _causal_conv(cp_sc, w_sconv_ref, S_CONV_W, tt)
    _carry_tail(xp_sc, mconv1_ref, tt)
    _carry_tail(cp_sc, sconv1_ref, tt)
    v = proj(w_main_ref, 1)
    hs = _head_slices()
    (out,), (y_b, sig_o, sig_ga, sig_gb) = _mlstm_tile(
        xq, v, ifp, [(c1_ref, n1_ref, m1_ref)], wq_ref, wk_ref, bias_ref, ghead_ref, tt,
        between=(lambda: _dot(proj(w_rest_ref, 0) * u, wsout_ref[...]),
                 lambda: _sigmoid(proj(w_main_ref, 2)),
                 lambda: _sigmoid(proj(w_rest_ref, 3)),
                 lambda: _sigmoid(proj(w_rest_ref, 4))))
    for hd, sl in enumerate(hs):
        hm_sc[:, sl] = (sig_o[:, sl] * out[hd]).astype(BF16)
    y_a = jnp.dot(hm_sc[...], wa_ref[...], preferred_element_type=F32)
    merged = sig_ga * y_a + sig_gb * y_b
    x1_ref[...] = x_ref[...] + _dot(merged, wo_ref[...])


def _mixer(x, mconv0, sconv0, c0, n0, m0, wts, *, bsz, t, tt):
    nt = t // tt
    tok = lambda b, i: (b * nt + i, 0)
    seq3 = lambda b, i: (b, 0, 0)
    seq4 = lambda b, i: (b, 0, 0, 0)
    big = pl.BlockSpec((tt, D_MODEL), tok)
    tail = pl.BlockSpec((None, SUBLANES, D_MODEL), seq3)
    c_spec = pl.BlockSpec((None, M_HEADS, M_HEAD_DIM, M_HEAD_DIM), seq4)
    n_spec = pl.BlockSpec((None, M_HEADS, M_HEAD_DIM), seq3)
    m_spec = pl.BlockSpec((None, SUBLANES, LANES), seq3)
    names = ("g_mix", "w_main", "w_if", "w_rest", "w_mconv", "w_sconv", "w_q", "w_k", "gate_bias", "g_head",
             "w_a", "w_sout", "w_o")
    weights = [wts[k] for k in names]
    return pl.pallas_call(
        functools.partial(_mixer_kernel, tt=tt),
        grid=(bsz, nt),
        in_specs=[big, tail, tail, c_spec, n_spec, m_spec] + [_resident(w.shape) for w in weights],
        out_specs=[big, c_spec, n_spec, m_spec, tail, tail],
        out_shape=[jax.ShapeDtypeStruct(x.shape, F32), jax.ShapeDtypeStruct(c0.shape, F32),
                   jax.ShapeDtypeStruct(n0.shape, F32), jax.ShapeDtypeStruct(m0.shape, F32),
                   jax.ShapeDtypeStruct(mconv0.shape, F32), jax.ShapeDtypeStruct(sconv0.shape, F32)],
        scratch_shapes=[pltpu.VMEM((tt + SUBLANES, D_MODEL), F32), pltpu.VMEM((tt + SUBLANES, D_MODEL), F32),
                        pltpu.VMEM((tt, D_MODEL), BF16)],
        compiler_params=pltpu.CompilerParams(dimension_semantics=("arbitrary", "arbitrary"),
                                             vmem_limit_bytes=VMEM_LIMIT),
        name="mixer",
    )(x, mconv0, sconv0, c0, n0, m0, *weights)


MOE_WINDOW = 1024
MOE_ROW_TILE = 128
MOE_ALIGN = 16
EXPERTS_PER_STEP = 4
MOE_STEPS = N_EXPERTS // EXPERTS_PER_STEP
MOE_CHUNK = 256
MOE_SORTED_ROWS = -(-(MOE_WINDOW + N_GROUPS * (MOE_ALIGN - 1) + MOE_ROW_TILE) // MOE_CHUNK) * MOE_CHUNK


def _split_bf16(x):
    hi = x.astype(BF16)
    lo = (x - hi.astype(F32)).astype(BF16)
    return hi, lo


CW_LANE_STRIDE = 40


def _pack_split3(cw):
    hi = cw.astype(BF16).astype(F32)
    r1 = cw - hi
    mid = r1.astype(BF16).astype(F32)
    lo = (r1 - mid).astype(BF16).astype(F32)
    return (hi + pltpu.roll(mid, CW_LANE_STRIDE, axis=1) + pltpu.roll(lo, 2 * CW_LANE_STRIDE, axis=1)).astype(BF16)


def _unpack_split3(packed):
    return (packed + pltpu.roll(packed, LANES - CW_LANE_STRIDE, axis=1)
            + pltpu.roll(packed, LANES - 2 * CW_LANE_STRIDE, axis=1))


def _route(logits):
    lane = lax.broadcasted_iota(jnp.int32, logits.shape, 1)
    neg = -jnp.inf
    big = jnp.int32(LANES)
    is_grp = lane < N_GROUPS
    g_max = jnp.max(jnp.where(is_grp, logits, neg), axis=-1, keepdims=True)
    g_sel = jnp.min(jnp.where(is_grp & (logits == g_max), lane, big), axis=-1, keepdims=True)
    p_grp = 1.0 / jnp.sum(jnp.where(is_grp, jnp.exp(logits - g_max), 0.0), axis=-1, keepdims=True)
    lo = N_GROUPS + g_sel * EXPERTS_PER_GROUP
    in_grp = (lane >= lo) & (lane < lo + EXPERTS_PER_GROUP)
    v1 = jnp.max(jnp.where(in_grp, logits, neg), axis=-1, keepdims=True)
    e1 = jnp.min(jnp.where(in_grp & (logits == v1), lane, big), axis=-1, keepdims=True)
    rest = in_grp & (lane != e1)
    v2 = jnp.max(jnp.where(rest, logits, neg), axis=-1, keepdims=True)
    e2 = jnp.min(jnp.where(rest & (logits == v2), lane, big), axis=-1, keepdims=True)
    z = jnp.exp(v2 - v1)
    w1 = p_grp / (1.0 + z)
    w2 = p_grp * z / (1.0 + z)
    return g_sel, jnp.where(lane == e1, w1, jnp.where(lane == e2, w2, 0.0))


def _moe_tail_kernel(x1_ref, p_ref, gffn_ref, wr_ref, br_ref, wg_ref, wu_ref, wd_ref,
                     gple_ref, wpg_ref, wpp_ref, gfin_ref, y_ref,
                     xs_sc, cws_sc, osort_sc, pos_sc, off_sm, nt_sm):
    s = pl.program_id(1)
    w = MOE_WINDOW

    @pl.when(s == 0)
    def _():
        hn = _rms(x1_ref[...], gffn_ref[...])
        hn_hi, hn_lo = _split_bf16(hn)
        hh_hl = jnp.dot(hn_hi, wr_ref[...], preferred_element_type=F32)
        logits = (hh_hl[:, :LANES] + hh_hl[:, LANES:]
                  + jnp.dot(hn_lo, wr_ref[:, :LANES], preferred_element_type=F32)) + br_ref[...]
        g_sel, cw = _route(logits)
        lane = lax.broadcasted_iota(jnp.int32, (w, LANES), 1)
        onehot = jnp.where(lane == g_sel, 1.0, 0.0)
        cum = _scan_rows(onehot, jnp.add, 0.0)
        cnt = cum[w - 1:w, :].astype(jnp.int32)
        cnt_pad = ((cnt + (MOE_ALIGN - 1)) // MOE_ALIGN) * MOE_ALIGN
        lane1 = lax.broadcasted_iota(jnp.int32, (1, LANES), 1)
        off = jnp.zeros((1, LANES), jnp.int32)
        for gi in range(N_GROUPS - 1):
            off = off + jnp.where(lane1 > gi, cnt_pad[:, gi:gi + 1], 0)
        n_tiles = (cnt + (MOE_ROW_TILE - 1)) // MOE_ROW_TILE
        for gi in range(N_GROUPS):
            off_sm[gi] = off[0, gi]
            nt_sm[gi] = n_tiles[0, gi]
        pos = jnp.sum(onehot * (off.astype(F32) + cum - 1.0), axis=-1, keepdims=True)
        pos_b = jnp.broadcast_to(pos, (w, LANES))
        pos_sc[...] = pos_b
        pos_row = pos_b.T[0:1, :].astype(jnp.int32)
        cw_packed = _pack_split3(cw)
        for c in range(MOE_SORTED_ROWS // MOE_CHUNK):
            rows = lax.broadcasted_iota(jnp.int32, (MOE_CHUNK, w), 0) + c * MOE_CHUNK
            sel = jnp.where(rows == pos_row, 1.0, 0.0).astype(BF16)
            sl = slice(c * MOE_CHUNK, (c + 1) * MOE_CHUNK)
            xs_sc[sl, :] = jnp.dot(sel, hn_hi, preferred_element_type=F32).astype(BF16)
            cws_sc[sl, :] = _unpack_split3(jnp.dot(sel, cw_packed, preferred_element_type=F32))
        osort_sc[...] = jnp.zeros(osort_sc.shape, F32)

    grp = s // (EXPERTS_PER_GROUP // EXPERTS_PER_STEP)
    row0 = off_sm[grp]
    lane_t = lax.broadcasted_iota(jnp.int32, (MOE_ROW_TILE, LANES), 1)
    first_lane = N_GROUPS + s * EXPERTS_PER_STEP

    def tile_body(i, carry):
        r0 = pl.multiple_of(row0 + i * MOE_ROW_TILE, MOE_ALIGN)
        xt = xs_sc[pl.ds(r0, MOE_ROW_TILE), :]
        cwt = cws_sc[pl.ds(r0, MOE_ROW_TILE), :]
        parts = []
        for e in range(EXPERTS_PER_STEP):
            hg = jnp.dot(xt, wg_ref[e], preferred_element_type=F32)
            hu = jnp.dot(xt, wu_ref[e], preferred_element_type=F32)
            col = jnp.sum(jnp.where(lane_t == first_lane + e, cwt, 0.0), axis=-1, keepdims=True)
            parts.append((_silu(hg) * hu * col).astype(BF16))
        he = jnp.concatenate(parts, axis=-1)
        out = jnp.dot(he, wd_ref[...], preferred_element_type=F32)
        osort_sc[pl.ds(r0, MOE_ROW_TILE), :] = osort_sc[pl.ds(r0, MOE_ROW_TILE), :] + out
        return carry

    lax.fori_loop(0, nt_sm[grp], tile_body, 0)

    @pl.when(s == MOE_STEPS - 1)
    def _():
        osort = osort_sc[...].astype(BF16)
        for c in range(w // MOE_CHUNK):
            sl = slice(c * MOE_CHUNK, (c + 1) * MOE_CHUNK)
            pos_col = pos_sc[sl, 0:1].astype(jnp.int32)
            cols = lax.broadcasted_iota(jnp.int32, (MOE_CHUNK, MOE_SORTED_ROWS), 1)
            sel = jnp.where(cols == pos_col, 1.0, 0.0).astype(BF16)
            x2 = x1_ref[sl, :] + jnp.dot(sel, osort, preferred_element_type=F32)
            gate = _sigmoid(_dot(_rms(x2, gple_ref[...]), wpg_ref[...]))
            x3 = x2 + gate * _dot(p_ref[sl, :], wpp_ref[...])
            y_ref[sl, :] = _rms(x3, gfin_ref[...])


def _moe_tail(x1, p, g_ffn, w_r, b_r, wg, wu, wd, g_ple, w_pg, w_pp, g_fin):
    n = x1.shape[0]
    tm = MOE_WINDOW
    row = lambda i, s: (i, 0)
    step = lambda i, s: (s, 0, 0)
    big = pl.BlockSpec((tm, D_MODEL), row)
    sw = EXPERTS_PER_STEP * D_EXPERT
    return pl.pallas_call(
        _moe_tail_kernel,
        grid=(n // tm, MOE_STEPS),
        in_specs=[big, pl.BlockSpec((tm, PLE_DIM), row), _resident(g_ffn.shape), _resident(w_r.shape),
                  _resident(b_r.shape),
                  pl.BlockSpec((EXPERTS_PER_STEP, D_MODEL, D_EXPERT), step),
                  pl.BlockSpec((EXPERTS_PER_STEP, D_MODEL, D_EXPERT), step),
                  pl.BlockSpec((None, sw, D_MODEL), step),
                  _resident(g_ple.shape), _resident(w_pg.shape), _resident(w_pp.shape), _resident(g_fin.shape)],
        out_specs=big,
        out_shape=jax.ShapeDtypeStruct((n, D_MODEL), F32),
        scratch_shapes=[pltpu.VMEM((MOE_SORTED_ROWS, D_MODEL), BF16), pltpu.VMEM((MOE_SORTED_ROWS, LANES), F32),
                        pltpu.VMEM((MOE_SORTED_ROWS, D_MODEL), F32), pltpu.VMEM((tm, LANES), F32),
                        pltpu.SMEM((N_GROUPS,), jnp.int32), pltpu.SMEM((N_GROUPS,), jnp.int32)],
        compiler_params=pltpu.CompilerParams(dimension_semantics=("arbitrary", "arbitrary"),
                                             vmem_limit_bytes=VMEM_LIMIT),
        name="moe_tail",
    )(x1, p, g_ffn, w_r, b_r, wg, wu, wd, g_ple, w_pg, w_pp, g_fin)


def _pad_tail(buf):
    return jnp.pad(buf, ((0, 0), (SUBLANES - buf.shape[1], 0), (0, 0)))


UNFUSED_SEQS_PER_STEP = 8


def _group(x, p, mconv0, sconv0, c0, n0, m0, wts, *, tt, tm, fused):
    bsz, t, _ = x.shape
    xf = x.reshape(bsz * t, D_MODEL)
    pf = p.reshape(bsz * t, PLE_DIM)
    m0p = jnp.broadcast_to(jnp.pad(m0, ((0, 0), (0, SUBLANES - M_HEADS)))[:, :, None], (bsz, SUBLANES, LANES))
    if fused:
        x1, c1, n1, m1p, mconv1, sconv1 = _mixer(xf, _pad_tail(mconv0), _pad_tail(sconv0), c0, n0, m0p, wts,
                                                 bsz=bsz, t=t, tt=tt)
    else:
        xm, v, o, ifp, sb, cx, ga, gb = _in_proj(xf, wts["g_mix"], wts["w_main"], wts["w_if"], wts["w_rest"], tm)
        hm, u, c1, n1, m1p, mconv1, sconv1 = _mlstm(
            xm, v, cx, ifp, _pad_tail(mconv0), _pad_tail(sconv0), c0, n0, m0p,
            wts["w_mconv"], wts["w_sconv"], wts["w_q"], wts["w_k"], wts["gate_bias"], wts["g_head"],
            bsz=bsz, t=t, tt=tt, nb=UNFUSED_SEQS_PER_STEP)
        x1 = _merge(xf, hm, o, ga, gb, sb, u, wts["w_a"], wts["w_sout"], wts["w_o"], tm)
    y = _moe_tail(x1, pf, wts["g_ffn"], wts["w_r"], wts["b_r"], wts["w_gate"], wts["w_up"],
                  wts["w_down"], wts["g_ple"], wts["w_ple_gate"], wts["w_ple_proj"], wts["g_final"])
    return (y.reshape(bsz, t, D_MODEL), c1[None], n1[None], m1p[None, :, :M_HEADS, 0],
            mconv1[None, :, SUBLANES - (M_CONV_W - 1):, :], sconv1[None, :, SUBLANES - (S_CONV_W - 1):, :])


def kernel(x_prompt, x_sample, p_prompt, p_sample, state_mlstm_C, state_mlstm_n, state_mlstm_m, state_mlstm_conv, state_sconv, g_mix, w_in, w_mconv, w_q, w_k, b_i, b_f, g_head, w_a, w_sconv, w_sout, w_o, g_ffn, w_rg, b_rg, w_re, b_re, w_gate, w_up, w_down, g_ple, w_ple_gate, w_ple_proj, g_final):
    assert g_mix.shape[0] == 1, "single-layer trunk"
    d = D_MODEL
    w_in0 = w_in[0]
    n_if = 2 * M_HEADS
    w_router = jnp.pad(jnp.concatenate([w_rg[0], w_re[0]], axis=1), ((0, 0), (0, LANES - N_GROUPS - N_EXPERTS)))
    wr_hi = w_router.astype(BF16)
    wr_lo = (w_router - wr_hi.astype(F32)).astype(BF16)
    w_r = jnp.concatenate([wr_hi, wr_lo], axis=1)
    sw = EXPERTS_PER_STEP * D_EXPERT
    wts = {
        "g_mix": g_mix,
        "w_main": w_in0[:, :3 * d].astype(BF16),
        "w_if": jnp.pad(w_in0[:, 3 * d:3 * d + n_if], ((0, 0), (0, LANES - n_if))).astype(BF16),
        "w_rest": w_in0[:, 3 * d + n_if:].astype(BF16),
        "w_mconv": w_mconv[0], "w_sconv": w_sconv[0],
        "w_q": w_q[0].astype(BF16), "w_k": w_k[0].astype(BF16),
        "gate_bias": jnp.pad(jnp.concatenate([b_i[0], b_f[0]])[None, :], ((0, 0), (0, LANES - n_if))),
        "g_head": g_head[0].reshape(1, d),
        "w_a": w_a[0].astype(BF16), "w_sout": w_sout[0].astype(BF16), "w_o": w_o[0].astype(BF16),
        "g_ffn": g_ffn, "w_r": w_r,
        "b_r": jnp.pad(jnp.concatenate([b_rg[0], b_re[0]])[None, :], ((0, 0), (0, LANES - N_GROUPS - N_EXPERTS))),
        "w_gate": w_gate[0].astype(BF16), "w_up": w_up[0].astype(BF16),
        "w_down": w_down[0].astype(BF16).reshape(MOE_STEPS, sw, d),
        "g_ple": g_ple, "w_ple_gate": w_ple_gate[0].astype(BF16), "w_ple_proj": w_ple_proj[0].astype(BF16),
        "g_final": g_final[None, :],
    }
    bp = x_prompt.shape[0]
    zeros = lambda *s: jnp.zeros(s, F32)
    yp, *st_p = _group(
        x_prompt, p_prompt[0], zeros(bp, M_CONV_W - 1, d), zeros(bp, S_CONV_W - 1, d),
        zeros(bp, M_HEADS, M_HEAD_DIM, M_HEAD_DIM), zeros(bp, M_HEADS, M_HEAD_DIM), zeros(bp, M_HEADS),
        wts, tt=256, tm=512, fused=True)
    ys, *st_s = _group(
        x_sample, p_sample[0], state_mlstm_conv[0], state_sconv[0],
        state_mlstm_C[0], state_mlstm_n[0], state_mlstm_m[0],
        wts, tt=x_sample.shape[1], tm=512, fused=False)
    return (yp, ys, *st_p, *st_s)
```

```python
import functools

import jax
import jax.numpy as jnp
from jax import lax
from jax.experimental import pallas as pl
from jax.experimental.pallas import tpu as pltpu

D_MODEL = 1024
M_HEADS = 4
M_HEAD_DIM = 256
M_CONV_W = 4
S_CONV_W = 3
N_GROUPS = 4
EXPERTS_PER_GROUP = 8
N_EXPERTS = N_GROUPS * EXPERTS_PER_GROUP
D_EXPERT = 256
PLE_DIM = 256
RMS_EPS = 1e-6

LANES = 128
SUBLANES = 8
VMEM_LIMIT = 56 * 1024 * 1024

BF16 = jnp.bfloat16
F32 = jnp.float32


def _rms(x, g):
    return x * lax.rsqrt(jnp.mean(x * x, axis=-1, keepdims=True) + RMS_EPS) * g


def _dot(a, b):
    return jnp.dot(a.astype(BF16), b.astype(BF16), preferred_element_type=F32)


def _sigmoid(x):
    return 0.5 * jnp.tanh(0.5 * x) + 0.5


def _silu(x):
    return x * _sigmoid(x)


def _resident(shape):
    nd = len(shape)
    return pl.BlockSpec(shape, lambda *_: (0,) * nd, pipeline_mode=pl.Buffered(1))


def _in_proj_kernel(x_ref, g_ref, w_main_ref, w_if_ref, w_rest_ref,
                    xm_ref, v_ref, o_ref, if_ref, sb_ref, cx_ref, ga_ref, gb_ref):
    h = _rms(x_ref[...], g_ref[...]).astype(BF16)
    d = D_MODEL
    xm_ref[...] = jnp.dot(h, w_main_ref[:, 0:d], preferred_element_type=F32)
    v_ref[...] = jnp.dot(h, w_main_ref[:, d:2 * d], preferred_element_type=F32)
    o_ref[...] = jnp.dot(h, w_main_ref[:, 2 * d:3 * d], preferred_element_type=F32)
    if_ref[...] = jnp.dot(h, w_if_ref[...], preferred_element_type=F32)
    sb_ref[...] = jnp.dot(h, w_rest_ref[:, 0:d], preferred_element_type=F32)
    s_c = jnp.dot(h, w_rest_ref[:, d:2 * d], preferred_element_type=F32)
    s_x = jnp.dot(h, w_rest_ref[:, 2 * d:3 * d], preferred_element_type=F32)
    cx_ref[...] = s_c * s_x
    ga_ref[...] = jnp.dot(h, w_rest_ref[:, 3 * d:4 * d], preferred_element_type=F32)
    gb_ref[...] = jnp.dot(h, w_rest_ref[:, 4 * d:5 * d], preferred_element_type=F32)


def _in_proj(x, g_mix, w_main, w_if, w_rest, tm):
    n = x.shape[0]
    row = lambda i: (i, 0)
    big = pl.BlockSpec((tm, D_MODEL), row)
    outs = [jax.ShapeDtypeStruct((n, D_MODEL), F32)] * 3 + [jax.ShapeDtypeStruct((n, LANES), F32)] \
        + [jax.ShapeDtypeStruct((n, D_MODEL), F32)] * 4
    return pl.pallas_call(
        _in_proj_kernel,
        grid=(n // tm,),
        in_specs=[big, _resident(g_mix.shape), _resident(w_main.shape), _resident(w_if.shape),
                  _resident(w_rest.shape)],
        out_specs=[big, big, big, pl.BlockSpec((tm, LANES), row), big, big, big, big],
        out_shape=outs,
        compiler_params=pltpu.CompilerParams(dimension_semantics=("arbitrary",),
                                             vmem_limit_bytes=VMEM_LIMIT),
        name="in_proj",
    )(x, g_mix, w_main, w_if, w_rest)


def _scan_rows(x, op, identity):
    sub = lax.broadcasted_iota(jnp.int32, (SUBLANES, x.shape[1]), 0)
    blocks, carry = [], None
    for i in range(x.shape[0] // SUBLANES):
        blk = x[i * SUBLANES:(i + 1) * SUBLANES, :]
        for shift in (1, 2, 4):
            blk = op(blk, jnp.where(sub >= shift, pltpu.roll(blk, shift, axis=0), identity))
        if carry is not None:
            blk = op(blk, carry)
        carry = jnp.broadcast_to(blk[SUBLANES - 1:SUBLANES, :], blk.shape)
        blocks.append(blk)
    return jnp.concatenate(blocks, axis=0)


def _log_sigmoid(x):
    return jnp.minimum(x, 0.0) - jnp.log1p(jnp.exp(-jnp.abs(x)))


def _seq_init(mconv0_ref, sconv0_ref, c0_ref, n0_ref, m0_ref, xp_sc, cp_sc, c1_ref, n1_ref, m1_ref):
    @pl.when(pl.program_id(1) == 0)
    def _():
        xp_sc[0:SUBLANES, :] = mconv0_ref[...]
        cp_sc[0:SUBLANES, :] = sconv0_ref[...]
        c1_ref[...] = c0_ref[...]
        n1_ref[...] = n0_ref[...]
        m1_ref[...] = m0_ref[...]


def _causal_conv(src_sc, w_ref, width, tt):
    acc = None
    for j in range(width):
        term = src_sc[pl.ds(SUBLANES - (width - 1) + j, tt), :] * w_ref[j:j + 1, :]
        acc = term if acc is None else acc + term
    return acc


def _carry_tail(src_sc, tail_ref, tt):
    tail = src_sc[tt:tt + SUBLANES, :]
    src_sc[0:SUBLANES, :] = tail
    tail_ref[...] = tail


def _head_slices():
    return [slice(h * M_HEAD_DIM, (h + 1) * M_HEAD_DIM) for h in range(M_HEADS)]


def _mlstm_tile(xq, v, ifp, states, wq_ref, wk_ref, bias_ref, ghead_ref, tt, between=()):
    nh = M_HEADS
    hs = _head_slices()
    seq_rows = [slice(j * tt, (j + 1) * tt) for j in range(len(states))]
    g = [ifp[rows, :] + bias_ref[...] for rows in seq_rows]
    b_c = [_scan_rows(_log_sigmoid(gj), jnp.add, 0.0) for gj in g]
    a_c = [gj - pltpu.roll(bj, LANES - nh, axis=1) for gj, bj in zip(g, b_c)]
    amax_c = [_scan_rows(aj, jnp.maximum, -jnp.inf) for aj in a_c]
    a_t = [aj.T for aj in a_c]

    r_idx = lax.broadcasted_iota(jnp.int32, (tt, tt), 0)
    c_idx = lax.broadcasted_iota(jnp.int32, (tt, tt), 1)
    causal = c_idx <= r_idx

    q_all = [_dot(xq[:, hs[h]], wq_ref[h]) * (M_HEAD_DIM ** -0.5) for h in range(nh)]
    k_all = [_dot(xq[:, hs[h]], wk_ref[h]) for h in range(nh)]
    units = [(j, h) for j in range(len(states)) for h in range(nh)]
    ids = range(len(units))
    q = [q_all[h][seq_rows[j], :] for j, h in units]
    k = [k_all[h][seq_rows[j], :] for j, h in units]
    vv = [v[seq_rows[j], hs[h]] for j, h in units]
    c_prev = [states[j][0][h] for j, h in units]
    n_prev = [states[j][1][h:h + 1, :] for j, h in units]
    m_prev = [states[j][2][h:h + 1, 0:1] for j, h in units]
    b_col = [b_c[j][:, nh + h:nh + h + 1] for j, h in units]
    a_col = [a_c[j][:, h:h + 1] for j, h in units]
    a_row = [a_t[j][h:h + 1, :] for j, h in units]
    b_last = [b_c[j][tt - 1:tt, nh + h:nh + h + 1] for j, h in units]
    mm_col = [jnp.maximum(m_prev[i], amax_c[j][:, h:h + 1]) for i, (j, h) in enumerate(units)]
    mm_last = [jnp.maximum(m_prev[i], amax_c[j][tt - 1:tt, h:h + 1]) for i, (j, h) in enumerate(units)]

    extra = []
    pending = list(between)

    def run_one():
        if pending:
            extra.append(pending.pop(0)())

    w_inter = [jnp.exp(m_prev[i] - mm_col[i]) for i in ids]
    qk = [lax.dot_general(q[i].astype(BF16), k[i].astype(BF16), (((1,), (1,)), ((), ())),
                          preferred_element_type=F32) for i in ids]
    run_one()
    s = [jnp.exp(jnp.where(causal, a_row[i] - mm_col[i], -jnp.inf)) * qk[i] for i in ids]
    run_one()
    num = [w_inter[i] * _dot(q[i], c_prev[i]) + _dot(s[i], vv[i]) for i in ids]
    den = [w_inter[i] * jnp.sum(q[i] * n_prev[i], axis=-1, keepdims=True)
           + jnp.sum(s[i], axis=-1, keepdims=True) for i in ids]
    run_one()
    out = [num[i] * (1.0 / jnp.maximum(jnp.abs(den[i]), jnp.exp(-(b_col[i] + mm_col[i])))) for i in ids]
    out = [out[i] * lax.rsqrt(jnp.mean(out[i] * out[i], axis=-1, keepdims=True) + RMS_EPS)
           * ghead_ref[:, hs[units[i][1]]] for i in ids]
    while pending:
        run_one()

    decay = [jnp.exp(m_prev[i] - mm_last[i]) for i in ids]
    wk = [jnp.exp(a_col[i] - mm_last[i]) * k[i] for i in ids]
    for i, (j, h) in enumerate(units):
        c_ref, n_ref, m_ref = states[j]
        c_ref[h] = decay[i] * c_prev[i] + lax.dot_general(
            wk[i].astype(BF16), vv[i].astype(BF16), (((0,), (0,)), ((), ())), preferred_element_type=F32)
        n_ref[h:h + 1, :] = decay[i] * n_prev[i] + jnp.sum(wk[i], axis=0, keepdims=True)
        m_ref[h:h + 1, :] = jnp.broadcast_to(b_last[i] + mm_last[i], (1, LANES))
    return [out[j * nh:(j + 1) * nh] for j in range(len(states))], extra


def _mlstm_kernel(xm_ref, v_ref, cx_ref, if_ref, mconv0_ref, sconv0_ref, c0_ref, n0_ref, m0_ref,
                  w_mconv_ref, w_sconv_ref, wq_ref, wk_ref, bias_ref, ghead_ref,
                  hm_ref, u_ref, c1_ref, n1_ref, m1_ref, mconv1_ref, sconv1_ref,
                  xp_sc, cp_sc, *, tt, nb):
    hs = _head_slices()
    xq, states = [], []
    for j in range(nb):
        rows = slice(j * tt, (j + 1) * tt)
        xp, cp, state = xp_sc.at[j], cp_sc.at[j], (c1_ref.at[j], n1_ref.at[j], m1_ref.at[j])
        _seq_init(mconv0_ref.at[j], sconv0_ref.at[j], c0_ref.at[j], n0_ref.at[j], m0_ref.at[j], xp, cp, *state)
        xp[SUBLANES:SUBLANES + tt, :] = xm_ref[rows, :]
        cp[SUBLANES:SUBLANES + tt, :] = cx_ref[rows, :]
        xq.append(_silu(_causal_conv(xp, w_mconv_ref, M_CONV_W, tt)))
        u_ref[rows, :] = _causal_conv(cp, w_sconv_ref, S_CONV_W, tt)
        _carry_tail(xp, mconv1_ref.at[j], tt)
        _carry_tail(cp, sconv1_ref.at[j], tt)
        states.append(state)
    xq = xq[0] if nb == 1 else jnp.concatenate(xq, axis=0)
    out, _ = _mlstm_tile(xq, v_ref[...], if_ref[...], states, wq_ref, wk_ref, bias_ref, ghead_ref, tt)
    for j in range(nb):
        for h, sl in enumerate(hs):
            hm_ref[j * tt:(j + 1) * tt, sl] = out[j][h]


def _mlstm(xm, v, cx, ifp, mconv0, sconv0, c0, n0, m0, w_mconv, w_sconv, wq, wk, bias, ghead, *, bsz, t, tt, nb):
    nt = t // tt
    n = bsz * t
    assert nb == 1 or nt == 1, "several sequences per step only when a step covers whole sequences"
    tok = lambda b, i: (b * nt + i, 0)
    seq3 = lambda b, i: (b, 0, 0)
    seq4 = lambda b, i: (b, 0, 0, 0)
    big = pl.BlockSpec((nb * tt, D_MODEL), tok)
    tail = pl.BlockSpec((nb, SUBLANES, D_MODEL), seq3)
    c_spec = pl.BlockSpec((nb, M_HEADS, M_HEAD_DIM, M_HEAD_DIM), seq4)
    n_spec = pl.BlockSpec((nb, M_HEADS, M_HEAD_DIM), seq3)
    m_spec = pl.BlockSpec((nb, SUBLANES, LANES), seq3)
    return pl.pallas_call(
        functools.partial(_mlstm_kernel, tt=tt, nb=nb),
        grid=(bsz // nb, nt),
        in_specs=[big, big, big, pl.BlockSpec((nb * tt, LANES), tok), tail, tail, c_spec, n_spec, m_spec,
                  _resident(w_mconv.shape), _resident(w_sconv.shape), _resident(wq.shape),
                  _resident(wk.shape), _resident(bias.shape), _resident(ghead.shape)],
        out_specs=[big, big, c_spec, n_spec, m_spec, tail, tail],
        out_shape=[jax.ShapeDtypeStruct((n, D_MODEL), F32), jax.ShapeDtypeStruct((n, D_MODEL), F32),
                   jax.ShapeDtypeStruct(c0.shape, F32), jax.ShapeDtypeStruct(n0.shape, F32),
                   jax.ShapeDtypeStruct(m0.shape, F32),
                   jax.ShapeDtypeStruct(mconv0.shape, F32), jax.ShapeDtypeStruct(sconv0.shape, F32)],
        scratch_shapes=[pltpu.VMEM((nb, tt + SUBLANES, D_MODEL), F32),
                        pltpu.VMEM((nb, tt + SUBLANES, D_MODEL), F32)],
        compiler_params=pltpu.CompilerParams(dimension_semantics=("arbitrary", "arbitrary"),
                                             vmem_limit_bytes=VMEM_LIMIT),
        name="mlstm",
    )(xm, v, cx, ifp, mconv0, sconv0, c0, n0, m0, w_mconv, w_sconv, wq, wk, bias, ghead)


def _merge_kernel(x_ref, hm_ref, o_ref, ga_ref, gb_ref, sb_ref, u_ref, wa_ref, wsout_ref, wo_ref, x1_ref):
    y_a = _dot(_sigmoid(o_ref[...]) * hm_ref[...], wa_ref[...])
    y_b = _dot(sb_ref[...] * u_ref[...], wsout_ref[...])
    merged = _sigmoid(ga_ref[...]) * y_a + _sigmoid(gb_ref[...]) * y_b
    x1_ref[...] = x_ref[...] + _dot(merged, wo_ref[...])


def _merge(x, hm, o, ga, gb, sb, u, w_a, w_sout, w_o, tm):
    n = x.shape[0]
    big = pl.BlockSpec((tm, D_MODEL), lambda i: (i, 0))
    return pl.pallas_call(
        _merge_kernel,
        grid=(n // tm,),
        in_specs=[big] * 7 + [_resident(w_a.shape), _resident(w_sout.shape), _resident(w_o.shape)],
        out_specs=big,
        out_shape=jax.ShapeDtypeStruct((n, D_MODEL), F32),
        compiler_params=pltpu.CompilerParams(dimension_semantics=("arbitrary",),
                                             vmem_limit_bytes=VMEM_LIMIT),
        name="merge",
    )(x, hm, o, ga, gb, sb, u, w_a, w_sout, w_o)


def _mixer_kernel(x_ref, mconv0_ref, sconv0_ref, c0_ref, n0_ref, m0_ref,
                  g_ref, w_main_ref, w_if_ref, w_rest_ref, w_mconv_ref, w_sconv_ref, wq_ref, wk_ref,
                  bias_ref, ghead_ref, wa_ref, wsout_ref, wo_ref,
                  x1_ref, c1_ref, n1_ref, m1_ref, mconv1_ref, sconv1_ref,
                  xp_sc, cp_sc, hm_sc, *, tt):
    _seq_init(mconv0_ref, sconv0_ref, c0_ref, n0_ref, m0_ref, xp_sc, cp_sc, c1_ref, n1_ref, m1_ref)
    d = D_MODEL
    h = _rms(x_ref[...], g_ref[...]).astype(BF16)

    def proj(w_ref, j):
        return jnp.dot(h, w_ref[:, j * d:(j + 1) * d], preferred_element_type=F32)

    xp_sc[SUBLANES:SUBLANES + tt, :] = proj(w_main_ref, 0)
    ifp = jnp.dot(h, w_if_ref[...], preferred_element_type=F32)
    cp_sc[SUBLANES:SUBLANES + tt, :] = proj(w_rest_ref, 1) * proj(w_rest_ref, 2)
    xq = _silu(_causal_conv(xp_sc, w_mconv_ref, M_CONV_W, tt))
    u = _causal_conv(cp_sc, w_sconv_ref, S_CONV_W, tt)
    _carry_tail(xp_sc, mconv1_ref, tt)
    _carry_tail(cp_sc, sconv1_ref, tt)
    v = proj(w_main_ref, 1)
    hs = _head_slices()
    (out,), (y_b, sig_o, sig_ga, sig_gb) = _mlstm_tile(
        xq, v, ifp, [(c1_ref, n1_ref, m1_ref)], wq_ref, wk_ref, bias_ref, ghead_ref, tt,
        between=(lambda: _dot(proj(w_rest_ref, 0) * u, wsout_ref[...]),
                 lambda: _sigmoid(proj(w_main_ref, 2)),
                 lambda: _sigmoid(proj(w_rest_ref, 3)),
                 lambda: _sigmoid(proj(w_rest_ref, 4))))
    for hd, sl in enumerate(hs):
        hm_sc[:, sl] = (sig_o[:, sl] * out[hd]).astype(BF16)
    y_a = jnp.dot(hm_sc[...], wa_ref[...], preferred_element_type=F32)
    merged = sig_ga * y_a + sig_gb * y_b
    x1_ref[...] = x_ref[...] + _dot(merged, wo_ref[...])


def _mixer(x, mconv0, sconv0, c0, n0, m0, wts, *, bsz, t, tt):
    nt = t // tt
    tok = lambda b, i: (b * nt + i, 0)
    seq3 = lambda b, i: (b, 0, 0)
    seq4 = lambda b, i: (b, 0, 0, 0)
    big = pl.BlockSpec((tt, D_MODEL), tok)
    tail = pl.BlockSpec((None, SUBLANES, D_MODEL), seq3)
    c_spec = pl.BlockSpec((None, M_HEADS, M_HEAD_DIM, M_HEAD_DIM), seq4)
    n_spec = pl.BlockSpec((None, M_HEADS, M_HEAD_DIM), seq3)
    m_spec = pl.BlockSpec((None, SUBLANES, LANES), seq3)
    names = ("g_mix", "w_main", "w_if", "w_rest", "w_mconv", "w_sconv", "w_q", "w_k", "gate_bias", "g_head",
             "w_a", "w_sout", "w_o")
    weights = [wts[k] for k in names]
    return pl.pallas_call(
        functools.partial(_mixer_kernel, tt=tt),
        grid=(bsz, nt),
        in_specs=[big, tail, tail, c_spec, n_spec, m_spec] + [_resident(w.shape) for w in weights],
        out_specs=[big, c_spec, n_spec, m_spec, tail, tail],
        out_shape=[jax.ShapeDtypeStruct(x.shape, F32), jax.ShapeDtypeStruct(c0.shape, F32),
                   jax.ShapeDtypeStruct(n0.shape, F32), jax.ShapeDtypeStruct(m0.shape, F32),
                   jax.ShapeDtypeStruct(mconv0.shape, F32), jax.ShapeDtypeStruct(sconv0.shape, F32)],
        scratch_shapes=[pltpu.VMEM((tt + SUBLANES, D_MODEL), F32), pltpu.VMEM((tt + SUBLANES, D_MODEL), F32),
                        pltpu.VMEM((tt, D_MODEL), BF16)],
        compiler_params=pltpu.CompilerParams(dimension_semantics=("arbitrary", "arbitrary"),
                                             vmem_limit_bytes=VMEM_LIMIT),
        name="mixer",
    )(x, mconv0, sconv0, c0, n0, m0, *weights)


MOE_WINDOW = 1024
MOE_ROW_TILE = 128
MOE_ALIGN = 16
EXPERTS_PER_STEP = EXPERTS_PER_GROUP
MOE_STEPS = N_EXPERTS // EXPERTS_PER_STEP
MOE_CHUNK = 256
MOE_SORTED_ROWS = -(-(MOE_WINDOW + N_GROUPS * (MOE_ALIGN - 1) + MOE_ROW_TILE) // MOE_CHUNK) * MOE_CHUNK


def _split_bf16(x):
    hi = x.astype(BF16)
    lo = (x - hi.astype(F32)).astype(BF16)
    return hi, lo


CW_LANE_STRIDE = 40


def _pack_split3(cw):
    hi = cw.astype(BF16).astype(F32)
    r1 = cw - hi
    mid = r1.astype(BF16).astype(F32)
    lo = (r1 - mid).astype(BF16).astype(F32)
    return (hi + pltpu.roll(mid, CW_LANE_STRIDE, axis=1) + pltpu.roll(lo, 2 * CW_LANE_STRIDE, axis=1)).astype(BF16)


def _unpack_split3(packed):
    return (packed + pltpu.roll(packed, LANES - CW_LANE_STRIDE, axis=1)
            + pltpu.roll(packed, LANES - 2 * CW_LANE_STRIDE, axis=1))


def _route(logits):
    lane = lax.broadcasted_iota(jnp.int32, logits.shape, 1)
    neg = -jnp.inf
    big = jnp.int32(LANES)
    is_grp = lane < N_GROUPS
    g_max = jnp.max(jnp.where(is_grp, logits, neg), axis=-1, keepdims=True)
    g_sel = jnp.min(jnp.where(is_grp & (logits == g_max), lane, big), axis=-1, keepdims=True)
    p_grp = 1.0 / jnp.sum(jnp.where(is_grp, jnp.exp(logits - g_max), 0.0), axis=-1, keepdims=True)
    lo = N_GROUPS + g_sel * EXPERTS_PER_GROUP
    in_grp = (lane >= lo) & (lane < lo + EXPERTS_PER_GROUP)
    v1 = jnp.max(jnp.where(in_grp, logits, neg), axis=-1, keepdims=True)
    e1 = jnp.min(jnp.where(in_grp & (logits == v1), lane, big), axis=-1, keepdims=True)
    rest = in_grp & (lane != e1)
    v2 = jnp.max(jnp.where(rest, logits, neg), axis=-1, keepdims=True)
    e2 = jnp.min(jnp.where(rest & (logits == v2), lane, big), axis=-1, keepdims=True)
    z = jnp.exp(v2 - v1)
    w1 = p_grp / (1.0 + z)
    w2 = p_grp * z / (1.0 + z)
    return g_sel, jnp.where(lane == e1, w1, jnp.where(lane == e2, w2, 0.0))


def _moe_tail_kernel(x1_ref, p_ref, gffn_ref, wr_ref, br_ref, wg_ref, wu_ref, wd_ref,
                     gple_ref, wpg_ref, wpp_ref, gfin_ref, y_ref,
                     xs_sc, cws_sc, osort_sc, pos_sc, off_sm, nt_sm):
    s = pl.program_id(1)
    w = MOE_WINDOW

    @pl.when(s == 0)
    def _():
        hn = _rms(x1_ref[...], gffn_ref[...])
        hn_hi, hn_lo = _split_bf16(hn)
        hh_hl = jnp.dot(hn_hi, wr_ref[...], preferred_element_type=F32)
        logits = (hh_hl[:, :LANES] + hh_hl[:, LANES:]
                  + jnp.dot(hn_lo, wr_ref[:, :LANES], preferred_element_type=F32)) + br_ref[...]
        g_sel, cw = _route(logits)
        lane = lax.broadcasted_iota(jnp.int32, (w, LANES), 1)
        onehot = jnp.where(lane == g_sel, 1.0, 0.0)
        cum = _scan_rows(onehot, jnp.add, 0.0)
        cnt = cum[w - 1:w, :].astype(jnp.int32)
        cnt_pad = ((cnt + (MOE_ALIGN - 1)) // MOE_ALIGN) * MOE_ALIGN
        lane1 = lax.broadcasted_iota(jnp.int32, (1, LANES), 1)
        off = jnp.zeros((1, LANES), jnp.int32)
        for gi in range(N_GROUPS - 1):
            off = off + jnp.where(lane1 > gi, cnt_pad[:, gi:gi + 1], 0)
        n_tiles = (cnt + (MOE_ROW_TILE - 1)) // MOE_ROW_TILE
        for gi in range(N_GROUPS):
            off_sm[gi] = off[0, gi]
            nt_sm[gi] = n_tiles[0, gi]
        pos = jnp.sum(onehot * (off.astype(F32) + cum - 1.0), axis=-1, keepdims=True)
        pos_b = jnp.broadcast_to(pos, (w, LANES))
        pos_sc[...] = pos_b
        pos_row = pos_b.T[0:1, :].astype(jnp.int32)
        cw_packed = _pack_split3(cw)
        for c in range(MOE_SORTED_ROWS // MOE_CHUNK):
            rows = lax.broadcasted_iota(jnp.int32, (MOE_CHUNK, w), 0) + c * MOE_CHUNK
            sel = jnp.where(rows == pos_row, 1.0, 0.0).astype(BF16)
            sl = slice(c * MOE_CHUNK, (c + 1) * MOE_CHUNK)
            xs_sc[sl, :] = jnp.dot(sel, hn_hi, preferred_element_type=F32).astype(BF16)
            cws_sc[sl, :] = _unpack_split3(jnp.dot(sel, cw_packed, preferred_element_type=F32))
        osort_sc[...] = jnp.zeros(osort_sc.shape, BF16)

    row0 = off_sm[s]
    lane_t = lax.broadcasted_iota(jnp.int32, (MOE_ROW_TILE, LANES), 1)
    first_lane = N_GROUPS + s * EXPERTS_PER_STEP

    def tile_body(i, carry):
        r0 = pl.multiple_of(row0 + i * MOE_ROW_TILE, MOE_ALIGN)
        xt = xs_sc[pl.ds(r0, MOE_ROW_TILE), :]
        cwt = cws_sc[pl.ds(r0, MOE_ROW_TILE), :]
        parts = []
        for e in range(EXPERTS_PER_STEP):
            hg = jnp.dot(xt, wg_ref[e], preferred_element_type=F32)
            hu = jnp.dot(xt, wu_ref[e], preferred_element_type=F32)
            col = jnp.sum(jnp.where(lane_t == first_lane + e, cwt, 0.0), axis=-1, keepdims=True)
            parts.append((_silu(hg) * hu * col).astype(BF16))
        he = jnp.concatenate(parts, axis=-1)
        out = jnp.dot(he, wd_ref[...], preferred_element_type=F32)
        osort_sc[pl.ds(r0, MOE_ROW_TILE), :] = out.astype(BF16)
        return carry

    lax.fori_loop(0, nt_sm[s], tile_body, 0)

    @pl.when(s == MOE_STEPS - 1)
    def _():
        osort = osort_sc[...]
        for c in range(w // MOE_CHUNK):
            sl = slice(c * MOE_CHUNK, (c + 1) * MOE_CHUNK)
            pos_col = pos_sc[sl, 0:1].astype(jnp.int32)
            cols = lax.broadcasted_iota(jnp.int32, (MOE_CHUNK, MOE_SORTED_ROWS), 1)
            sel = jnp.where(cols == pos_col, 1.0, 0.0).astype(BF16)
            x2 = x1_ref[sl, :] + jnp.dot(sel, osort, preferred_element_type=F32)
            gate = _sigmoid(_dot(_rms(x2, gple_ref[...]), wpg_ref[...]))
            x3 = x2 + gate * _dot(p_ref[sl, :], wpp_ref[...])
            y_ref[sl, :] = _rms(x3, gfin_ref[...])


def _moe_tail(x1, p, g_ffn, w_r, b_r, wg, wu, wd, g_ple, w_pg, w_pp, g_fin):
    n = x1.shape[0]
    tm = MOE_WINDOW
    row = lambda i, s: (i, 0)
    step = lambda i, s: (s, 0, 0)
    big = pl.BlockSpec((tm, D_MODEL), row)
    sw = EXPERTS_PER_STEP * D_EXPERT
    return pl.pallas_call(
        _moe_tail_kernel,
        grid=(n // tm, MOE_STEPS),
        in_specs=[pl.BlockSpec((tm, D_MODEL), row, pipeline_mode=pl.Buffered(1)),
                  pl.BlockSpec((tm, PLE_DIM), row, pipeline_mode=pl.Buffered(1)),
                  _resident(g_ffn.shape), _resident(w_r.shape),
                  _resident(b_r.shape),
                  pl.BlockSpec((EXPERTS_PER_STEP, D_MODEL, D_EXPERT), step),
                  pl.BlockSpec((EXPERTS_PER_STEP, D_MODEL, D_EXPERT), step),
                  pl.BlockSpec((None, sw, D_MODEL), step),
                  _resident(g_ple.shape), _resident(w_pg.shape), _resident(w_pp.shape), _resident(g_fin.shape)],
        out_specs=big,
        out_shape=jax.ShapeDtypeStruct((n, D_MODEL), F32),
        scratch_shapes=[pltpu.VMEM((MOE_SORTED_ROWS, D_MODEL), BF16), pltpu.VMEM((MOE_SORTED_ROWS, LANES), F32),
                        pltpu.VMEM((MOE_SORTED_ROWS, D_MODEL), BF16), pltpu.VMEM((tm, LANES), F32),
                        pltpu.SMEM((N_GROUPS,), jnp.int32), pltpu.SMEM((N_GROUPS,), jnp.int32)],
        compiler_params=pltpu.CompilerParams(dimension_semantics=("arbitrary", "arbitrary"),
                                             vmem_limit_bytes=VMEM_LIMIT),
        name="moe_tail",
    )(x1, p, g_ffn, w_r, b_r, wg, wu, wd, g_ple, w_pg, w_pp, g_fin)


def _pad_tail(buf):
    return jnp.pad(buf, ((0, 0), (SUBLANES - buf.shape[1], 0), (0, 0)))


UNFUSED_SEQS_PER_STEP = 8


def _group(x, p, mconv0, sconv0, c0, n0, m0, wts, *, tt, tm, fused):
    bsz, t, _ = x.shape
    xf = x.reshape(bsz * t, D_MODEL)
    pf = p.reshape(bsz * t, PLE_DIM)
    m0p = jnp.broadcast_to(jnp.pad(m0, ((0, 0), (0, SUBLANES - M_HEADS)))[:, :, None], (bsz, SUBLANES, LANES))
    if fused:
        x1, c1, n1, m1p, mconv1, sconv1 = _mixer(xf, _pad_tail(mconv0), _pad_tail(sconv0), c0, n0, m0p, wts,
                                                 bsz=bsz, t=t, tt=tt)
    else:
        xm, v, o, ifp, sb, cx, ga, gb = _in_proj(xf, wts["g_mix"], wts["w_main"], wts["w_if"], wts["w_rest"], tm)
        hm, u, c1, n1, m1p, mconv1, sconv1 = _mlstm(
            xm, v, cx, ifp, _pad_tail(mconv0), _pad_tail(sconv0), c0, n0, m0p,
            wts["w_mconv"], wts["w_sconv"], wts["w_q"], wts["w_k"], wts["gate_bias"], wts["g_head"],
            bsz=bsz, t=t, tt=tt, nb=UNFUSED_SEQS_PER_STEP)
        x1 = _merge(xf, hm, o, ga, gb, sb, u, wts["w_a"], wts["w_sout"], wts["w_o"], tm)
    y = _moe_tail(x1, pf, wts["g_ffn"], wts["w_r"], wts["b_r"], wts["w_gate"], wts["w_up"],
                  wts["w_down"], wts["g_ple"], wts["w_ple_gate"], wts["w_ple_proj"], wts["g_final"])
    return (y.reshape(bsz, t, D_MODEL), c1[None], n1[None], m1p[None, :, :M_HEADS, 0],
            mconv1[None, :, SUBLANES - (M_CONV_W - 1):, :], sconv1[None, :, SUBLANES - (S_CONV_W - 1):, :])


def kernel(x_prompt, x_sample, p_prompt, p_sample, state_mlstm_C, state_mlstm_n, state_mlstm_m, state_mlstm_conv, state_sconv, g_mix, w_in, w_mconv, w_q, w_k, b_i, b_f, g_head, w_a, w_sconv, w_sout, w_o, g_ffn, w_rg, b_rg, w_re, b_re, w_gate, w_up, w_down, g_ple, w_ple_gate, w_ple_proj, g_final):
    assert g_mix.shape[0] == 1, "single-layer trunk"
    d = D_MODEL
    w_in0 = w_in[0]
    n_if = 2 * M_HEADS
    w_router = jnp.pad(jnp.concatenate([w_rg[0], w_re[0]], axis=1), ((0, 0), (0, LANES - N_GROUPS - N_EXPERTS)))
    wr_hi = w_router.astype(BF16)
    wr_lo = (w_router - wr_hi.astype(F32)).astype(BF16)
    w_r = jnp.concatenate([wr_hi, wr_lo], axis=1)
    sw = EXPERTS_PER_STEP * D_EXPERT
    wts = {
        "g_mix": g_mix,
        "w_main": w_in0[:, :3 * d].astype(BF16),
        "w_if": jnp.pad(w_in0[:, 3 * d:3 * d + n_if], ((0, 0), (0, LANES - n_if))).astype(BF16),
        "w_rest": w_in0[:, 3 * d + n_if:].astype(BF16),
        "w_mconv": w_mconv[0], "w_sconv": w_sconv[0],
        "w_q": w_q[0].astype(BF16), "w_k": w_k[0].astype(BF16),
        "gate_bias": jnp.pad(jnp.concatenate([b_i[0], b_f[0]])[None, :], ((0, 0), (0, LANES - n_if))),
        "g_head": g_head[0].reshape(1, d),
        "w_a": w_a[0].astype(BF16), "w_sout": w_sout[0].astype(BF16), "w_o": w_o[0].astype(BF16),
        "g_ffn": g_ffn, "w_r": w_r,
        "b_r": jnp.pad(jnp.concatenate([b_rg[0], b_re[0]])[None, :], ((0, 0), (0, LANES - N_GROUPS - N_EXPERTS))),
        "w_gate": w_gate[0].astype(BF16), "w_up": w_up[0].astype(BF16),
        "w_down": w_down[0].astype(BF16).reshape(MOE_STEPS, sw, d),
        "g_ple": g_ple, "w_ple_gate": w_ple_gate[0].astype(BF16), "w_ple_proj": w_ple_proj[0].astype(BF16),
        "g_final": g_final[None, :],
    }
    bp = x_prompt.shape[0]
    zeros = lambda *s: jnp.zeros(s, F32)
    yp, *st_p = _group(
        x_prompt, p_prompt[0], zeros(bp, M_CONV_W - 1, d), zeros(bp, S_CONV_W - 1, d),
        zeros(bp, M_HEADS, M_HEAD_DIM, M_HEAD_DIM), zeros(bp, M_HEADS, M_HEAD_DIM), zeros(bp, M_HEADS),
        wts, tt=256, tm=512, fused=True)
    ys, *st_s = _group(
        x_sample, p_sample[0], state_mlstm_conv[0], state_sconv[0],
        state_mlstm_C[0], state_mlstm_n[0], state_mlstm_m[0],
        wts, tt=x_sample.shape[1], tm=512, fused=False)
    return (yp, ys, *st_p, *st_s)
```

```python
import functools

import jax
import jax.numpy as jnp
from jax import lax
from jax.experimental import pallas as pl
from jax.experimental.pallas import tpu as pltpu

D_MODEL = 1024
M_HEADS = 4
M_HEAD_DIM = 256
M_CONV_W = 4
S_CONV_W = 3
N_GROUPS = 4
EXPERTS_PER_GROUP = 8
N_EXPERTS = N_GROUPS * EXPERTS_PER_GROUP
D_EXPERT = 256
PLE_DIM = 256
RMS_EPS = 1e-6

LANES = 128
SUBLANES = 8
VMEM_LIMIT = 60 * 1024 * 1024

BF16 = jnp.bfloat16
F32 = jnp.float32


def _rms(x, g):
    return x * lax.rsqrt(jnp.mean(x * x, axis=-1, keepdims=True) + RMS_EPS) * g


def _dot(a, b):
    return jnp.dot(a.astype(BF16), b.astype(BF16), preferred_element_type=F32)


def _sigmoid(x):
    return 0.5 * jnp.tanh(0.5 * x) + 0.5


def _silu(x):
    return x * _sigmoid(x)


def _resident(shape):
    nd = len(shape)
    return pl.BlockSpec(shape, lambda *_: (0,) * nd, pipeline_mode=pl.Buffered(1))


def _in_proj_kernel(x_ref, g_ref, w_main_ref, w_if_ref, w_rest_ref,
                    xm_ref, v_ref, o_ref, if_ref, sb_ref, cx_ref, ga_ref, gb_ref):
    h = _rms(x_ref[...], g_ref[...]).astype(BF16)
    d = D_MODEL
    xm_ref[...] = jnp.dot(h, w_main_ref[:, 0:d], preferred_element_type=F32)
    v_ref[...] = jnp.dot(h, w_main_ref[:, d:2 * d], preferred_element_type=F32)
    o_ref[...] = jnp.dot(h, w_main_ref[:, 2 * d:3 * d], preferred_element_type=F32)
    if_ref[...] = jnp.dot(h, w_if_ref[...], preferred_element_type=F32)
    sb_ref[...] = jnp.dot(h, w_rest_ref[:, 0:d], preferred_element_type=F32)
    s_c = jnp.dot(h, w_rest_ref[:, d:2 * d], preferred_element_type=F32)
    s_x = jnp.dot(h, w_rest_ref[:, 2 * d:3 * d], preferred_element_type=F32)
    cx_ref[...] = s_c * s_x
    ga_ref[...] = jnp.dot(h, w_rest_ref[:, 3 * d:4 * d], preferred_element_type=F32)
    gb_ref[...] = jnp.dot(h, w_rest_ref[:, 4 * d:5 * d], preferred_element_type=F32)


def _in_proj(x, g_mix, w_main, w_if, w_rest, tm):
    n = x.shape[0]
    row = lambda i: (i, 0)
    big = pl.BlockSpec((tm, D_MODEL), row)
    outs = [jax.ShapeDtypeStruct((n, D_MODEL), F32)] * 3 + [jax.ShapeDtypeStruct((n, LANES), F32)] \
        + [jax.ShapeDtypeStruct((n, D_MODEL), F32)] * 4
    return pl.pallas_call(
        _in_proj_kernel,
        grid=(n // tm,),
        in_specs=[big, _resident(g_mix.shape), _resident(w_main.shape), _resident(w_if.shape),
                  _resident(w_rest.shape)],
        out_specs=[big, big, big, pl.BlockSpec((tm, LANES), row), big, big, big, big],
        out_shape=outs,
        compiler_params=pltpu.CompilerParams(dimension_semantics=("arbitrary",),
                                             vmem_limit_bytes=VMEM_LIMIT),
        name="in_proj",
    )(x, g_mix, w_main, w_if, w_rest)


def _scan_rows(x, op, identity):
    sub = lax.broadcasted_iota(jnp.int32, (SUBLANES, x.shape[1]), 0)
    blocks, carry = [], None
    for i in range(x.shape[0] // SUBLANES):
        blk = x[i * SUBLANES:(i + 1) * SUBLANES, :]
        for shift in (1, 2, 4):
            blk = op(blk, jnp.where(sub >= shift, pltpu.roll(blk, shift, axis=0), identity))
        if carry is not None:
            blk = op(blk, carry)
        carry = jnp.broadcast_to(blk[SUBLANES - 1:SUBLANES, :], blk.shape)
        blocks.append(blk)
    return jnp.concatenate(blocks, axis=0)


def _log_sigmoid(x):
    return jnp.minimum(x, 0.0) - jnp.log1p(jnp.exp(-jnp.abs(x)))


def _seq_init(mconv0_ref, sconv0_ref, c0_ref, n0_ref, m0_ref, xp_sc, cp_sc, c1_ref, n1_ref, m1_ref):
    @pl.when(pl.program_id(1) == 0)
    def _():
        xp_sc[0:SUBLANES, :] = mconv0_ref[...]
        cp_sc[0:SUBLANES, :] = sconv0_ref[...]
        c1_ref[...] = c0_ref[...]
        n1_ref[...] = n0_ref[...]
        m1_ref[...] = m0_ref[...]


def _causal_conv(src_sc, w_ref, width, tt):
    acc = None
    for j in range(width):
        term = src_sc[pl.ds(SUBLANES - (width - 1) + j, tt), :] * w_ref[j:j + 1, :]
        acc = term if acc is None else acc + term
    return acc


def _carry_tail(src_sc, tail_ref, tt):
    tail = src_sc[tt:tt + SUBLANES, :]
    src_sc[0:SUBLANES, :] = tail
    tail_ref[...] = tail


def _head_slices():
    return [slice(h * M_HEAD_DIM, (h + 1) * M_HEAD_DIM) for h in range(M_HEADS)]


def _mlstm_tile(xq, v, ifp, states, wq_ref, wk_ref, bias_ref, ghead_ref, tt, between=()):
    nh = M_HEADS
    hs = _head_slices()
    seq_rows = [slice(j * tt, (j + 1) * tt) for j in range(len(states))]
    g = [ifp[rows, :] + bias_ref[...] for rows in seq_rows]
    b_c = [_scan_rows(_log_sigmoid(gj), jnp.add, 0.0) for gj in g]
    a_c = [gj - pltpu.roll(bj, LANES - nh, axis=1) for gj, bj in zip(g, b_c)]
    amax_c = [_scan_rows(aj, jnp.maximum, -jnp.inf) for aj in a_c]
    a_t = [aj.T for aj in a_c]

    r_idx = lax.broadcasted_iota(jnp.int32, (tt, tt), 0)
    c_idx = lax.broadcasted_iota(jnp.int32, (tt, tt), 1)
    causal = c_idx <= r_idx

    q_all = [_dot(xq[:, hs[h]], wq_ref[h]) * (M_HEAD_DIM ** -0.5) for h in range(nh)]
    k_all = [_dot(xq[:, hs[h]], wk_ref[h]) for h in range(nh)]
    units = [(j, h) for j in range(len(states)) for h in range(nh)]
    ids = range(len(units))
    q = [q_all[h][seq_rows[j], :] for j, h in units]
    k = [k_all[h][seq_rows[j], :] for j, h in units]
    vv = [v[seq_rows[j], hs[h]] for j, h in units]
    c_prev = [states[j][0][h] for j, h in units]
    n_prev = [states[j][1][h:h + 1, :] for j, h in units]
    m_prev = [states[j][2][h:h + 1, 0:1] for j, h in units]
    b_col = [b_c[j][:, nh + h:nh + h + 1] for j, h in units]
    a_col = [a_c[j][:, h:h + 1] for j, h in units]
    a_row = [a_t[j][h:h + 1, :] for j, h in units]
    b_last = [b_c[j][tt - 1:tt, nh + h:nh + h + 1] for j, h in units]
    mm_col = [jnp.maximum(m_prev[i], amax_c[j][:, h:h + 1]) for i, (j, h) in enumerate(units)]
    mm_last = [jnp.maximum(m_prev[i], amax_c[j][tt - 1:tt, h:h + 1]) for i, (j, h) in enumerate(units)]

    extra = []
    pending = list(between)

    def run_one():
        if pending:
            extra.append(pending.pop(0)())

    w_inter = [jnp.exp(m_prev[i] - mm_col[i]) for i in ids]
    qk = [lax.dot_general(q[i].astype(BF16), k[i].astype(BF16), (((1,), (1,)), ((), ())),
                          preferred_element_type=F32) for i in ids]
    run_one()
    s = [jnp.exp(jnp.where(causal, a_row[i] - mm_col[i], -jnp.inf)) * qk[i] for i in ids]
    run_one()
    num = [w_inter[i] * _dot(q[i], c_prev[i]) + _dot(s[i], vv[i]) for i in ids]
    den = [w_inter[i] * jnp.sum(q[i] * n_prev[i], axis=-1, keepdims=True)
           + jnp.sum(s[i], axis=-1, keepdims=True) for i in ids]
    run_one()
    out = [num[i] * (1.0 / jnp.maximum(jnp.abs(den[i]), jnp.exp(-(b_col[i] + mm_col[i])))) for i in ids]
    out = [out[i] * lax.rsqrt(jnp.mean(out[i] * out[i], axis=-1, keepdims=True) + RMS_EPS)
           * ghead_ref[:, hs[units[i][1]]] for i in ids]
    while pending:
        run_one()

    decay = [jnp.exp(m_prev[i] - mm_last[i]) for i in ids]
    wk = [jnp.exp(a_col[i] - mm_last[i]) * k[i] for i in ids]
    for i, (j, h) in enumerate(units):
        c_ref, n_ref, m_ref = states[j]
        c_ref[h] = decay[i] * c_prev[i] + lax.dot_general(
            wk[i].astype(BF16), vv[i].astype(BF16), (((0,), (0,)), ((), ())), preferred_element_type=F32)
        n_ref[h:h + 1, :] = decay[i] * n_prev[i] + jnp.sum(wk[i], axis=0, keepdims=True)
        m_ref[h:h + 1, :] = jnp.broadcast_to(b_last[i] + mm_last[i], (1, LANES))
    return [out[j * nh:(j + 1) * nh] for j in range(len(states))], extra


def _mlstm_kernel(xm_ref, v_ref, cx_ref, if_ref, mconv0_ref, sconv0_ref, c0_ref, n0_ref, m0_ref,
                  w_mconv_ref, w_sconv_ref, wq_ref, wk_ref, bias_ref, ghead_ref,
                  hm_ref, u_ref, c1_ref, n1_ref, m1_ref, mconv1_ref, sconv1_ref,
                  xp_sc, cp_sc, *, tt, nb):
    hs = _head_slices()
    xq, states = [], []
    for j in range(nb):
        rows = slice(j * tt, (j + 1) * tt)
        xp, cp, state = xp_sc.at[j], cp_sc.at[j], (c1_ref.at[j], n1_ref.at[j], m1_ref.at[j])
        _seq_init(mconv0_ref.at[j], sconv0_ref.at[j], c0_ref.at[j], n0_ref.at[j], m0_ref.at[j], xp, cp, *state)
        xp[SUBLANES:SUBLANES + tt, :] = xm_ref[rows, :]
        cp[SUBLANES:SUBLANES + tt, :] = cx_ref[rows, :]
        xq.append(_silu(_causal_conv(xp, w_mconv_ref, M_CONV_W, tt)))
        u_ref[rows, :] = _causal_conv(cp, w_sconv_ref, S_CONV_W, tt)
        _carry_tail(xp, mconv1_ref.at[j], tt)
        _carry_tail(cp, sconv1_ref.at[j], tt)
        states.append(state)
    xq = xq[0] if nb == 1 else jnp.concatenate(xq, axis=0)
    out, _ = _mlstm_tile(xq, v_ref[...], if_ref[...], states, wq_ref, wk_ref, bias_ref, ghead_ref, tt)
    for j in range(nb):
        for h, sl in enumerate(hs):
            hm_ref[j * tt:(j + 1) * tt, sl] = out[j][h]


def _mlstm(xm, v, cx, ifp, mconv0, sconv0, c0, n0, m0, w_mconv, w_sconv, wq, wk, bias, ghead, *, bsz, t, tt, nb):
    nt = t // tt
    n = bsz * t
    assert nb == 1 or nt == 1, "several sequences per step only when a step covers whole sequences"
    tok = lambda b, i: (b * nt + i, 0)
    seq3 = lambda b, i: (b, 0, 0)
    seq4 = lambda b, i: (b, 0, 0, 0)
    big = pl.BlockSpec((nb * tt, D_MODEL), tok)
    tail = pl.BlockSpec((nb, SUBLANES, D_MODEL), seq3)
    c_spec = pl.BlockSpec((nb, M_HEADS, M_HEAD_DIM, M_HEAD_DIM), seq4)
    n_spec = pl.BlockSpec((nb, M_HEADS, M_HEAD_DIM), seq3)
    m_spec = pl.BlockSpec((nb, SUBLANES, LANES), seq3)
    return pl.pallas_call(
        functools.partial(_mlstm_kernel, tt=tt, nb=nb),
        grid=(bsz // nb, nt),
        in_specs=[big, big, big, pl.BlockSpec((nb * tt, LANES), tok), tail, tail, c_spec, n_spec, m_spec,
                  _resident(w_mconv.shape), _resident(w_sconv.shape), _resident(wq.shape),
                  _resident(wk.shape), _resident(bias.shape), _resident(ghead.shape)],
        out_specs=[big, big, c_spec, n_spec, m_spec, tail, tail],
        out_shape=[jax.ShapeDtypeStruct((n, D_MODEL), F32), jax.ShapeDtypeStruct((n, D_MODEL), F32),
                   jax.ShapeDtypeStruct(c0.shape, F32), jax.ShapeDtypeStruct(n0.shape, F32),
                   jax.ShapeDtypeStruct(m0.shape, F32),
                   jax.ShapeDtypeStruct(mconv0.shape, F32), jax.ShapeDtypeStruct(sconv0.shape, F32)],
        scratch_shapes=[pltpu.VMEM((nb, tt + SUBLANES, D_MODEL), F32),
                        pltpu.VMEM((nb, tt + SUBLANES, D_MODEL), F32)],
        compiler_params=pltpu.CompilerParams(dimension_semantics=("arbitrary", "arbitrary"),
                                             vmem_limit_bytes=VMEM_LIMIT),
        name="mlstm",
    )(xm, v, cx, ifp, mconv0, sconv0, c0, n0, m0, w_mconv, w_sconv, wq, wk, bias, ghead)


def _merge_kernel(x_ref, hm_ref, o_ref, ga_ref, gb_ref, sb_ref, u_ref, wa_ref, wsout_ref, wo_ref, x1_ref):
    y_a = _dot(_sigmoid(o_ref[...]) * hm_ref[...], wa_ref[...])
    y_b = _dot(sb_ref[...] * u_ref[...], wsout_ref[...])
    merged = _sigmoid(ga_ref[...]) * y_a + _sigmoid(gb_ref[...]) * y_b
    x1_ref[...] = x_ref[...] + _dot(merged, wo_ref[...])


def _merge(x, hm, o, ga, gb, sb, u, w_a, w_sout, w_o, tm):
    n = x.shape[0]
    big = pl.BlockSpec((tm, D_MODEL), lambda i: (i, 0))
    return pl.pallas_call(
        _merge_kernel,
        grid=(n // tm,),
        in_specs=[big] * 7 + [_resident(w_a.shape), _resident(w_sout.shape), _resident(w_o.shape)],
        out_specs=big,
        out_shape=jax.ShapeDtypeStruct((n, D_MODEL), F32),
        compiler_params=pltpu.CompilerParams(dimension_semantics=("arbitrary",),
                                             vmem_limit_bytes=VMEM_LIMIT),
        name="merge",
    )(x, hm, o, ga, gb, sb, u, w_a, w_sout, w_o)


def _mixer_kernel(x_ref, mconv0_ref, sconv0_ref, c0_ref, n0_ref, m0_ref,
                  g_ref, w_main_ref, w_if_ref, w_rest_ref, w_mconv_ref, w_sconv_ref, wq_ref, wk_ref,
                  bias_ref, ghead_ref, wa_ref, wsout_ref, wo_ref,
                  x1_ref, c1_ref, n1_ref, m1_ref, mconv1_ref, sconv1_ref,
                  xp_sc, cp_sc, hm_sc, *, tt):
    _seq_init(mconv0_ref, sconv0_ref, c0_ref, n0_ref, m0_ref, xp_sc, cp_sc, c1_ref, n1_ref, m1_ref)
    d = D_MODEL
    h = _rms(x_ref[...], g_ref[...]).astype(BF16)

    def proj(w_ref, j):
        return jnp.dot(h, w_ref[:, j * d:(j + 1) * d], preferred_element_type=F32)

    xp_sc[SUBLANES:SUBLANES + tt, :] = proj(w_main_ref, 0)
    ifp = jnp.dot(h, w_if_ref[...], preferred_element_type=F32)
    cp_sc[SUBLANES:SUBLANES + tt, :] = proj(w_rest_ref, 1) * proj(w_rest_ref, 2)
    xq = _silu(_causal_conv(xp_sc, w_mconv_ref, M_CONV_W, tt))
    u = _causal_conv(cp_sc, w_sconv_ref, S_CONV_W, tt)
    _carry_tail(xp_sc, mconv1_ref, tt)
    _carry_tail(cp_sc, sconv1_ref, tt)
    v = proj(w_main_ref, 1)
    hs = _head_slices()
    (out,), (y_b, sig_o, sig_ga, sig_gb) = _mlstm_tile(
        xq, v, ifp, [(c1_ref, n1_ref, m1_ref)], wq_ref, wk_ref, bias_ref, ghead_ref, tt,
        between=(lambda: _dot(proj(w_rest_ref, 0) * u, wsout_ref[...]),
                 lambda: _sigmoid(proj(w_main_ref, 2)),
                 lambda: _sigmoid(proj(w_rest_ref, 3)),
                 lambda: _sigmoid(proj(w_rest_ref, 4))))
    for hd, sl in enumerate(hs):
        hm_sc[:, sl] = (sig_o[:, sl] * out[hd]).astype(BF16)
    y_a = jnp.dot(hm_sc[...], wa_ref[...], preferred_element_type=F32)
    merged = sig_ga * y_a + sig_gb * y_b
    x1_ref[...] = x_ref[...] + _dot(merged, wo_ref[...])


def _mixer(x, mconv0, sconv0, c0, n0, m0, wts, *, bsz, t, tt):
    nt = t // tt
    tok = lambda b, i: (b * nt + i, 0)
    seq3 = lambda b, i: (b, 0, 0)
    seq4 = lambda b, i: (b, 0, 0, 0)
    big = pl.BlockSpec((tt, D_MODEL), tok)
    tail = pl.BlockSpec((None, SUBLANES, D_MODEL), seq3)
    c_spec = pl.BlockSpec((None, M_HEADS, M_HEAD_DIM, M_HEAD_DIM), seq4)
    n_spec = pl.BlockSpec((None, M_HEADS, M_HEAD_DIM), seq3)
    m_spec = pl.BlockSpec((None, SUBLANES, LANES), seq3)
    names = ("g_mix", "w_main", "w_if", "w_rest", "w_mconv", "w_sconv", "w_q", "w_k", "gate_bias", "g_head",
             "w_a", "w_sout", "w_o")
    weights = [wts[k] for k in names]
    return pl.pallas_call(
        functools.partial(_mixer_kernel, tt=tt),
        grid=(bsz, nt),
        in_specs=[big, tail, tail, c_spec, n_spec, m_spec] + [_resident(w.shape) for w in weights],
        out_specs=[big, c_spec, n_spec, m_spec, tail, tail],
        out_shape=[jax.ShapeDtypeStruct(x.shape, F32), jax.ShapeDtypeStruct(c0.shape, F32),
                   jax.ShapeDtypeStruct(n0.shape, F32), jax.ShapeDtypeStruct(m0.shape, F32),
                   jax.ShapeDtypeStruct(mconv0.shape, F32), jax.ShapeDtypeStruct(sconv0.shape, F32)],
        scratch_shapes=[pltpu.VMEM((tt + SUBLANES, D_MODEL), F32), pltpu.VMEM((tt + SUBLANES, D_MODEL), F32),
                        pltpu.VMEM((tt, D_MODEL), BF16)],
        compiler_params=pltpu.CompilerParams(dimension_semantics=("arbitrary", "arbitrary"),
                                             vmem_limit_bytes=VMEM_LIMIT),
        name="mixer",
    )(x, mconv0, sconv0, c0, n0, m0, *weights)


MOE_WINDOW = 1024
MOE_ROW_TILE = 128
MOE_ALIGN = 16
EXPERTS_PER_STEP = EXPERTS_PER_GROUP
MOE_STEPS = N_EXPERTS // EXPERTS_PER_STEP
MOE_CHUNK = 256
MOE_SORTED_ROWS = -(-(MOE_WINDOW + N_GROUPS * (MOE_ALIGN - 1) + MOE_ROW_TILE) // MOE_CHUNK) * MOE_CHUNK


def _split_bf16(x):
    hi = x.astype(BF16)
    lo = (x - hi.astype(F32)).astype(BF16)
    return hi, lo


CW_LANE_STRIDE = 40


def _pack_split3(cw):
    hi = cw.astype(BF16).astype(F32)
    r1 = cw - hi
    mid = r1.astype(BF16).astype(F32)
    lo = (r1 - mid).astype(BF16).astype(F32)
    return (hi + pltpu.roll(mid, CW_LANE_STRIDE, axis=1) + pltpu.roll(lo, 2 * CW_LANE_STRIDE, axis=1)).astype(BF16)


def _unpack_split3(packed):
    return (packed + pltpu.roll(packed, LANES - CW_LANE_STRIDE, axis=1)
            + pltpu.roll(packed, LANES - 2 * CW_LANE_STRIDE, axis=1))


def _route(logits):
    lane = lax.broadcasted_iota(jnp.int32, logits.shape, 1)
    neg = -jnp.inf
    big = jnp.int32(LANES)
    is_grp = lane < N_GROUPS
    g_max = jnp.max(jnp.where(is_grp, logits, neg), axis=-1, keepdims=True)
    g_sel = jnp.min(jnp.where(is_grp & (logits == g_max), lane, big), axis=-1, keepdims=True)
    p_grp = 1.0 / jnp.sum(jnp.where(is_grp, jnp.exp(logits - g_max), 0.0), axis=-1, keepdims=True)
    lo = N_GROUPS + g_sel * EXPERTS_PER_GROUP
    in_grp = (lane >= lo) & (lane < lo + EXPERTS_PER_GROUP)
    v1 = jnp.max(jnp.where(in_grp, logits, neg), axis=-1, keepdims=True)
    e1 = jnp.min(jnp.where(in_grp & (logits == v1), lane, big), axis=-1, keepdims=True)
    rest = in_grp & (lane != e1)
    v2 = jnp.max(jnp.where(rest, logits, neg), axis=-1, keepdims=True)
    e2 = jnp.min(jnp.where(rest & (logits == v2), lane, big), axis=-1, keepdims=True)
    z = jnp.exp(v2 - v1)
    w1 = p_grp / (1.0 + z)
    w2 = p_grp * z / (1.0 + z)
    return g_sel, jnp.where(lane == e1, w1, jnp.where(lane == e2, w2, 0.0))


def _moe_tail_kernel(x1_ref, p_ref, gffn_ref, wr_ref, br_ref, wg_ref, wu_ref, wd_ref,
                     gple_ref, wpg_ref, wpp_ref, gfin_ref, y_ref,
                     xs_sc, cws_sc, osort_sc, pos_sc, off_sm, nt_sm):
    s = pl.program_id(1)
    w = MOE_WINDOW

    @pl.when(s == 0)
    def _():
        hn = _rms(x1_ref[...], gffn_ref[...])
        hn_hi, hn_lo = _split_bf16(hn)
        hh_hl = jnp.dot(hn_hi, wr_ref[...], preferred_element_type=F32)
        logits = (hh_hl[:, :LANES] + hh_hl[:, LANES:]
                  + jnp.dot(hn_lo, wr_ref[:, :LANES], preferred_element_type=F32)) + br_ref[...]
        g_sel, cw = _route(logits)
        lane = lax.broadcasted_iota(jnp.int32, (w, LANES), 1)
        onehot = jnp.where(lane == g_sel, 1.0, 0.0)
        cum = _scan_rows(onehot, jnp.add, 0.0)
        cnt = cum[w - 1:w, :].astype(jnp.int32)
        cnt_pad = ((cnt + (MOE_ALIGN - 1)) // MOE_ALIGN) * MOE_ALIGN
        lane1 = lax.broadcasted_iota(jnp.int32, (1, LANES), 1)
        off = jnp.zeros((1, LANES), jnp.int32)
        for gi in range(N_GROUPS - 1):
            off = off + jnp.where(lane1 > gi, cnt_pad[:, gi:gi + 1], 0)
        n_tiles = (cnt + (MOE_ROW_TILE - 1)) // MOE_ROW_TILE
        for gi in range(N_GROUPS):
            off_sm[gi] = off[0, gi]
            nt_sm[gi] = n_tiles[0, gi]
        pos = jnp.sum(onehot * (off.astype(F32) + cum - 1.0), axis=-1, keepdims=True)
        pos_b = jnp.broadcast_to(pos, (w, LANES))
        pos_sc[...] = pos_b
        pos_row = pos_b.T[0:1, :].astype(jnp.int32)
        cw_packed = _pack_split3(cw)
        for c in range(MOE_SORTED_ROWS // MOE_CHUNK):
            rows = lax.broadcasted_iota(jnp.int32, (MOE_CHUNK, w), 0) + c * MOE_CHUNK
            sel = jnp.where(rows == pos_row, 1.0, 0.0).astype(BF16)
            sl = slice(c * MOE_CHUNK, (c + 1) * MOE_CHUNK)
            xs_sc[sl, :] = jnp.dot(sel, hn_hi, preferred_element_type=F32).astype(BF16)
            cws_sc[sl, :] = _unpack_split3(jnp.dot(sel, cw_packed, preferred_element_type=F32))
        osort_sc[...] = jnp.zeros(osort_sc.shape, BF16)

    row0 = off_sm[s]
    lane_t = lax.broadcasted_iota(jnp.int32, (MOE_ROW_TILE, LANES), 1)
    first_lane = N_GROUPS + s * EXPERTS_PER_STEP

    def tile_body(i, carry):
        r0 = pl.multiple_of(row0 + i * MOE_ROW_TILE, MOE_ALIGN)
        xt = xs_sc[pl.ds(r0, MOE_ROW_TILE), :]
        cwt = cws_sc[pl.ds(r0, MOE_ROW_TILE), :]
        parts = []
        for e in range(EXPERTS_PER_STEP):
            hg = jnp.dot(xt, wg_ref[e], preferred_element_type=F32)
            hu = jnp.dot(xt, wu_ref[e], preferred_element_type=F32)
            col = jnp.sum(jnp.where(lane_t == first_lane + e, cwt, 0.0), axis=-1, keepdims=True)
            parts.append((_silu(hg) * hu * col).astype(BF16))
        he = jnp.concatenate(parts, axis=-1)
        out = jnp.dot(he, wd_ref[...], preferred_element_type=F32)
        osort_sc[pl.ds(r0, MOE_ROW_TILE), :] = out.astype(BF16)
        return carry

    lax.fori_loop(0, nt_sm[s], tile_body, 0)

    @pl.when(s == MOE_STEPS - 1)
    def _():
        osort = osort_sc[...]
        for c in range(w // MOE_CHUNK):
            sl = slice(c * MOE_CHUNK, (c + 1) * MOE_CHUNK)
            pos_col = pos_sc[sl, 0:1].astype(jnp.int32)
            cols = lax.broadcasted_iota(jnp.int32, (MOE_CHUNK, MOE_SORTED_ROWS), 1)
            sel = jnp.where(cols == pos_col, 1.0, 0.0).astype(BF16)
            x2 = x1_ref[sl, :] + jnp.dot(sel, osort, preferred_element_type=F32)
            gate = _sigmoid(_dot(_rms(x2, gple_ref[...]), wpg_ref[...]))
            x3 = x2 + gate * _dot(p_ref[sl, :], wpp_ref[...])
            y_ref[sl, :] = _rms(x3, gfin_ref[...])


def _moe_tail(x1, p, g_ffn, w_r, b_r, wg, wu, wd, g_ple, w_pg, w_pp, g_fin):
    n = x1.shape[0]
    tm = MOE_WINDOW
    row = lambda i, s: (i, 0)
    step = lambda i, s: (s, 0, 0)
    big = pl.BlockSpec((tm, D_MODEL), row)
    sw = EXPERTS_PER_STEP * D_EXPERT
    return pl.pallas_call(
        _moe_tail_kernel,
        grid=(n // tm, MOE_STEPS),
        in_specs=[big, pl.BlockSpec((tm, PLE_DIM), row), _resident(g_ffn.shape), _resident(w_r.shape),
                  _resident(b_r.shape),
                  pl.BlockSpec((EXPERTS_PER_STEP, D_MODEL, D_EXPERT), step),
                  pl.BlockSpec((EXPERTS_PER_STEP, D_MODEL, D_EXPERT), step),
                  pl.BlockSpec((None, sw, D_MODEL), step),
                  _resident(g_ple.shape), _resident(w_pg.shape), _resident(w_pp.shape), _resident(g_fin.shape)],
        out_specs=big,
        out_shape=jax.ShapeDtypeStruct((n, D_MODEL), F32),
        scratch_shapes=[pltpu.VMEM((MOE_SORTED_ROWS, D_MODEL), BF16), pltpu.VMEM((MOE_SORTED_ROWS, LANES), F32),
                        pltpu.VMEM((MOE_SORTED_ROWS, D_MODEL), BF16), pltpu.VMEM((tm, LANES), F32),
                        pltpu.SMEM((N_GROUPS,), jnp.int32), pltpu.SMEM((N_GROUPS,), jnp.int32)],
        compiler_params=pltpu.CompilerParams(dimension_semantics=("arbitrary", "arbitrary"),
                                             vmem_limit_bytes=VMEM_LIMIT),
        name="moe_tail",
    )(x1, p, g_ffn, w_r, b_r, wg, wu, wd, g_ple, w_pg, w_pp, g_fin)


def _pad_tail(buf):
    return jnp.pad(buf, ((0, 0), (SUBLANES - buf.shape[1], 0), (0, 0)))


UNFUSED_SEQS_PER_STEP = 8


def _group(x, p, mconv0, sconv0, c0, n0, m0, wts, *, tt, tm, fused):
    bsz, t, _ = x.shape
    xf = x.reshape(bsz * t, D_MODEL)
    pf = p.reshape(bsz * t, PLE_DIM)
    m0p = jnp.broadcast_to(jnp.pad(m0, ((0, 0), (0, SUBLANES - M_HEADS)))[:, :, None], (bsz, SUBLANES, LANES))
    if fused:
        x1, c1, n1, m1p, mconv1, sconv1 = _mixer(xf, _pad_tail(mconv0), _pad_tail(sconv0), c0, n0, m0p, wts,
                                                 bsz=bsz, t=t, tt=tt)
    else:
        xm, v, o, ifp, sb, cx, ga, gb = _in_proj(xf, wts["g_mix"], wts["w_main"], wts["w_if"], wts["w_rest"], tm)
        hm, u, c1, n1, m1p, mconv1, sconv1 = _mlstm(
            xm, v, cx, ifp, _pad_tail(mconv0), _pad_tail(sconv0), c0, n0, m0p,
            wts["w_mconv"], wts["w_sconv"], wts["w_q"], wts["w_k"], wts["gate_bias"], wts["g_head"],
            bsz=bsz, t=t, tt=tt, nb=UNFUSED_SEQS_PER_STEP)
        x1 = _merge(xf, hm, o, ga, gb, sb, u, wts["w_a"], wts["w_sout"], wts["w_o"], tm)
    y = _moe_tail(x1, pf, wts["g_ffn"], wts["w_r"], wts["b_r"], wts["w_gate"], wts["w_up"],
                  wts["w_down"], wts["g_ple"], wts["w_ple_gate"], wts["w_ple_proj"], wts["g_final"])
    return (y.reshape(bsz, t, D_MODEL), c1[None], n1[None], m1p[None, :, :M_HEADS, 0],
            mconv1[None, :, SUBLANES - (M_CONV_W - 1):, :], sconv1[None, :, SUBLANES - (S_CONV_W - 1):, :])


def kernel(x_prompt, x_sample, p_prompt, p_sample, state_mlstm_C, state_mlstm_n, state_mlstm_m, state_mlstm_conv, state_sconv, g_mix, w_in, w_mconv, w_q, w_k, b_i, b_f, g_head, w_a, w_sconv, w_sout, w_o, g_ffn, w_rg, b_rg, w_re, b_re, w_gate, w_up, w_down, g_ple, w_ple_gate, w_ple_proj, g_final):
    assert g_mix.shape[0] == 1, "single-layer trunk"
    d = D_MODEL
    w_in0 = w_in[0]
    n_if = 2 * M_HEADS
    w_router = jnp.pad(jnp.concatenate([w_rg[0], w_re[0]], axis=1), ((0, 0), (0, LANES - N_GROUPS - N_EXPERTS)))
    wr_hi = w_router.astype(BF16)
    wr_lo = (w_router - wr_hi.astype(F32)).astype(BF16)
    w_r = jnp.concatenate([wr_hi, wr_lo], axis=1)
    sw = EXPERTS_PER_STEP * D_EXPERT
    wts = {
        "g_mix": g_mix,
        "w_main": w_in0[:, :3 * d].astype(BF16),
        "w_if": jnp.pad(w_in0[:, 3 * d:3 * d + n_if], ((0, 0), (0, LANES - n_if))).astype(BF16),
        "w_rest": w_in0[:, 3 * d + n_if:].astype(BF16),
        "w_mconv": w_mconv[0], "w_sconv": w_sconv[0],
        "w_q": w_q[0].astype(BF16), "w_k": w_k[0].astype(BF16),
        "gate_bias": jnp.pad(jnp.concatenate([b_i[0], b_f[0]])[None, :], ((0, 0), (0, LANES - n_if))),
        "g_head": g_head[0].reshape(1, d),
        "w_a": w_a[0].astype(BF16), "w_sout": w_sout[0].astype(BF16), "w_o": w_o[0].astype(BF16),
        "g_ffn": g_ffn, "w_r": w_r,
        "b_r": jnp.pad(jnp.concatenate([b_rg[0], b_re[0]])[None, :], ((0, 0), (0, LANES - N_GROUPS - N_EXPERTS))),
        "w_gate": w_gate[0].astype(BF16), "w_up": w_up[0].astype(BF16),
        "w_down": w_down[0].astype(BF16).reshape(MOE_STEPS, sw, d),
        "g_ple": g_ple, "w_ple_gate": w_ple_gate[0].astype(BF16), "w_ple_proj": w_ple_proj[0].astype(BF16),
        "g_final": g_final[None, :],
    }
    bp = x_prompt.shape[0]
    zeros = lambda *s: jnp.zeros(s, F32)
    yp, *st_p = _group(
        x_prompt, p_prompt[0], zeros(bp, M_CONV_W - 1, d), zeros(bp, S_CONV_W - 1, d),
        zeros(bp, M_HEADS, M_HEAD_DIM, M_HEAD_DIM), zeros(bp, M_HEADS, M_HEAD_DIM), zeros(bp, M_HEADS),
        wts, tt=256, tm=512, fused=True)
    ys, *st_s = _group(
        x_sample, p_sample[0], state_mlstm_conv[0], state_sconv[0],
        state_mlstm_C[0], state_mlstm_n[0], state_mlstm_m[0],
        wts, tt=x_sample.shape[1], tm=512, fused=False)
    return (yp, ys, *st_p, *st_s)
```

```python
import functools

import jax
import jax.numpy as jnp
from jax import lax
from jax.experimental import pallas as pl
from jax.experimental.pallas import tpu as pltpu

D_MODEL = 1024
M_HEADS = 4
M_HEAD_DIM = 256
M_CONV_W = 4
S_CONV_W = 3
N_GROUPS = 4
EXPERTS_PER_GROUP = 8
N_EXPERTS = N_GROUPS * EXPERTS_PER_GROUP
D_EXPERT = 256
PLE_DIM = 256
RMS_EPS = 1e-6

LANES = 128
SUBLANES = 8
VMEM_LIMIT = 60 * 1024 * 1024

BF16 = jnp.bfloat16
F32 = jnp.float32


def _rms(x, g):
    return x * lax.rsqrt(jnp.mean(x * x, axis=-1, keepdims=True) + RMS_EPS) * g


def _dot(a, b):
    return jnp.dot(a.astype(BF16), b.astype(BF16), preferred_element_type=F32)


def _sigmoid(x):
    return 0.5 * jnp.tanh(0.5 * x) + 0.5


def _silu(x):
    return x * _sigmoid(x)


def _resident(shape):
    nd = len(shape)
    return pl.BlockSpec(shape, lambda *_: (0,) * nd, pipeline_mode=pl.Buffered(1))


def _in_proj_kernel(x_ref, g_ref, w_main_ref, w_if_ref, w_rest_ref,
                    xm_ref, v_ref, o_ref, if_ref, sb_ref, cx_ref, ga_ref, gb_ref):
    h = _rms(x_ref[...], g_ref[...]).astype(BF16)
    d = D_MODEL
    xm_ref[...] = jnp.dot(h, w_main_ref[:, 0:d], preferred_element_type=F32)
    v_ref[...] = jnp.dot(h, w_main_ref[:, d:2 * d], preferred_element_type=F32)
    o_ref[...] = jnp.dot(h, w_main_ref[:, 2 * d:3 * d], preferred_element_type=F32)
    if_ref[...] = jnp.dot(h, w_if_ref[...], preferred_element_type=F32)
    sb_ref[...] = jnp.dot(h, w_rest_ref[:, 0:d], preferred_element_type=F32)
    s_c = jnp.dot(h, w_rest_ref[:, d:2 * d], preferred_element_type=F32)
    s_x = jnp.dot(h, w_rest_ref[:, 2 * d:3 * d], preferred_element_type=F32)
    cx_ref[...] = s_c * s_x
    ga_ref[...] = jnp.dot(h, w_rest_ref[:, 3 * d:4 * d], preferred_element_type=F32)
    gb_ref[...] = jnp.dot(h, w_rest_ref[:, 4 * d:5 * d], preferred_element_type=F32)


def _in_proj(x, g_mix, w_main, w_if, w_rest, tm):
    n = x.shape[0]
    row = lambda i: (i, 0)
    big = pl.BlockSpec((tm, D_MODEL), row)
    outs = [jax.ShapeDtypeStruct((n, D_MODEL), F32)] * 3 + [jax.ShapeDtypeStruct((n, LANES), F32)] \
        + [jax.ShapeDtypeStruct((n, D_MODEL), F32)] * 4
    return pl.pallas_call(
        _in_proj_kernel,
        grid=(n // tm,),
        in_specs=[big, _resident(g_mix.shape), _resident(w_main.shape), _resident(w_if.shape),
                  _resident(w_rest.shape)],
        out_specs=[big, big, big, pl.BlockSpec((tm, LANES), row), big, big, big, big],
        out_shape=outs,
        compiler_params=pltpu.CompilerParams(dimension_semantics=("arbitrary",),
                                             vmem_limit_bytes=VMEM_LIMIT),
        name="in_proj",
    )(x, g_mix, w_main, w_if, w_rest)


def _scan_rows(x, op, identity):
    sub = lax.broadcasted_iota(jnp.int32, (SUBLANES, x.shape[1]), 0)
    blocks, carry = [], None
    for i in range(x.shape[0] // SUBLANES):
        blk = x[i * SUBLANES:(i + 1) * SUBLANES, :]
        for shift in (1, 2, 4):
            blk = op(blk, jnp.where(sub >= shift, pltpu.roll(blk, shift, axis=0), identity))
        if carry is not None:
            blk = op(blk, carry)
        carry = jnp.broadcast_to(blk[SUBLANES - 1:SUBLANES, :], blk.shape)
        blocks.append(blk)
    return jnp.concatenate(blocks, axis=0)


def _log_sigmoid(x):
    return jnp.minimum(x, 0.0) - jnp.log1p(jnp.exp(-jnp.abs(x)))


def _seq_init(mconv0_ref, sconv0_ref, c0_ref, n0_ref, m0_ref, xp_sc, cp_sc, c1_ref, n1_ref, m1_ref):
    @pl.when(pl.program_id(1) == 0)
    def _():
        xp_sc[0:SUBLANES, :] = mconv0_ref[...]
        cp_sc[0:SUBLANES, :] = sconv0_ref[...]
        c1_ref[...] = c0_ref[...]
        n1_ref[...] = n0_ref[...]
        m1_ref[...] = m0_ref[...]


def _causal_conv(src_sc, w_ref, width, tt):
    acc = None
    for j in range(width):
        term = src_sc[pl.ds(SUBLANES - (width - 1) + j, tt), :] * w_ref[j:j + 1, :]
        acc = term if acc is None else acc + term
    return acc


def _carry_tail(src_sc, tail_ref, tt):
    tail = src_sc[tt:tt + SUBLANES, :]
    src_sc[0:SUBLANES, :] = tail
    tail_ref[...] = tail


def _head_slices():
    return [slice(h * M_HEAD_DIM, (h + 1) * M_HEAD_DIM) for h in range(M_HEADS)]


def _mlstm_tile(xq, v, ifp, states, wq_ref, wk_ref, bias_ref, ghead_ref, tt, between=()):
    nh = M_HEADS
    hs = _head_slices()
    seq_rows = [slice(j * tt, (j + 1) * tt) for j in range(len(states))]
    g = [ifp[rows, :] + bias_ref[...] for rows in seq_rows]
    b_c = [_scan_rows(_log_sigmoid(gj), jnp.add, 0.0) for gj in g]
    a_c = [gj - pltpu.roll(bj, LANES - nh, axis=1) for gj, bj in zip(g, b_c)]
    amax_c = [_scan_rows(aj, jnp.maximum, -jnp.inf) for aj in a_c]
    a_t = [aj.T for aj in a_c]

    r_idx = lax.broadcasted_iota(jnp.int32, (tt, tt), 0)
    c_idx = lax.broadcasted_iota(jnp.int32, (tt, tt), 1)
    causal = c_idx <= r_idx

    q_all = [_dot(xq[:, hs[h]], wq_ref[h]) * (M_HEAD_DIM ** -0.5) for h in range(nh)]
    k_all = [_dot(xq[:, hs[h]], wk_ref[h]) for h in range(nh)]
    units = [(j, h) for j in range(len(states)) for h in range(nh)]
    ids = range(len(units))
    q = [q_all[h][seq_rows[j], :] for j, h in units]
    k = [k_all[h][seq_rows[j], :] for j, h in units]
    vv = [v[seq_rows[j], hs[h]] for j, h in units]
    c_prev = [states[j][0][h] for j, h in units]
    n_prev = [states[j][1][h:h + 1, :] for j, h in units]
    m_prev = [states[j][2][h:h + 1, 0:1] for j, h in units]
    b_col = [b_c[j][:, nh + h:nh + h + 1] for j, h in units]
    a_col = [a_c[j][:, h:h + 1] for j, h in units]
    a_row = [a_t[j][h:h + 1, :] for j, h in units]
    b_last = [b_c[j][tt - 1:tt, nh + h:nh + h + 1] for j, h in units]
    mm_col = [jnp.maximum(m_prev[i], amax_c[j][:, h:h + 1]) for i, (j, h) in enumerate(units)]
    mm_last = [jnp.maximum(m_prev[i], amax_c[j][tt - 1:tt, h:h + 1]) for i, (j, h) in enumerate(units)]

    extra = []
    pending = list(between)

    def run_one():
        if pending:
            extra.append(pending.pop(0)())

    w_inter = [jnp.exp(m_prev[i] - mm_col[i]) for i in ids]
    qk = [lax.dot_general(q[i].astype(BF16), k[i].astype(BF16), (((1,), (1,)), ((), ())),
                          preferred_element_type=F32) for i in ids]
    run_one()
    s = [jnp.exp(jnp.where(causal, a_row[i] - mm_col[i], -jnp.inf)) * qk[i] for i in ids]
    run_one()
    num = [w_inter[i] * _dot(q[i], c_prev[i]) + _dot(s[i], vv[i]) for i in ids]
    den = [w_inter[i] * jnp.sum(q[i] * n_prev[i], axis=-1, keepdims=True)
           + jnp.sum(s[i], axis=-1, keepdims=True) for i in ids]
    run_one()
    out = [num[i] * (1.0 / jnp.maximum(jnp.abs(den[i]), jnp.exp(-(b_col[i] + mm_col[i])))) for i in ids]
    out = [out[i] * lax.rsqrt(jnp.mean(out[i] * out[i], axis=-1, keepdims=True) + RMS_EPS)
           * ghead_ref[:, hs[units[i][1]]] for i in ids]
    while pending:
        run_one()

    decay = [jnp.exp(m_prev[i] - mm_last[i]) for i in ids]
    wk = [jnp.exp(a_col[i] - mm_last[i]) * k[i] for i in ids]
    for i, (j, h) in enumerate(units):
        c_ref, n_ref, m_ref = states[j]
        c_ref[h] = decay[i] * c_prev[i] + lax.dot_general(
            wk[i].astype(BF16), vv[i].astype(BF16), (((0,), (0,)), ((), ())), preferred_element_type=F32)
        n_ref[h:h + 1, :] = decay[i] * n_prev[i] + jnp.sum(wk[i], axis=0, keepdims=True)
        m_ref[h:h + 1, :] = jnp.broadcast_to(b_last[i] + mm_last[i], (1, LANES))
    return [out[j * nh:(j + 1) * nh] for j in range(len(states))], extra


def _mlstm_kernel(xm_ref, v_ref, cx_ref, if_ref, mconv0_ref, sconv0_ref, c0_ref, n0_ref, m0_ref,
                  w_mconv_ref, w_sconv_ref, wq_ref, wk_ref, bias_ref, ghead_ref,
                  hm_ref, u_ref, c1_ref, n1_ref, m1_ref, mconv1_ref, sconv1_ref,
                  xp_sc, cp_sc, *, tt, nb):
    hs = _head_slices()
    xq, states = [], []
    for j in range(nb):
        rows = slice(j * tt, (j + 1) * tt)
        xp, cp, state = xp_sc.at[j], cp_sc.at[j], (c1_ref.at[j], n1_ref.at[j], m1_ref.at[j])
        _seq_init(mconv0_ref.at[j], sconv0_ref.at[j], c0_ref.at[j], n0_ref.at[j], m0_ref.at[j], xp, cp, *state)
        xp[SUBLANES:SUBLANES + tt, :] = xm_ref[rows, :]
        cp[SUBLANES:SUBLANES + tt, :] = cx_ref[rows, :]
        xq.append(_silu(_causal_conv(xp, w_mconv_ref, M_CONV_W, tt)))
        u_ref[rows, :] = _causal_conv(cp, w_sconv_ref, S_CONV_W, tt)
        _carry_tail(xp, mconv1_ref.at[j], tt)
        _carry_tail(cp, sconv1_ref.at[j], tt)
        states.append(state)
    xq = xq[0] if nb == 1 else jnp.concatenate(xq, axis=0)
    out, _ = _mlstm_tile(xq, v_ref[...], if_ref[...], states, wq_ref, wk_ref, bias_ref, ghead_ref, tt)
    for j in range(nb):
        for h, sl in enumerate(hs):
            hm_ref[j * tt:(j + 1) * tt, sl] = out[j][h]


def _mlstm(xm, v, cx, ifp, mconv0, sconv0, c0, n0, m0, w_mconv, w_sconv, wq, wk, bias, ghead, *, bsz, t, tt, nb):
    nt = t // tt
    n = bsz * t
    assert nb == 1 or nt == 1, "several sequences per step only when a step covers whole sequences"
    tok = lambda b, i: (b * nt + i, 0)
    seq3 = lambda b, i: (b, 0, 0)
    seq4 = lambda b, i: (b, 0, 0, 0)
    big = pl.BlockSpec((nb * tt, D_MODEL), tok)
    tail = pl.BlockSpec((nb, SUBLANES, D_MODEL), seq3)
    c_spec = pl.BlockSpec((nb, M_HEADS, M_HEAD_DIM, M_HEAD_DIM), seq4)
    n_spec = pl.BlockSpec((nb, M_HEADS, M_HEAD_DIM), seq3)
    m_spec = pl.BlockSpec((nb, SUBLANES, LANES), seq3)
    return pl.pallas_call(
        functools.partial(_mlstm_kernel, tt=tt, nb=nb),
        grid=(bsz // nb, nt),
        in_specs=[big, big, big, pl.BlockSpec((nb * tt, LANES), tok), tail, tail, c_spec, n_spec, m_spec,
                  _resident(w_mconv.shape), _resident(w_sconv.shape), _resident(wq.shape),
                  _resident(wk.shape), _resident(bias.shape), _resident(ghead.shape)],
        out_specs=[big, big, c_spec, n_spec, m_spec, tail, tail],
        out_shape=[jax.ShapeDtypeStruct((n, D_MODEL), F32), jax.ShapeDtypeStruct((n, D_MODEL), F32),
                   jax.ShapeDtypeStruct(c0.shape, F32), jax.ShapeDtypeStruct(n0.shape, F32),
                   jax.ShapeDtypeStruct(m0.shape, F32),
                   jax.ShapeDtypeStruct(mconv0.shape, F32), jax.ShapeDtypeStruct(sconv0.shape, F32)],
        scratch_shapes=[pltpu.VMEM((nb, tt + SUBLANES, D_MODEL), F32),
                        pltpu.VMEM((nb, tt + SUBLANES, D_MODEL), F32)],
        compiler_params=pltpu.CompilerParams(dimension_semantics=("arbitrary", "arbitrary"),
                                             vmem_limit_bytes=VMEM_LIMIT),
        name="mlstm",
    )(xm, v, cx, ifp, mconv0, sconv0, c0, n0, m0, w_mconv, w_sconv, wq, wk, bias, ghead)


def _merge_kernel(x_ref, hm_ref, o_ref, ga_ref, gb_ref, sb_ref, u_ref, wa_ref, wsout_ref, wo_ref, x1_ref):
    y_a = _dot(_sigmoid(o_ref[...]) * hm_ref[...], wa_ref[...])
    y_b = _dot(sb_ref[...] * u_ref[...], wsout_ref[...])
    merged = _sigmoid(ga_ref[...]) * y_a + _sigmoid(gb_ref[...]) * y_b
    x1_ref[...] = x_ref[...] + _dot(merged, wo_ref[...])


def _merge(x, hm, o, ga, gb, sb, u, w_a, w_sout, w_o, tm):
    n = x.shape[0]
    big = pl.BlockSpec((tm, D_MODEL), lambda i: (i, 0))
    return pl.pallas_call(
        _merge_kernel,
        grid=(n // tm,),
        in_specs=[big] * 7 + [_resident(w_a.shape), _resident(w_sout.shape), _resident(w_o.shape)],
        out_specs=big,
        out_shape=jax.ShapeDtypeStruct((n, D_MODEL), F32),
        compiler_params=pltpu.CompilerParams(dimension_semantics=("arbitrary",),
                                             vmem_limit_bytes=VMEM_LIMIT),
        name="merge",
    )(x, hm, o, ga, gb, sb, u, w_a, w_sout, w_o)


def _mixer_kernel(x_ref, mconv0_ref, sconv0_ref, c0_ref, n0_ref, m0_ref,
                  g_ref, w_main_ref, w_if_ref, w_rest_ref, w_mconv_ref, w_sconv_ref, wq_ref, wk_ref,
                  bias_ref, ghead_ref, wa_ref, wsout_ref, wo_ref,
                  x1_ref, c1_ref, n1_ref, m1_ref, mconv1_ref, sconv1_ref,
                  xp_sc, cp_sc, hm_sc, *, tt):
    _seq_init(mconv0_ref, sconv0_ref, c0_ref, n0_ref, m0_ref, xp_sc, cp_sc, c1_ref, n1_ref, m1_ref)
    d = D_MODEL
    h = _rms(x_ref[...], g_ref[...]).astype(BF16)

    def proj(w_ref, j):
        return jnp.dot(h, w_ref[:, j * d:(j + 1) * d], preferred_element_type=F32)

    xp_sc[SUBLANES:SUBLANES + tt, :] = proj(w_main_ref, 0)
    ifp = jnp.dot(h, w_if_ref[...], preferred_element_type=F32)
    cp_sc[SUBLANES:SUBLANES + tt, :] = proj(w_rest_ref, 1) * proj(w_rest_ref, 2)
    xq = _silu(_causal_conv(xp_sc, w_mconv_ref, M_CONV_W, tt))
    u = _causal_conv(cp_sc, w_sconv_ref, S_CONV_W, tt)
    _carry_tail(xp_sc, mconv1_ref, tt)
    _carry_tail(cp_sc, sconv1_ref, tt)
    v = proj(w_main_ref, 1)
    hs = _head_slices()
    (out,), (y_b, sig_o, sig_ga, sig_gb) = _mlstm_tile(
        xq, v, ifp, [(c1_ref, n1_ref, m1_ref)], wq_ref, wk_ref, bias_ref, ghead_ref, tt,
        between=(lambda: _dot(proj(w_rest_ref, 0) * u, wsout_ref[...]),
                 lambda: _sigmoid(proj(w_main_ref, 2)),
                 lambda: _sigmoid(proj(w_rest_ref, 3)),
                 lambda: _sigmoid(proj(w_rest_ref, 4))))
    for hd, sl in enumerate(hs):
        hm_sc[:, sl] = (sig_o[:, sl] * out[hd]).astype(BF16)
    y_a = jnp.dot(hm_sc[...], wa_ref[...], preferred_element_type=F32)
    merged = sig_ga * y_a + sig_gb * y_b
    x1_ref[...] = x_ref[...] + _dot(merged, wo_ref[...])


def _mixer(x, mconv0, sconv0, c0, n0, m0, wts, *, bsz, t, tt):
    nt = t // tt
    tok = lambda b, i: (b * nt + i, 0)
    seq3 = lambda b, i: (b, 0, 0)
    seq4 = lambda b, i: (b, 0, 0, 0)
    big = pl.BlockSpec((tt, D_MODEL), tok)
    tail = pl.BlockSpec((None, SUBLANES, D_MODEL), seq3)
    c_spec = pl.BlockSpec((None, M_HEADS, M_HEAD_DIM, M_HEAD_DIM), seq4)
    n_spec = pl.BlockSpec((None, M_HEADS, M_HEAD_DIM), seq3)
    m_spec = pl.BlockSpec((None, SUBLANES, LANES), seq3)
    names = ("g_mix", "w_main", "w_if", "w_rest", "w_mconv", "w_sconv", "w_q", "w_k", "gate_bias", "g_head",
             "w_a", "w_sout", "w_o")
    weights = [wts[k] for k in names]
    return pl.pallas_call(
        functools.partial(_mixer_kernel, tt=tt),
        grid=(bsz, nt),
        in_specs=[big, tail, tail, c_spec, n_spec, m_spec] + [_resident(w.shape) for w in weights],
        out_specs=[big, c_spec, n_spec, m_spec, tail, tail],
        out_shape=[jax.ShapeDtypeStruct(x.shape, F32), jax.ShapeDtypeStruct(c0.shape, F32),
                   jax.ShapeDtypeStruct(n0.shape, F32), jax.ShapeDtypeStruct(m0.shape, F32),
                   jax.ShapeDtypeStruct(mconv0.shape, F32), jax.ShapeDtypeStruct(sconv0.shape, F32)],
        scratch_shapes=[pltpu.VMEM((tt + SUBLANES, D_MODEL), F32), pltpu.VMEM((tt + SUBLANES, D_MODEL), F32),
                        pltpu.VMEM((tt, D_MODEL), BF16)],
        compiler_params=pltpu.CompilerParams(dimension_semantics=("arbitrary", "arbitrary"),
                                             vmem_limit_bytes=VMEM_LIMIT),
        name="mixer",
    )(x, mconv0, sconv0, c0, n0, m0, *weights)


MOE_WINDOW = 1024
MOE_ROW_TILE = 128
MOE_ALIGN = 16
EXPERTS_PER_STEP = EXPERTS_PER_GROUP
MOE_STEPS = N_EXPERTS // EXPERTS_PER_STEP
MOE_CHUNK = 256
MOE_SORT_CHUNK = 1280
MOE_UNSORT_CHUNK = 512
MOE_SORTED_ROWS = -(-(MOE_WINDOW + N_GROUPS * (MOE_ALIGN - 1) + MOE_ROW_TILE) // MOE_CHUNK) * MOE_CHUNK


def _split_bf16(x):
    hi = x.astype(BF16)
    lo = (x - hi.astype(F32)).astype(BF16)
    return hi, lo


CW_LANE_STRIDE = 40


def _pack_split3(cw):
    hi = cw.astype(BF16).astype(F32)
    r1 = cw - hi
    mid = r1.astype(BF16).astype(F32)
    lo = (r1 - mid).astype(BF16).astype(F32)
    return (hi + pltpu.roll(mid, CW_LANE_STRIDE, axis=1) + pltpu.roll(lo, 2 * CW_LANE_STRIDE, axis=1)).astype(BF16)


def _unpack_split3(packed):
    return (packed + pltpu.roll(packed, LANES - CW_LANE_STRIDE, axis=1)
            + pltpu.roll(packed, LANES - 2 * CW_LANE_STRIDE, axis=1))


def _route(logits):
    lane = lax.broadcasted_iota(jnp.int32, logits.shape, 1)
    neg = -jnp.inf
    big = jnp.int32(LANES)
    is_grp = lane < N_GROUPS
    g_max = jnp.max(jnp.where(is_grp, logits, neg), axis=-1, keepdims=True)
    g_sel = jnp.min(jnp.where(is_grp & (logits == g_max), lane, big), axis=-1, keepdims=True)
    p_grp = 1.0 / jnp.sum(jnp.where(is_grp, jnp.exp(logits - g_max), 0.0), axis=-1, keepdims=True)
    lo = N_GROUPS + g_sel * EXPERTS_PER_GROUP
    in_grp = (lane >= lo) & (lane < lo + EXPERTS_PER_GROUP)
    v1 = jnp.max(jnp.where(in_grp, logits, neg), axis=-1, keepdims=True)
    e1 = jnp.min(jnp.where(in_grp & (logits == v1), lane, big), axis=-1, keepdims=True)
    rest = in_grp & (lane != e1)
    v2 = jnp.max(jnp.where(rest, logits, neg), axis=-1, keepdims=True)
    e2 = jnp.min(jnp.where(rest & (logits == v2), lane, big), axis=-1, keepdims=True)
    z = jnp.exp(v2 - v1)
    w1 = p_grp / (1.0 + z)
    w2 = p_grp * z / (1.0 + z)
    return g_sel, jnp.where(lane == e1, w1, jnp.where(lane == e2, w2, 0.0))


def _moe_tail_kernel(x1_ref, p_ref, gffn_ref, wr_ref, br_ref, wg_ref, wu_ref, wd_ref,
                     gple_ref, wpg_ref, wpp_ref, gfin_ref, y_ref,
                     xs_sc, cws_sc, osort_sc, pos_sc, off_sm, nt_sm):
    s = pl.program_id(1)
    w = MOE_WINDOW

    @pl.when(s == 0)
    def _():
        hn = _rms(x1_ref[...], gffn_ref[...])
        hn_hi, hn_lo = _split_bf16(hn)
        hh_hl = jnp.dot(hn_hi, wr_ref[...], preferred_element_type=F32)
        logits = (hh_hl[:, :LANES] + hh_hl[:, LANES:]
                  + jnp.dot(hn_lo, wr_ref[:, :LANES], preferred_element_type=F32)) + br_ref[...]
        g_sel, cw = _route(logits)
        lane = lax.broadcasted_iota(jnp.int32, (w, LANES), 1)
        onehot = jnp.where(lane == g_sel, 1.0, 0.0)
        cum = _scan_rows(onehot, jnp.add, 0.0)
        cnt = cum[w - 1:w, :].astype(jnp.int32)
        cnt_pad = ((cnt + (MOE_ALIGN - 1)) // MOE_ALIGN) * MOE_ALIGN
        lane1 = lax.broadcasted_iota(jnp.int32, (1, LANES), 1)
        off = jnp.zeros((1, LANES), jnp.int32)
        for gi in range(N_GROUPS - 1):
            off = off + jnp.where(lane1 > gi, cnt_pad[:, gi:gi + 1], 0)
        n_tiles = (cnt + (MOE_ROW_TILE - 1)) // MOE_ROW_TILE
        for gi in range(N_GROUPS):
            off_sm[gi] = off[0, gi]
            nt_sm[gi] = n_tiles[0, gi]
        pos = jnp.sum(onehot * (off.astype(F32) + cum - 1.0), axis=-1, keepdims=True)
        pos_b = jnp.broadcast_to(pos, (w, LANES))
        pos_sc[...] = pos_b
        pos_row = pos_b.T[0:1, :].astype(jnp.int32)
        cw_packed = _pack_split3(cw)
        for c in range(MOE_SORTED_ROWS // MOE_SORT_CHUNK):
            rows = lax.broadcasted_iota(jnp.int32, (MOE_SORT_CHUNK, w), 0) + c * MOE_SORT_CHUNK
            sel = jnp.where(rows == pos_row, 1.0, 0.0).astype(BF16)
            sl = slice(c * MOE_SORT_CHUNK, (c + 1) * MOE_SORT_CHUNK)
            xs_sc[sl, :] = jnp.dot(sel, hn_hi, preferred_element_type=F32).astype(BF16)
            cws_sc[sl, :] = _unpack_split3(jnp.dot(sel, cw_packed, preferred_element_type=F32))
        osort_sc[...] = jnp.zeros(osort_sc.shape, BF16)

    row0 = off_sm[s]
    lane_t = lax.broadcasted_iota(jnp.int32, (MOE_ROW_TILE, LANES), 1)
    first_lane = N_GROUPS + s * EXPERTS_PER_STEP

    def tile_body(i, carry):
        r0 = pl.multiple_of(row0 + i * MOE_ROW_TILE, MOE_ALIGN)
        xt = xs_sc[pl.ds(r0, MOE_ROW_TILE), :]
        cwt = cws_sc[pl.ds(r0, MOE_ROW_TILE), :]
        parts = []
        for e in range(EXPERTS_PER_STEP):
            hg = jnp.dot(xt, wg_ref[e], preferred_element_type=F32)
            hu = jnp.dot(xt, wu_ref[e], preferred_element_type=F32)
            col = jnp.sum(jnp.where(lane_t == first_lane + e, cwt, 0.0), axis=-1, keepdims=True)
            parts.append((_silu(hg) * hu * col).astype(BF16))
        he = jnp.concatenate(parts, axis=-1)
        out = jnp.dot(he, wd_ref[...], preferred_element_type=F32)
        osort_sc[pl.ds(r0, MOE_ROW_TILE), :] = out.astype(BF16)
        return carry

    lax.fori_loop(0, nt_sm[s], tile_body, 0)

    @pl.when(s == MOE_STEPS - 1)
    def _():
        osort = osort_sc[...]
        for c in range(w // MOE_UNSORT_CHUNK):
            sl = slice(c * MOE_UNSORT_CHUNK, (c + 1) * MOE_UNSORT_CHUNK)
            pos_col = pos_sc[sl, 0:1].astype(jnp.int32)
            cols = lax.broadcasted_iota(jnp.int32, (MOE_UNSORT_CHUNK, MOE_SORTED_ROWS), 1)
            sel = jnp.where(cols == pos_col, 1.0, 0.0).astype(BF16)
            x2 = x1_ref[sl, :] + jnp.dot(sel, osort, preferred_element_type=F32)
            gate = _sigmoid(_dot(_rms(x2, gple_ref[...]), wpg_ref[...]))
            x3 = x2 + gate * _dot(p_ref[sl, :], wpp_ref[...])
            y_ref[sl, :] = _rms(x3, gfin_ref[...])


def _moe_tail(x1, p, g_ffn, w_r, b_r, wg, wu, wd, g_ple, w_pg, w_pp, g_fin):
    n = x1.shape[0]
    tm = MOE_WINDOW
    row = lambda i, s: (i, 0)
    step = lambda i, s: (s, 0, 0)
    big = pl.BlockSpec((tm, D_MODEL), row)
    sw = EXPERTS_PER_STEP * D_EXPERT
    return pl.pallas_call(
        _moe_tail_kernel,
        grid=(n // tm, MOE_STEPS),
        in_specs=[big, pl.BlockSpec((tm, PLE_DIM), row), _resident(g_ffn.shape), _resident(w_r.shape),
                  _resident(b_r.shape),
                  pl.BlockSpec((EXPERTS_PER_STEP, D_MODEL, D_EXPERT), step),
                  pl.BlockSpec((EXPERTS_PER_STEP, D_MODEL, D_EXPERT), step),
                  pl.BlockSpec((None, sw, D_MODEL), step),
                  _resident(g_ple.shape), _resident(w_pg.shape), _resident(w_pp.shape), _resident(g_fin.shape)],
        out_specs=big,
        out_shape=jax.ShapeDtypeStruct((n, D_MODEL), F32),
        scratch_shapes=[pltpu.VMEM((MOE_SORTED_ROWS, D_MODEL), BF16), pltpu.VMEM((MOE_SORTED_ROWS, LANES), F32),
                        pltpu.VMEM((MOE_SORTED_ROWS, D_MODEL), BF16), pltpu.VMEM((tm, LANES), F32),
                        pltpu.SMEM((N_GROUPS,), jnp.int32), pltpu.SMEM((N_GROUPS,), jnp.int32)],
        compiler_params=pltpu.CompilerParams(dimension_semantics=("arbitrary", "arbitrary"),
                                             vmem_limit_bytes=VMEM_LIMIT),
        name="moe_tail",
    )(x1, p, g_ffn, w_r, b_r, wg, wu, wd, g_ple, w_pg, w_pp, g_fin)


def _pad_tail(buf):
    return jnp.pad(buf, ((0, 0), (SUBLANES - buf.shape[1], 0), (0, 0)))


UNFUSED_SEQS_PER_STEP = 8


def _group(x, p, mconv0, sconv0, c0, n0, m0, wts, *, tt, tm, fused):
    bsz, t, _ = x.shape
    xf = x.reshape(bsz * t, D_MODEL)
    pf = p.reshape(bsz * t, PLE_DIM)
    m0p = jnp.broadcast_to(jnp.pad(m0, ((0, 0), (0, SUBLANES - M_HEADS)))[:, :, None], (bsz, SUBLANES, LANES))
    if fused:
        x1, c1, n1, m1p, mconv1, sconv1 = _mixer(xf, _pad_tail(mconv0), _pad_tail(sconv0), c0, n0, m0p, wts,
                                                 bsz=bsz, t=t, tt=tt)
    else:
        xm, v, o, ifp, sb, cx, ga, gb = _in_proj(xf, wts["g_mix"], wts["w_main"], wts["w_if"], wts["w_rest"], tm)
        hm, u, c1, n1, m1p, mconv1, sconv1 = _mlstm(
            xm, v, cx, ifp, _pad_tail(mconv0), _pad_tail(sconv0), c0, n0, m0p,
            wts["w_mconv"], wts["w_sconv"], wts["w_q"], wts["w_k"], wts["gate_bias"], wts["g_head"],
            bsz=bsz, t=t, tt=tt, nb=UNFUSED_SEQS_PER_STEP)
        x1 = _merge(xf, hm, o, ga, gb, sb, u, wts["w_a"], wts["w_sout"], wts["w_o"], tm)
    y = _moe_tail(x1, pf, wts["g_ffn"], wts["w_r"], wts["b_r"], wts["w_gate"], wts["w_up"],
                  wts["w_down"], wts["g_ple"], wts["w_ple_gate"], wts["w_ple_proj"], wts["g_final"])
    return (y.reshape(bsz, t, D_MODEL), c1[None], n1[None], m1p[None, :, :M_HEADS, 0],
            mconv1[None, :, SUBLANES - (M_CONV_W - 1):, :], sconv1[None, :, SUBLANES - (S_CONV_W - 1):, :])


def kernel(x_prompt, x_sample, p_prompt, p_sample, state_mlstm_C, state_mlstm_n, state_mlstm_m, state_mlstm_conv, state_sconv, g_mix, w_in, w_mconv, w_q, w_k, b_i, b_f, g_head, w_a, w_sconv, w_sout, w_o, g_ffn, w_rg, b_rg, w_re, b_re, w_gate, w_up, w_down, g_ple, w_ple_gate, w_ple_proj, g_final):
    assert g_mix.shape[0] == 1, "single-layer trunk"
    d = D_MODEL
    w_in0 = w_in[0]
    n_if = 2 * M_HEADS
    w_router = jnp.pad(jnp.concatenate([w_rg[0], w_re[0]], axis=1), ((0, 0), (0, LANES - N_GROUPS - N_EXPERTS)))
    wr_hi = w_router.astype(BF16)
    wr_lo = (w_router - wr_hi.astype(F32)).astype(BF16)
    w_r = jnp.concatenate([wr_hi, wr_lo], axis=1)
    sw = EXPERTS_PER_STEP * D_EXPERT
    wts = {
        "g_mix": g_mix,
        "w_main": w_in0[:, :3 * d].astype(BF16),
        "w_if": jnp.pad(w_in0[:, 3 * d:3 * d + n_if], ((0, 0), (0, LANES - n_if))).astype(BF16),
        "w_rest": w_in0[:, 3 * d + n_if:].astype(BF16),
        "w_mconv": w_mconv[0], "w_sconv": w_sconv[0],
        "w_q": w_q[0].astype(BF16), "w_k": w_k[0].astype(BF16),
        "gate_bias": jnp.pad(jnp.concatenate([b_i[0], b_f[0]])[None, :], ((0, 0), (0, LANES - n_if))),
        "g_head": g_head[0].reshape(1, d),
        "w_a": w_a[0].astype(BF16), "w_sout": w_sout[0].astype(BF16), "w_o": w_o[0].astype(BF16),
        "g_ffn": g_ffn, "w_r": w_r,
        "b_r": jnp.pad(jnp.concatenate([b_rg[0], b_re[0]])[None, :], ((0, 0), (0, LANES - N_GROUPS - N_EXPERTS))),
        "w_gate": w_gate[0].astype(BF16), "w_up": w_up[0].astype(BF16),
        "w_down": w_down[0].astype(BF16).reshape(MOE_STEPS, sw, d),
        "g_ple": g_ple, "w_ple_gate": w_ple_gate[0].astype(BF16), "w_ple_proj": w_ple_proj[0].astype(BF16),
        "g_final": g_final[None, :],
    }
    bp = x_prompt.shape[0]
    zeros = lambda *s: jnp.zeros(s, F32)
    yp, *st_p = _group(
        x_prompt, p_prompt[0], zeros(bp, M_CONV_W - 1, d), zeros(bp, S_CONV_W - 1, d),
        zeros(bp, M_HEADS, M_HEAD_DIM, M_HEAD_DIM), zeros(bp, M_HEADS, M_HEAD_DIM), zeros(bp, M_HEADS),
        wts, tt=256, tm=512, fused=True)
    ys, *st_s = _group(
        x_sample, p_sample[0], state_mlstm_conv[0], state_sconv[0],
        state_mlstm_C[0], state_mlstm_n[0], state_mlstm_m[0],
        wts, tt=x_sample.shape[1], tm=512, fused=False)
    return (yp, ys, *st_p, *st_s)
```

```python
import functools

import jax
import jax.numpy as jnp
from jax import lax
from jax.experimental import pallas as pl
from jax.experimental.pallas import tpu as pltpu

D_MODEL = 1024
M_HEADS = 4
M_HEAD_DIM = 256
M_CONV_W = 4
S_CONV_W = 3
N_GROUPS = 4
EXPERTS_PER_GROUP = 8
N_EXPERTS = N_GROUPS * EXPERTS_PER_GROUP
D_EXPERT = 256
PLE_DIM = 256
RMS_EPS = 1e-6

LANES = 128
SUBLANES = 8
VMEM_LIMIT = 60 * 1024 * 1024

BF16 = jnp.bfloat16
F32 = jnp.float32


def _rms(x, g):
    return x * lax.rsqrt(jnp.mean(x * x, axis=-1, keepdims=True) + RMS_EPS) * g


def _dot(a, b):
    return jnp.dot(a.astype(BF16), b.astype(BF16), preferred_element_type=F32)


def _sigmoid(x):
    return 0.5 * jnp.tanh(0.5 * x) + 0.5


def _silu(x):
    return x * _sigmoid(x)


def _resident(shape):
    nd = len(shape)
    return pl.BlockSpec(shape, lambda *_: (0,) * nd, pipeline_mode=pl.Buffered(1))


def _in_proj_kernel(x_ref, g_ref, w_main_ref, w_if_ref, w_rest_ref,
                    xm_ref, v_ref, o_ref, if_ref, sb_ref, cx_ref, ga_ref, gb_ref):
    h = _rms(x_ref[...], g_ref[...]).astype(BF16)
    d = D_MODEL
    xm_ref[...] = jnp.dot(h, w_main_ref[:, 0:d], preferred_element_type=F32)
    v_ref[...] = jnp.dot(h, w_main_ref[:, d:2 * d], preferred_element_type=F32)
    o_ref[...] = jnp.dot(h, w_main_ref[:, 2 * d:3 * d], preferred_element_type=F32)
    if_ref[...] = jnp.dot(h, w_if_ref[...], preferred_element_type=F32)
    sb_ref[...] = jnp.dot(h, w_rest_ref[:, 0:d], preferred_element_type=F32)
    s_c = jnp.dot(h, w_rest_ref[:, d:2 * d], preferred_element_type=F32)
    s_x = jnp.dot(h, w_rest_ref[:, 2 * d:3 * d], preferred_element_type=F32)
    cx_ref[...] = s_c * s_x
    ga_ref[...] = jnp.dot(h, w_rest_ref[:, 3 * d:4 * d], preferred_element_type=F32)
    gb_ref[...] = jnp.dot(h, w_rest_ref[:, 4 * d:5 * d], preferred_element_type=F32)


def _in_proj(x, g_mix, w_main, w_if, w_rest, tm):
    n = x.shape[0]
    row = lambda i: (i, 0)
    big = pl.BlockSpec((tm, D_MODEL), row)
    outs = [jax.ShapeDtypeStruct((n, D_MODEL), F32)] * 3 + [jax.ShapeDtypeStruct((n, LANES), F32)] \
        + [jax.ShapeDtypeStruct((n, D_MODEL), F32)] * 4
    return pl.pallas_call(
        _in_proj_kernel,
        grid=(n // tm,),
        in_specs=[big, _resident(g_mix.shape), _resident(w_main.shape), _resident(w_if.shape),
                  _resident(w_rest.shape)],
        out_specs=[big, big, big, pl.BlockSpec((tm, LANES), row), big, big, big, big],
        out_shape=outs,
        compiler_params=pltpu.CompilerParams(dimension_semantics=("arbitrary",),
                                             vmem_limit_bytes=VMEM_LIMIT),
        name="in_proj",
    )(x, g_mix, w_main, w_if, w_rest)


def _scan_rows(x, op, identity):
    sub = lax.broadcasted_iota(jnp.int32, (SUBLANES, x.shape[1]), 0)
    blocks, carry = [], None
    for i in range(x.shape[0] // SUBLANES):
        blk = x[i * SUBLANES:(i + 1) * SUBLANES, :]
        for shift in (1, 2, 4):
            blk = op(blk, jnp.where(sub >= shift, pltpu.roll(blk, shift, axis=0), identity))
        if carry is not None:
            blk = op(blk, carry)
        carry = jnp.broadcast_to(blk[SUBLANES - 1:SUBLANES, :], blk.shape)
        blocks.append(blk)
    return jnp.concatenate(blocks, axis=0)


def _log_sigmoid(x):
    return jnp.minimum(x, 0.0) - jnp.log1p(jnp.exp(-jnp.abs(x)))


def _seq_init(mconv0_ref, sconv0_ref, c0_ref, n0_ref, m0_ref, xp_sc, cp_sc, c1_ref, n1_ref, m1_ref):
    @pl.when(pl.program_id(1) == 0)
    def _():
        xp_sc[0:SUBLANES, :] = mconv0_ref[...]
        cp_sc[0:SUBLANES, :] = sconv0_ref[...]
        c1_ref[...] = c0_ref[...]
        n1_ref[...] = n0_ref[...]
        m1_ref[...] = m0_ref[...]


def _causal_conv(src_sc, w_ref, width, tt):
    acc = None
    for j in range(width):
        term = src_sc[pl.ds(SUBLANES - (width - 1) + j, tt), :] * w_ref[j:j + 1, :]
        acc = term if acc is None else acc + term
    return acc


def _carry_tail(src_sc, tail_ref, tt):
    tail = src_sc[tt:tt + SUBLANES, :]
    src_sc[0:SUBLANES, :] = tail
    tail_ref[...] = tail


def _head_slices():
    return [slice(h * M_HEAD_DIM, (h + 1) * M_HEAD_DIM) for h in range(M_HEADS)]


def _mlstm_tile(xq, v, ifp, states, wq_ref, wk_ref, bias_ref, ghead_ref, tt, between=()):
    nh = M_HEADS
    hs = _head_slices()
    seq_rows = [slice(j * tt, (j + 1) * tt) for j in range(len(states))]
    g = [ifp[rows, :] + bias_ref[...] for rows in seq_rows]
    b_c = [_scan_rows(_log_sigmoid(gj), jnp.add, 0.0) for gj in g]
    a_c = [gj - pltpu.roll(bj, LANES - nh, axis=1) for gj, bj in zip(g, b_c)]
    amax_c = [_scan_rows(aj, jnp.maximum, -jnp.inf) for aj in a_c]
    a_t = [aj.T for aj in a_c]

    r_idx = lax.broadcasted_iota(jnp.int32, (tt, tt), 0)
    c_idx = lax.broadcasted_iota(jnp.int32, (tt, tt), 1)
    causal = c_idx <= r_idx

    q_all = [_dot(xq[:, hs[h]], wq_ref[h]) * (M_HEAD_DIM ** -0.5) for h in range(nh)]
    k_all = [_dot(xq[:, hs[h]], wk_ref[h]) for h in range(nh)]
    units = [(j, h) for j in range(len(states)) for h in range(nh)]
    ids = range(len(units))
    q = [q_all[h][seq_rows[j], :] for j, h in units]
    k = [k_all[h][seq_rows[j], :] for j, h in units]
    vv = [v[seq_rows[j], hs[h]] for j, h in units]
    c_prev = [states[j][0][h] for j, h in units]
    n_prev = [states[j][1][h:h + 1, :] for j, h in units]
    m_prev = [states[j][2][h:h + 1, 0:1] for j, h in units]
    b_col = [b_c[j][:, nh + h:nh + h + 1] for j, h in units]
    a_col = [a_c[j][:, h:h + 1] for j, h in units]
    a_row = [a_t[j][h:h + 1, :] for j, h in units]
    b_last = [b_c[j][tt - 1:tt, nh + h:nh + h + 1] for j, h in units]
    mm_col = [jnp.maximum(m_prev[i], amax_c[j][:, h:h + 1]) for i, (j, h) in enumerate(units)]
    mm_last = [jnp.maximum(m_prev[i], amax_c[j][tt - 1:tt, h:h + 1]) for i, (j, h) in enumerate(units)]

    extra = []
    pending = list(between)

    def run_one():
        if pending:
            extra.append(pending.pop(0)())

    w_inter = [jnp.exp(m_prev[i] - mm_col[i]) for i in ids]
    qk = [lax.dot_general(q[i].astype(BF16), k[i].astype(BF16), (((1,), (1,)), ((), ())),
                          preferred_element_type=F32) for i in ids]
    run_one()
    s = [jnp.exp(jnp.where(causal, a_row[i] - mm_col[i], -jnp.inf)) * qk[i] for i in ids]
    run_one()
    num = [w_inter[i] * _dot(q[i], c_prev[i]) + _dot(s[i], vv[i]) for i in ids]
    den = [w_inter[i] * jnp.sum(q[i] * n_prev[i], axis=-1, keepdims=True)
           + jnp.sum(s[i], axis=-1, keepdims=True) for i in ids]
    run_one()
    out = [num[i] * (1.0 / jnp.maximum(jnp.abs(den[i]), jnp.exp(-(b_col[i] + mm_col[i])))) for i in ids]
    out = [out[i] * lax.rsqrt(jnp.mean(out[i] * out[i], axis=-1, keepdims=True) + RMS_EPS)
           * ghead_ref[:, hs[units[i][1]]] for i in ids]
    while pending:
        run_one()

    decay = [jnp.exp(m_prev[i] - mm_last[i]) for i in ids]
    wk = [jnp.exp(a_col[i] - mm_last[i]) * k[i] for i in ids]
    for i, (j, h) in enumerate(units):
        c_ref, n_ref, m_ref = states[j]
        c_ref[h] = decay[i] * c_prev[i] + lax.dot_general(
            wk[i].astype(BF16), vv[i].astype(BF16), (((0,), (0,)), ((), ())), preferred_element_type=F32)
        n_ref[h:h + 1, :] = decay[i] * n_prev[i] + jnp.sum(wk[i], axis=0, keepdims=True)
        m_ref[h:h + 1, :] = jnp.broadcast_to(b_last[i] + mm_last[i], (1, LANES))
    return [out[j * nh:(j + 1) * nh] for j in range(len(states))], extra


def _mlstm_kernel(xm_ref, v_ref, cx_ref, if_ref, mconv0_ref, sconv0_ref, c0_ref, n0_ref, m0_ref,
                  w_mconv_ref, w_sconv_ref, wq_ref, wk_ref, bias_ref, ghead_ref,
                  hm_ref, u_ref, c1_ref, n1_ref, m1_ref, mconv1_ref, sconv1_ref,
                  xp_sc, cp_sc, *, tt, nb):
    hs = _head_slices()
    xq, states = [], []
    for j in range(nb):
        rows = slice(j * tt, (j + 1) * tt)
        xp, cp, state = xp_sc.at[j], cp_sc.at[j], (c1_ref.at[j], n1_ref.at[j], m1_ref.at[j])
        _seq_init(mconv0_ref.at[j], sconv0_ref.at[j], c0_ref.at[j], n0_ref.at[j], m0_ref.at[j], xp, cp, *state)
        xp[SUBLANES:SUBLANES + tt, :] = xm_ref[rows, :]
        cp[SUBLANES:SUBLANES + tt, :] = cx_ref[rows, :]
        xq.append(_silu(_causal_conv(xp, w_mconv_ref, M_CONV_W, tt)))
        u_ref[rows, :] = _causal_conv(cp, w_sconv_ref, S_CONV_W, tt)
        _carry_tail(xp, mconv1_ref.at[j], tt)
        _carry_tail(cp, sconv1_ref.at[j], tt)
        states.append(state)
    xq = xq[0] if nb == 1 else jnp.concatenate(xq, axis=0)
    out, _ = _mlstm_tile(xq, v_ref[...], if_ref[...], states, wq_ref, wk_ref, bias_ref, ghead_ref, tt)
    for j in range(nb):
        for h, sl in enumerate(hs):
            hm_ref[j * tt:(j + 1) * tt, sl] = out[j][h]


def _mlstm(xm, v, cx, ifp, mconv0, sconv0, c0, n0, m0, w_mconv, w_sconv, wq, wk, bias, ghead, *, bsz, t, tt, nb):
    nt = t // tt
    n = bsz * t
    assert nb == 1 or nt == 1, "several sequences per step only when a step covers whole sequences"
    tok = lambda b, i: (b * nt + i, 0)
    seq3 = lambda b, i: (b, 0, 0)
    seq4 = lambda b, i: (b, 0, 0, 0)
    big = pl.BlockSpec((nb * tt, D_MODEL), tok)
    tail = pl.BlockSpec((nb, SUBLANES, D_MODEL), seq3)
    c_spec = pl.BlockSpec((nb, M_HEADS, M_HEAD_DIM, M_HEAD_DIM), seq4)
    n_spec = pl.BlockSpec((nb, M_HEADS, M_HEAD_DIM), seq3)
    m_spec = pl.BlockSpec((nb, SUBLANES, LANES), seq3)
    return pl.pallas_call(
        functools.partial(_mlstm_kernel, tt=tt, nb=nb),
        grid=(bsz // nb, nt),
        in_specs=[big, big, big, pl.BlockSpec((nb * tt, LANES), tok), tail, tail, c_spec, n_spec, m_spec,
                  _resident(w_mconv.shape), _resident(w_sconv.shape), _resident(wq.shape),
                  _resident(wk.shape), _resident(bias.shape), _resident(ghead.shape)],
        out_specs=[big, big, c_spec, n_spec, m_spec, tail, tail],
        out_shape=[jax.ShapeDtypeStruct((n, D_MODEL), F32), jax.ShapeDtypeStruct((n, D_MODEL), F32),
                   jax.ShapeDtypeStruct(c0.shape, F32), jax.ShapeDtypeStruct(n0.shape, F32),
                   jax.ShapeDtypeStruct(m0.shape, F32),
                   jax.ShapeDtypeStruct(mconv0.shape, F32), jax.ShapeDtypeStruct(sconv0.shape, F32)],
        scratch_shapes=[pltpu.VMEM((nb, tt + SUBLANES, D_MODEL), F32),
                        pltpu.VMEM((nb, tt + SUBLANES, D_MODEL), F32)],
        compiler_params=pltpu.CompilerParams(dimension_semantics=("arbitrary", "arbitrary"),
                                             vmem_limit_bytes=VMEM_LIMIT),
        name="mlstm",
    )(xm, v, cx, ifp, mconv0, sconv0, c0, n0, m0, w_mconv, w_sconv, wq, wk, bias, ghead)


def _merge_kernel(x_ref, hm_ref, o_ref, ga_ref, gb_ref, sb_ref, u_ref, wa_ref, wsout_ref, wo_ref, x1_ref):
    y_a = _dot(_sigmoid(o_ref[...]) * hm_ref[...], wa_ref[...])
    y_b = _dot(sb_ref[...] * u_ref[...], wsout_ref[...])
    merged = _sigmoid(ga_ref[...]) * y_a + _sigmoid(gb_ref[...]) * y_b
    x1_ref[...] = x_ref[...] + _dot(merged, wo_ref[...])


def _merge(x, hm, o, ga, gb, sb, u, w_a, w_sout, w_o, tm):
    n = x.shape[0]
    big = pl.BlockSpec((tm, D_MODEL), lambda i: (i, 0))
    return pl.pallas_call(
        _merge_kernel,
        grid=(n // tm,),
        in_specs=[big] * 7 + [_resident(w_a.shape), _resident(w_sout.shape), _resident(w_o.shape)],
        out_specs=big,
        out_shape=jax.ShapeDtypeStruct((n, D_MODEL), F32),
        compiler_params=pltpu.CompilerParams(dimension_semantics=("arbitrary",),
                                             vmem_limit_bytes=VMEM_LIMIT),
        name="merge",
    )(x, hm, o, ga, gb, sb, u, w_a, w_sout, w_o)


def _mixer_kernel(x_ref, g_ref, w_main_ref, w_if_ref, w_rest_ref, w_mconv_ref, w_sconv_ref, wq_ref, wk_ref,
                  bias_ref, ghead_ref, wa_ref, wsout_ref, wo_ref, wg32_ref, wu32_ref, wd32_ref,
                  x1_ref, c1_ref, n1_ref, m1_ref, mconv1_ref, sconv1_ref, wg16_ref, wu16_ref, wd16_ref,
                  xp_sc, cp_sc, hm_sc, *, tt):
    @pl.when(pl.program_id(1) == 0)
    def _():
        xp_sc[0:SUBLANES, :] = jnp.zeros((SUBLANES, D_MODEL), F32)
        cp_sc[0:SUBLANES, :] = jnp.zeros((SUBLANES, D_MODEL), F32)
        c1_ref[...] = jnp.zeros(c1_ref.shape, F32)
        n1_ref[...] = jnp.zeros(n1_ref.shape, F32)
        m1_ref[...] = jnp.zeros(m1_ref.shape, F32)

    wg16_ref[...] = wg32_ref[...].astype(BF16)
    wu16_ref[...] = wu32_ref[...].astype(BF16)
    wd16_ref[...] = wd32_ref[...].astype(BF16)

    d = D_MODEL
    h = _rms(x_ref[...], g_ref[...]).astype(BF16)

    def proj(w_ref, j):
        return jnp.dot(h, w_ref[:, j * d:(j + 1) * d], preferred_element_type=F32)

    xp_sc[SUBLANES:SUBLANES + tt, :] = proj(w_main_ref, 0)
    ifp = jnp.dot(h, w_if_ref[...], preferred_element_type=F32)
    cp_sc[SUBLANES:SUBLANES + tt, :] = proj(w_rest_ref, 1) * proj(w_rest_ref, 2)
    xq = _silu(_causal_conv(xp_sc, w_mconv_ref, M_CONV_W, tt))
    u = _causal_conv(cp_sc, w_sconv_ref, S_CONV_W, tt)
    _carry_tail(xp_sc, mconv1_ref, tt)
    _carry_tail(cp_sc, sconv1_ref, tt)
    v = proj(w_main_ref, 1)
    hs = _head_slices()
    (out,), (y_b, sig_o, sig_ga, sig_gb) = _mlstm_tile(
        xq, v, ifp, [(c1_ref, n1_ref, m1_ref)], wq_ref, wk_ref, bias_ref, ghead_ref, tt,
        between=(lambda: _dot(proj(w_rest_ref, 0) * u, wsout_ref[...]),
                 lambda: _sigmoid(proj(w_main_ref, 2)),
                 lambda: _sigmoid(proj(w_rest_ref, 3)),
                 lambda: _sigmoid(proj(w_rest_ref, 4))))
    for hd, sl in enumerate(hs):
        hm_sc[:, sl] = (sig_o[:, sl] * out[hd]).astype(BF16)
    y_a = jnp.dot(hm_sc[...], wa_ref[...], preferred_element_type=F32)
    merged = sig_ga * y_a + sig_gb * y_b
    x1_ref[...] = x_ref[...] + _dot(merged, wo_ref[...])


def _mixer(x, wts, experts_f32, *, bsz, t, tt):
    nt = t // tt
    steps = bsz * nt
    tok = lambda b, i: (b * nt + i, 0)
    seq3 = lambda b, i: (b, 0, 0)
    seq4 = lambda b, i: (b, 0, 0, 0)
    big = pl.BlockSpec((tt, D_MODEL), tok)
    tail = pl.BlockSpec((None, SUBLANES, D_MODEL), seq3)
    c_spec = pl.BlockSpec((None, M_HEADS, M_HEAD_DIM, M_HEAD_DIM), seq4)
    n_spec = pl.BlockSpec((None, M_HEADS, M_HEAD_DIM), seq3)
    m_spec = pl.BlockSpec((None, SUBLANES, LANES), seq3)
    names = ("g_mix", "w_main", "w_if", "w_rest", "w_mconv", "w_sconv", "w_q", "w_k", "gate_bias", "g_head",
             "w_a", "w_sout", "w_o")
    weights = [wts[k] for k in names]
    flat = [w.reshape(-1, w.shape[-1]) for w in experts_f32]
    cast_specs = [pl.BlockSpec((w.shape[0] // steps, w.shape[1]), tok) for w in flat]
    outs = pl.pallas_call(
        functools.partial(_mixer_kernel, tt=tt),
        grid=(bsz, nt),
        in_specs=[big] + [_resident(w.shape) for w in weights] + cast_specs,
        out_specs=[big, c_spec, n_spec, m_spec, tail, tail] + cast_specs,
        out_shape=[jax.ShapeDtypeStruct(x.shape, F32),
                   jax.ShapeDtypeStruct((bsz, M_HEADS, M_HEAD_DIM, M_HEAD_DIM), F32),
                   jax.ShapeDtypeStruct((bsz, M_HEADS, M_HEAD_DIM), F32),
                   jax.ShapeDtypeStruct((bsz, SUBLANES, LANES), F32),
                   jax.ShapeDtypeStruct((bsz, SUBLANES, D_MODEL), F32),
                   jax.ShapeDtypeStruct((bsz, SUBLANES, D_MODEL), F32)]
                  + [jax.ShapeDtypeStruct(w.shape, BF16) for w in flat],
        scratch_shapes=[pltpu.VMEM((tt + SUBLANES, D_MODEL), F32), pltpu.VMEM((tt + SUBLANES, D_MODEL), F32),
                        pltpu.VMEM((tt, D_MODEL), BF16)],
        compiler_params=pltpu.CompilerParams(dimension_semantics=("arbitrary", "arbitrary"),
                                             vmem_limit_bytes=VMEM_LIMIT),
        name="mixer",
    )(x, *weights, *flat)
    return outs[:6], [o.reshape(w.shape) for o, w in zip(outs[6:], experts_f32)]


MOE_WINDOW = 1024
MOE_ROW_TILE = 128
MOE_ALIGN = 16
EXPERTS_PER_STEP = EXPERTS_PER_GROUP
MOE_STEPS = N_EXPERTS // EXPERTS_PER_STEP
MOE_CHUNK = 256
MOE_SORT_CHUNK = 1280
MOE_UNSORT_CHUNK = 512
MOE_SORTED_ROWS = -(-(MOE_WINDOW + N_GROUPS * (MOE_ALIGN - 1) + MOE_ROW_TILE) // MOE_CHUNK) * MOE_CHUNK


def _split_bf16(x):
    hi = x.astype(BF16)
    lo = (x - hi.astype(F32)).astype(BF16)
    return hi, lo


CW_LANE_STRIDE = 40


def _pack_split3(cw):
    hi = cw.astype(BF16).astype(F32)
    r1 = cw - hi
    mid = r1.astype(BF16).astype(F32)
    lo = (r1 - mid).astype(BF16).astype(F32)
    return (hi + pltpu.roll(mid, CW_LANE_STRIDE, axis=1) + pltpu.roll(lo, 2 * CW_LANE_STRIDE, axis=1)).astype(BF16)


def _unpack_split3(packed):
    return (packed + pltpu.roll(packed, LANES - CW_LANE_STRIDE, axis=1)
            + pltpu.roll(packed, LANES - 2 * CW_LANE_STRIDE, axis=1))


def _route(logits):
    lane = lax.broadcasted_iota(jnp.int32, logits.shape, 1)
    neg = -jnp.inf
    big = jnp.int32(LANES)
    is_grp = lane < N_GROUPS
    g_max = jnp.max(jnp.where(is_grp, logits, neg), axis=-1, keepdims=True)
    g_sel = jnp.min(jnp.where(is_grp & (logits == g_max), lane, big), axis=-1, keepdims=True)
    p_grp = 1.0 / jnp.sum(jnp.where(is_grp, jnp.exp(logits - g_max), 0.0), axis=-1, keepdims=True)
    lo = N_GROUPS + g_sel * EXPERTS_PER_GROUP
    in_grp = (lane >= lo) & (lane < lo + EXPERTS_PER_GROUP)
    v1 = jnp.max(jnp.where(in_grp, logits, neg), axis=-1, keepdims=True)
    e1 = jnp.min(jnp.where(in_grp & (logits == v1), lane, big), axis=-1, keepdims=True)
    rest = in_grp & (lane != e1)
    v2 = jnp.max(jnp.where(rest, logits, neg), axis=-1, keepdims=True)
    e2 = jnp.min(jnp.where(rest & (logits == v2), lane, big), axis=-1, keepdims=True)
    z = jnp.exp(v2 - v1)
    w1 = p_grp / (1.0 + z)
    w2 = p_grp * z / (1.0 + z)
    return g_sel, jnp.where(lane == e1, w1, jnp.where(lane == e2, w2, 0.0))


def _moe_tail_kernel(x1_ref, p_ref, gffn_ref, wr_ref, br_ref, wg_ref, wu_ref, wd_ref,
                     gple_ref, wpg_ref, wpp_ref, gfin_ref, y_ref,
                     xs_sc, cws_sc, osort_sc, pos_sc, off_sm, nt_sm):
    s = pl.program_id(1)
    w = MOE_WINDOW

    @pl.when(s == 0)
    def _():
        hn = _rms(x1_ref[...], gffn_ref[...])
        hn_hi, hn_lo = _split_bf16(hn)
        hh_hl = jnp.dot(hn_hi, wr_ref[...], preferred_element_type=F32)
        logits = (hh_hl[:, :LANES] + hh_hl[:, LANES:]
                  + jnp.dot(hn_lo, wr_ref[:, :LANES], preferred_element_type=F32)) + br_ref[...]
        g_sel, cw = _route(logits)
        lane = lax.broadcasted_iota(jnp.int32, (w, LANES), 1)
        onehot = jnp.where(lane == g_sel, 1.0, 0.0)
        cum = _scan_rows(onehot, jnp.add, 0.0)
        cnt = cum[w - 1:w, :].astype(jnp.int32)
        cnt_pad = ((cnt + (MOE_ALIGN - 1)) // MOE_ALIGN) * MOE_ALIGN
        lane1 = lax.broadcasted_iota(jnp.int32, (1, LANES), 1)
        off = jnp.zeros((1, LANES), jnp.int32)
        for gi in range(N_GROUPS - 1):
            off = off + jnp.where(lane1 > gi, cnt_pad[:, gi:gi + 1], 0)
        n_tiles = (cnt + (MOE_ROW_TILE - 1)) // MOE_ROW_TILE
        for gi in range(N_GROUPS):
            off_sm[gi] = off[0, gi]
            nt_sm[gi] = n_tiles[0, gi]
        pos = jnp.sum(onehot * (off.astype(F32) + cum - 1.0), axis=-1, keepdims=True)
        pos_b = jnp.broadcast_to(pos, (w, LANES))
        pos_sc[...] = pos_b
        pos_row = pos_b.T[0:1, :].astype(jnp.int32)
        cw_packed = _pack_split3(cw)
        for c in range(MOE_SORTED_ROWS // MOE_SORT_CHUNK):
            rows = lax.broadcasted_iota(jnp.int32, (MOE_SORT_CHUNK, w), 0) + c * MOE_SORT_CHUNK
            sel = jnp.where(rows == pos_row, 1.0, 0.0).astype(BF16)
            sl = slice(c * MOE_SORT_CHUNK, (c + 1) * MOE_SORT_CHUNK)
            xs_sc[sl, :] = jnp.dot(sel, hn_hi, preferred_element_type=F32).astype(BF16)
            cws_sc[sl, :] = _unpack_split3(jnp.dot(sel, cw_packed, preferred_element_type=F32))
        osort_sc[...] = jnp.zeros(osort_sc.shape, BF16)

    row0 = off_sm[s]
    lane_t = lax.broadcasted_iota(jnp.int32, (MOE_ROW_TILE, LANES), 1)
    first_lane = N_GROUPS + s * EXPERTS_PER_STEP

    def tile_body(i, carry):
        r0 = pl.multiple_of(row0 + i * MOE_ROW_TILE, MOE_ALIGN)
        xt = xs_sc[pl.ds(r0, MOE_ROW_TILE), :]
        cwt = cws_sc[pl.ds(r0, MOE_ROW_TILE), :]
        parts = []
        for e in range(EXPERTS_PER_STEP):
            hg = jnp.dot(xt, wg_ref[e], preferred_element_type=F32)
            hu = jnp.dot(xt, wu_ref[e], preferred_element_type=F32)
            col = jnp.sum(jnp.where(lane_t == first_lane + e, cwt, 0.0), axis=-1, keepdims=True)
            parts.append((_silu(hg) * hu * col).astype(BF16))
        he = jnp.concatenate(parts, axis=-1)
        out = jnp.dot(he, wd_ref[...], preferred_element_type=F32)
        osort_sc[pl.ds(r0, MOE_ROW_TILE), :] = out.astype(BF16)
        return carry

    lax.fori_loop(0, nt_sm[s], tile_body, 0)

    @pl.when(s == MOE_STEPS - 1)
    def _():
        osort = osort_sc[...]
        for c in range(w // MOE_UNSORT_CHUNK):
            sl = slice(c * MOE_UNSORT_CHUNK, (c + 1) * MOE_UNSORT_CHUNK)
            pos_col = pos_sc[sl, 0:1].astype(jnp.int32)
            cols = lax.broadcasted_iota(jnp.int32, (MOE_UNSORT_CHUNK, MOE_SORTED_ROWS), 1)
            sel = jnp.where(cols == pos_col, 1.0, 0.0).astype(BF16)
            x2 = x1_ref[sl, :] + jnp.dot(sel, osort, preferred_element_type=F32)
            gate = _sigmoid(_dot(_rms(x2, gple_ref[...]), wpg_ref[...]))
            x3 = x2 + gate * _dot(p_ref[sl, :], wpp_ref[...])
            y_ref[sl, :] = _rms(x3, gfin_ref[...])


def _moe_tail(x1, p, g_ffn, w_r, b_r, wg, wu, wd, g_ple, w_pg, w_pp, g_fin):
    n = x1.shape[0]
    tm = MOE_WINDOW
    row = lambda i, s: (i, 0)
    step = lambda i, s: (s, 0, 0)
    big = pl.BlockSpec((tm, D_MODEL), row)
    sw = EXPERTS_PER_STEP * D_EXPERT
    return pl.pallas_call(
        _moe_tail_kernel,
        grid=(n // tm, MOE_STEPS),
        in_specs=[big, pl.BlockSpec((tm, PLE_DIM), row), _resident(g_ffn.shape), _resident(w_r.shape),
                  _resident(b_r.shape),
                  pl.BlockSpec((EXPERTS_PER_STEP, D_MODEL, D_EXPERT), step),
                  pl.BlockSpec((EXPERTS_PER_STEP, D_MODEL, D_EXPERT), step),
                  pl.BlockSpec((None, sw, D_MODEL), step),
                  _resident(g_ple.shape), _resident(w_pg.shape), _resident(w_pp.shape), _resident(g_fin.shape)],
        out_specs=big,
        out_shape=jax.ShapeDtypeStruct((n, D_MODEL), F32),
        scratch_shapes=[pltpu.VMEM((MOE_SORTED_ROWS, D_MODEL), BF16), pltpu.VMEM((MOE_SORTED_ROWS, LANES), F32),
                        pltpu.VMEM((MOE_SORTED_ROWS, D_MODEL), BF16), pltpu.VMEM((tm, LANES), F32),
                        pltpu.SMEM((N_GROUPS,), jnp.int32), pltpu.SMEM((N_GROUPS,), jnp.int32)],
        compiler_params=pltpu.CompilerParams(dimension_semantics=("arbitrary", "arbitrary"),
                                             vmem_limit_bytes=VMEM_LIMIT),
        name="moe_tail",
    )(x1, p, g_ffn, w_r, b_r, wg, wu, wd, g_ple, w_pg, w_pp, g_fin)


PROMPT_TIME_TILE = 256
SAMPLE_ROW_TILE = 512
SAMPLE_SEQS_PER_STEP = 8


def _pad_tail(buf):
    return jnp.pad(buf, ((0, 0), (SUBLANES - buf.shape[1], 0), (0, 0)))


def _moe(x1, p, wts):
    return _moe_tail(x1, p, wts["g_ffn"], wts["w_r"], wts["b_r"], wts["w_gate"], wts["w_up"], wts["w_down"],
                     wts["g_ple"], wts["w_ple_gate"], wts["w_ple_proj"], wts["g_final"])


def _outputs(y, shape, c1, n1, m1p, mconv1, sconv1):
    return (y.reshape(shape), c1[None], n1[None], m1p[None, :, :M_HEADS, 0],
            mconv1[None, :, SUBLANES - (M_CONV_W - 1):, :], sconv1[None, :, SUBLANES - (S_CONV_W - 1):, :])


def _prompt_group(x, p, wts, experts_f32, *, tt):
    bsz, t, _ = x.shape
    (x1, *states), (wg, wu, wd) = _mixer(x.reshape(bsz * t, D_MODEL), wts, experts_f32, bsz=bsz, t=t, tt=tt)
    wts.update(w_gate=wg, w_up=wu, w_down=wd.reshape(MOE_STEPS, EXPERTS_PER_STEP * D_EXPERT, D_MODEL))
    return _outputs(_moe(x1, p.reshape(bsz * t, PLE_DIM), wts), x.shape, *states)


def _sample_group(x, p, mconv0, sconv0, c0, n0, m0, wts, *, tm):
    bsz, t, _ = x.shape
    xf = x.reshape(bsz * t, D_MODEL)
    m0p = jnp.broadcast_to(jnp.pad(m0, ((0, 0), (0, SUBLANES - M_HEADS)))[:, :, None], (bsz, SUBLANES, LANES))
    xm, v, o, ifp, sb, cx, ga, gb = _in_proj(xf, wts["g_mix"], wts["w_main"], wts["w_if"], wts["w_rest"], tm)
    hm, u, *states = _mlstm(
        xm, v, cx, ifp, _pad_tail(mconv0), _pad_tail(sconv0), c0, n0, m0p,
        wts["w_mconv"], wts["w_sconv"], wts["w_q"], wts["w_k"], wts["gate_bias"], wts["g_head"],
        bsz=bsz, t=t, tt=t, nb=SAMPLE_SEQS_PER_STEP)
    x1 = _merge(xf, hm, o, ga, gb, sb, u, wts["w_a"], wts["w_sout"], wts["w_o"], tm)
    return _outputs(_moe(x1, p.reshape(bsz * t, PLE_DIM), wts), x.shape, *states)


def kernel(x_prompt, x_sample, p_prompt, p_sample, state_mlstm_C, state_mlstm_n, state_mlstm_m, state_mlstm_conv, state_sconv, g_mix, w_in, w_mconv, w_q, w_k, b_i, b_f, g_head, w_a, w_sconv, w_sout, w_o, g_ffn, w_rg, b_rg, w_re, b_re, w_gate, w_up, w_down, g_ple, w_ple_gate, w_ple_proj, g_final):
    assert g_mix.shape[0] == 1, "single-layer trunk"
    d = D_MODEL
    w_in0 = w_in[0]
    n_if = 2 * M_HEADS
    w_router = jnp.pad(jnp.concatenate([w_rg[0], w_re[0]], axis=1), ((0, 0), (0, LANES - N_GROUPS - N_EXPERTS)))
    wr_hi = w_router.astype(BF16)
    wr_lo = (w_router - wr_hi.astype(F32)).astype(BF16)
    w_r = jnp.concatenate([wr_hi, wr_lo], axis=1)
    wts = {
        "g_mix": g_mix,
        "w_main": w_in0[:, :3 * d].astype(BF16),
        "w_if": jnp.pad(w_in0[:, 3 * d:3 * d + n_if], ((0, 0), (0, LANES - n_if))).astype(BF16),
        "w_rest": w_in0[:, 3 * d + n_if:].astype(BF16),
        "w_mconv": w_mconv[0], "w_sconv": w_sconv[0],
        "w_q": w_q[0].astype(BF16), "w_k": w_k[0].astype(BF16),
        "gate_bias": jnp.pad(jnp.concatenate([b_i[0], b_f[0]])[None, :], ((0, 0), (0, LANES - n_if))),
        "g_head": g_head[0].reshape(1, d),
        "w_a": w_a[0].astype(BF16), "w_sout": w_sout[0].astype(BF16), "w_o": w_o[0].astype(BF16),
        "g_ffn": g_ffn, "w_r": w_r,
        "b_r": jnp.pad(jnp.concatenate([b_rg[0], b_re[0]])[None, :], ((0, 0), (0, LANES - N_GROUPS - N_EXPERTS))),
        "g_ple": g_ple, "w_ple_gate": w_ple_gate[0].astype(BF16), "w_ple_proj": w_ple_proj[0].astype(BF16),
        "g_final": g_final[None, :],
    }
    yp, *st_p = _prompt_group(x_prompt, p_prompt[0], wts, (w_gate[0], w_up[0], w_down[0]), tt=PROMPT_TIME_TILE)
    ys, *st_s = _sample_group(x_sample, p_sample[0], state_mlstm_conv[0], state_sconv[0],
                              state_mlstm_C[0], state_mlstm_n[0], state_mlstm_m[0], wts, tm=SAMPLE_ROW_TILE)
    return (yp, ys, *st_p, *st_s)
```

```python
import functools

import jax
import jax.numpy as jnp
from jax import lax
from jax.experimental import pallas as pl
from jax.experimental.pallas import tpu as pltpu

D_MODEL = 1024
M_HEADS = 4
M_HEAD_DIM = 256
M_CONV_W = 4
S_CONV_W = 3
N_GROUPS = 4
EXPERTS_PER_GROUP = 8
N_EXPERTS = N_GROUPS * EXPERTS_PER_GROUP
D_EXPERT = 256
PLE_DIM = 256
RMS_EPS = 1e-6

LANES = 128
SUBLANES = 8
VMEM_LIMIT = 60 * 1024 * 1024

BF16 = jnp.bfloat16
F32 = jnp.float32


def _rms(x, g):
    return x * lax.rsqrt(jnp.mean(x * x, axis=-1, keepdims=True) + RMS_EPS) * g


def _dot(a, b):
    return jnp.dot(a.astype(BF16), b.astype(BF16), preferred_element_type=F32)


def _sigmoid(x):
    return 0.5 * jnp.tanh(0.5 * x) + 0.5


def _silu(x):
    return x * _sigmoid(x)


def _resident(shape):
    nd = len(shape)
    return pl.BlockSpec(shape, lambda *_: (0,) * nd, pipeline_mode=pl.Buffered(1))


_GATE_COL = 3 * D_MODEL
_REST_COL = _GATE_COL + LANES
IN_COL = {"xm": 0, "v": D_MODEL, "o": 2 * D_MODEL, "sb": _REST_COL, "sc": _REST_COL + D_MODEL,
          "sx": _REST_COL + 2 * D_MODEL, "ga": _REST_COL + 3 * D_MODEL, "gb": _REST_COL + 4 * D_MODEL}


def _projector(h, w_in_ref):
    def proj(name):
        lo, width = (_GATE_COL, LANES) if name == "gates" else (IN_COL[name], D_MODEL)
        return jnp.dot(h, w_in_ref[:, lo:lo + width], preferred_element_type=F32)
    return proj


def _in_proj_kernel(x_ref, g_ref, w_in_ref,
                    xm_ref, v_ref, o_ref, if_ref, sb_ref, cx_ref, ga_ref, gb_ref):
    proj = _projector(_rms(x_ref[...], g_ref[...]).astype(BF16), w_in_ref)
    xm_ref[...] = proj("xm")
    v_ref[...] = proj("v")
    o_ref[...] = proj("o")
    if_ref[...] = proj("gates")
    sb_ref[...] = proj("sb")
    cx_ref[...] = proj("sc") * proj("sx")
    ga_ref[...] = proj("ga")
    gb_ref[...] = proj("gb")


def _in_proj(x, g_mix, w_in, tm):
    n = x.shape[0]
    row = lambda i: (i, 0)
    big = pl.BlockSpec((tm, D_MODEL), row)
    outs = [jax.ShapeDtypeStruct((n, D_MODEL), F32)] * 3 + [jax.ShapeDtypeStruct((n, LANES), F32)] \
        + [jax.ShapeDtypeStruct((n, D_MODEL), F32)] * 4
    return pl.pallas_call(
        _in_proj_kernel,
        grid=(n // tm,),
        in_specs=[big, _resident(g_mix.shape), _resident(w_in.shape)],
        out_specs=[big, big, big, pl.BlockSpec((tm, LANES), row), big, big, big, big],
        out_shape=outs,
        compiler_params=pltpu.CompilerParams(dimension_semantics=("arbitrary",),
                                             vmem_limit_bytes=VMEM_LIMIT),
        name="in_proj",
    )(x, g_mix, w_in)


def _scan_rows(x, op, identity):
    sub = lax.broadcasted_iota(jnp.int32, (SUBLANES, x.shape[1]), 0)
    blocks, carry = [], None
    for i in range(x.shape[0] // SUBLANES):
        blk = x[i * SUBLANES:(i + 1) * SUBLANES, :]
        for shift in (1, 2, 4):
            blk = op(blk, jnp.where(sub >= shift, pltpu.roll(blk, shift, axis=0), identity))
        if carry is not None:
            blk = op(blk, carry)
        carry = jnp.broadcast_to(blk[SUBLANES - 1:SUBLANES, :], blk.shape)
        blocks.append(blk)
    return jnp.concatenate(blocks, axis=0)


def _log_sigmoid(x):
    return jnp.minimum(x, 0.0) - jnp.log1p(jnp.exp(-jnp.abs(x)))


def _seq_init(mconv0_ref, sconv0_ref, c0_ref, n0_ref, m0_ref, xp_sc, cp_sc, c1_ref, n1_ref, m1_ref):
    @pl.when(pl.program_id(1) == 0)
    def _():
        xp_sc[0:SUBLANES, :] = mconv0_ref[...]
        cp_sc[0:SUBLANES, :] = sconv0_ref[...]
        c1_ref[...] = c0_ref[...]
        n1_ref[...] = n0_ref[...]
        m1_ref[...] = m0_ref[...]


def _causal_conv(src_sc, w_ref, width, tt):
    acc = None
    for j in range(width):
        term = src_sc[pl.ds(SUBLANES - (width - 1) + j, tt), :] * w_ref[j:j + 1, :]
        acc = term if acc is None else acc + term
    return acc


def _carry_tail(src_sc, tail_ref, tt):
    tail = src_sc[tt:tt + SUBLANES, :]
    src_sc[0:SUBLANES, :] = tail
    tail_ref[...] = tail


def _head_slices():
    return [slice(h * M_HEAD_DIM, (h + 1) * M_HEAD_DIM) for h in range(M_HEADS)]


def _mlstm_tile(xq, v, ifp, states, wq_ref, wk_ref, bias_ref, ghead_ref, tt, between=()):
    nh = M_HEADS
    hs = _head_slices()
    seq_rows = [slice(j * tt, (j + 1) * tt) for j in range(len(states))]
    g = [ifp[rows, :] + bias_ref[...] for rows in seq_rows]
    b_c = [_scan_rows(_log_sigmoid(gj), jnp.add, 0.0) for gj in g]
    a_c = [gj - pltpu.roll(bj, LANES - nh, axis=1) for gj, bj in zip(g, b_c)]
    amax_c = [_scan_rows(aj, jnp.maximum, -jnp.inf) for aj in a_c]
    a_t = [aj.T for aj in a_c]

    r_idx = lax.broadcasted_iota(jnp.int32, (tt, tt), 0)
    c_idx = lax.broadcasted_iota(jnp.int32, (tt, tt), 1)
    causal = c_idx <= r_idx

    q_all = [_dot(xq[:, hs[h]], wq_ref[h]) * (M_HEAD_DIM ** -0.5) for h in range(nh)]
    k_all = [_dot(xq[:, hs[h]], wk_ref[h]) for h in range(nh)]
    units = [(j, h) for j in range(len(states)) for h in range(nh)]
    ids = range(len(units))
    q = [q_all[h][seq_rows[j], :] for j, h in units]
    k = [k_all[h][seq_rows[j], :] for j, h in units]
    vv = [v[seq_rows[j], hs[h]] for j, h in units]
    c_prev = [states[j][0][h] for j, h in units]
    n_prev = [states[j][1][h:h + 1, :] for j, h in units]
    m_prev = [states[j][2][h:h + 1, 0:1] for j, h in units]
    b_col = [b_c[j][:, nh + h:nh + h + 1] for j, h in units]
    a_col = [a_c[j][:, h:h + 1] for j, h in units]
    a_row = [a_t[j][h:h + 1, :] for j, h in units]
    b_last = [b_c[j][tt - 1:tt, nh + h:nh + h + 1] for j, h in units]
    mm_col = [jnp.maximum(m_prev[i], amax_c[j][:, h:h + 1]) for i, (j, h) in enumerate(units)]
    mm_last = [jnp.maximum(m_prev[i], amax_c[j][tt - 1:tt, h:h + 1]) for i, (j, h) in enumerate(units)]

    extra = []
    pending = list(between)

    def run_one():
        if pending:
            extra.append(pending.pop(0)())

    w_inter = [jnp.exp(m_prev[i] - mm_col[i]) for i in ids]
    qk = [lax.dot_general(q[i].astype(BF16), k[i].astype(BF16), (((1,), (1,)), ((), ())),
                          preferred_element_type=F32) for i in ids]
    run_one()
    s = [jnp.exp(jnp.where(causal, a_row[i] - mm_col[i], -jnp.inf)) * qk[i] for i in ids]
    run_one()
    num = [w_inter[i] * _dot(q[i], c_prev[i]) + _dot(s[i], vv[i]) for i in ids]
    den = [w_inter[i] * jnp.sum(q[i] * n_prev[i], axis=-1, keepdims=True)
           + jnp.sum(s[i], axis=-1, keepdims=True) for i in ids]
    run_one()
    out = [num[i] * (1.0 / jnp.maximum(jnp.abs(den[i]), jnp.exp(-(b_col[i] + mm_col[i])))) for i in ids]
    out = [out[i] * lax.rsqrt(jnp.mean(out[i] * out[i], axis=-1, keepdims=True) + RMS_EPS)
           * ghead_ref[:, hs[units[i][1]]] for i in ids]
    while pending:
        run_one()

    decay = [jnp.exp(m_prev[i] - mm_last[i]) for i in ids]
    wk = [jnp.exp(a_col[i] - mm_last[i]) * k[i] for i in ids]
    for i, (j, h) in enumerate(units):
        c_ref, n_ref, m_ref = states[j]
        c_ref[h] = decay[i] * c_prev[i] + lax.dot_general(
            wk[i].astype(BF16), vv[i].astype(BF16), (((0,), (0,)), ((), ())), preferred_element_type=F32)
        n_ref[h:h + 1, :] = decay[i] * n_prev[i] + jnp.sum(wk[i], axis=0, keepdims=True)
        m_ref[h:h + 1, :] = jnp.broadcast_to(b_last[i] + mm_last[i], (1, LANES))
    return [out[j * nh:(j + 1) * nh] for j in range(len(states))], extra


def _mlstm_kernel(xm_ref, v_ref, cx_ref, if_ref, mconv0_ref, sconv0_ref, c0_ref, n0_ref, m0_ref,
                  w_mconv_ref, w_sconv_ref, wq_ref, wk_ref, bias_ref, ghead_ref,
                  hm_ref, u_ref, c1_ref, n1_ref, m1_ref, mconv1_ref, sconv1_ref,
                  xp_sc, cp_sc, *, tt, nb):
    hs = _head_slices()
    xq, states = [], []
    for j in range(nb):
        rows = slice(j * tt, (j + 1) * tt)
        xp, cp, state = xp_sc.at[j], cp_sc.at[j], (c1_ref.at[j], n1_ref.at[j], m1_ref.at[j])
        _seq_init(mconv0_ref.at[j], sconv0_ref.at[j], c0_ref.at[j], n0_ref.at[j], m0_ref.at[j], xp, cp, *state)
        xp[SUBLANES:SUBLANES + tt, :] = xm_ref[rows, :]
        cp[SUBLANES:SUBLANES + tt, :] = cx_ref[rows, :]
        xq.append(_silu(_causal_conv(xp, w_mconv_ref, M_CONV_W, tt)))
        u_ref[rows, :] = _causal_conv(cp, w_sconv_ref, S_CONV_W, tt)
        _carry_tail(xp, mconv1_ref.at[j], tt)
        _carry_tail(cp, sconv1_ref.at[j], tt)
        states.append(state)
    xq = xq[0] if nb == 1 else jnp.concatenate(xq, axis=0)
    out, _ = _mlstm_tile(xq, v_ref[...], if_ref[...], states, wq_ref, wk_ref, bias_ref, ghead_ref, tt)
    for j in range(nb):
        for h, sl in enumerate(hs):
            hm_ref[j * tt:(j + 1) * tt, sl] = out[j][h]


def _mlstm(xm, v, cx, ifp, mconv0, sconv0, c0, n0, m0, w_mconv, w_sconv, wq, wk, bias, ghead, *, bsz, t, tt, nb):
    nt = t // tt
    n = bsz * t
    assert nb == 1 or nt == 1, "several sequences per step only when a step covers whole sequences"
    tok = lambda b, i: (b * nt + i, 0)
    seq3 = lambda b, i: (b, 0, 0)
    seq4 = lambda b, i: (b, 0, 0, 0)
    big = pl.BlockSpec((nb * tt, D_MODEL), tok)
    tail = pl.BlockSpec((nb, SUBLANES, D_MODEL), seq3)
    c_spec = pl.BlockSpec((nb, M_HEADS, M_HEAD_DIM, M_HEAD_DIM), seq4)
    n_spec = pl.BlockSpec((nb, M_HEADS, M_HEAD_DIM), seq3)
    m_spec = pl.BlockSpec((nb, SUBLANES, LANES), seq3)
    return pl.pallas_call(
        functools.partial(_mlstm_kernel, tt=tt, nb=nb),
        grid=(bsz // nb, nt),
        in_specs=[big, big, big, pl.BlockSpec((nb * tt, LANES), tok), tail, tail, c_spec, n_spec, m_spec,
                  _resident(w_mconv.shape), _resident(w_sconv.shape), _resident(wq.shape),
                  _resident(wk.shape), _resident(bias.shape), _resident(ghead.shape)],
        out_specs=[big, big, c_spec, n_spec, m_spec, tail, tail],
        out_shape=[jax.ShapeDtypeStruct((n, D_MODEL), F32), jax.ShapeDtypeStruct((n, D_MODEL), F32),
                   jax.ShapeDtypeStruct(c0.shape, F32), jax.ShapeDtypeStruct(n0.shape, F32),
                   jax.ShapeDtypeStruct(m0.shape, F32),
                   jax.ShapeDtypeStruct(mconv0.shape, F32), jax.ShapeDtypeStruct(sconv0.shape, F32)],
        scratch_shapes=[pltpu.VMEM((nb, tt + SUBLANES, D_MODEL), F32),
                        pltpu.VMEM((nb, tt + SUBLANES, D_MODEL), F32)],
        compiler_params=pltpu.CompilerParams(dimension_semantics=("arbitrary", "arbitrary"),
                                             vmem_limit_bytes=VMEM_LIMIT),
        name="mlstm",
    )(xm, v, cx, ifp, mconv0, sconv0, c0, n0, m0, w_mconv, w_sconv, wq, wk, bias, ghead)


def _merge_kernel(x_ref, hm_ref, o_ref, ga_ref, gb_ref, sb_ref, u_ref, wa_ref, wsout_ref, wo_ref, x1_ref):
    y_a = _dot(_sigmoid(o_ref[...]) * hm_ref[...], wa_ref[...])
    y_b = _dot(sb_ref[...] * u_ref[...], wsout_ref[...])
    merged = _sigmoid(ga_ref[...]) * y_a + _sigmoid(gb_ref[...]) * y_b
    x1_ref[...] = x_ref[...] + _dot(merged, wo_ref[...])


def _merge(x, hm, o, ga, gb, sb, u, w_a, w_sout, w_o, tm):
    n = x.shape[0]
    big = pl.BlockSpec((tm, D_MODEL), lambda i: (i, 0))
    return pl.pallas_call(
        _merge_kernel,
        grid=(n // tm,),
        in_specs=[big] * 7 + [_resident(w_a.shape), _resident(w_sout.shape), _resident(w_o.shape)],
        out_specs=big,
        out_shape=jax.ShapeDtypeStruct((n, D_MODEL), F32),
        compiler_params=pltpu.CompilerParams(dimension_semantics=("arbitrary",),
                                             vmem_limit_bytes=VMEM_LIMIT),
        name="merge",
    )(x, hm, o, ga, gb, sb, u, w_a, w_sout, w_o)


def _mixer_kernel(x_ref, g_ref, w_in_ref, w_mconv_ref, w_sconv_ref, wq_ref, wk_ref,
                  bias_ref, ghead_ref, wa_ref, wsout_ref, wo_ref, wg32_ref, wu32_ref, wd32_ref,
                  x1_ref, c1_ref, n1_ref, m1_ref, mconv1_ref, sconv1_ref, wg16_ref, wu16_ref, wd16_ref,
                  xp_sc, cp_sc, hm_sc, *, tt):
    @pl.when(pl.program_id(1) == 0)
    def _():
        xp_sc[0:SUBLANES, :] = jnp.zeros((SUBLANES, D_MODEL), F32)
        cp_sc[0:SUBLANES, :] = jnp.zeros((SUBLANES, D_MODEL), F32)
        c1_ref[...] = jnp.zeros(c1_ref.shape, F32)
        n1_ref[...] = jnp.zeros(n1_ref.shape, F32)
        m1_ref[...] = jnp.zeros(m1_ref.shape, F32)

    wg16_ref[...] = wg32_ref[...].astype(BF16)
    wu16_ref[...] = wu32_ref[...].astype(BF16)
    wd16_ref[...] = wd32_ref[...].astype(BF16)

    proj = _projector(_rms(x_ref[...], g_ref[...]).astype(BF16), w_in_ref)
    xp_sc[SUBLANES:SUBLANES + tt, :] = proj("xm")
    ifp = proj("gates")
    cp_sc[SUBLANES:SUBLANES + tt, :] = proj("sc") * proj("sx")
    xq = _silu(_causal_conv(xp_sc, w_mconv_ref, M_CONV_W, tt))
    u = _causal_conv(cp_sc, w_sconv_ref, S_CONV_W, tt)
    _carry_tail(xp_sc, mconv1_ref, tt)
    _carry_tail(cp_sc, sconv1_ref, tt)
    v = proj("v")
    hs = _head_slices()
    (out,), (y_b, sig_o, sig_ga, sig_gb) = _mlstm_tile(
        xq, v, ifp, [(c1_ref, n1_ref, m1_ref)], wq_ref, wk_ref, bias_ref, ghead_ref, tt,
        between=(lambda: _dot(proj("sb") * u, wsout_ref[...]),
                 lambda: _sigmoid(proj("o")),
                 lambda: _sigmoid(proj("ga")),
                 lambda: _sigmoid(proj("gb"))))
    for hd, sl in enumerate(hs):
        hm_sc[:, sl] = (sig_o[:, sl] * out[hd]).astype(BF16)
    y_a = jnp.dot(hm_sc[...], wa_ref[...], preferred_element_type=F32)
    merged = sig_ga * y_a + sig_gb * y_b
    x1_ref[...] = x_ref[...] + _dot(merged, wo_ref[...])


def _mixer(x, wts, experts_f32, *, bsz, t, tt):
    nt = t // tt
    steps = bsz * nt
    tok = lambda b, i: (b * nt + i, 0)
    seq3 = lambda b, i: (b, 0, 0)
    seq4 = lambda b, i: (b, 0, 0, 0)
    big = pl.BlockSpec((tt, D_MODEL), tok)
    tail = pl.BlockSpec((None, SUBLANES, D_MODEL), seq3)
    c_spec = pl.BlockSpec((None, M_HEADS, M_HEAD_DIM, M_HEAD_DIM), seq4)
    n_spec = pl.BlockSpec((None, M_HEADS, M_HEAD_DIM), seq3)
    m_spec = pl.BlockSpec((None, SUBLANES, LANES), seq3)
    names = ("g_mix", "w_in", "w_mconv", "w_sconv", "w_q", "w_k", "gate_bias", "g_head",
             "w_a", "w_sout", "w_o")
    weights = [wts[k] for k in names]
    flat = [w.reshape(-1, w.shape[-1]) for w in experts_f32]
    cast_specs = [pl.BlockSpec((w.shape[0] // steps, w.shape[1]), tok) for w in flat]
    outs = pl.pallas_call(
        functools.partial(_mixer_kernel, tt=tt),
        grid=(bsz, nt),
        in_specs=[big] + [_resident(w.shape) for w in weights] + cast_specs,
        out_specs=[big, c_spec, n_spec, m_spec, tail, tail] + cast_specs,
        out_shape=[jax.ShapeDtypeStruct(x.shape, F32),
                   jax.ShapeDtypeStruct((bsz, M_HEADS, M_HEAD_DIM, M_HEAD_DIM), F32),
                   jax.ShapeDtypeStruct((bsz, M_HEADS, M_HEAD_DIM), F32),
                   jax.ShapeDtypeStruct((bsz, SUBLANES, LANES), F32),
                   jax.ShapeDtypeStruct((bsz, SUBLANES, D_MODEL), F32),
                   jax.ShapeDtypeStruct((bsz, SUBLANES, D_MODEL), F32)]
                  + [jax.ShapeDtypeStruct(w.shape, BF16) for w in flat],
        scratch_shapes=[pltpu.VMEM((tt + SUBLANES, D_MODEL), F32), pltpu.VMEM((tt + SUBLANES, D_MODEL), F32),
                        pltpu.VMEM((tt, D_MODEL), BF16)],
        compiler_params=pltpu.CompilerParams(dimension_semantics=("arbitrary", "arbitrary"),
                                             vmem_limit_bytes=VMEM_LIMIT),
        name="mixer",
    )(x, *weights, *flat)
    return outs[:6], [o.reshape(w.shape) for o, w in zip(outs[6:], experts_f32)]


MOE_WINDOW = 1024
MOE_ROW_TILE = 128
MOE_ALIGN = 16
EXPERTS_PER_STEP = EXPERTS_PER_GROUP
MOE_STEPS = N_EXPERTS // EXPERTS_PER_STEP
MOE_CHUNK = 256
MOE_SORT_CHUNK = 1280
MOE_UNSORT_CHUNK = 512
MOE_SORTED_ROWS = -(-(MOE_WINDOW + N_GROUPS * (MOE_ALIGN - 1) + MOE_ROW_TILE) // MOE_CHUNK) * MOE_CHUNK


def _split_bf16(x):
    hi = x.astype(BF16)
    lo = (x - hi.astype(F32)).astype(BF16)
    return hi, lo


CW_LANE_STRIDE = 40


def _pack_split3(cw):
    hi = cw.astype(BF16).astype(F32)
    r1 = cw - hi
    mid = r1.astype(BF16).astype(F32)
    lo = (r1 - mid).astype(BF16).astype(F32)
    return (hi + pltpu.roll(mid, CW_LANE_STRIDE, axis=1) + pltpu.roll(lo, 2 * CW_LANE_STRIDE, axis=1)).astype(BF16)


def _unpack_split3(packed):
    return (packed + pltpu.roll(packed, LANES - CW_LANE_STRIDE, axis=1)
            + pltpu.roll(packed, LANES - 2 * CW_LANE_STRIDE, axis=1))


def _route(logits):
    lane = lax.broadcasted_iota(jnp.int32, logits.shape, 1)
    neg = -jnp.inf
    big = jnp.int32(LANES)
    is_grp = lane < N_GROUPS
    g_max = jnp.max(jnp.where(is_grp, logits, neg), axis=-1, keepdims=True)
    g_sel = jnp.min(jnp.where(is_grp & (logits == g_max), lane, big), axis=-1, keepdims=True)
    p_grp = 1.0 / jnp.sum(jnp.where(is_grp, jnp.exp(logits - g_max), 0.0), axis=-1, keepdims=True)
    lo = N_GROUPS + g_sel * EXPERTS_PER_GROUP
    in_grp = (lane >= lo) & (lane < lo + EXPERTS_PER_GROUP)
    v1 = jnp.max(jnp.where(in_grp, logits, neg), axis=-1, keepdims=True)
    e1 = jnp.min(jnp.where(in_grp & (logits == v1), lane, big), axis=-1, keepdims=True)
    rest = in_grp & (lane != e1)
    v2 = jnp.max(jnp.where(rest, logits, neg), axis=-1, keepdims=True)
    e2 = jnp.min(jnp.where(rest & (logits == v2), lane, big), axis=-1, keepdims=True)
    z = jnp.exp(v2 - v1)
    w1 = p_grp / (1.0 + z)
    w2 = p_grp * z / (1.0 + z)
    return g_sel, jnp.where(lane == e1, w1, jnp.where(lane == e2, w2, 0.0))


def _moe_tail_kernel(x1_ref, p_ref, gffn_ref, wr_ref, br_ref, wg_ref, wu_ref, wd_ref,
                     gple_ref, wpg_ref, wpp_ref, gfin_ref, y_ref,
                     xs_sc, cws_sc, osort_sc, pos_sc, off_sm, nt_sm):
    s = pl.program_id(1)
    w = MOE_WINDOW

    @pl.when(s == 0)
    def _():
        hn = _rms(x1_ref[...], gffn_ref[...])
        hn_hi, hn_lo = _split_bf16(hn)
        hh_hl = jnp.dot(hn_hi, wr_ref[...], preferred_element_type=F32)
        logits = (hh_hl[:, :LANES] + hh_hl[:, LANES:]
                  + jnp.dot(hn_lo, wr_ref[:, :LANES], preferred_element_type=F32)) + br_ref[...]
        g_sel, cw = _route(logits)
        lane = lax.broadcasted_iota(jnp.int32, (w, LANES), 1)
        onehot = jnp.where(lane == g_sel, 1.0, 0.0)
        cum = _scan_rows(onehot, jnp.add, 0.0)
        cnt = cum[w - 1:w, :].astype(jnp.int32)
        cnt_pad = ((cnt + (MOE_ALIGN - 1)) // MOE_ALIGN) * MOE_ALIGN
        lane1 = lax.broadcasted_iota(jnp.int32, (1, LANES), 1)
        off = jnp.zeros((1, LANES), jnp.int32)
        for gi in range(N_GROUPS - 1):
            off = off + jnp.where(lane1 > gi, cnt_pad[:, gi:gi + 1], 0)
        n_tiles = (cnt + (MOE_ROW_TILE - 1)) // MOE_ROW_TILE
        for gi in range(N_GROUPS):
            off_sm[gi] = off[0, gi]
            nt_sm[gi] = n_tiles[0, gi]
        pos = jnp.sum(onehot * (off.astype(F32) + cum - 1.0), axis=-1, keepdims=True)
        pos_b = jnp.broadcast_to(pos, (w, LANES))
        pos_sc[...] = pos_b
        pos_row = pos_b.T[0:1, :].astype(jnp.int32)
        cw_packed = _pack_split3(cw)
        for c in range(MOE_SORTED_ROWS // MOE_SORT_CHUNK):
            rows = lax.broadcasted_iota(jnp.int32, (MOE_SORT_CHUNK, w), 0) + c * MOE_SORT_CHUNK
            sel = jnp.where(rows == pos_row, 1.0, 0.0).astype(BF16)
            sl = slice(c * MOE_SORT_CHUNK, (c + 1) * MOE_SORT_CHUNK)
            xs_sc[sl, :] = jnp.dot(sel, hn_hi, preferred_element_type=F32).astype(BF16)
            cws_sc[sl, :] = _unpack_split3(jnp.dot(sel, cw_packed, preferred_element_type=F32))
        osort_sc[...] = jnp.zeros(osort_sc.shape, BF16)

    row0 = off_sm[s]
    lane_t = lax.broadcasted_iota(jnp.int32, (MOE_ROW_TILE, LANES), 1)
    first_lane = N_GROUPS + s * EXPERTS_PER_STEP

    def tile_body(i, carry):
        r0 = pl.multiple_of(row0 + i * MOE_ROW_TILE, MOE_ALIGN)
        xt = xs_sc[pl.ds(r0, MOE_ROW_TILE), :]
        cwt = cws_sc[pl.ds(r0, MOE_ROW_TILE), :]
        parts = []
        for e in range(EXPERTS_PER_STEP):
            hg = jnp.dot(xt, wg_ref[e], preferred_element_type=F32)
            hu = jnp.dot(xt, wu_ref[e], preferred_element_type=F32)
            col = jnp.sum(jnp.where(lane_t == first_lane + e, cwt, 0.0), axis=-1, keepdims=True)
            parts.append((_silu(hg) * hu * col).astype(BF16))
        he = jnp.concatenate(parts, axis=-1)
        out = jnp.dot(he, wd_ref[...], preferred_element_type=F32)
        osort_sc[pl.ds(r0, MOE_ROW_TILE), :] = out.astype(BF16)
        return carry

    lax.fori_loop(0, nt_sm[s], tile_body, 0)

    @pl.when(s == MOE_STEPS - 1)
    def _():
        osort = osort_sc[...]
        for c in range(w // MOE_UNSORT_CHUNK):
            sl = slice(c * MOE_UNSORT_CHUNK, (c + 1) * MOE_UNSORT_CHUNK)
            pos_col = pos_sc[sl, 0:1].astype(jnp.int32)
            cols = lax.broadcasted_iota(jnp.int32, (MOE_UNSORT_CHUNK, MOE_SORTED_ROWS), 1)
            sel = jnp.where(cols == pos_col, 1.0, 0.0).astype(BF16)
            x2 = x1_ref[sl, :] + jnp.dot(sel, osort, preferred_element_type=F32)
            gate = _sigmoid(_dot(_rms(x2, gple_ref[...]), wpg_ref[...]))
            x3 = x2 + gate * _dot(p_ref[sl, :], wpp_ref[...])
            y_ref[sl, :] = _rms(x3, gfin_ref[...])


def _moe_tail(x1, p, g_ffn, w_r, b_r, wg, wu, wd, g_ple, w_pg, w_pp, g_fin):
    n = x1.shape[0]
    tm = MOE_WINDOW
    row = lambda i, s: (i, 0)
    step = lambda i, s: (s, 0, 0)
    big = pl.BlockSpec((tm, D_MODEL), row)
    sw = EXPERTS_PER_STEP * D_EXPERT
    return pl.pallas_call(
        _moe_tail_kernel,
        grid=(n // tm, MOE_STEPS),
        in_specs=[big, pl.BlockSpec((tm, PLE_DIM), row), _resident(g_ffn.shape), _resident(w_r.shape),
                  _resident(b_r.shape),
                  pl.BlockSpec((EXPERTS_PER_STEP, D_MODEL, D_EXPERT), step),
                  pl.BlockSpec((EXPERTS_PER_STEP, D_MODEL, D_EXPERT), step),
                  pl.BlockSpec((None, sw, D_MODEL), step),
                  _resident(g_ple.shape), _resident(w_pg.shape), _resident(w_pp.shape), _resident(g_fin.shape)],
        out_specs=big,
        out_shape=jax.ShapeDtypeStruct((n, D_MODEL), F32),
        scratch_shapes=[pltpu.VMEM((MOE_SORTED_ROWS, D_MODEL), BF16), pltpu.VMEM((MOE_SORTED_ROWS, LANES), F32),
                        pltpu.VMEM((MOE_SORTED_ROWS, D_MODEL), BF16), pltpu.VMEM((tm, LANES), F32),
                        pltpu.SMEM((N_GROUPS,), jnp.int32), pltpu.SMEM((N_GROUPS,), jnp.int32)],
        compiler_params=pltpu.CompilerParams(dimension_semantics=("arbitrary", "arbitrary"),
                                             vmem_limit_bytes=VMEM_LIMIT),
        name="moe_tail",
    )(x1, p, g_ffn, w_r, b_r, wg, wu, wd, g_ple, w_pg, w_pp, g_fin)


PROMPT_TIME_TILE = 256
SAMPLE_ROW_TILE = 512
SAMPLE_SEQS_PER_STEP = 8


def _pad_tail(buf):
    return jnp.pad(buf, ((0, 0), (SUBLANES - buf.shape[1], 0), (0, 0)))


def _moe(x1, p, wts):
    return _moe_tail(x1, p, wts["g_ffn"], wts["w_r"], wts["b_r"], wts["w_gate"], wts["w_up"], wts["w_down"],
                     wts["g_ple"], wts["w_ple_gate"], wts["w_ple_proj"], wts["g_final"])


def _outputs(y, shape, c1, n1, m1p, mconv1, sconv1):
    return (y.reshape(shape), c1[None], n1[None], m1p[None, :, :M_HEADS, 0],
            mconv1[None, :, SUBLANES - (M_CONV_W - 1):, :], sconv1[None, :, SUBLANES - (S_CONV_W - 1):, :])


def _prompt_group(x, p, wts, experts_f32, *, tt):
    bsz, t, _ = x.shape
    (x1, *states), (wg, wu, wd) = _mixer(x.reshape(bsz * t, D_MODEL), wts, experts_f32, bsz=bsz, t=t, tt=tt)
    wts.update(w_gate=wg, w_up=wu, w_down=wd.reshape(MOE_STEPS, EXPERTS_PER_STEP * D_EXPERT, D_MODEL))
    return _outputs(_moe(x1, p.reshape(bsz * t, PLE_DIM), wts), x.shape, *states)


def _sample_group(x, p, mconv0, sconv0, c0, n0, m0, wts, *, tm):
    bsz, t, _ = x.shape
    xf = x.reshape(bsz * t, D_MODEL)
    m0p = jnp.broadcast_to(jnp.pad(m0, ((0, 0), (0, SUBLANES - M_HEADS)))[:, :, None], (bsz, SUBLANES, LANES))
    xm, v, o, ifp, sb, cx, ga, gb = _in_proj(xf, wts["g_mix"], wts["w_in"], tm)
    hm, u, *states = _mlstm(
        xm, v, cx, ifp, _pad_tail(mconv0), _pad_tail(sconv0), c0, n0, m0p,
        wts["w_mconv"], wts["w_sconv"], wts["w_q"], wts["w_k"], wts["gate_bias"], wts["g_head"],
        bsz=bsz, t=t, tt=t, nb=SAMPLE_SEQS_PER_STEP)
    x1 = _merge(xf, hm, o, ga, gb, sb, u, wts["w_a"], wts["w_sout"], wts["w_o"], tm)
    return _outputs(_moe(x1, p.reshape(bsz * t, PLE_DIM), wts), x.shape, *states)


def kernel(x_prompt, x_sample, p_prompt, p_sample, state_mlstm_C, state_mlstm_n, state_mlstm_m, state_mlstm_conv, state_sconv, g_mix, w_in, w_mconv, w_q, w_k, b_i, b_f, g_head, w_a, w_sconv, w_sout, w_o, g_ffn, w_rg, b_rg, w_re, b_re, w_gate, w_up, w_down, g_ple, w_ple_gate, w_ple_proj, g_final):
    assert g_mix.shape[0] == 1, "single-layer trunk"
    d = D_MODEL
    w_in0 = w_in[0]
    n_if = 2 * M_HEADS
    w_router = jnp.pad(jnp.concatenate([w_rg[0], w_re[0]], axis=1), ((0, 0), (0, LANES - N_GROUPS - N_EXPERTS)))
    wr_hi = w_router.astype(BF16)
    wr_lo = (w_router - wr_hi.astype(F32)).astype(BF16)
    w_r = jnp.concatenate([wr_hi, wr_lo], axis=1)
    wts = {
        "g_mix": g_mix,
        "w_in": jnp.concatenate([w_in0[:, :_GATE_COL],
                                 jnp.pad(w_in0[:, _GATE_COL:_GATE_COL + n_if], ((0, 0), (0, LANES - n_if))),
                                 w_in0[:, _GATE_COL + n_if:]], axis=1).astype(BF16),
        "w_mconv": w_mconv[0], "w_sconv": w_sconv[0],
        "w_q": w_q[0].astype(BF16), "w_k": w_k[0].astype(BF16),
        "gate_bias": jnp.pad(jnp.concatenate([b_i[0], b_f[0]])[None, :], ((0, 0), (0, LANES - n_if))),
        "g_head": g_head[0].reshape(1, d),
        "w_a": w_a[0].astype(BF16), "w_sout": w_sout[0].astype(BF16), "w_o": w_o[0].astype(BF16),
        "g_ffn": g_ffn, "w_r": w_r,
        "b_r": jnp.pad(jnp.concatenate([b_rg[0], b_re[0]])[None, :], ((0, 0), (0, LANES - N_GROUPS - N_EXPERTS))),
        "g_ple": g_ple, "w_ple_gate": w_ple_gate[0].astype(BF16), "w_ple_proj": w_ple_proj[0].astype(BF16),
        "g_final": g_final[None, :],
    }
    yp, *st_p = _prompt_group(x_prompt, p_prompt[0], wts, (w_gate[0], w_up[0], w_down[0]), tt=PROMPT_TIME_TILE)
    ys, *st_s = _sample_group(x_sample, p_sample[0], state_mlstm_conv[0], state_sconv[0],
                              state_mlstm_C[0], state_mlstm_n[0], state_mlstm_m[0], wts, tm=SAMPLE_ROW_TILE)
    return (yp, ys, *st_p, *st_s)
```

```python
import functools

import jax
import jax.numpy as jnp
from jax import lax
from jax.experimental import pallas as pl
from jax.experimental.pallas import tpu as pltpu

D_MODEL = 1024
M_HEADS = 4
M_HEAD_DIM = 256
M_CONV_W = 4
S_CONV_W = 3
N_GROUPS = 4
EXPERTS_PER_GROUP = 8
N_EXPERTS = N_GROUPS * EXPERTS_PER_GROUP
D_EXPERT = 256
PLE_DIM = 256
RMS_EPS = 1e-6

LANES = 128
SUBLANES = 8
VMEM_LIMIT = 60 * 1024 * 1024

BF16 = jnp.bfloat16
F32 = jnp.float32


def _rms(x, g):
    return x * lax.rsqrt(jnp.mean(x * x, axis=-1, keepdims=True) + RMS_EPS) * g


def _dot(a, b):
    return jnp.dot(a.astype(BF16), b.astype(BF16), preferred_element_type=F32)


def _sigmoid(x):
    return 0.5 * jnp.tanh(0.5 * x) + 0.5


def _silu(x):
    return x * _sigmoid(x)


def _resident(shape):
    nd = len(shape)
    return pl.BlockSpec(shape, lambda *_: (0,) * nd, pipeline_mode=pl.Buffered(1))


_GATE_COL = 3 * D_MODEL
_REST_COL = _GATE_COL + LANES
IN_COL = {"xm": 0, "v": D_MODEL, "o": 2 * D_MODEL, "sb": _REST_COL, "sc": _REST_COL + D_MODEL,
          "sx": _REST_COL + 2 * D_MODEL, "ga": _REST_COL + 3 * D_MODEL, "gb": _REST_COL + 4 * D_MODEL}


def _projector(h, w_in_ref):
    def proj(name):
        lo, width = (_GATE_COL, LANES) if name == "gates" else (IN_COL[name], D_MODEL)
        return jnp.dot(h, w_in_ref[:, lo:lo + width], preferred_element_type=F32)
    return proj


def _pack_w_in_kernel(w_ref, out_ref):
    n_if = 2 * M_HEADS
    rows = w_ref.shape[0]
    out_ref[:, 0:_GATE_COL] = w_ref[:, 0:_GATE_COL].astype(BF16)
    gates = jnp.concatenate([w_ref[:, _GATE_COL:_GATE_COL + n_if], jnp.zeros((rows, LANES - n_if), F32)], axis=1)
    out_ref[:, _GATE_COL:_REST_COL] = gates.astype(BF16)
    out_ref[:, _REST_COL:] = w_ref[:, _GATE_COL + n_if:].astype(BF16)


def _pack_w_in(w_in):
    rows, cols = w_in.shape
    tr = 128
    out_cols = cols - 2 * M_HEADS + LANES
    return pl.pallas_call(
        _pack_w_in_kernel,
        grid=(rows // tr,),
        in_specs=[pl.BlockSpec((tr, cols), lambda i: (i, 0))],
        out_specs=pl.BlockSpec((tr, out_cols), lambda i: (i, 0)),
        out_shape=jax.ShapeDtypeStruct((rows, out_cols), BF16),
        compiler_params=pltpu.CompilerParams(dimension_semantics=("arbitrary",), vmem_limit_bytes=VMEM_LIMIT),
        name="pack_w_in",
    )(w_in)


def _in_proj_kernel(x_ref, g_ref, w_in_ref,
                    xm_ref, v_ref, o_ref, if_ref, sb_ref, cx_ref, ga_ref, gb_ref):
    proj = _projector(_rms(x_ref[...], g_ref[...]).astype(BF16), w_in_ref)
    xm_ref[...] = proj("xm")
    v_ref[...] = proj("v")
    o_ref[...] = proj("o")
    if_ref[...] = proj("gates")
    sb_ref[...] = proj("sb")
    cx_ref[...] = proj("sc") * proj("sx")
    ga_ref[...] = proj("ga")
    gb_ref[...] = proj("gb")


def _in_proj(x, g_mix, w_in, tm):
    n = x.shape[0]
    row = lambda i: (i, 0)
    big = pl.BlockSpec((tm, D_MODEL), row)
    outs = [jax.ShapeDtypeStruct((n, D_MODEL), F32)] * 3 + [jax.ShapeDtypeStruct((n, LANES), F32)] \
        + [jax.ShapeDtypeStruct((n, D_MODEL), F32)] * 4
    return pl.pallas_call(
        _in_proj_kernel,
        grid=(n // tm,),
        in_specs=[big, _resident(g_mix.shape), _resident(w_in.shape)],
        out_specs=[big, big, big, pl.BlockSpec((tm, LANES), row), big, big, big, big],
        out_shape=outs,
        compiler_params=pltpu.CompilerParams(dimension_semantics=("arbitrary",),
                                             vmem_limit_bytes=VMEM_LIMIT),
        name="in_proj",
    )(x, g_mix, w_in)


def _scan_rows(x, op, identity):
    sub = lax.broadcasted_iota(jnp.int32, (SUBLANES, x.shape[1]), 0)
    blocks, carry = [], None
    for i in range(x.shape[0] // SUBLANES):
        blk = x[i * SUBLANES:(i + 1) * SUBLANES, :]
        for shift in (1, 2, 4):
            blk = op(blk, jnp.where(sub >= shift, pltpu.roll(blk, shift, axis=0), identity))
        if carry is not None:
            blk = op(blk, carry)
        carry = jnp.broadcast_to(blk[SUBLANES - 1:SUBLANES, :], blk.shape)
        blocks.append(blk)
    return jnp.concatenate(blocks, axis=0)


def _log_sigmoid(x):
    return jnp.minimum(x, 0.0) - jnp.log1p(jnp.exp(-jnp.abs(x)))


def _seq_init(mconv0_ref, sconv0_ref, c0_ref, n0_ref, m0_ref, xp_sc, cp_sc, c1_ref, n1_ref, m1_ref):
    @pl.when(pl.program_id(1) == 0)
    def _():
        xp_sc[0:SUBLANES, :] = mconv0_ref[...]
        cp_sc[0:SUBLANES, :] = sconv0_ref[...]
        c1_ref[...] = c0_ref[...]
        n1_ref[...] = n0_ref[...]
        m1_ref[...] = m0_ref[...]


def _causal_conv(src_sc, w_ref, width, tt):
    acc = None
    for j in range(width):
        term = src_sc[pl.ds(SUBLANES - (width - 1) + j, tt), :] * w_ref[j:j + 1, :]
        acc = term if acc is None else acc + term
    return acc


def _carry_tail(src_sc, tail_ref, tt):
    tail = src_sc[tt:tt + SUBLANES, :]
    src_sc[0:SUBLANES, :] = tail
    tail_ref[...] = tail


def _head_slices():
    return [slice(h * M_HEAD_DIM, (h + 1) * M_HEAD_DIM) for h in range(M_HEADS)]


def _mlstm_tile(xq, v, ifp, states, wq_ref, wk_ref, bias_ref, ghead_ref, tt, between=()):
    nh = M_HEADS
    hs = _head_slices()
    seq_rows = [slice(j * tt, (j + 1) * tt) for j in range(len(states))]
    g = [ifp[rows, :] + bias_ref[...] for rows in seq_rows]
    b_c = [_scan_rows(_log_sigmoid(gj), jnp.add, 0.0) for gj in g]
    a_c = [gj - pltpu.roll(bj, LANES - nh, axis=1) for gj, bj in zip(g, b_c)]
    amax_c = [_scan_rows(aj, jnp.maximum, -jnp.inf) for aj in a_c]
    a_t = [aj.T for aj in a_c]

    r_idx = lax.broadcasted_iota(jnp.int32, (tt, tt), 0)
    c_idx = lax.broadcasted_iota(jnp.int32, (tt, tt), 1)
    causal = c_idx <= r_idx

    q_all = [_dot(xq[:, hs[h]], wq_ref[h]) * (M_HEAD_DIM ** -0.5) for h in range(nh)]
    k_all = [_dot(xq[:, hs[h]], wk_ref[h]) for h in range(nh)]
    units = [(j, h) for j in range(len(states)) for h in range(nh)]
    ids = range(len(units))
    q = [q_all[h][seq_rows[j], :] for j, h in units]
    k = [k_all[h][seq_rows[j], :] for j, h in units]
    vv = [v[seq_rows[j], hs[h]] for j, h in units]
    c_prev = [states[j][0][h] for j, h in units]
    n_prev = [states[j][1][h:h + 1, :] for j, h in units]
    m_prev = [states[j][2][h:h + 1, 0:1] for j, h in units]
    b_col = [b_c[j][:, nh + h:nh + h + 1] for j, h in units]
    a_col = [a_c[j][:, h:h + 1] for j, h in units]
    a_row = [a_t[j][h:h + 1, :] for j, h in units]
    b_last = [b_c[j][tt - 1:tt, nh + h:nh + h + 1] for j, h in units]
    mm_col = [jnp.maximum(m_prev[i], amax_c[j][:, h:h + 1]) for i, (j, h) in enumerate(units)]
    mm_last = [jnp.maximum(m_prev[i], amax_c[j][tt - 1:tt, h:h + 1]) for i, (j, h) in enumerate(units)]

    extra = []
    pending = list(between)

    def run_one():
        if pending:
            extra.append(pending.pop(0)())

    w_inter = [jnp.exp(m_prev[i] - mm_col[i]) for i in ids]
    qk = [lax.dot_general(q[i].astype(BF16), k[i].astype(BF16), (((1,), (1,)), ((), ())),
                          preferred_element_type=F32) for i in ids]
    run_one()
    s = [jnp.exp(jnp.where(causal, a_row[i] - mm_col[i], -jnp.inf)) * qk[i] for i in ids]
    run_one()
    num = [w_inter[i] * _dot(q[i], c_prev[i]) + _dot(s[i], vv[i]) for i in ids]
    den = [w_inter[i] * jnp.sum(q[i] * n_prev[i], axis=-1, keepdims=True)
           + jnp.sum(s[i], axis=-1, keepdims=True) for i in ids]
    run_one()
    out = [num[i] * (1.0 / jnp.maximum(jnp.abs(den[i]), jnp.exp(-(b_col[i] + mm_col[i])))) for i in ids]
    out = [out[i] * lax.rsqrt(jnp.mean(out[i] * out[i], axis=-1, keepdims=True) + RMS_EPS)
           * ghead_ref[:, hs[units[i][1]]] for i in ids]
    while pending:
        run_one()

    decay = [jnp.exp(m_prev[i] - mm_last[i]) for i in ids]
    wk = [jnp.exp(a_col[i] - mm_last[i]) * k[i] for i in ids]
    for i, (j, h) in enumerate(units):
        c_ref, n_ref, m_ref = states[j]
        c_ref[h] = decay[i] * c_prev[i] + lax.dot_general(
            wk[i].astype(BF16), vv[i].astype(BF16), (((0,), (0,)), ((), ())), preferred_element_type=F32)
        n_ref[h:h + 1, :] = decay[i] * n_prev[i] + jnp.sum(wk[i], axis=0, keepdims=True)
        m_ref[h:h + 1, :] = jnp.broadcast_to(b_last[i] + mm_last[i], (1, LANES))
    return [out[j * nh:(j + 1) * nh] for j in range(len(states))], extra


def _mlstm_kernel(xm_ref, v_ref, cx_ref, if_ref, mconv0_ref, sconv0_ref, c0_ref, n0_ref, m0_ref,
                  w_mconv_ref, w_sconv_ref, wq_ref, wk_ref, bias_ref, ghead_ref,
                  hm_ref, u_ref, c1_ref, n1_ref, m1_ref, mconv1_ref, sconv1_ref,
                  xp_sc, cp_sc, *, tt, nb):
    hs = _head_slices()
    xq, states = [], []
    for j in range(nb):
        rows = slice(j * tt, (j + 1) * tt)
        xp, cp, state = xp_sc.at[j], cp_sc.at[j], (c1_ref.at[j], n1_ref.at[j], m1_ref.at[j])
        _seq_init(mconv0_ref.at[j], sconv0_ref.at[j], c0_ref.at[j], n0_ref.at[j], m0_ref.at[j], xp, cp, *state)
        xp[SUBLANES:SUBLANES + tt, :] = xm_ref[rows, :]
        cp[SUBLANES:SUBLANES + tt, :] = cx_ref[rows, :]
        xq.append(_silu(_causal_conv(xp, w_mconv_ref, M_CONV_W, tt)))
        u_ref[rows, :] = _causal_conv(cp, w_sconv_ref, S_CONV_W, tt)
        _carry_tail(xp, mconv1_ref.at[j], tt)
        _carry_tail(cp, sconv1_ref.at[j], tt)
        states.append(state)
    xq = xq[0] if nb == 1 else jnp.concatenate(xq, axis=0)
    out, _ = _mlstm_tile(xq, v_ref[...], if_ref[...], states, wq_ref, wk_ref, bias_ref, ghead_ref, tt)
    for j in range(nb):
        for h, sl in enumerate(hs):
            hm_ref[j * tt:(j + 1) * tt, sl] = out[j][h]


def _mlstm(xm, v, cx, ifp, mconv0, sconv0, c0, n0, m0, w_mconv, w_sconv, wq, wk, bias, ghead, *, bsz, t, tt, nb):
    nt = t // tt
    n = bsz * t
    assert nb == 1 or nt == 1, "several sequences per step only when a step covers whole sequences"
    tok = lambda b, i: (b * nt + i, 0)
    seq3 = lambda b, i: (b, 0, 0)
    seq4 = lambda b, i: (b, 0, 0, 0)
    big = pl.BlockSpec((nb * tt, D_MODEL), tok)
    tail = pl.BlockSpec((nb, SUBLANES, D_MODEL), seq3)
    c_spec = pl.BlockSpec((nb, M_HEADS, M_HEAD_DIM, M_HEAD_DIM), seq4)
    n_spec = pl.BlockSpec((nb, M_HEADS, M_HEAD_DIM), seq3)
    m_spec = pl.BlockSpec((nb, SUBLANES, LANES), seq3)
    return pl.pallas_call(
        functools.partial(_mlstm_kernel, tt=tt, nb=nb),
        grid=(bsz // nb, nt),
        in_specs=[big, big, big, pl.BlockSpec((nb * tt, LANES), tok), tail, tail, c_spec, n_spec, m_spec,
                  _resident(w_mconv.shape), _resident(w_sconv.shape), _resident(wq.shape),
                  _resident(wk.shape), _resident(bias.shape), _resident(ghead.shape)],
        out_specs=[big, big, c_spec, n_spec, m_spec, tail, tail],
        out_shape=[jax.ShapeDtypeStruct((n, D_MODEL), F32), jax.ShapeDtypeStruct((n, D_MODEL), F32),
                   jax.ShapeDtypeStruct(c0.shape, F32), jax.ShapeDtypeStruct(n0.shape, F32),
                   jax.ShapeDtypeStruct(m0.shape, F32),
                   jax.ShapeDtypeStruct(mconv0.shape, F32), jax.ShapeDtypeStruct(sconv0.shape, F32)],
        scratch_shapes=[pltpu.VMEM((nb, tt + SUBLANES, D_MODEL), F32),
                        pltpu.VMEM((nb, tt + SUBLANES, D_MODEL), F32)],
        compiler_params=pltpu.CompilerParams(dimension_semantics=("arbitrary", "arbitrary"),
                                             vmem_limit_bytes=VMEM_LIMIT),
        name="mlstm",
    )(xm, v, cx, ifp, mconv0, sconv0, c0, n0, m0, w_mconv, w_sconv, wq, wk, bias, ghead)


def _merge_kernel(x_ref, hm_ref, o_ref, ga_ref, gb_ref, sb_ref, u_ref, wa_ref, wsout_ref, wo_ref, x1_ref):
    y_a = _dot(_sigmoid(o_ref[...]) * hm_ref[...], wa_ref[...])
    y_b = _dot(sb_ref[...] * u_ref[...], wsout_ref[...])
    merged = _sigmoid(ga_ref[...]) * y_a + _sigmoid(gb_ref[...]) * y_b
    x1_ref[...] = x_ref[...] + _dot(merged, wo_ref[...])


def _merge(x, hm, o, ga, gb, sb, u, w_a, w_sout, w_o, tm):
    n = x.shape[0]
    big = pl.BlockSpec((tm, D_MODEL), lambda i: (i, 0))
    return pl.pallas_call(
        _merge_kernel,
        grid=(n // tm,),
        in_specs=[big] * 7 + [_resident(w_a.shape), _resident(w_sout.shape), _resident(w_o.shape)],
        out_specs=big,
        out_shape=jax.ShapeDtypeStruct((n, D_MODEL), F32),
        compiler_params=pltpu.CompilerParams(dimension_semantics=("arbitrary",),
                                             vmem_limit_bytes=VMEM_LIMIT),
        name="merge",
    )(x, hm, o, ga, gb, sb, u, w_a, w_sout, w_o)


def _mixer_kernel(x_ref, g_ref, w_in_ref, w_mconv_ref, w_sconv_ref, wq_ref, wk_ref,
                  bias_ref, ghead_ref, wa_ref, wsout_ref, wo_ref, wg32_ref, wu32_ref, wd32_ref,
                  x1_ref, c1_ref, n1_ref, m1_ref, mconv1_ref, sconv1_ref, wg16_ref, wu16_ref, wd16_ref,
                  xp_sc, cp_sc, hm_sc, *, tt):
    @pl.when(pl.program_id(1) == 0)
    def _():
        xp_sc[0:SUBLANES, :] = jnp.zeros((SUBLANES, D_MODEL), F32)
        cp_sc[0:SUBLANES, :] = jnp.zeros((SUBLANES, D_MODEL), F32)
        c1_ref[...] = jnp.zeros(c1_ref.shape, F32)
        n1_ref[...] = jnp.zeros(n1_ref.shape, F32)
        m1_ref[...] = jnp.zeros(m1_ref.shape, F32)

    wg16_ref[...] = wg32_ref[...].astype(BF16)
    wu16_ref[...] = wu32_ref[...].astype(BF16)
    wd16_ref[...] = wd32_ref[...].astype(BF16)

    proj = _projector(_rms(x_ref[...], g_ref[...]).astype(BF16), w_in_ref)
    xp_sc[SUBLANES:SUBLANES + tt, :] = proj("xm")
    ifp = proj("gates")
    cp_sc[SUBLANES:SUBLANES + tt, :] = proj("sc") * proj("sx")
    xq = _silu(_causal_conv(xp_sc, w_mconv_ref, M_CONV_W, tt))
    u = _causal_conv(cp_sc, w_sconv_ref, S_CONV_W, tt)
    _carry_tail(xp_sc, mconv1_ref, tt)
    _carry_tail(cp_sc, sconv1_ref, tt)
    v = proj("v")
    hs = _head_slices()
    (out,), (y_b, sig_o, sig_ga, sig_gb) = _mlstm_tile(
        xq, v, ifp, [(c1_ref, n1_ref, m1_ref)], wq_ref, wk_ref, bias_ref, ghead_ref, tt,
        between=(lambda: _dot(proj("sb") * u, wsout_ref[...]),
                 lambda: _sigmoid(proj("o")),
                 lambda: _sigmoid(proj("ga")),
                 lambda: _sigmoid(proj("gb"))))
    for hd, sl in enumerate(hs):
        hm_sc[:, sl] = (sig_o[:, sl] * out[hd]).astype(BF16)
    y_a = jnp.dot(hm_sc[...], wa_ref[...], preferred_element_type=F32)
    merged = sig_ga * y_a + sig_gb * y_b
    x1_ref[...] = x_ref[...] + _dot(merged, wo_ref[...])


def _mixer(x, wts, experts_f32, *, bsz, t, tt):
    nt = t // tt
    steps = bsz * nt
    tok = lambda b, i: (b * nt + i, 0)
    seq3 = lambda b, i: (b, 0, 0)
    seq4 = lambda b, i: (b, 0, 0, 0)
    big = pl.BlockSpec((tt, D_MODEL), tok)
    tail = pl.BlockSpec((None, SUBLANES, D_MODEL), seq3)
    c_spec = pl.BlockSpec((None, M_HEADS, M_HEAD_DIM, M_HEAD_DIM), seq4)
    n_spec = pl.BlockSpec((None, M_HEADS, M_HEAD_DIM), seq3)
    m_spec = pl.BlockSpec((None, SUBLANES, LANES), seq3)
    names = ("g_mix", "w_in", "w_mconv", "w_sconv", "w_q", "w_k", "gate_bias", "g_head",
             "w_a", "w_sout", "w_o")
    weights = [wts[k] for k in names]
    flat = [w.reshape(-1, w.shape[-1]) for w in experts_f32]
    cast_specs = [pl.BlockSpec((w.shape[0] // steps, w.shape[1]), tok) for w in flat]
    outs = pl.pallas_call(
        functools.partial(_mixer_kernel, tt=tt),
        grid=(bsz, nt),
        in_specs=[big] + [_resident(w.shape) for w in weights] + cast_specs,
        out_specs=[big, c_spec, n_spec, m_spec, tail, tail] + cast_specs,
        out_shape=[jax.ShapeDtypeStruct(x.shape, F32),
                   jax.ShapeDtypeStruct((bsz, M_HEADS, M_HEAD_DIM, M_HEAD_DIM), F32),
                   jax.ShapeDtypeStruct((bsz, M_HEADS, M_HEAD_DIM), F32),
                   jax.ShapeDtypeStruct((bsz, SUBLANES, LANES), F32),
                   jax.ShapeDtypeStruct((bsz, SUBLANES, D_MODEL), F32),
                   jax.ShapeDtypeStruct((bsz, SUBLANES, D_MODEL), F32)]
                  + [jax.ShapeDtypeStruct(w.shape, BF16) for w in flat],
        scratch_shapes=[pltpu.VMEM((tt + SUBLANES, D_MODEL), F32), pltpu.VMEM((tt + SUBLANES, D_MODEL), F32),
                        pltpu.VMEM((tt, D_MODEL), BF16)],
        compiler_params=pltpu.CompilerParams(dimension_semantics=("arbitrary", "arbitrary"),
                                             vmem_limit_bytes=VMEM_LIMIT),
        name="mixer",
    )(x, *weights, *flat)
    return outs[:6], [o.reshape(w.shape) for o, w in zip(outs[6:], experts_f32)]


MOE_WINDOW = 1024
MOE_ROW_TILE = 128
MOE_ALIGN = 16
EXPERTS_PER_STEP = EXPERTS_PER_GROUP
MOE_STEPS = N_EXPERTS // EXPERTS_PER_STEP
MOE_CHUNK = 256
MOE_SORT_CHUNK = 1280
MOE_UNSORT_CHUNK = 512
MOE_SORTED_ROWS = -(-(MOE_WINDOW + N_GROUPS * (MOE_ALIGN - 1) + MOE_ROW_TILE) // MOE_CHUNK) * MOE_CHUNK


def _split_bf16(x):
    hi = x.astype(BF16)
    lo = (x - hi.astype(F32)).astype(BF16)
    return hi, lo


CW_LANE_STRIDE = 40


def _pack_split3(cw):
    hi = cw.astype(BF16).astype(F32)
    r1 = cw - hi
    mid = r1.astype(BF16).astype(F32)
    lo = (r1 - mid).astype(BF16).astype(F32)
    return (hi + pltpu.roll(mid, CW_LANE_STRIDE, axis=1) + pltpu.roll(lo, 2 * CW_LANE_STRIDE, axis=1)).astype(BF16)


def _unpack_split3(packed):
    return (packed + pltpu.roll(packed, LANES - CW_LANE_STRIDE, axis=1)
            + pltpu.roll(packed, LANES - 2 * CW_LANE_STRIDE, axis=1))


def _route(logits):
    lane = lax.broadcasted_iota(jnp.int32, logits.shape, 1)
    neg = -jnp.inf
    big = jnp.int32(LANES)
    is_grp = lane < N_GROUPS
    g_max = jnp.max(jnp.where(is_grp, logits, neg), axis=-1, keepdims=True)
    g_sel = jnp.min(jnp.where(is_grp & (logits == g_max), lane, big), axis=-1, keepdims=True)
    p_grp = 1.0 / jnp.sum(jnp.where(is_grp, jnp.exp(logits - g_max), 0.0), axis=-1, keepdims=True)
    lo = N_GROUPS + g_sel * EXPERTS_PER_GROUP
    in_grp = (lane >= lo) & (lane < lo + EXPERTS_PER_GROUP)
    v1 = jnp.max(jnp.where(in_grp, logits, neg), axis=-1, keepdims=True)
    e1 = jnp.min(jnp.where(in_grp & (logits == v1), lane, big), axis=-1, keepdims=True)
    rest = in_grp & (lane != e1)
    v2 = jnp.max(jnp.where(rest, logits, neg), axis=-1, keepdims=True)
    e2 = jnp.min(jnp.where(rest & (logits == v2), lane, big), axis=-1, keepdims=True)
    z = jnp.exp(v2 - v1)
    w1 = p_grp / (1.0 + z)
    w2 = p_grp * z / (1.0 + z)
    return g_sel, jnp.where(lane == e1, w1, jnp.where(lane == e2, w2, 0.0))


def _moe_tail_kernel(x1_ref, p_ref, gffn_ref, wr_ref, br_ref, wg_ref, wu_ref, wd_ref,
                     gple_ref, wpg_ref, wpp_ref, gfin_ref, y_ref,
                     xs_sc, cws_sc, osort_sc, pos_sc, off_sm, nt_sm):
    s = pl.program_id(1)
    w = MOE_WINDOW

    @pl.when(s == 0)
    def _():
        hn = _rms(x1_ref[...], gffn_ref[...])
        hn_hi, hn_lo = _split_bf16(hn)
        hh_hl = jnp.dot(hn_hi, wr_ref[...], preferred_element_type=F32)
        logits = (hh_hl[:, :LANES] + hh_hl[:, LANES:]
                  + jnp.dot(hn_lo, wr_ref[:, :LANES], preferred_element_type=F32)) + br_ref[...]
        g_sel, cw = _route(logits)
        lane = lax.broadcasted_iota(jnp.int32, (w, LANES), 1)
        onehot = jnp.where(lane == g_sel, 1.0, 0.0)
        cum = _scan_rows(onehot, jnp.add, 0.0)
        cnt = cum[w - 1:w, :].astype(jnp.int32)
        cnt_pad = ((cnt + (MOE_ALIGN - 1)) // MOE_ALIGN) * MOE_ALIGN
        lane1 = lax.broadcasted_iota(jnp.int32, (1, LANES), 1)
        off = jnp.zeros((1, LANES), jnp.int32)
        for gi in range(N_GROUPS - 1):
            off = off + jnp.where(lane1 > gi, cnt_pad[:, gi:gi + 1], 0)
        n_tiles = (cnt + (MOE_ROW_TILE - 1)) // MOE_ROW_TILE
        for gi in range(N_GROUPS):
            off_sm[gi] = off[0, gi]
            nt_sm[gi] = n_tiles[0, gi]
        pos = jnp.sum(onehot * (off.astype(F32) + cum - 1.0), axis=-1, keepdims=True)
        pos_b = jnp.broadcast_to(pos, (w, LANES))
        pos_sc[...] = pos_b
        pos_row = pos_b.T[0:1, :].astype(jnp.int32)
        cw_packed = _pack_split3(cw)
        for c in range(MOE_SORTED_ROWS // MOE_SORT_CHUNK):
            rows = lax.broadcasted_iota(jnp.int32, (MOE_SORT_CHUNK, w), 0) + c * MOE_SORT_CHUNK
            sel = jnp.where(rows == pos_row, 1.0, 0.0).astype(BF16)
            sl = slice(c * MOE_SORT_CHUNK, (c + 1) * MOE_SORT_CHUNK)
            xs_sc[sl, :] = jnp.dot(sel, hn_hi, preferred_element_type=F32).astype(BF16)
            cws_sc[sl, :] = _unpack_split3(jnp.dot(sel, cw_packed, preferred_element_type=F32))
        osort_sc[...] = jnp.zeros(osort_sc.shape, BF16)

    row0 = off_sm[s]
    lane_t = lax.broadcasted_iota(jnp.int32, (MOE_ROW_TILE, LANES), 1)
    first_lane = N_GROUPS + s * EXPERTS_PER_STEP

    def tile_body(i, carry):
        r0 = pl.multiple_of(row0 + i * MOE_ROW_TILE, MOE_ALIGN)
        xt = xs_sc[pl.ds(r0, MOE_ROW_TILE), :]
        cwt = cws_sc[pl.ds(r0, MOE_ROW_TILE), :]
        parts = []
        for e in range(EXPERTS_PER_STEP):
            hg = jnp.dot(xt, wg_ref[e], preferred_element_type=F32)
            hu = jnp.dot(xt, wu_ref[e], preferred_element_type=F32)
            col = jnp.sum(jnp.where(lane_t == first_lane + e, cwt, 0.0), axis=-1, keepdims=True)
            parts.append((_silu(hg) * hu * col).astype(BF16))
        he = jnp.concatenate(parts, axis=-1)
        out = jnp.dot(he, wd_ref[...], preferred_element_type=F32)
        osort_sc[pl.ds(r0, MOE_ROW_TILE), :] = out.astype(BF16)
        return carry

    lax.fori_loop(0, nt_sm[s], tile_body, 0)

    @pl.when(s == MOE_STEPS - 1)
    def _():
        osort = osort_sc[...]
        for c in range(w // MOE_UNSORT_CHUNK):
            sl = slice(c * MOE_UNSORT_CHUNK, (c + 1) * MOE_UNSORT_CHUNK)
            pos_col = pos_sc[sl, 0:1].astype(jnp.int32)
            cols = lax.broadcasted_iota(jnp.int32, (MOE_UNSORT_CHUNK, MOE_SORTED_ROWS), 1)
            sel = jnp.where(cols == pos_col, 1.0, 0.0).astype(BF16)
            x2 = x1_ref[sl, :] + jnp.dot(sel, osort, preferred_element_type=F32)
            gate = _sigmoid(_dot(_rms(x2, gple_ref[...]), wpg_ref[...]))
            x3 = x2 + gate * _dot(p_ref[sl, :], wpp_ref[...])
            y_ref[sl, :] = _rms(x3, gfin_ref[...])


def _moe_tail(x1, p, g_ffn, w_r, b_r, wg, wu, wd, g_ple, w_pg, w_pp, g_fin):
    n = x1.shape[0]
    tm = MOE_WINDOW
    row = lambda i, s: (i, 0)
    step = lambda i, s: (s, 0, 0)
    big = pl.BlockSpec((tm, D_MODEL), row)
    sw = EXPERTS_PER_STEP * D_EXPERT
    return pl.pallas_call(
        _moe_tail_kernel,
        grid=(n // tm, MOE_STEPS),
        in_specs=[big, pl.BlockSpec((tm, PLE_DIM), row), _resident(g_ffn.shape), _resident(w_r.shape),
                  _resident(b_r.shape),
                  pl.BlockSpec((EXPERTS_PER_STEP, D_MODEL, D_EXPERT), step),
                  pl.BlockSpec((EXPERTS_PER_STEP, D_MODEL, D_EXPERT), step),
                  pl.BlockSpec((None, sw, D_MODEL), step),
                  _resident(g_ple.shape), _resident(w_pg.shape), _resident(w_pp.shape), _resident(g_fin.shape)],
        out_specs=big,
        out_shape=jax.ShapeDtypeStruct((n, D_MODEL), F32),
        scratch_shapes=[pltpu.VMEM((MOE_SORTED_ROWS, D_MODEL), BF16), pltpu.VMEM((MOE_SORTED_ROWS, LANES), F32),
                        pltpu.VMEM((MOE_SORTED_ROWS, D_MODEL), BF16), pltpu.VMEM((tm, LANES), F32),
                        pltpu.SMEM((N_GROUPS,), jnp.int32), pltpu.SMEM((N_GROUPS,), jnp.int32)],
        compiler_params=pltpu.CompilerParams(dimension_semantics=("arbitrary", "arbitrary"),
                                             vmem_limit_bytes=VMEM_LIMIT),
        name="moe_tail",
    )(x1, p, g_ffn, w_r, b_r, wg, wu, wd, g_ple, w_pg, w_pp, g_fin)


PROMPT_TIME_TILE = 256
SAMPLE_ROW_TILE = 512
SAMPLE_SEQS_PER_STEP = 8


def _pad_tail(buf):
    return jnp.pad(buf, ((0, 0), (SUBLANES - buf.shape[1], 0), (0, 0)))


def _moe(x1, p, wts):
    return _moe_tail(x1, p, wts["g_ffn"], wts["w_r"], wts["b_r"], wts["w_gate"], wts["w_up"], wts["w_down"],
                     wts["g_ple"], wts["w_ple_gate"], wts["w_ple_proj"], wts["g_final"])


def _outputs(y, shape, c1, n1, m1p, mconv1, sconv1):
    return (y.reshape(shape), c1[None], n1[None], m1p[None, :, :M_HEADS, 0],
            mconv1[None, :, SUBLANES - (M_CONV_W - 1):, :], sconv1[None, :, SUBLANES - (S_CONV_W - 1):, :])


def _prompt_group(x, p, wts, experts_f32, *, tt):
    bsz, t, _ = x.shape
    (x1, *states), (wg, wu, wd) = _mixer(x.reshape(bsz * t, D_MODEL), wts, experts_f32, bsz=bsz, t=t, tt=tt)
    wts.update(w_gate=wg, w_up=wu, w_down=wd.reshape(MOE_STEPS, EXPERTS_PER_STEP * D_EXPERT, D_MODEL))
    return _outputs(_moe(x1, p.reshape(bsz * t, PLE_DIM), wts), x.shape, *states)


def _sample_group(x, p, mconv0, sconv0, c0, n0, m0, wts, *, tm):
    bsz, t, _ = x.shape
    xf = x.reshape(bsz * t, D_MODEL)
    m0p = jnp.broadcast_to(jnp.pad(m0, ((0, 0), (0, SUBLANES - M_HEADS)))[:, :, None], (bsz, SUBLANES, LANES))
    xm, v, o, ifp, sb, cx, ga, gb = _in_proj(xf, wts["g_mix"], wts["w_in"], tm)
    hm, u, *states = _mlstm(
        xm, v, cx, ifp, _pad_tail(mconv0), _pad_tail(sconv0), c0, n0, m0p,
        wts["w_mconv"], wts["w_sconv"], wts["w_q"], wts["w_k"], wts["gate_bias"], wts["g_head"],
        bsz=bsz, t=t, tt=t, nb=SAMPLE_SEQS_PER_STEP)
    x1 = _merge(xf, hm, o, ga, gb, sb, u, wts["w_a"], wts["w_sout"], wts["w_o"], tm)
    return _outputs(_moe(x1, p.reshape(bsz * t, PLE_DIM), wts), x.shape, *states)


def kernel(x_prompt, x_sample, p_prompt, p_sample, state_mlstm_C, state_mlstm_n, state_mlstm_m, state_mlstm_conv, state_sconv, g_mix, w_in, w_mconv, w_q, w_k, b_i, b_f, g_head, w_a, w_sconv, w_sout, w_o, g_ffn, w_rg, b_rg, w_re, b_re, w_gate, w_up, w_down, g_ple, w_ple_gate, w_ple_proj, g_final):
    assert g_mix.shape[0] == 1, "single-layer trunk"
    d = D_MODEL
    w_in0 = w_in[0]
    n_if = 2 * M_HEADS
    w_router = jnp.pad(jnp.concatenate([w_rg[0], w_re[0]], axis=1), ((0, 0), (0, LANES - N_GROUPS - N_EXPERTS)))
    wr_hi = w_router.astype(BF16)
    wr_lo = (w_router - wr_hi.astype(F32)).astype(BF16)
    w_r = jnp.concatenate([wr_hi, wr_lo], axis=1)
    wts = {
        "g_mix": g_mix,
        "w_in": _pack_w_in(w_in0),
        "w_mconv": w_mconv[0], "w_sconv": w_sconv[0],
        "w_q": w_q[0].astype(BF16), "w_k": w_k[0].astype(BF16),
        "gate_bias": jnp.pad(jnp.concatenate([b_i[0], b_f[0]])[None, :], ((0, 0), (0, LANES - n_if))),
        "g_head": g_head[0].reshape(1, d),
        "w_a": w_a[0].astype(BF16), "w_sout": w_sout[0].astype(BF16), "w_o": w_o[0].astype(BF16),
        "g_ffn": g_ffn, "w_r": w_r,
        "b_r": jnp.pad(jnp.concatenate([b_rg[0], b_re[0]])[None, :], ((0, 0), (0, LANES - N_GROUPS - N_EXPERTS))),
        "g_ple": g_ple, "w_ple_gate": w_ple_gate[0].astype(BF16), "w_ple_proj": w_ple_proj[0].astype(BF16),
        "g_final": g_final[None, :],
    }
    yp, *st_p = _prompt_group(x_prompt, p_prompt[0], wts, (w_gate[0], w_up[0], w_down[0]), tt=PROMPT_TIME_TILE)
    ys, *st_s = _sample_group(x_sample, p_sample[0], state_mlstm_conv[0], state_sconv[0],
                              state_mlstm_C[0], state_mlstm_n[0], state_mlstm_m[0], wts, tm=SAMPLE_ROW_TILE)
    return (yp, ys, *st_p, *st_s)
```

```python
import functools

import jax
import jax.numpy as jnp
from jax import lax
from jax.experimental import pallas as pl
from jax.experimental.pallas import tpu as pltpu

D_MODEL = 1024
M_HEADS = 4
M_HEAD_DIM = 256
M_CONV_W = 4
S_CONV_W = 3
N_GROUPS = 4
EXPERTS_PER_GROUP = 8
N_EXPERTS = N_GROUPS * EXPERTS_PER_GROUP
D_EXPERT = 256
PLE_DIM = 256
RMS_EPS = 1e-6

LANES = 128
SUBLANES = 8
VMEM_LIMIT = 60 * 1024 * 1024

BF16 = jnp.bfloat16
F32 = jnp.float32


def _rms(x, g):
    return x * lax.rsqrt(jnp.mean(x * x, axis=-1, keepdims=True) + RMS_EPS) * g


def _dot(a, b):
    return jnp.dot(a.astype(BF16), b.astype(BF16), preferred_element_type=F32)


def _sigmoid(x):
    return 0.5 * jnp.tanh(0.5 * x) + 0.5


def _silu(x):
    return x * _sigmoid(x)


def _resident(shape):
    nd = len(shape)
    return pl.BlockSpec(shape, lambda *_: (0,) * nd, pipeline_mode=pl.Buffered(1))


_GATE_COL = 3 * D_MODEL
_REST_COL = _GATE_COL + LANES
IN_COL = {"xm": 0, "v": D_MODEL, "o": 2 * D_MODEL, "sb": _REST_COL, "sc": _REST_COL + D_MODEL,
          "sx": _REST_COL + 2 * D_MODEL, "ga": _REST_COL + 3 * D_MODEL, "gb": _REST_COL + 4 * D_MODEL}


def _projector(h, w_in_ref):
    def proj(name):
        lo, width = (_GATE_COL, LANES) if name == "gates" else (IN_COL[name], D_MODEL)
        return jnp.dot(h, w_in_ref[:, lo:lo + width], preferred_element_type=F32)
    return proj


def _pack_w_in_kernel(w_ref, out_ref):
    n_if = 2 * M_HEADS
    rows = w_ref.shape[0]
    out_ref[:, 0:_GATE_COL] = w_ref[:, 0:_GATE_COL].astype(BF16)
    gates = jnp.concatenate([w_ref[:, _GATE_COL:_GATE_COL + n_if], jnp.zeros((rows, LANES - n_if), F32)], axis=1)
    out_ref[:, _GATE_COL:_REST_COL] = gates.astype(BF16)
    out_ref[:, _REST_COL:] = w_ref[:, _GATE_COL + n_if:].astype(BF16)


def _pack_w_in(w_in):
    _, rows, cols = w_in.shape
    tr = 128
    out_cols = cols - 2 * M_HEADS + LANES
    return pl.pallas_call(
        _pack_w_in_kernel,
        grid=(rows // tr,),
        in_specs=[pl.BlockSpec((None, tr, cols), lambda i: (0, i, 0))],
        out_specs=pl.BlockSpec((tr, out_cols), lambda i: (i, 0)),
        out_shape=jax.ShapeDtypeStruct((rows, out_cols), BF16),
        compiler_params=pltpu.CompilerParams(dimension_semantics=("arbitrary",), vmem_limit_bytes=VMEM_LIMIT),
        name="pack_w_in",
    )(w_in)


def _in_proj_kernel(x_ref, g_ref, w_in_ref,
                    xm_ref, v_ref, o_ref, if_ref, sb_ref, cx_ref, ga_ref, gb_ref):
    proj = _projector(_rms(x_ref[...], g_ref[...]).astype(BF16), w_in_ref)
    xm_ref[...] = proj("xm")
    v_ref[...] = proj("v")
    o_ref[...] = proj("o")
    if_ref[...] = proj("gates")
    sb_ref[...] = proj("sb")
    cx_ref[...] = proj("sc") * proj("sx")
    ga_ref[...] = proj("ga")
    gb_ref[...] = proj("gb")


def _in_proj(x, g_mix, w_in, tm):
    n = x.shape[0]
    row = lambda i: (i, 0)
    big = pl.BlockSpec((tm, D_MODEL), row)
    outs = [jax.ShapeDtypeStruct((n, D_MODEL), F32)] * 3 + [jax.ShapeDtypeStruct((n, LANES), F32)] \
        + [jax.ShapeDtypeStruct((n, D_MODEL), F32)] * 4
    return pl.pallas_call(
        _in_proj_kernel,
        grid=(n // tm,),
        in_specs=[big, _resident(g_mix.shape), _resident(w_in.shape)],
        out_specs=[big, big, big, pl.BlockSpec((tm, LANES), row), big, big, big, big],
        out_shape=outs,
        compiler_params=pltpu.CompilerParams(dimension_semantics=("arbitrary",),
                                             vmem_limit_bytes=VMEM_LIMIT),
        name="in_proj",
    )(x, g_mix, w_in)


def _scan_rows(x, op, identity):
    sub = lax.broadcasted_iota(jnp.int32, (SUBLANES, x.shape[1]), 0)
    blocks, carry = [], None
    for i in range(x.shape[0] // SUBLANES):
        blk = x[i * SUBLANES:(i + 1) * SUBLANES, :]
        for shift in (1, 2, 4):
            blk = op(blk, jnp.where(sub >= shift, pltpu.roll(blk, shift, axis=0), identity))
        if carry is not None:
            blk = op(blk, carry)
        carry = jnp.broadcast_to(blk[SUBLANES - 1:SUBLANES, :], blk.shape)
        blocks.append(blk)
    return jnp.concatenate(blocks, axis=0)


def _log_sigmoid(x):
    return jnp.minimum(x, 0.0) - jnp.log1p(jnp.exp(-jnp.abs(x)))


def _seq_init(mconv0_ref, sconv0_ref, c0_ref, n0_ref, m0_ref, xp_sc, cp_sc, c1_ref, n1_ref, m1_ref):
    @pl.when(pl.program_id(1) == 0)
    def _():
        xp_sc[0:SUBLANES, :] = mconv0_ref[...]
        cp_sc[0:SUBLANES, :] = sconv0_ref[...]
        c1_ref[...] = c0_ref[...]
        n1_ref[...] = n0_ref[...]
        m1_ref[...] = m0_ref[...]


def _causal_conv(src_sc, w_ref, width, tt):
    acc = None
    for j in range(width):
        term = src_sc[pl.ds(SUBLANES - (width - 1) + j, tt), :] * w_ref[j:j + 1, :]
        acc = term if acc is None else acc + term
    return acc


def _carry_tail(src_sc, tail_ref, tt):
    tail = src_sc[tt:tt + SUBLANES, :]
    src_sc[0:SUBLANES, :] = tail
    tail_ref[...] = tail


def _head_slices():
    return [slice(h * M_HEAD_DIM, (h + 1) * M_HEAD_DIM) for h in range(M_HEADS)]


def _mlstm_tile(xq, v, ifp, states, wq_ref, wk_ref, bias_ref, ghead_ref, tt, between=()):
    nh = M_HEADS
    hs = _head_slices()
    seq_rows = [slice(j * tt, (j + 1) * tt) for j in range(len(states))]
    g = [ifp[rows, :] + bias_ref[...] for rows in seq_rows]
    b_c = [_scan_rows(_log_sigmoid(gj), jnp.add, 0.0) for gj in g]
    a_c = [gj - pltpu.roll(bj, LANES - nh, axis=1) for gj, bj in zip(g, b_c)]
    amax_c = [_scan_rows(aj, jnp.maximum, -jnp.inf) for aj in a_c]
    a_t = [aj.T for aj in a_c]

    r_idx = lax.broadcasted_iota(jnp.int32, (tt, tt), 0)
    c_idx = lax.broadcasted_iota(jnp.int32, (tt, tt), 1)
    causal = c_idx <= r_idx

    q_all = [_dot(xq[:, hs[h]], wq_ref[h]) * (M_HEAD_DIM ** -0.5) for h in range(nh)]
    k_all = [_dot(xq[:, hs[h]], wk_ref[h]) for h in range(nh)]
    units = [(j, h) for j in range(len(states)) for h in range(nh)]
    ids = range(len(units))
    q = [q_all[h][seq_rows[j], :] for j, h in units]
    k = [k_all[h][seq_rows[j], :] for j, h in units]
    vv = [v[seq_rows[j], hs[h]] for j, h in units]
    c_prev = [states[j][0][h] for j, h in units]
    n_prev = [states[j][1][h:h + 1, :] for j, h in units]
    m_prev = [states[j][2][h:h + 1, 0:1] for j, h in units]
    b_col = [b_c[j][:, nh + h:nh + h + 1] for j, h in units]
    a_col = [a_c[j][:, h:h + 1] for j, h in units]
    a_row = [a_t[j][h:h + 1, :] for j, h in units]
    b_last = [b_c[j][tt - 1:tt, nh + h:nh + h + 1] for j, h in units]
    mm_col = [jnp.maximum(m_prev[i], amax_c[j][:, h:h + 1]) for i, (j, h) in enumerate(units)]
    mm_last = [jnp.maximum(m_prev[i], amax_c[j][tt - 1:tt, h:h + 1]) for i, (j, h) in enumerate(units)]

    extra = []
    pending = list(between)

    def run_one():
        if pending:
            extra.append(pending.pop(0)())

    w_inter = [jnp.exp(m_prev[i] - mm_col[i]) for i in ids]
    qk = [lax.dot_general(q[i].astype(BF16), k[i].astype(BF16), (((1,), (1,)), ((), ())),
                          preferred_element_type=F32) for i in ids]
    run_one()
    s = [jnp.exp(jnp.where(causal, a_row[i] - mm_col[i], -jnp.inf)) * qk[i] for i in ids]
    run_one()
    num = [w_inter[i] * _dot(q[i], c_prev[i]) + _dot(s[i], vv[i]) for i in ids]
    den = [w_inter[i] * jnp.sum(q[i] * n_prev[i], axis=-1, keepdims=True)
           + jnp.sum(s[i], axis=-1, keepdims=True) for i in ids]
    run_one()
    out = [num[i] * (1.0 / jnp.maximum(jnp.abs(den[i]), jnp.exp(-(b_col[i] + mm_col[i])))) for i in ids]
    out = [out[i] * lax.rsqrt(jnp.mean(out[i] * out[i], axis=-1, keepdims=True) + RMS_EPS)
           * ghead_ref[:, hs[units[i][1]]] for i in ids]
    while pending:
        run_one()

    decay = [jnp.exp(m_prev[i] - mm_last[i]) for i in ids]
    wk = [jnp.exp(a_col[i] - mm_last[i]) * k[i] for i in ids]
    for i, (j, h) in enumerate(units):
        c_ref, n_ref, m_ref = states[j]
        c_ref[h] = decay[i] * c_prev[i] + lax.dot_general(
            wk[i].astype(BF16), vv[i].astype(BF16), (((0,), (0,)), ((), ())), preferred_element_type=F32)
        n_ref[h:h + 1, :] = decay[i] * n_prev[i] + jnp.sum(wk[i], axis=0, keepdims=True)
        m_ref[h:h + 1, :] = jnp.broadcast_to(b_last[i] + mm_last[i], (1, LANES))
    return [out[j * nh:(j + 1) * nh] for j in range(len(states))], extra


def _mlstm_kernel(xm_ref, v_ref, cx_ref, if_ref, mconv0_ref, sconv0_ref, c0_ref, n0_ref, m0_ref,
                  w_mconv_ref, w_sconv_ref, wq_ref, wk_ref, bias_ref, ghead_ref,
                  hm_ref, u_ref, c1_ref, n1_ref, m1_ref, mconv1_ref, sconv1_ref,
                  xp_sc, cp_sc, *, tt, nb):
    hs = _head_slices()
    xq, states = [], []
    for j in range(nb):
        rows = slice(j * tt, (j + 1) * tt)
        xp, cp, state = xp_sc.at[j], cp_sc.at[j], (c1_ref.at[j], n1_ref.at[j], m1_ref.at[j])
        _seq_init(mconv0_ref.at[j], sconv0_ref.at[j], c0_ref.at[j], n0_ref.at[j], m0_ref.at[j], xp, cp, *state)
        xp[SUBLANES:SUBLANES + tt, :] = xm_ref[rows, :]
        cp[SUBLANES:SUBLANES + tt, :] = cx_ref[rows, :]
        xq.append(_silu(_causal_conv(xp, w_mconv_ref, M_CONV_W, tt)))
        u_ref[rows, :] = _causal_conv(cp, w_sconv_ref, S_CONV_W, tt)
        _carry_tail(xp, mconv1_ref.at[j], tt)
        _carry_tail(cp, sconv1_ref.at[j], tt)
        states.append(state)
    xq = xq[0] if nb == 1 else jnp.concatenate(xq, axis=0)
    out, _ = _mlstm_tile(xq, v_ref[...], if_ref[...], states, wq_ref, wk_ref, bias_ref, ghead_ref, tt)
    for j in range(nb):
        for h, sl in enumerate(hs):
            hm_ref[j * tt:(j + 1) * tt, sl] = out[j][h]


def _mlstm(xm, v, cx, ifp, mconv0, sconv0, c0, n0, m0, w_mconv, w_sconv, wq, wk, bias, ghead, *, bsz, t, tt, nb):
    nt = t // tt
    n = bsz * t
    assert nb == 1 or nt == 1, "several sequences per step only when a step covers whole sequences"
    tok = lambda b, i: (b * nt + i, 0)
    seq3 = lambda b, i: (b, 0, 0)
    seq4 = lambda b, i: (b, 0, 0, 0)
    big = pl.BlockSpec((nb * tt, D_MODEL), tok)
    tail = pl.BlockSpec((nb, SUBLANES, D_MODEL), seq3)
    c_spec = pl.BlockSpec((nb, M_HEADS, M_HEAD_DIM, M_HEAD_DIM), seq4)
    n_spec = pl.BlockSpec((nb, M_HEADS, M_HEAD_DIM), seq3)
    m_spec = pl.BlockSpec((nb, SUBLANES, LANES), seq3)
    return pl.pallas_call(
        functools.partial(_mlstm_kernel, tt=tt, nb=nb),
        grid=(bsz // nb, nt),
        in_specs=[big, big, big, pl.BlockSpec((nb * tt, LANES), tok), tail, tail, c_spec, n_spec, m_spec,
                  _resident(w_mconv.shape), _resident(w_sconv.shape), _resident(wq.shape),
                  _resident(wk.shape), _resident(bias.shape), _resident(ghead.shape)],
        out_specs=[big, big, c_spec, n_spec, m_spec, tail, tail],
        out_shape=[jax.ShapeDtypeStruct((n, D_MODEL), F32), jax.ShapeDtypeStruct((n, D_MODEL), F32),
                   jax.ShapeDtypeStruct(c0.shape, F32), jax.ShapeDtypeStruct(n0.shape, F32),
                   jax.ShapeDtypeStruct(m0.shape, F32),
                   jax.ShapeDtypeStruct(mconv0.shape, F32), jax.ShapeDtypeStruct(sconv0.shape, F32)],
        scratch_shapes=[pltpu.VMEM((nb, tt + SUBLANES, D_MODEL), F32),
                        pltpu.VMEM((nb, tt + SUBLANES, D_MODEL), F32)],
        compiler_params=pltpu.CompilerParams(dimension_semantics=("arbitrary", "arbitrary"),
                                             vmem_limit_bytes=VMEM_LIMIT),
        name="mlstm",
    )(xm, v, cx, ifp, mconv0, sconv0, c0, n0, m0, w_mconv, w_sconv, wq, wk, bias, ghead)


def _merge_kernel(x_ref, hm_ref, o_ref, ga_ref, gb_ref, sb_ref, u_ref, wa_ref, wsout_ref, wo_ref, x1_ref):
    y_a = _dot(_sigmoid(o_ref[...]) * hm_ref[...], wa_ref[...])
    y_b = _dot(sb_ref[...] * u_ref[...], wsout_ref[...])
    merged = _sigmoid(ga_ref[...]) * y_a + _sigmoid(gb_ref[...]) * y_b
    x1_ref[...] = x_ref[...] + _dot(merged, wo_ref[...])


def _merge(x, hm, o, ga, gb, sb, u, w_a, w_sout, w_o, tm):
    n = x.shape[0]
    big = pl.BlockSpec((tm, D_MODEL), lambda i: (i, 0))
    return pl.pallas_call(
        _merge_kernel,
        grid=(n // tm,),
        in_specs=[big] * 7 + [_resident(w_a.shape), _resident(w_sout.shape), _resident(w_o.shape)],
        out_specs=big,
        out_shape=jax.ShapeDtypeStruct((n, D_MODEL), F32),
        compiler_params=pltpu.CompilerParams(dimension_semantics=("arbitrary",),
                                             vmem_limit_bytes=VMEM_LIMIT),
        name="merge",
    )(x, hm, o, ga, gb, sb, u, w_a, w_sout, w_o)


def _mixer_kernel(x_ref, g_ref, w_in_ref, w_mconv_ref, w_sconv_ref, wq_ref, wk_ref,
                  bias_ref, ghead_ref, wa_ref, wsout_ref, wo_ref, wg32_ref, wu32_ref, wd32_ref,
                  x1_ref, c1_ref, n1_ref, m1_ref, mconv1_ref, sconv1_ref, wg16_ref, wu16_ref, wd16_ref,
                  xp_sc, cp_sc, hm_sc, *, tt):
    @pl.when(pl.program_id(1) == 0)
    def _():
        xp_sc[0:SUBLANES, :] = jnp.zeros((SUBLANES, D_MODEL), F32)
        cp_sc[0:SUBLANES, :] = jnp.zeros((SUBLANES, D_MODEL), F32)
        c1_ref[...] = jnp.zeros(c1_ref.shape, F32)
        n1_ref[...] = jnp.zeros(n1_ref.shape, F32)
        m1_ref[...] = jnp.zeros(m1_ref.shape, F32)

    wg16_ref[...] = wg32_ref[...].astype(BF16)
    wu16_ref[...] = wu32_ref[...].astype(BF16)
    wd16_ref[...] = wd32_ref[...].astype(BF16)

    proj = _projector(_rms(x_ref[...], g_ref[...]).astype(BF16), w_in_ref)
    xp_sc[SUBLANES:SUBLANES + tt, :] = proj("xm")
    ifp = proj("gates")
    cp_sc[SUBLANES:SUBLANES + tt, :] = proj("sc") * proj("sx")
    xq = _silu(_causal_conv(xp_sc, w_mconv_ref, M_CONV_W, tt))
    u = _causal_conv(cp_sc, w_sconv_ref, S_CONV_W, tt)
    _carry_tail(xp_sc, mconv1_ref, tt)
    _carry_tail(cp_sc, sconv1_ref, tt)
    v = proj("v")
    hs = _head_slices()
    (out,), (y_b, sig_o, sig_ga, sig_gb) = _mlstm_tile(
        xq, v, ifp, [(c1_ref, n1_ref, m1_ref)], wq_ref, wk_ref, bias_ref, ghead_ref, tt,
        between=(lambda: _dot(proj("sb") * u, wsout_ref[...]),
                 lambda: _sigmoid(proj("o")),
                 lambda: _sigmoid(proj("ga")),
                 lambda: _sigmoid(proj("gb"))))
    for hd, sl in enumerate(hs):
        hm_sc[:, sl] = (sig_o[:, sl] * out[hd]).astype(BF16)
    y_a = jnp.dot(hm_sc[...], wa_ref[...], preferred_element_type=F32)
    merged = sig_ga * y_a + sig_gb * y_b
    x1_ref[...] = x_ref[...] + _dot(merged, wo_ref[...])


def _mixer(x, wts, experts_f32, *, bsz, t, tt):
    nt = t // tt
    steps = bsz * nt
    tok = lambda b, i: (b * nt + i, 0)
    seq3 = lambda b, i: (b, 0, 0)
    seq4 = lambda b, i: (b, 0, 0, 0)
    big = pl.BlockSpec((tt, D_MODEL), tok)
    tail = pl.BlockSpec((None, SUBLANES, D_MODEL), seq3)
    c_spec = pl.BlockSpec((None, M_HEADS, M_HEAD_DIM, M_HEAD_DIM), seq4)
    n_spec = pl.BlockSpec((None, M_HEADS, M_HEAD_DIM), seq3)
    m_spec = pl.BlockSpec((None, SUBLANES, LANES), seq3)
    names = ("g_mix", "w_in", "w_mconv", "w_sconv", "w_q", "w_k", "gate_bias", "g_head",
             "w_a", "w_sout", "w_o")
    weights = [wts[k] for k in names]
    flat = [w.reshape(-1, w.shape[-1]) for w in experts_f32]
    cast_specs = [pl.BlockSpec((w.shape[0] // steps, w.shape[1]), tok) for w in flat]
    outs = pl.pallas_call(
        functools.partial(_mixer_kernel, tt=tt),
        grid=(bsz, nt),
        in_specs=[big] + [_resident(w.shape) for w in weights] + cast_specs,
        out_specs=[big, c_spec, n_spec, m_spec, tail, tail] + cast_specs,
        out_shape=[jax.ShapeDtypeStruct(x.shape, F32),
                   jax.ShapeDtypeStruct((bsz, M_HEADS, M_HEAD_DIM, M_HEAD_DIM), F32),
                   jax.ShapeDtypeStruct((bsz, M_HEADS, M_HEAD_DIM), F32),
                   jax.ShapeDtypeStruct((bsz, SUBLANES, LANES), F32),
                   jax.ShapeDtypeStruct((bsz, SUBLANES, D_MODEL), F32),
                   jax.ShapeDtypeStruct((bsz, SUBLANES, D_MODEL), F32)]
                  + [jax.ShapeDtypeStruct(w.shape, BF16) for w in flat],
        scratch_shapes=[pltpu.VMEM((tt + SUBLANES, D_MODEL), F32), pltpu.VMEM((tt + SUBLANES, D_MODEL), F32),
                        pltpu.VMEM((tt, D_MODEL), BF16)],
        compiler_params=pltpu.CompilerParams(dimension_semantics=("arbitrary", "arbitrary"),
                                             vmem_limit_bytes=VMEM_LIMIT),
        name="mixer",
    )(x, *weights, *flat)
    return outs[:6], [o.reshape(w.shape) for o, w in zip(outs[6:], experts_f32)]


MOE_WINDOW = 1024
MOE_ROW_TILE = 128
MOE_ALIGN = 16
EXPERTS_PER_STEP = EXPERTS_PER_GROUP
MOE_STEPS = N_EXPERTS // EXPERTS_PER_STEP
MOE_CHUNK = 256
MOE_SORT_CHUNK = 1280
MOE_UNSORT_CHUNK = 512
MOE_SORTED_ROWS = -(-(MOE_WINDOW + N_GROUPS * (MOE_ALIGN - 1) + MOE_ROW_TILE) // MOE_CHUNK) * MOE_CHUNK


def _split_bf16(x):
    hi = x.astype(BF16)
    lo = (x - hi.astype(F32)).astype(BF16)
    return hi, lo


CW_LANE_STRIDE = 40


def _pack_split3(cw):
    hi = cw.astype(BF16).astype(F32)
    r1 = cw - hi
    mid = r1.astype(BF16).astype(F32)
    lo = (r1 - mid).astype(BF16).astype(F32)
    return (hi + pltpu.roll(mid, CW_LANE_STRIDE, axis=1) + pltpu.roll(lo, 2 * CW_LANE_STRIDE, axis=1)).astype(BF16)


def _unpack_split3(packed):
    return (packed + pltpu.roll(packed, LANES - CW_LANE_STRIDE, axis=1)
            + pltpu.roll(packed, LANES - 2 * CW_LANE_STRIDE, axis=1))


def _route(logits):
    lane = lax.broadcasted_iota(jnp.int32, logits.shape, 1)
    neg = -jnp.inf
    big = jnp.int32(LANES)
    is_grp = lane < N_GROUPS
    g_max = jnp.max(jnp.where(is_grp, logits, neg), axis=-1, keepdims=True)
    g_sel = jnp.min(jnp.where(is_grp & (logits == g_max), lane, big), axis=-1, keepdims=True)
    p_grp = 1.0 / jnp.sum(jnp.where(is_grp, jnp.exp(logits - g_max), 0.0), axis=-1, keepdims=True)
    lo = N_GROUPS + g_sel * EXPERTS_PER_GROUP
    in_grp = (lane >= lo) & (lane < lo + EXPERTS_PER_GROUP)
    v1 = jnp.max(jnp.where(in_grp, logits, neg), axis=-1, keepdims=True)
    e1 = jnp.min(jnp.where(in_grp & (logits == v1), lane, big), axis=-1, keepdims=True)
    rest = in_grp & (lane != e1)
    v2 = jnp.max(jnp.where(rest, logits, neg), axis=-1, keepdims=True)
    e2 = jnp.min(jnp.where(rest & (logits == v2), lane, big), axis=-1, keepdims=True)
    z = jnp.exp(v2 - v1)
    w1 = p_grp / (1.0 + z)
    w2 = p_grp * z / (1.0 + z)
    return g_sel, jnp.where(lane == e1, w1, jnp.where(lane == e2, w2, 0.0))


def _moe_tail_kernel(x1_ref, p_ref, gffn_ref, wr_ref, br_ref, wg_ref, wu_ref, wd_ref,
                     gple_ref, wpg_ref, wpp_ref, gfin_ref, y_ref,
                     xs_sc, cws_sc, osort_sc, pos_sc, off_sm, nt_sm):
    s = pl.program_id(1)
    w = MOE_WINDOW

    @pl.when(s == 0)
    def _():
        hn = _rms(x1_ref[...], gffn_ref[...])
        hn_hi, hn_lo = _split_bf16(hn)
        hh_hl = jnp.dot(hn_hi, wr_ref[...], preferred_element_type=F32)
        logits = (hh_hl[:, :LANES] + hh_hl[:, LANES:]
                  + jnp.dot(hn_lo, wr_ref[:, :LANES], preferred_element_type=F32)) + br_ref[...]
        g_sel, cw = _route(logits)
        lane = lax.broadcasted_iota(jnp.int32, (w, LANES), 1)
        onehot = jnp.where(lane == g_sel, 1.0, 0.0)
        cum = _scan_rows(onehot, jnp.add, 0.0)
        cnt = cum[w - 1:w, :].astype(jnp.int32)
        cnt_pad = ((cnt + (MOE_ALIGN - 1)) // MOE_ALIGN) * MOE_ALIGN
        lane1 = lax.broadcasted_iota(jnp.int32, (1, LANES), 1)
        off = jnp.zeros((1, LANES), jnp.int32)
        for gi in range(N_GROUPS - 1):
            off = off + jnp.where(lane1 > gi, cnt_pad[:, gi:gi + 1], 0)
        n_tiles = (cnt + (MOE_ROW_TILE - 1)) // MOE_ROW_TILE
        for gi in range(N_GROUPS):
            off_sm[gi] = off[0, gi]
            nt_sm[gi] = n_tiles[0, gi]
        pos = jnp.sum(onehot * (off.astype(F32) + cum - 1.0), axis=-1, keepdims=True)
        pos_b = jnp.broadcast_to(pos, (w, LANES))
        pos_sc[...] = pos_b
        pos_row = pos_b.T[0:1, :].astype(jnp.int32)
        cw_packed = _pack_split3(cw)
        for c in range(MOE_SORTED_ROWS // MOE_SORT_CHUNK):
            rows = lax.broadcasted_iota(jnp.int32, (MOE_SORT_CHUNK, w), 0) + c * MOE_SORT_CHUNK
            sel = jnp.where(rows == pos_row, 1.0, 0.0).astype(BF16)
            sl = slice(c * MOE_SORT_CHUNK, (c + 1) * MOE_SORT_CHUNK)
            xs_sc[sl, :] = jnp.dot(sel, hn_hi, preferred_element_type=F32).astype(BF16)
            cws_sc[sl, :] = _unpack_split3(jnp.dot(sel, cw_packed, preferred_element_type=F32))
        osort_sc[...] = jnp.zeros(osort_sc.shape, BF16)

    row0 = off_sm[s]
    lane_t = lax.broadcasted_iota(jnp.int32, (MOE_ROW_TILE, LANES), 1)
    first_lane = N_GROUPS + s * EXPERTS_PER_STEP

    def tile_body(i, carry):
        r0 = pl.multiple_of(row0 + i * MOE_ROW_TILE, MOE_ALIGN)
        xt = xs_sc[pl.ds(r0, MOE_ROW_TILE), :]
        cwt = cws_sc[pl.ds(r0, MOE_ROW_TILE), :]
        parts = []
        for e in range(EXPERTS_PER_STEP):
            hg = jnp.dot(xt, wg_ref[e], preferred_element_type=F32)
            hu = jnp.dot(xt, wu_ref[e], preferred_element_type=F32)
            col = jnp.sum(jnp.where(lane_t == first_lane + e, cwt, 0.0), axis=-1, keepdims=True)
            parts.append((_silu(hg) * hu * col).astype(BF16))
        he = jnp.concatenate(parts, axis=-1)
        out = jnp.dot(he, wd_ref[...], preferred_element_type=F32)
        osort_sc[pl.ds(r0, MOE_ROW_TILE), :] = out.astype(BF16)
        return carry

    lax.fori_loop(0, nt_sm[s], tile_body, 0)

    @pl.when(s == MOE_STEPS - 1)
    def _():
        osort = osort_sc[...]
        for c in range(w // MOE_UNSORT_CHUNK):
            sl = slice(c * MOE_UNSORT_CHUNK, (c + 1) * MOE_UNSORT_CHUNK)
            pos_col = pos_sc[sl, 0:1].astype(jnp.int32)
            cols = lax.broadcasted_iota(jnp.int32, (MOE_UNSORT_CHUNK, MOE_SORTED_ROWS), 1)
            sel = jnp.where(cols == pos_col, 1.0, 0.0).astype(BF16)
            x2 = x1_ref[sl, :] + jnp.dot(sel, osort, preferred_element_type=F32)
            gate = _sigmoid(_dot(_rms(x2, gple_ref[...]), wpg_ref[...]))
            x3 = x2 + gate * _dot(p_ref[sl, :], wpp_ref[...])
            y_ref[sl, :] = _rms(x3, gfin_ref[...])


def _moe_tail(x1, p, g_ffn, w_r, b_r, wg, wu, wd, g_ple, w_pg, w_pp, g_fin):
    n = x1.shape[0]
    tm = MOE_WINDOW
    row = lambda i, s: (i, 0)
    step = lambda i, s: (s, 0, 0)
    big = pl.BlockSpec((tm, D_MODEL), row)
    sw = EXPERTS_PER_STEP * D_EXPERT
    return pl.pallas_call(
        _moe_tail_kernel,
        grid=(n // tm, MOE_STEPS),
        in_specs=[big, pl.BlockSpec((tm, PLE_DIM), row), _resident(g_ffn.shape), _resident(w_r.shape),
                  _resident(b_r.shape),
                  pl.BlockSpec((EXPERTS_PER_STEP, D_MODEL, D_EXPERT), step),
                  pl.BlockSpec((EXPERTS_PER_STEP, D_MODEL, D_EXPERT), step),
                  pl.BlockSpec((None, sw, D_MODEL), step),
                  _resident(g_ple.shape), _resident(w_pg.shape), _resident(w_pp.shape), _resident(g_fin.shape)],
        out_specs=big,
        out_shape=jax.ShapeDtypeStruct((n, D_MODEL), F32),
        scratch_shapes=[pltpu.VMEM((MOE_SORTED_ROWS, D_MODEL), BF16), pltpu.VMEM((MOE_SORTED_ROWS, LANES), F32),
                        pltpu.VMEM((MOE_SORTED_ROWS, D_MODEL), BF16), pltpu.VMEM((tm, LANES), F32),
                        pltpu.SMEM((N_GROUPS,), jnp.int32), pltpu.SMEM((N_GROUPS,), jnp.int32)],
        compiler_params=pltpu.CompilerParams(dimension_semantics=("arbitrary", "arbitrary"),
                                             vmem_limit_bytes=VMEM_LIMIT),
        name="moe_tail",
    )(x1, p, g_ffn, w_r, b_r, wg, wu, wd, g_ple, w_pg, w_pp, g_fin)


PROMPT_TIME_TILE = 256
SAMPLE_ROW_TILE = 512
SAMPLE_SEQS_PER_STEP = 8


def _pad_tail(buf):
    return jnp.pad(buf, ((0, 0), (SUBLANES - buf.shape[1], 0), (0, 0)))


def _moe(x1, p, wts):
    return _moe_tail(x1, p, wts["g_ffn"], wts["w_r"], wts["b_r"], wts["w_gate"], wts["w_up"], wts["w_down"],
                     wts["g_ple"], wts["w_ple_gate"], wts["w_ple_proj"], wts["g_final"])


def _outputs(y, shape, c1, n1, m1p, mconv1, sconv1):
    return (y.reshape(shape), c1[None], n1[None], m1p[None, :, :M_HEADS, 0],
            mconv1[None, :, SUBLANES - (M_CONV_W - 1):, :], sconv1[None, :, SUBLANES - (S_CONV_W - 1):, :])


def _prompt_group(x, p, wts, experts_f32, *, tt):
    bsz, t, _ = x.shape
    (x1, *states), (wg, wu, wd) = _mixer(x.reshape(bsz * t, D_MODEL), wts, experts_f32, bsz=bsz, t=t, tt=tt)
    wts.update(w_gate=wg, w_up=wu, w_down=wd.reshape(MOE_STEPS, EXPERTS_PER_STEP * D_EXPERT, D_MODEL))
    return _outputs(_moe(x1, p.reshape(bsz * t, PLE_DIM), wts), x.shape, *states)


def _sample_group(x, p, mconv0, sconv0, c0, n0, m0, wts, *, tm):
    bsz, t, _ = x.shape
    xf = x.reshape(bsz * t, D_MODEL)
    m0p = jnp.broadcast_to(jnp.pad(m0, ((0, 0), (0, SUBLANES - M_HEADS)))[:, :, None], (bsz, SUBLANES, LANES))
    xm, v, o, ifp, sb, cx, ga, gb = _in_proj(xf, wts["g_mix"], wts["w_in"], tm)
    hm, u, *states = _mlstm(
        xm, v, cx, ifp, _pad_tail(mconv0), _pad_tail(sconv0), c0, n0, m0p,
        wts["w_mconv"], wts["w_sconv"], wts["w_q"], wts["w_k"], wts["gate_bias"], wts["g_head"],
        bsz=bsz, t=t, tt=t, nb=SAMPLE_SEQS_PER_STEP)
    x1 = _merge(xf, hm, o, ga, gb, sb, u, wts["w_a"], wts["w_sout"], wts["w_o"], tm)
    return _outputs(_moe(x1, p.reshape(bsz * t, PLE_DIM), wts), x.shape, *states)


def kernel(x_prompt, x_sample, p_prompt, p_sample, state_mlstm_C, state_mlstm_n, state_mlstm_m, state_mlstm_conv, state_sconv, g_mix, w_in, w_mconv, w_q, w_k, b_i, b_f, g_head, w_a, w_sconv, w_sout, w_o, g_ffn, w_rg, b_rg, w_re, b_re, w_gate, w_up, w_down, g_ple, w_ple_gate, w_ple_proj, g_final):
    assert g_mix.shape[0] == 1, "single-layer trunk"
    d = D_MODEL
    n_if = 2 * M_HEADS
    w_router = jnp.pad(jnp.concatenate([w_rg[0], w_re[0]], axis=1), ((0, 0), (0, LANES - N_GROUPS - N_EXPERTS)))
    wr_hi = w_router.astype(BF16)
    wr_lo = (w_router - wr_hi.astype(F32)).astype(BF16)
    w_r = jnp.concatenate([wr_hi, wr_lo], axis=1)
    wts = {
        "g_mix": g_mix,
        "w_in": _pack_w_in(w_in),
        "w_mconv": w_mconv[0], "w_sconv": w_sconv[0],
        "w_q": w_q[0].astype(BF16), "w_k": w_k[0].astype(BF16),
        "gate_bias": jnp.pad(jnp.concatenate([b_i[0], b_f[0]])[None, :], ((0, 0), (0, LANES - n_if))),
        "g_head": g_head[0].reshape(1, d),
        "w_a": w_a[0].astype(BF16), "w_sout": w_sout[0].astype(BF16), "w_o": w_o[0].astype(BF16),
        "g_ffn": g_ffn, "w_r": w_r,
        "b_r": jnp.pad(jnp.concatenate([b_rg[0], b_re[0]])[None, :], ((0, 0), (0, LANES - N_GROUPS - N_EXPERTS))),
        "g_ple": g_ple, "w_ple_gate": w_ple_gate[0].astype(BF16), "w_ple_proj": w_ple_proj[0].astype(BF16),
        "g_final": g_final[None, :],
    }
    yp, *st_p = _prompt_group(x_prompt, p_prompt[0], wts, (w_gate[0], w_up[0], w_down[0]), tt=PROMPT_TIME_TILE)
    ys, *st_s = _sample_group(x_sample, p_sample[0], state_mlstm_conv[0], state_sconv[0],
                              state_mlstm_C[0], state_mlstm_n[0], state_mlstm_m[0], wts, tm=SAMPLE_ROW_TILE)
    return (yp, ys, *st_p, *st_s)
```

```python
import functools

import jax
import jax.numpy as jnp
from jax import lax
from jax.experimental import pallas as pl
from jax.experimental.pallas import tpu as pltpu

D_MODEL = 1024
M_HEADS = 4
M_HEAD_DIM = 256
M_CONV_W = 4
S_CONV_W = 3
N_GROUPS = 4
EXPERTS_PER_GROUP = 8
N_EXPERTS = N_GROUPS * EXPERTS_PER_GROUP
D_EXPERT = 256
PLE_DIM = 256
RMS_EPS = 1e-6

LANES = 128
SUBLANES = 8
VMEM_LIMIT = 60 * 1024 * 1024

BF16 = jnp.bfloat16
F32 = jnp.float32


def _rms(x, g):
    return x * lax.rsqrt(jnp.mean(x * x, axis=-1, keepdims=True) + RMS_EPS) * g


def _dot(a, b):
    return jnp.dot(a.astype(BF16), b.astype(BF16), preferred_element_type=F32)


def _sigmoid(x):
    return 0.5 * jnp.tanh(0.5 * x) + 0.5


def _silu(x):
    return x * _sigmoid(x)


def _resident(shape):
    nd = len(shape)
    return pl.BlockSpec(shape, lambda *_: (0,) * nd, pipeline_mode=pl.Buffered(1))


_GATE_COL = 3 * D_MODEL
_REST_COL = _GATE_COL + LANES
IN_COL = {"xm": 0, "v": D_MODEL, "o": 2 * D_MODEL, "sb": _REST_COL, "sc": _REST_COL + D_MODEL,
          "sx": _REST_COL + 2 * D_MODEL, "ga": _REST_COL + 3 * D_MODEL, "gb": _REST_COL + 4 * D_MODEL}


def _projector(h, w_in_ref):
    def proj(name):
        lo, width = (_GATE_COL, LANES) if name == "gates" else (IN_COL[name], D_MODEL)
        return jnp.dot(h, w_in_ref[:, lo:lo + width], preferred_element_type=F32)
    return proj


def _in_proj_kernel(x_ref, g_ref, w_in_ref,
                    xm_ref, v_ref, o_ref, if_ref, sb_ref, cx_ref, ga_ref, gb_ref):
    proj = _projector(_rms(x_ref[...], g_ref[...]).astype(BF16), w_in_ref)
    xm_ref[...] = proj("xm")
    v_ref[...] = proj("v")
    o_ref[...] = proj("o")
    if_ref[...] = proj("gates")
    sb_ref[...] = proj("sb")
    cx_ref[...] = proj("sc") * proj("sx")
    ga_ref[...] = proj("ga")
    gb_ref[...] = proj("gb")


def _in_proj(x, g_mix, w_in, tm):
    n = x.shape[0]
    row = lambda i: (i, 0)
    big = pl.BlockSpec((tm, D_MODEL), row)
    outs = [jax.ShapeDtypeStruct((n, D_MODEL), F32)] * 3 + [jax.ShapeDtypeStruct((n, LANES), F32)] \
        + [jax.ShapeDtypeStruct((n, D_MODEL), F32)] * 4
    return pl.pallas_call(
        _in_proj_kernel,
        grid=(n // tm,),
        in_specs=[big, _resident(g_mix.shape), _resident(w_in.shape)],
        out_specs=[big, big, big, pl.BlockSpec((tm, LANES), row), big, big, big, big],
        out_shape=outs,
        compiler_params=pltpu.CompilerParams(dimension_semantics=("arbitrary",),
                                             vmem_limit_bytes=VMEM_LIMIT),
        name="in_proj",
    )(x, g_mix, w_in)


def _scan_rows(x, op, identity):
    sub = lax.broadcasted_iota(jnp.int32, (SUBLANES, x.shape[1]), 0)
    blocks, carry = [], None
    for i in range(x.shape[0] // SUBLANES):
        blk = x[i * SUBLANES:(i + 1) * SUBLANES, :]
        for shift in (1, 2, 4):
            blk = op(blk, jnp.where(sub >= shift, pltpu.roll(blk, shift, axis=0), identity))
        if carry is not None:
            blk = op(blk, carry)
        carry = jnp.broadcast_to(blk[SUBLANES - 1:SUBLANES, :], blk.shape)
        blocks.append(blk)
    return jnp.concatenate(blocks, axis=0)


def _log_sigmoid(x):
    return jnp.minimum(x, 0.0) - jnp.log1p(jnp.exp(-jnp.abs(x)))


def _seq_init(mconv0_ref, sconv0_ref, c0_ref, n0_ref, m0_ref, xp_sc, cp_sc, c1_ref, n1_ref, m1_ref):
    @pl.when(pl.program_id(1) == 0)
    def _():
        xp_sc[0:SUBLANES, :] = mconv0_ref[...]
        cp_sc[0:SUBLANES, :] = sconv0_ref[...]
        c1_ref[...] = c0_ref[...]
        n1_ref[...] = n0_ref[...]
        m1_ref[...] = m0_ref[...]


def _causal_conv(src_sc, w_ref, width, tt):
    acc = None
    for j in range(width):
        term = src_sc[pl.ds(SUBLANES - (width - 1) + j, tt), :] * w_ref[j:j + 1, :]
        acc = term if acc is None else acc + term
    return acc


def _carry_tail(src_sc, tail_ref, tt):
    tail = src_sc[tt:tt + SUBLANES, :]
    src_sc[0:SUBLANES, :] = tail
    tail_ref[...] = tail


def _head_slices():
    return [slice(h * M_HEAD_DIM, (h + 1) * M_HEAD_DIM) for h in range(M_HEADS)]


def _mlstm_tile(xq, v, ifp, states, wq_ref, wk_ref, bias_ref, ghead_ref, tt, between=()):
    nh = M_HEADS
    hs = _head_slices()
    seq_rows = [slice(j * tt, (j + 1) * tt) for j in range(len(states))]
    g = [ifp[rows, :] + bias_ref[...] for rows in seq_rows]
    b_c = [_scan_rows(_log_sigmoid(gj), jnp.add, 0.0) for gj in g]
    a_c = [gj - pltpu.roll(bj, LANES - nh, axis=1) for gj, bj in zip(g, b_c)]
    amax_c = [_scan_rows(aj, jnp.maximum, -jnp.inf) for aj in a_c]
    a_t = [aj.T for aj in a_c]

    r_idx = lax.broadcasted_iota(jnp.int32, (tt, tt), 0)
    c_idx = lax.broadcasted_iota(jnp.int32, (tt, tt), 1)
    causal = c_idx <= r_idx

    q_all = [_dot(xq[:, hs[h]], wq_ref[h]) * (M_HEAD_DIM ** -0.5) for h in range(nh)]
    k_all = [_dot(xq[:, hs[h]], wk_ref[h]) for h in range(nh)]
    units = [(j, h) for j in range(len(states)) for h in range(nh)]
    ids = range(len(units))
    q = [q_all[h][seq_rows[j], :] for j, h in units]
    k = [k_all[h][seq_rows[j], :] for j, h in units]
    vv = [v[seq_rows[j], hs[h]] for j, h in units]
    c_prev = [states[j][0][h] for j, h in units]
    n_prev = [states[j][1][h:h + 1, :] for j, h in units]
    m_prev = [states[j][2][h:h + 1, 0:1] for j, h in units]
    b_col = [b_c[j][:, nh + h:nh + h + 1] for j, h in units]
    a_col = [a_c[j][:, h:h + 1] for j, h in units]
    a_row = [a_t[j][h:h + 1, :] for j, h in units]
    b_last = [b_c[j][tt - 1:tt, nh + h:nh + h + 1] for j, h in units]
    mm_col = [jnp.maximum(m_prev[i], amax_c[j][:, h:h + 1]) for i, (j, h) in enumerate(units)]
    mm_last = [jnp.maximum(m_prev[i], amax_c[j][tt - 1:tt, h:h + 1]) for i, (j, h) in enumerate(units)]

    extra = []
    pending = list(between)

    def run_one():
        if pending:
            extra.append(pending.pop(0)())

    w_inter = [jnp.exp(m_prev[i] - mm_col[i]) for i in ids]
    qk = [lax.dot_general(q[i].astype(BF16), k[i].astype(BF16), (((1,), (1,)), ((), ())),
                          preferred_element_type=F32) for i in ids]
    run_one()
    s = [jnp.exp(jnp.where(causal, a_row[i] - mm_col[i], -jnp.inf)) * qk[i] for i in ids]
    run_one()
    num = [w_inter[i] * _dot(q[i], c_prev[i]) + _dot(s[i], vv[i]) for i in ids]
    den = [w_inter[i] * jnp.sum(q[i] * n_prev[i], axis=-1, keepdims=True)
           + jnp.sum(s[i], axis=-1, keepdims=True) for i in ids]
    run_one()
    out = [num[i] * (1.0 / jnp.maximum(jnp.abs(den[i]), jnp.exp(-(b_col[i] + mm_col[i])))) for i in ids]
    out = [out[i] * lax.rsqrt(jnp.mean(out[i] * out[i], axis=-1, keepdims=True) + RMS_EPS)
           * ghead_ref[:, hs[units[i][1]]] for i in ids]
    while pending:
        run_one()

    decay = [jnp.exp(m_prev[i] - mm_last[i]) for i in ids]
    wk = [jnp.exp(a_col[i] - mm_last[i]) * k[i] for i in ids]
    for i, (j, h) in enumerate(units):
        c_ref, n_ref, m_ref = states[j]
        c_ref[h] = decay[i] * c_prev[i] + lax.dot_general(
            wk[i].astype(BF16), vv[i].astype(BF16), (((0,), (0,)), ((), ())), preferred_element_type=F32)
        n_ref[h:h + 1, :] = decay[i] * n_prev[i] + jnp.sum(wk[i], axis=0, keepdims=True)
        m_ref[h:h + 1, :] = jnp.broadcast_to(b_last[i] + mm_last[i], (1, LANES))
    return [out[j * nh:(j + 1) * nh] for j in range(len(states))], extra


def _mlstm_kernel(xm_ref, v_ref, cx_ref, if_ref, mconv0_ref, sconv0_ref, c0_ref, n0_ref, m0_ref,
                  w_mconv_ref, w_sconv_ref, wq_ref, wk_ref, bias_ref, ghead_ref,
                  hm_ref, u_ref, c1_ref, n1_ref, m1_ref, mconv1_ref, sconv1_ref,
                  xp_sc, cp_sc, *, tt, nb):
    hs = _head_slices()
    xq, states = [], []
    for j in range(nb):
        rows = slice(j * tt, (j + 1) * tt)
        xp, cp, state = xp_sc.at[j], cp_sc.at[j], (c1_ref.at[j], n1_ref.at[j], m1_ref.at[j])
        _seq_init(mconv0_ref.at[j], sconv0_ref.at[j], c0_ref.at[j], n0_ref.at[j], m0_ref.at[j], xp, cp, *state)
        xp[SUBLANES:SUBLANES + tt, :] = xm_ref[rows, :]
        cp[SUBLANES:SUBLANES + tt, :] = cx_ref[rows, :]
        xq.append(_silu(_causal_conv(xp, w_mconv_ref, M_CONV_W, tt)))
        u_ref[rows, :] = _causal_conv(cp, w_sconv_ref, S_CONV_W, tt)
        _carry_tail(xp, mconv1_ref.at[j], tt)
        _carry_tail(cp, sconv1_ref.at[j], tt)
        states.append(state)
    xq = xq[0] if nb == 1 else jnp.concatenate(xq, axis=0)
    out, _ = _mlstm_tile(xq, v_ref[...], if_ref[...], states, wq_ref, wk_ref, bias_ref, ghead_ref, tt)
    for j in range(nb):
        for h, sl in enumerate(hs):
            hm_ref[j * tt:(j + 1) * tt, sl] = out[j][h]


def _mlstm(xm, v, cx, ifp, mconv0, sconv0, c0, n0, m0, w_mconv, w_sconv, wq, wk, bias, ghead, *, bsz, t, tt, nb):
    nt = t // tt
    n = bsz * t
    assert nb == 1 or nt == 1, "several sequences per step only when a step covers whole sequences"
    tok = lambda b, i: (b * nt + i, 0)
    seq3 = lambda b, i: (b, 0, 0)
    seq4 = lambda b, i: (b, 0, 0, 0)
    big = pl.BlockSpec((nb * tt, D_MODEL), tok)
    tail = pl.BlockSpec((nb, SUBLANES, D_MODEL), seq3)
    c_spec = pl.BlockSpec((nb, M_HEADS, M_HEAD_DIM, M_HEAD_DIM), seq4)
    n_spec = pl.BlockSpec((nb, M_HEADS, M_HEAD_DIM), seq3)
    m_spec = pl.BlockSpec((nb, SUBLANES, LANES), seq3)
    return pl.pallas_call(
        functools.partial(_mlstm_kernel, tt=tt, nb=nb),
        grid=(bsz // nb, nt),
        in_specs=[big, big, big, pl.BlockSpec((nb * tt, LANES), tok), tail, tail, c_spec, n_spec, m_spec,
                  _resident(w_mconv.shape), _resident(w_sconv.shape), _resident(wq.shape),
                  _resident(wk.shape), _resident(bias.shape), _resident(ghead.shape)],
        out_specs=[big, big, c_spec, n_spec, m_spec, tail, tail],
        out_shape=[jax.ShapeDtypeStruct((n, D_MODEL), F32), jax.ShapeDtypeStruct((n, D_MODEL), F32),
                   jax.ShapeDtypeStruct(c0.shape, F32), jax.ShapeDtypeStruct(n0.shape, F32),
                   jax.ShapeDtypeStruct(m0.shape, F32),
                   jax.ShapeDtypeStruct(mconv0.shape, F32), jax.ShapeDtypeStruct(sconv0.shape, F32)],
        scratch_shapes=[pltpu.VMEM((nb, tt + SUBLANES, D_MODEL), F32),
                        pltpu.VMEM((nb, tt + SUBLANES, D_MODEL), F32)],
        compiler_params=pltpu.CompilerParams(dimension_semantics=("arbitrary", "arbitrary"),
                                             vmem_limit_bytes=VMEM_LIMIT),
        name="mlstm",
    )(xm, v, cx, ifp, mconv0, sconv0, c0, n0, m0, w_mconv, w_sconv, wq, wk, bias, ghead)


def _merge_kernel(x_ref, hm_ref, o_ref, ga_ref, gb_ref, sb_ref, u_ref, wa_ref, wsout_ref, wo_ref, x1_ref):
    y_a = _dot(_sigmoid(o_ref[...]) * hm_ref[...], wa_ref[...])
    y_b = _dot(sb_ref[...] * u_ref[...], wsout_ref[...])
    merged = _sigmoid(ga_ref[...]) * y_a + _sigmoid(gb_ref[...]) * y_b
    x1_ref[...] = x_ref[...] + _dot(merged, wo_ref[...])


def _merge(x, hm, o, ga, gb, sb, u, w_a, w_sout, w_o, tm):
    n = x.shape[0]
    big = pl.BlockSpec((tm, D_MODEL), lambda i: (i, 0))
    return pl.pallas_call(
        _merge_kernel,
        grid=(n // tm,),
        in_specs=[big] * 7 + [_resident(w_a.shape), _resident(w_sout.shape), _resident(w_o.shape)],
        out_specs=big,
        out_shape=jax.ShapeDtypeStruct((n, D_MODEL), F32),
        compiler_params=pltpu.CompilerParams(dimension_semantics=("arbitrary",),
                                             vmem_limit_bytes=VMEM_LIMIT),
        name="merge",
    )(x, hm, o, ga, gb, sb, u, w_a, w_sout, w_o)


def _mixer_kernel(x_ref, g_ref, w_in_ref, w_mconv_ref, w_sconv_ref, wq_ref, wk_ref,
                  bias_ref, ghead_ref, wa_ref, wsout_ref, wo_ref, wg32_ref, wu32_ref, wd32_ref,
                  x1_ref, c1_ref, n1_ref, m1_ref, mconv1_ref, sconv1_ref, wg16_ref, wu16_ref, wd16_ref,
                  xp_sc, cp_sc, hm_sc, *, tt):
    @pl.when(pl.program_id(1) == 0)
    def _():
        xp_sc[0:SUBLANES, :] = jnp.zeros((SUBLANES, D_MODEL), F32)
        cp_sc[0:SUBLANES, :] = jnp.zeros((SUBLANES, D_MODEL), F32)
        c1_ref[...] = jnp.zeros(c1_ref.shape, F32)
        n1_ref[...] = jnp.zeros(n1_ref.shape, F32)
        m1_ref[...] = jnp.zeros(m1_ref.shape, F32)

    wg16_ref[...] = wg32_ref[...].astype(BF16)
    wu16_ref[...] = wu32_ref[...].astype(BF16)
    wd16_ref[...] = wd32_ref[...].astype(BF16)

    proj = _projector(_rms(x_ref[...], g_ref[...]).astype(BF16), w_in_ref)
    xp_sc[SUBLANES:SUBLANES + tt, :] = proj("xm")
    ifp = proj("gates")
    cp_sc[SUBLANES:SUBLANES + tt, :] = proj("sc") * proj("sx")
    xq = _silu(_causal_conv(xp_sc, w_mconv_ref, M_CONV_W, tt))
    u = _causal_conv(cp_sc, w_sconv_ref, S_CONV_W, tt)
    _carry_tail(xp_sc, mconv1_ref, tt)
    _carry_tail(cp_sc, sconv1_ref, tt)
    v = proj("v")
    hs = _head_slices()
    (out,), (y_b, sig_o, sig_ga, sig_gb) = _mlstm_tile(
        xq, v, ifp, [(c1_ref, n1_ref, m1_ref)], wq_ref, wk_ref, bias_ref, ghead_ref, tt,
        between=(lambda: _dot(proj("sb") * u, wsout_ref[...]),
                 lambda: _sigmoid(proj("o")),
                 lambda: _sigmoid(proj("ga")),
                 lambda: _sigmoid(proj("gb"))))
    for hd, sl in enumerate(hs):
        hm_sc[:, sl] = (sig_o[:, sl] * out[hd]).astype(BF16)
    y_a = jnp.dot(hm_sc[...], wa_ref[...], preferred_element_type=F32)
    merged = sig_ga * y_a + sig_gb * y_b
    x1_ref[...] = x_ref[...] + _dot(merged, wo_ref[...])


def _mixer(x, wts, experts_f32, *, bsz, t, tt):
    nt = t // tt
    steps = bsz * nt
    tok = lambda b, i: (b * nt + i, 0)
    seq3 = lambda b, i: (b, 0, 0)
    seq4 = lambda b, i: (b, 0, 0, 0)
    big = pl.BlockSpec((tt, D_MODEL), tok)
    tail = pl.BlockSpec((None, SUBLANES, D_MODEL), seq3)
    c_spec = pl.BlockSpec((None, M_HEADS, M_HEAD_DIM, M_HEAD_DIM), seq4)
    n_spec = pl.BlockSpec((None, M_HEADS, M_HEAD_DIM), seq3)
    m_spec = pl.BlockSpec((None, SUBLANES, LANES), seq3)
    names = ("g_mix", "w_in", "w_mconv", "w_sconv", "w_q", "w_k", "gate_bias", "g_head",
             "w_a", "w_sout", "w_o")
    weights = [wts[k] for k in names]
    flat = [w.reshape(-1, w.shape[-1]) for w in experts_f32]
    cast_specs = [pl.BlockSpec((w.shape[0] // steps, w.shape[1]), tok) for w in flat]
    outs = pl.pallas_call(
        functools.partial(_mixer_kernel, tt=tt),
        grid=(bsz, nt),
        in_specs=[big] + [_resident(w.shape) for w in weights] + cast_specs,
        out_specs=[big, c_spec, n_spec, m_spec, tail, tail] + cast_specs,
        out_shape=[jax.ShapeDtypeStruct(x.shape, F32),
                   jax.ShapeDtypeStruct((bsz, M_HEADS, M_HEAD_DIM, M_HEAD_DIM), F32),
                   jax.ShapeDtypeStruct((bsz, M_HEADS, M_HEAD_DIM), F32),
                   jax.ShapeDtypeStruct((bsz, SUBLANES, LANES), F32),
                   jax.ShapeDtypeStruct((bsz, SUBLANES, D_MODEL), F32),
                   jax.ShapeDtypeStruct((bsz, SUBLANES, D_MODEL), F32)]
                  + [jax.ShapeDtypeStruct(w.shape, BF16) for w in flat],
        scratch_shapes=[pltpu.VMEM((tt + SUBLANES, D_MODEL), F32), pltpu.VMEM((tt + SUBLANES, D_MODEL), F32),
                        pltpu.VMEM((tt, D_MODEL), BF16)],
        compiler_params=pltpu.CompilerParams(dimension_semantics=("arbitrary", "arbitrary"),
                                             vmem_limit_bytes=VMEM_LIMIT),
        name="mixer",
    )(x, *weights, *flat)
    return outs[:6], [o.reshape(w.shape) for o, w in zip(outs[6:], experts_f32)]


MOE_WINDOW = 1024
MOE_ROW_TILE = 128
MOE_ALIGN = 16
EXPERTS_PER_STEP = EXPERTS_PER_GROUP
MOE_STEPS = N_EXPERTS // EXPERTS_PER_STEP
MOE_CHUNK = 256
MOE_SORT_CHUNK = 1280
MOE_UNSORT_CHUNK = 512
MOE_SORTED_ROWS = -(-(MOE_WINDOW + N_GROUPS * (MOE_ALIGN - 1) + MOE_ROW_TILE) // MOE_CHUNK) * MOE_CHUNK


def _split_bf16(x):
    hi = x.astype(BF16)
    lo = (x - hi.astype(F32)).astype(BF16)
    return hi, lo


CW_LANE_STRIDE = 40


def _pack_split3(cw):
    hi = cw.astype(BF16).astype(F32)
    r1 = cw - hi
    mid = r1.astype(BF16).astype(F32)
    lo = (r1 - mid).astype(BF16).astype(F32)
    return (hi + pltpu.roll(mid, CW_LANE_STRIDE, axis=1) + pltpu.roll(lo, 2 * CW_LANE_STRIDE, axis=1)).astype(BF16)


def _unpack_split3(packed):
    return (packed + pltpu.roll(packed, LANES - CW_LANE_STRIDE, axis=1)
            + pltpu.roll(packed, LANES - 2 * CW_LANE_STRIDE, axis=1))


def _route(logits):
    lane = lax.broadcasted_iota(jnp.int32, logits.shape, 1)
    neg = -jnp.inf
    big = jnp.int32(LANES)
    is_grp = lane < N_GROUPS
    g_max = jnp.max(jnp.where(is_grp, logits, neg), axis=-1, keepdims=True)
    g_sel = jnp.min(jnp.where(is_grp & (logits == g_max), lane, big), axis=-1, keepdims=True)
    p_grp = 1.0 / jnp.sum(jnp.where(is_grp, jnp.exp(logits - g_max), 0.0), axis=-1, keepdims=True)
    lo = N_GROUPS + g_sel * EXPERTS_PER_GROUP
    in_grp = (lane >= lo) & (lane < lo + EXPERTS_PER_GROUP)
    v1 = jnp.max(jnp.where(in_grp, logits, neg), axis=-1, keepdims=True)
    e1 = jnp.min(jnp.where(in_grp & (logits == v1), lane, big), axis=-1, keepdims=True)
    rest = in_grp & (lane != e1)
    v2 = jnp.max(jnp.where(rest, logits, neg), axis=-1, keepdims=True)
    e2 = jnp.min(jnp.where(rest & (logits == v2), lane, big), axis=-1, keepdims=True)
    z = jnp.exp(v2 - v1)
    w1 = p_grp / (1.0 + z)
    w2 = p_grp * z / (1.0 + z)
    return g_sel, jnp.where(lane == e1, w1, jnp.where(lane == e2, w2, 0.0))


def _moe_tail_kernel(x1_ref, p_ref, gffn_ref, wr_ref, br_ref, wg_ref, wu_ref, wd_ref,
                     gple_ref, wpg_ref, wpp_ref, gfin_ref, y_ref,
                     xs_sc, cws_sc, osort_sc, pos_sc, off_sm, nt_sm):
    s = pl.program_id(1)
    w = MOE_WINDOW

    @pl.when(s == 0)
    def _():
        hn = _rms(x1_ref[...], gffn_ref[...])
        hn_hi, hn_lo = _split_bf16(hn)
        hh_hl = jnp.dot(hn_hi, wr_ref[...], preferred_element_type=F32)
        logits = (hh_hl[:, :LANES] + hh_hl[:, LANES:]
                  + jnp.dot(hn_lo, wr_ref[:, :LANES], preferred_element_type=F32)) + br_ref[...]
        g_sel, cw = _route(logits)
        lane = lax.broadcasted_iota(jnp.int32, (w, LANES), 1)
        onehot = jnp.where(lane == g_sel, 1.0, 0.0)
        cum = _scan_rows(onehot, jnp.add, 0.0)
        cnt = cum[w - 1:w, :].astype(jnp.int32)
        cnt_pad = ((cnt + (MOE_ALIGN - 1)) // MOE_ALIGN) * MOE_ALIGN
        lane1 = lax.broadcasted_iota(jnp.int32, (1, LANES), 1)
        off = jnp.zeros((1, LANES), jnp.int32)
        for gi in range(N_GROUPS - 1):
            off = off + jnp.where(lane1 > gi, cnt_pad[:, gi:gi + 1], 0)
        n_tiles = (cnt + (MOE_ROW_TILE - 1)) // MOE_ROW_TILE
        for gi in range(N_GROUPS):
            off_sm[gi] = off[0, gi]
            nt_sm[gi] = n_tiles[0, gi]
        pos = jnp.sum(onehot * (off.astype(F32) + cum - 1.0), axis=-1, keepdims=True)
        pos_b = jnp.broadcast_to(pos, (w, LANES))
        pos_sc[...] = pos_b
        pos_row = pos_b.T[0:1, :].astype(jnp.int32)
        cw_packed = _pack_split3(cw)
        for c in range(MOE_SORTED_ROWS // MOE_SORT_CHUNK):
            rows = lax.broadcasted_iota(jnp.int32, (MOE_SORT_CHUNK, w), 0) + c * MOE_SORT_CHUNK
            sel = jnp.where(rows == pos_row, 1.0, 0.0).astype(BF16)
            sl = slice(c * MOE_SORT_CHUNK, (c + 1) * MOE_SORT_CHUNK)
            xs_sc[sl, :] = jnp.dot(sel, hn_hi, preferred_element_type=F32).astype(BF16)
            cws_sc[sl, :] = _unpack_split3(jnp.dot(sel, cw_packed, preferred_element_type=F32))
        osort_sc[...] = jnp.zeros(osort_sc.shape, BF16)

    row0 = off_sm[s]
    lane_t = lax.broadcasted_iota(jnp.int32, (MOE_ROW_TILE, LANES), 1)
    first_lane = N_GROUPS + s * EXPERTS_PER_STEP

    def tile_body(i, carry):
        r0 = pl.multiple_of(row0 + i * MOE_ROW_TILE, MOE_ALIGN)
        xt = xs_sc[pl.ds(r0, MOE_ROW_TILE), :]
        cwt = cws_sc[pl.ds(r0, MOE_ROW_TILE), :]
        parts = []
        for e in range(EXPERTS_PER_STEP):
            hg = jnp.dot(xt, wg_ref[e], preferred_element_type=F32)
            hu = jnp.dot(xt, wu_ref[e], preferred_element_type=F32)
            col = jnp.sum(jnp.where(lane_t == first_lane + e, cwt, 0.0), axis=-1, keepdims=True)
            parts.append((_silu(hg) * hu * col).astype(BF16))
        he = jnp.concatenate(parts, axis=-1)
        out = jnp.dot(he, wd_ref[...], preferred_element_type=F32)
        osort_sc[pl.ds(r0, MOE_ROW_TILE), :] = out.astype(BF16)
        return carry

    lax.fori_loop(0, nt_sm[s], tile_body, 0)

    @pl.when(s == MOE_STEPS - 1)
    def _():
        osort = osort_sc[...]
        for c in range(w // MOE_UNSORT_CHUNK):
            sl = slice(c * MOE_UNSORT_CHUNK, (c + 1) * MOE_UNSORT_CHUNK)
            pos_col = pos_sc[sl, 0:1].astype(jnp.int32)
            cols = lax.broadcasted_iota(jnp.int32, (MOE_UNSORT_CHUNK, MOE_SORTED_ROWS), 1)
            sel = jnp.where(cols == pos_col, 1.0, 0.0).astype(BF16)
            x2 = x1_ref[sl, :] + jnp.dot(sel, osort, preferred_element_type=F32)
            gate = _sigmoid(_dot(_rms(x2, gple_ref[...]), wpg_ref[...]))
            x3 = x2 + gate * _dot(p_ref[sl, :], wpp_ref[...])
            y_ref[sl, :] = _rms(x3, gfin_ref[...])


def _moe_tail(x1, p, g_ffn, w_r, b_r, wg, wu, wd, g_ple, w_pg, w_pp, g_fin):
    n = x1.shape[0]
    tm = MOE_WINDOW
    row = lambda i, s: (i, 0)
    step = lambda i, s: (s, 0, 0)
    big = pl.BlockSpec((tm, D_MODEL), row)
    sw = EXPERTS_PER_STEP * D_EXPERT
    return pl.pallas_call(
        _moe_tail_kernel,
        grid=(n // tm, MOE_STEPS),
        in_specs=[big, pl.BlockSpec((tm, PLE_DIM), row), _resident(g_ffn.shape), _resident(w_r.shape),
                  _resident(b_r.shape),
                  pl.BlockSpec((EXPERTS_PER_STEP, D_MODEL, D_EXPERT), step),
                  pl.BlockSpec((EXPERTS_PER_STEP, D_MODEL, D_EXPERT), step),
                  pl.BlockSpec((None, sw, D_MODEL), step),
                  _resident(g_ple.shape), _resident(w_pg.shape), _resident(w_pp.shape), _resident(g_fin.shape)],
        out_specs=big,
        out_shape=jax.ShapeDtypeStruct((n, D_MODEL), F32),
        scratch_shapes=[pltpu.VMEM((MOE_SORTED_ROWS, D_MODEL), BF16), pltpu.VMEM((MOE_SORTED_ROWS, LANES), F32),
                        pltpu.VMEM((MOE_SORTED_ROWS, D_MODEL), BF16), pltpu.VMEM((tm, LANES), F32),
                        pltpu.SMEM((N_GROUPS,), jnp.int32), pltpu.SMEM((N_GROUPS,), jnp.int32)],
        compiler_params=pltpu.CompilerParams(dimension_semantics=("arbitrary", "arbitrary"),
                                             vmem_limit_bytes=VMEM_LIMIT),
        name="moe_tail",
    )(x1, p, g_ffn, w_r, b_r, wg, wu, wd, g_ple, w_pg, w_pp, g_fin)


PROMPT_TIME_TILE = 256
SAMPLE_ROW_TILE = 512
SAMPLE_SEQS_PER_STEP = 8


def _pad_tail(buf):
    return jnp.pad(buf, ((0, 0), (SUBLANES - buf.shape[1], 0), (0, 0)))


def _moe(x1, p, wts):
    return _moe_tail(x1, p, wts["g_ffn"], wts["w_r"], wts["b_r"], wts["w_gate"], wts["w_up"], wts["w_down"],
                     wts["g_ple"], wts["w_ple_gate"], wts["w_ple_proj"], wts["g_final"])


def _outputs(y, shape, c1, n1, m1p, mconv1, sconv1):
    return (y.reshape(shape), c1[None], n1[None], m1p[None, :, :M_HEADS, 0],
            mconv1[None, :, SUBLANES - (M_CONV_W - 1):, :], sconv1[None, :, SUBLANES - (S_CONV_W - 1):, :])


def _prompt_group(x, p, wts, experts_f32, *, tt):
    bsz, t, _ = x.shape
    (x1, *states), (wg, wu, wd) = _mixer(x.reshape(bsz * t, D_MODEL), wts, experts_f32, bsz=bsz, t=t, tt=tt)
    wts.update(w_gate=wg, w_up=wu, w_down=wd.reshape(MOE_STEPS, EXPERTS_PER_STEP * D_EXPERT, D_MODEL))
    return _outputs(_moe(x1, p.reshape(bsz * t, PLE_DIM), wts), x.shape, *states)


def _sample_group(x, p, mconv0, sconv0, c0, n0, m0, wts, *, tm):
    bsz, t, _ = x.shape
    xf = x.reshape(bsz * t, D_MODEL)
    m0p = jnp.broadcast_to(jnp.pad(m0, ((0, 0), (0, SUBLANES - M_HEADS)))[:, :, None], (bsz, SUBLANES, LANES))
    xm, v, o, ifp, sb, cx, ga, gb = _in_proj(xf, wts["g_mix"], wts["w_in"], tm)
    hm, u, *states = _mlstm(
        xm, v, cx, ifp, _pad_tail(mconv0), _pad_tail(sconv0), c0, n0, m0p,
        wts["w_mconv"], wts["w_sconv"], wts["w_q"], wts["w_k"], wts["gate_bias"], wts["g_head"],
        bsz=bsz, t=t, tt=t, nb=SAMPLE_SEQS_PER_STEP)
    x1 = _merge(xf, hm, o, ga, gb, sb, u, wts["w_a"], wts["w_sout"], wts["w_o"], tm)
    return _outputs(_moe(x1, p.reshape(bsz * t, PLE_DIM), wts), x.shape, *states)


def kernel(x_prompt, x_sample, p_prompt, p_sample, state_mlstm_C, state_mlstm_n, state_mlstm_m, state_mlstm_conv, state_sconv, g_mix, w_in, w_mconv, w_q, w_k, b_i, b_f, g_head, w_a, w_sconv, w_sout, w_o, g_ffn, w_rg, b_rg, w_re, b_re, w_gate, w_up, w_down, g_ple, w_ple_gate, w_ple_proj, g_final):
    assert g_mix.shape[0] == 1, "single-layer trunk"
    d = D_MODEL
    n_if = 2 * M_HEADS
    w_in0 = lax.optimization_barrier(w_in[0].astype(BF16))
    w_router =jnp.pad(jnp.concatenate([w_rg[0], w_re[0]], axis=1), ((0, 0), (0, LANES - N_GROUPS - N_EXPERTS)))
    wr_hi = w_router.astype(BF16)
    wr_lo = (w_router - wr_hi.astype(F32)).astype(BF16)
    w_r = jnp.concatenate([wr_hi, wr_lo], axis=1)
    wts = {
        "g_mix": g_mix,
        "w_in": jnp.concatenate([w_in0[:, :_GATE_COL],
                                 jnp.pad(w_in0[:, _GATE_COL:_GATE_COL + n_if], ((0, 0), (0, LANES - n_if))),
                                 w_in0[:, _GATE_COL + n_if:]], axis=1),
        "w_mconv": w_mconv[0], "w_sconv": w_sconv[0],
        "w_q": w_q[0].astype(BF16), "w_k": w_k[0].astype(BF16),
        "gate_bias": jnp.pad(jnp.concatenate([b_i[0], b_f[0]])[None, :], ((0, 0), (0, LANES - n_if))),
        "g_head": g_head[0].reshape(1, d),
        "w_a": w_a[0].astype(BF16), "w_sout": w_sout[0].astype(BF16), "w_o": w_o[0].astype(BF16),
        "g_ffn": g_ffn, "w_r": w_r,
        "b_r": jnp.pad(jnp.concatenate([b_rg[0], b_re[0]])[None, :], ((0, 0), (0, LANES - N_GROUPS - N_EXPERTS))),
        "g_ple": g_ple, "w_ple_gate": w_ple_gate[0].astype(BF16), "w_ple_proj": w_ple_proj[0].astype(BF16),
        "g_final": g_final[None, :],
    }
    yp, *st_p = _prompt_group(x_prompt, p_prompt[0], wts, (w_gate[0], w_up[0], w_down[0]), tt=PROMPT_TIME_TILE)
    ys, *st_s = _sample_group(x_sample, p_sample[0], state_mlstm_conv[0], state_sconv[0],
                              state_mlstm_C[0], state_mlstm_n[0], state_mlstm_m[0], wts, tm=SAMPLE_ROW_TILE)
    return (yp, ys, *st_p, *st_s)
```

```python
import functools

import jax
import jax.numpy as jnp
from jax import lax
from jax.experimental import pallas as pl
from jax.experimental.pallas import tpu as pltpu

D_MODEL = 1024
M_HEADS = 4
M_HEAD_DIM = 256
M_CONV_W = 4
S_CONV_W = 3
N_GROUPS = 4
EXPERTS_PER_GROUP = 8
N_EXPERTS = N_GROUPS * EXPERTS_PER_GROUP
D_EXPERT = 256
PLE_DIM = 256
RMS_EPS = 1e-6

LANES = 128
SUBLANES = 8
VMEM_LIMIT = 60 * 1024 * 1024

BF16 = jnp.bfloat16
F32 = jnp.float32


def _rms(x, g):
    return x * lax.rsqrt(jnp.mean(x * x, axis=-1, keepdims=True) + RMS_EPS) * g


def _dot(a, b):
    return jnp.dot(a.astype(BF16), b.astype(BF16), preferred_element_type=F32)


def _sigmoid(x):
    return 0.5 * jnp.tanh(0.5 * x) + 0.5


def _silu(x):
    return x * _sigmoid(x)


def _resident(shape):
    nd = len(shape)
    return pl.BlockSpec(shape, lambda *_: (0,) * nd, pipeline_mode=pl.Buffered(1))


_GATE_COL = 3 * D_MODEL
_REST_COL = _GATE_COL + LANES
IN_COL = {"xm": 0, "v": D_MODEL, "o": 2 * D_MODEL, "sb": _REST_COL, "sc": _REST_COL + D_MODEL,
          "sx": _REST_COL + 2 * D_MODEL, "ga": _REST_COL + 3 * D_MODEL, "gb": _REST_COL + 4 * D_MODEL}


def _projector(h, w_in_ref):
    def proj(name):
        lo, width = (_GATE_COL, LANES) if name == "gates" else (IN_COL[name], D_MODEL)
        return jnp.dot(h, w_in_ref[:, lo:lo + width], preferred_element_type=F32)
    return proj


def _in_proj_kernel(x_ref, g_ref, w_in_ref,
                    xm_ref, v_ref, o_ref, if_ref, sb_ref, cx_ref, ga_ref, gb_ref):
    proj = _projector(_rms(x_ref[...], g_ref[...]).astype(BF16), w_in_ref)
    xm_ref[...] = proj("xm")
    v_ref[...] = proj("v")
    o_ref[...] = proj("o")
    if_ref[...] = proj("gates")
    sb_ref[...] = proj("sb")
    cx_ref[...] = proj("sc") * proj("sx")
    ga_ref[...] = proj("ga")
    gb_ref[...] = proj("gb")


def _in_proj(x, g_mix, w_in, tm):
    n = x.shape[0]
    row = lambda i: (i, 0)
    big = pl.BlockSpec((tm, D_MODEL), row)
    outs = [jax.ShapeDtypeStruct((n, D_MODEL), F32)] * 3 + [jax.ShapeDtypeStruct((n, LANES), F32)] \
        + [jax.ShapeDtypeStruct((n, D_MODEL), F32)] * 4
    return pl.pallas_call(
        _in_proj_kernel,
        grid=(n // tm,),
        in_specs=[big, _resident(g_mix.shape), _resident(w_in.shape)],
        out_specs=[big, big, big, pl.BlockSpec((tm, LANES), row), big, big, big, big],
        out_shape=outs,
        compiler_params=pltpu.CompilerParams(dimension_semantics=("arbitrary",),
                                             vmem_limit_bytes=VMEM_LIMIT),
        name="in_proj",
    )(x, g_mix, w_in)


def _scan_rows(x, op, identity):
    sub = lax.broadcasted_iota(jnp.int32, (SUBLANES, x.shape[1]), 0)
    blocks, carry = [], None
    for i in range(x.shape[0] // SUBLANES):
        blk = x[i * SUBLANES:(i + 1) * SUBLANES, :]
        for shift in (1, 2, 4):
            blk = op(blk, jnp.where(sub >= shift, pltpu.roll(blk, shift, axis=0), identity))
        if carry is not None:
            blk = op(blk, carry)
        carry = jnp.broadcast_to(blk[SUBLANES - 1:SUBLANES, :], blk.shape)
        blocks.append(blk)
    return jnp.concatenate(blocks, axis=0)


def _log_sigmoid(x):
    return jnp.minimum(x, 0.0) - jnp.log1p(jnp.exp(-jnp.abs(x)))


def _seq_init(mconv0_ref, sconv0_ref, c0_ref, n0_ref, m0_ref, xp_sc, cp_sc, c1_ref, n1_ref, m1_ref):
    @pl.when(pl.program_id(1) == 0)
    def _():
        xp_sc[0:SUBLANES, :] = mconv0_ref[...]
        cp_sc[0:SUBLANES, :] = sconv0_ref[...]
        c1_ref[...] = c0_ref[...]
        n1_ref[...] = n0_ref[...]
        m1_ref[...] = m0_ref[...]


def _causal_conv(src_sc, w_ref, width, tt):
    acc = None
    for j in range(width):
        term = src_sc[pl.ds(SUBLANES - (width - 1) + j, tt), :] * w_ref[j:j + 1, :]
        acc = term if acc is None else acc + term
    return acc


def _carry_tail(src_sc, tail_ref, tt):
    tail = src_sc[tt:tt + SUBLANES, :]
    src_sc[0:SUBLANES, :] = tail
    tail_ref[...] = tail


def _head_slices():
    return [slice(h * M_HEAD_DIM, (h + 1) * M_HEAD_DIM) for h in range(M_HEADS)]


def _mlstm_tile(xq, v, ifp, states, wq_ref, wk_ref, bias_ref, ghead_ref, tt, between=()):
    nh = M_HEADS
    hs = _head_slices()
    seq_rows = [slice(j * tt, (j + 1) * tt) for j in range(len(states))]
    g = [ifp[rows, :] + bias_ref[...] for rows in seq_rows]
    b_c = [_scan_rows(_log_sigmoid(gj), jnp.add, 0.0) for gj in g]
    a_c = [gj - pltpu.roll(bj, LANES - nh, axis=1) for gj, bj in zip(g, b_c)]
    amax_c = [_scan_rows(aj, jnp.maximum, -jnp.inf) for aj in a_c]
    a_t = [aj.T for aj in a_c]

    r_idx = lax.broadcasted_iota(jnp.int32, (tt, tt), 0)
    c_idx = lax.broadcasted_iota(jnp.int32, (tt, tt), 1)
    causal = c_idx <= r_idx

    q_all = [_dot(xq[:, hs[h]], wq_ref[h]) * (M_HEAD_DIM ** -0.5) for h in range(nh)]
    k_all = [_dot(xq[:, hs[h]], wk_ref[h]) for h in range(nh)]
    units = [(j, h) for j in range(len(states)) for h in range(nh)]
    ids = range(len(units))
    q = [q_all[h][seq_rows[j], :] for j, h in units]
    k = [k_all[h][seq_rows[j], :] for j, h in units]
    vv = [v[seq_rows[j], hs[h]] for j, h in units]
    c_prev = [states[j][0][h] for j, h in units]
    n_prev = [states[j][1][h:h + 1, :] for j, h in units]
    m_prev = [states[j][2][h:h + 1, 0:1] for j, h in units]
    b_col = [b_c[j][:, nh + h:nh + h + 1] for j, h in units]
    a_col = [a_c[j][:, h:h + 1] for j, h in units]
    a_row = [a_t[j][h:h + 1, :] for j, h in units]
    b_last = [b_c[j][tt - 1:tt, nh + h:nh + h + 1] for j, h in units]
    mm_col = [jnp.maximum(m_prev[i], amax_c[j][:, h:h + 1]) for i, (j, h) in enumerate(units)]
    mm_last = [jnp.maximum(m_prev[i], amax_c[j][tt - 1:tt, h:h + 1]) for i, (j, h) in enumerate(units)]

    extra = []
    pending = list(between)

    def run_one():
        if pending:
            extra.append(pending.pop(0)())

    w_inter = [jnp.exp(m_prev[i] - mm_col[i]) for i in ids]
    qk = [lax.dot_general(q[i].astype(BF16), k[i].astype(BF16), (((1,), (1,)), ((), ())),
                          preferred_element_type=F32) for i in ids]
    run_one()
    s = [jnp.exp(jnp.where(causal, a_row[i] - mm_col[i], -jnp.inf)) * qk[i] for i in ids]
    run_one()
    num = [w_inter[i] * _dot(q[i], c_prev[i]) + _dot(s[i], vv[i]) for i in ids]
    den = [w_inter[i] * jnp.sum(q[i] * n_prev[i], axis=-1, keepdims=True)
           + jnp.sum(s[i], axis=-1, keepdims=True) for i in ids]
    run_one()
    out = [num[i] * (1.0 / jnp.maximum(jnp.abs(den[i]), jnp.exp(-(b_col[i] + mm_col[i])))) for i in ids]
    out = [out[i] * lax.rsqrt(jnp.mean(out[i] * out[i], axis=-1, keepdims=True) + RMS_EPS)
           * ghead_ref[:, hs[units[i][1]]] for i in ids]
    while pending:
        run_one()

    decay = [jnp.exp(m_prev[i] - mm_last[i]) for i in ids]
    wk = [jnp.exp(a_col[i] - mm_last[i]) * k[i] for i in ids]
    for i, (j, h) in enumerate(units):
        c_ref, n_ref, m_ref = states[j]
        c_ref[h] = decay[i] * c_prev[i] + lax.dot_general(
            wk[i].astype(BF16), vv[i].astype(BF16), (((0,), (0,)), ((), ())), preferred_element_type=F32)
        n_ref[h:h + 1, :] = decay[i] * n_prev[i] + jnp.sum(wk[i], axis=0, keepdims=True)
        m_ref[h:h + 1, :] = jnp.broadcast_to(b_last[i] + mm_last[i], (1, LANES))
    return [out[j * nh:(j + 1) * nh] for j in range(len(states))], extra


def _merge(x, hm, u, o, sb, ga, gb, wa_ref, wsout_ref, wo_ref):
    y_a = _dot(_sigmoid(o) * hm, wa_ref[...])
    y_b = _dot(sb * u, wsout_ref[...])
    return x + _dot(_sigmoid(ga) * y_a + _sigmoid(gb) * y_b, wo_ref[...])


def _mlstm_kernel(x_ref, xm_ref, v_ref, cx_ref, if_ref, o_ref, sb_ref, ga_ref, gb_ref,
                  mconv0_ref, sconv0_ref, c0_ref, n0_ref, m0_ref,
                  w_mconv_ref, w_sconv_ref, wq_ref, wk_ref, bias_ref, ghead_ref, wa_ref, wsout_ref, wo_ref,
                  x1_ref, c1_ref, n1_ref, m1_ref, mconv1_ref, sconv1_ref,
                  xp_sc, cp_sc, *, tt, nb):
    xq, u, states = [], [], []
    for j in range(nb):
        rows = slice(j * tt, (j + 1) * tt)
        xp, cp, state = xp_sc.at[j], cp_sc.at[j], (c1_ref.at[j], n1_ref.at[j], m1_ref.at[j])
        _seq_init(mconv0_ref.at[j], sconv0_ref.at[j], c0_ref.at[j], n0_ref.at[j], m0_ref.at[j], xp, cp, *state)
        xp[SUBLANES:SUBLANES + tt, :] = xm_ref[rows, :]
        cp[SUBLANES:SUBLANES + tt, :] = cx_ref[rows, :]
        xq.append(_silu(_causal_conv(xp, w_mconv_ref, M_CONV_W, tt)))
        u.append(_causal_conv(cp, w_sconv_ref, S_CONV_W, tt))
        _carry_tail(xp, mconv1_ref.at[j], tt)
        _carry_tail(cp, sconv1_ref.at[j], tt)
        states.append(state)
    stack = lambda parts: parts[0] if len(parts) == 1 else jnp.concatenate(parts, axis=0)
    out, _ = _mlstm_tile(stack(xq), v_ref[...], if_ref[...], states, wq_ref, wk_ref, bias_ref, ghead_ref, tt)
    hm = stack([jnp.concatenate(heads, axis=1) for heads in out])
    x1_ref[...] = _merge(x_ref[...], hm, stack(u), o_ref[...], sb_ref[...], ga_ref[...], gb_ref[...],
                         wa_ref, wsout_ref, wo_ref)


def _mlstm(x, xm, v, cx, ifp, o, sb, ga, gb, mconv0, sconv0, c0, n0, m0, wts, *, bsz, t, tt, nb):
    nt = t // tt
    assert nb == 1 or nt == 1, "several sequences per step only when a step covers whole sequences"
    tok = lambda b, i: (b * nt + i, 0)
    seq3 = lambda b, i: (b, 0, 0)
    seq4 = lambda b, i: (b, 0, 0, 0)
    big = pl.BlockSpec((nb * tt, D_MODEL), tok)
    tail = pl.BlockSpec((nb, SUBLANES, D_MODEL), seq3)
    c_spec = pl.BlockSpec((nb, M_HEADS, M_HEAD_DIM, M_HEAD_DIM), seq4)
    n_spec = pl.BlockSpec((nb, M_HEADS, M_HEAD_DIM), seq3)
    m_spec = pl.BlockSpec((nb, SUBLANES, LANES), seq3)
    names = ("w_mconv", "w_sconv", "w_q", "w_k", "gate_bias", "g_head", "w_a", "w_sout", "w_o")
    weights = [wts[k] for k in names]
    return pl.pallas_call(
        functools.partial(_mlstm_kernel, tt=tt, nb=nb),
        grid=(bsz // nb, nt),
        in_specs=[big] * 4 + [pl.BlockSpec((nb * tt, LANES), tok)] + [big] * 4
                 + [tail, tail, c_spec, n_spec, m_spec] + [_resident(w.shape) for w in weights],
        out_specs=[big, c_spec, n_spec, m_spec, tail, tail],
        out_shape=[jax.ShapeDtypeStruct(x.shape, F32),
                   jax.ShapeDtypeStruct(c0.shape, F32), jax.ShapeDtypeStruct(n0.shape, F32),
                   jax.ShapeDtypeStruct(m0.shape, F32),
                   jax.ShapeDtypeStruct(mconv0.shape, F32), jax.ShapeDtypeStruct(sconv0.shape, F32)],
        scratch_shapes=[pltpu.VMEM((nb, tt + SUBLANES, D_MODEL), F32),
                        pltpu.VMEM((nb, tt + SUBLANES, D_MODEL), F32)],
        compiler_params=pltpu.CompilerParams(dimension_semantics=("arbitrary", "arbitrary"),
                                             vmem_limit_bytes=VMEM_LIMIT),
        name="mlstm",
    )(x, xm, v, cx, ifp, o, sb, ga, gb, mconv0, sconv0, c0, n0, m0, *weights)


def _mixer_kernel(x_ref, g_ref, w_in_ref, w_mconv_ref, w_sconv_ref, wq_ref, wk_ref,
                  bias_ref, ghead_ref, wa_ref, wsout_ref, wo_ref, wg32_ref, wu32_ref, wd32_ref,
                  x1_ref, c1_ref, n1_ref, m1_ref, mconv1_ref, sconv1_ref, wg16_ref, wu16_ref, wd16_ref,
                  xp_sc, cp_sc, hm_sc, *, tt):
    @pl.when(pl.program_id(1) == 0)
    def _():
        xp_sc[0:SUBLANES, :] = jnp.zeros((SUBLANES, D_MODEL), F32)
        cp_sc[0:SUBLANES, :] = jnp.zeros((SUBLANES, D_MODEL), F32)
        c1_ref[...] = jnp.zeros(c1_ref.shape, F32)
        n1_ref[...] = jnp.zeros(n1_ref.shape, F32)
        m1_ref[...] = jnp.zeros(m1_ref.shape, F32)

    wg16_ref[...] = wg32_ref[...].astype(BF16)
    wu16_ref[...] = wu32_ref[...].astype(BF16)
    wd16_ref[...] = wd32_ref[...].astype(BF16)

    proj = _projector(_rms(x_ref[...], g_ref[...]).astype(BF16), w_in_ref)
    xp_sc[SUBLANES:SUBLANES + tt, :] = proj("xm")
    ifp = proj("gates")
    cp_sc[SUBLANES:SUBLANES + tt, :] = proj("sc") * proj("sx")
    xq = _silu(_causal_conv(xp_sc, w_mconv_ref, M_CONV_W, tt))
    u = _causal_conv(cp_sc, w_sconv_ref, S_CONV_W, tt)
    _carry_tail(xp_sc, mconv1_ref, tt)
    _carry_tail(cp_sc, sconv1_ref, tt)
    v = proj("v")
    hs = _head_slices()
    (out,), (y_b, sig_o, sig_ga, sig_gb) = _mlstm_tile(
        xq, v, ifp, [(c1_ref, n1_ref, m1_ref)], wq_ref, wk_ref, bias_ref, ghead_ref, tt,
        between=(lambda: _dot(proj("sb") * u, wsout_ref[...]),
                 lambda: _sigmoid(proj("o")),
                 lambda: _sigmoid(proj("ga")),
                 lambda: _sigmoid(proj("gb"))))
    for hd, sl in enumerate(hs):
        hm_sc[:, sl] = (sig_o[:, sl] * out[hd]).astype(BF16)
    y_a = jnp.dot(hm_sc[...], wa_ref[...], preferred_element_type=F32)
    merged = sig_ga * y_a + sig_gb * y_b
    x1_ref[...] = x_ref[...] + _dot(merged, wo_ref[...])


def _mixer(x, wts, experts_f32, *, bsz, t, tt):
    nt = t // tt
    steps = bsz * nt
    tok = lambda b, i: (b * nt + i, 0)
    seq3 = lambda b, i: (b, 0, 0)
    seq4 = lambda b, i: (b, 0, 0, 0)
    big = pl.BlockSpec((tt, D_MODEL), tok)
    tail = pl.BlockSpec((None, SUBLANES, D_MODEL), seq3)
    c_spec = pl.BlockSpec((None, M_HEADS, M_HEAD_DIM, M_HEAD_DIM), seq4)
    n_spec = pl.BlockSpec((None, M_HEADS, M_HEAD_DIM), seq3)
    m_spec = pl.BlockSpec((None, SUBLANES, LANES), seq3)
    names = ("g_mix", "w_in", "w_mconv", "w_sconv", "w_q", "w_k", "gate_bias", "g_head",
             "w_a", "w_sout", "w_o")
    weights = [wts[k] for k in names]
    flat = [w.reshape(-1, w.shape[-1]) for w in experts_f32]
    cast_specs = [pl.BlockSpec((w.shape[0] // steps, w.shape[1]), tok) for w in flat]
    outs = pl.pallas_call(
        functools.partial(_mixer_kernel, tt=tt),
        grid=(bsz, nt),
        in_specs=[big] + [_resident(w.shape) for w in weights] + cast_specs,
        out_specs=[big, c_spec, n_spec, m_spec, tail, tail] + cast_specs,
        out_shape=[jax.ShapeDtypeStruct(x.shape, F32),
                   jax.ShapeDtypeStruct((bsz, M_HEADS, M_HEAD_DIM, M_HEAD_DIM), F32),
                   jax.ShapeDtypeStruct((bsz, M_HEADS, M_HEAD_DIM), F32),
                   jax.ShapeDtypeStruct((bsz, SUBLANES, LANES), F32),
                   jax.ShapeDtypeStruct((bsz, SUBLANES, D_MODEL), F32),
                   jax.ShapeDtypeStruct((bsz, SUBLANES, D_MODEL), F32)]
                  + [jax.ShapeDtypeStruct(w.shape, BF16) for w in flat],
        scratch_shapes=[pltpu.VMEM((tt + SUBLANES, D_MODEL), F32), pltpu.VMEM((tt + SUBLANES, D_MODEL), F32),
                        pltpu.VMEM((tt, D_MODEL), BF16)],
        compiler_params=pltpu.CompilerParams(dimension_semantics=("arbitrary", "arbitrary"),
                                             vmem_limit_bytes=VMEM_LIMIT),
        name="mixer",
    )(x, *weights, *flat)
    return outs[:6], [o.reshape(w.shape) for o, w in zip(outs[6:], experts_f32)]


MOE_WINDOW = 1024
MOE_ROW_TILE = 128
MOE_ALIGN = 16
EXPERTS_PER_STEP = EXPERTS_PER_GROUP
MOE_STEPS = N_EXPERTS // EXPERTS_PER_STEP
MOE_CHUNK = 256
MOE_SORT_CHUNK = 1280
MOE_UNSORT_CHUNK = 512
MOE_SORTED_ROWS = -(-(MOE_WINDOW + N_GROUPS * (MOE_ALIGN - 1) + MOE_ROW_TILE) // MOE_CHUNK) * MOE_CHUNK


def _split_bf16(x):
    hi = x.astype(BF16)
    lo = (x - hi.astype(F32)).astype(BF16)
    return hi, lo


CW_LANE_STRIDE = 40


def _pack_split3(cw):
    hi = cw.astype(BF16).astype(F32)
    r1 = cw - hi
    mid = r1.astype(BF16).astype(F32)
    lo = (r1 - mid).astype(BF16).astype(F32)
    return (hi + pltpu.roll(mid, CW_LANE_STRIDE, axis=1) + pltpu.roll(lo, 2 * CW_LANE_STRIDE, axis=1)).astype(BF16)


def _unpack_split3(packed):
    return (packed + pltpu.roll(packed, LANES - CW_LANE_STRIDE, axis=1)
            + pltpu.roll(packed, LANES - 2 * CW_LANE_STRIDE, axis=1))


def _route(logits):
    lane = lax.broadcasted_iota(jnp.int32, logits.shape, 1)
    neg = -jnp.inf
    big = jnp.int32(LANES)
    is_grp = lane < N_GROUPS
    g_max = jnp.max(jnp.where(is_grp, logits, neg), axis=-1, keepdims=True)
    g_sel = jnp.min(jnp.where(is_grp & (logits == g_max), lane, big), axis=-1, keepdims=True)
    p_grp = 1.0 / jnp.sum(jnp.where(is_grp, jnp.exp(logits - g_max), 0.0), axis=-1, keepdims=True)
    lo = N_GROUPS + g_sel * EXPERTS_PER_GROUP
    in_grp = (lane >= lo) & (lane < lo + EXPERTS_PER_GROUP)
    v1 = jnp.max(jnp.where(in_grp, logits, neg), axis=-1, keepdims=True)
    e1 = jnp.min(jnp.where(in_grp & (logits == v1), lane, big), axis=-1, keepdims=True)
    rest = in_grp & (lane != e1)
    v2 = jnp.max(jnp.where(rest, logits, neg), axis=-1, keepdims=True)
    e2 = jnp.min(jnp.where(rest & (logits == v2), lane, big), axis=-1, keepdims=True)
    z = jnp.exp(v2 - v1)
    w1 = p_grp / (1.0 + z)
    w2 = p_grp * z / (1.0 + z)
    return g_sel, jnp.where(lane == e1, w1, jnp.where(lane == e2, w2, 0.0))


def _moe_tail_kernel(x1_ref, p_ref, gffn_ref, wr_ref, br_ref, wg_ref, wu_ref, wd_ref,
                     gple_ref, wpg_ref, wpp_ref, gfin_ref, y_ref,
                     xs_sc, cws_sc, osort_sc, pos_sc, off_sm, nt_sm):
    s = pl.program_id(1)
    w = MOE_WINDOW

    @pl.when(s == 0)
    def _():
        hn = _rms(x1_ref[...], gffn_ref[...])
        hn_hi, hn_lo = _split_bf16(hn)
        hh_hl = jnp.dot(hn_hi, wr_ref[...], preferred_element_type=F32)
        logits = (hh_hl[:, :LANES] + hh_hl[:, LANES:]
                  + jnp.dot(hn_lo, wr_ref[:, :LANES], preferred_element_type=F32)) + br_ref[...]
        g_sel, cw = _route(logits)
        lane = lax.broadcasted_iota(jnp.int32, (w, LANES), 1)
        onehot = jnp.where(lane == g_sel, 1.0, 0.0)
        cum = _scan_rows(onehot, jnp.add, 0.0)
        cnt = cum[w - 1:w, :].astype(jnp.int32)
        cnt_pad = ((cnt + (MOE_ALIGN - 1)) // MOE_ALIGN) * MOE_ALIGN
        lane1 = lax.broadcasted_iota(jnp.int32, (1, LANES), 1)
        off = jnp.zeros((1, LANES), jnp.int32)
        for gi in range(N_GROUPS - 1):
            off = off + jnp.where(lane1 > gi, cnt_pad[:, gi:gi + 1], 0)
        n_tiles = (cnt + (MOE_ROW_TILE - 1)) // MOE_ROW_TILE
        for gi in range(N_GROUPS):
            off_sm[gi] = off[0, gi]
            nt_sm[gi] = n_tiles[0, gi]
        pos = jnp.sum(onehot * (off.astype(F32) + cum - 1.0), axis=-1, keepdims=True)
        pos_b = jnp.broadcast_to(pos, (w, LANES))
        pos_sc[...] = pos_b
        pos_row = pos_b.T[0:1, :].astype(jnp.int32)
        cw_packed = _pack_split3(cw)
        for c in range(MOE_SORTED_ROWS // MOE_SORT_CHUNK):
            rows = lax.broadcasted_iota(jnp.int32, (MOE_SORT_CHUNK, w), 0) + c * MOE_SORT_CHUNK
            sel = jnp.where(rows == pos_row, 1.0, 0.0).astype(BF16)
            sl = slice(c * MOE_SORT_CHUNK, (c + 1) * MOE_SORT_CHUNK)
            xs_sc[sl, :] = jnp.dot(sel, hn_hi, preferred_element_type=F32).astype(BF16)
            cws_sc[sl, :] = _unpack_split3(jnp.dot(sel, cw_packed, preferred_element_type=F32))
        osort_sc[...] = jnp.zeros(osort_sc.shape, BF16)

    row0 = off_sm[s]
    lane_t = lax.broadcasted_iota(jnp.int32, (MOE_ROW_TILE, LANES), 1)
    first_lane = N_GROUPS + s * EXPERTS_PER_STEP

    def tile_body(i, carry):
        r0 = pl.multiple_of(row0 + i * MOE_ROW_TILE, MOE_ALIGN)
        xt = xs_sc[pl.ds(r0, MOE_ROW_TILE), :]
        cwt = cws_sc[pl.ds(r0, MOE_ROW_TILE), :]
        parts = []
        for e in range(EXPERTS_PER_STEP):
            hg = jnp.dot(xt, wg_ref[e], preferred_element_type=F32)
            hu = jnp.dot(xt, wu_ref[e], preferred_element_type=F32)
            col = jnp.sum(jnp.where(lane_t == first_lane + e, cwt, 0.0), axis=-1, keepdims=True)
            parts.append((_silu(hg) * hu * col).astype(BF16))
        he = jnp.concatenate(parts, axis=-1)
        out = jnp.dot(he, wd_ref[...], preferred_element_type=F32)
        osort_sc[pl.ds(r0, MOE_ROW_TILE), :] = out.astype(BF16)
        return carry

    lax.fori_loop(0, nt_sm[s], tile_body, 0)

    @pl.when(s == MOE_STEPS - 1)
    def _():
        osort = osort_sc[...]
        for c in range(w // MOE_UNSORT_CHUNK):
            sl = slice(c * MOE_UNSORT_CHUNK, (c + 1) * MOE_UNSORT_CHUNK)
            pos_col = pos_sc[sl, 0:1].astype(jnp.int32)
            cols = lax.broadcasted_iota(jnp.int32, (MOE_UNSORT_CHUNK, MOE_SORTED_ROWS), 1)
            sel = jnp.where(cols == pos_col, 1.0, 0.0).astype(BF16)
            x2 = x1_ref[sl, :] + jnp.dot(sel, osort, preferred_element_type=F32)
            gate = _sigmoid(_dot(_rms(x2, gple_ref[...]), wpg_ref[...]))
            x3 = x2 + gate * _dot(p_ref[sl, :], wpp_ref[...])
            y_ref[sl, :] = _rms(x3, gfin_ref[...])


def _moe_tail(x1, p, g_ffn, w_r, b_r, wg, wu, wd, g_ple, w_pg, w_pp, g_fin):
    n = x1.shape[0]
    tm = MOE_WINDOW
    row = lambda i, s: (i, 0)
    step = lambda i, s: (s, 0, 0)
    big = pl.BlockSpec((tm, D_MODEL), row)
    sw = EXPERTS_PER_STEP * D_EXPERT
    return pl.pallas_call(
        _moe_tail_kernel,
        grid=(n // tm, MOE_STEPS),
        in_specs=[big, pl.BlockSpec((tm, PLE_DIM), row), _resident(g_ffn.shape), _resident(w_r.shape),
                  _resident(b_r.shape),
                  pl.BlockSpec((EXPERTS_PER_STEP, D_MODEL, D_EXPERT), step),
                  pl.BlockSpec((EXPERTS_PER_STEP, D_MODEL, D_EXPERT), step),
                  pl.BlockSpec((None, sw, D_MODEL), step),
                  _resident(g_ple.shape), _resident(w_pg.shape), _resident(w_pp.shape), _resident(g_fin.shape)],
        out_specs=big,
        out_shape=jax.ShapeDtypeStruct((n, D_MODEL), F32),
        scratch_shapes=[pltpu.VMEM((MOE_SORTED_ROWS, D_MODEL), BF16), pltpu.VMEM((MOE_SORTED_ROWS, LANES), F32),
                        pltpu.VMEM((MOE_SORTED_ROWS, D_MODEL), BF16), pltpu.VMEM((tm, LANES), F32),
                        pltpu.SMEM((N_GROUPS,), jnp.int32), pltpu.SMEM((N_GROUPS,), jnp.int32)],
        compiler_params=pltpu.CompilerParams(dimension_semantics=("arbitrary", "arbitrary"),
                                             vmem_limit_bytes=VMEM_LIMIT),
        name="moe_tail",
    )(x1, p, g_ffn, w_r, b_r, wg, wu, wd, g_ple, w_pg, w_pp, g_fin)


PROMPT_TIME_TILE = 256
SAMPLE_ROW_TILE = 512
SAMPLE_SEQS_PER_STEP = 8


def _pad_tail(buf):
    return jnp.pad(buf, ((0, 0), (SUBLANES - buf.shape[1], 0), (0, 0)))


def _moe(x1, p, wts):
    return _moe_tail(x1, p, wts["g_ffn"], wts["w_r"], wts["b_r"], wts["w_gate"], wts["w_up"], wts["w_down"],
                     wts["g_ple"], wts["w_ple_gate"], wts["w_ple_proj"], wts["g_final"])


def _outputs(y, shape, c1, n1, m1p, mconv1, sconv1):
    return (y.reshape(shape), c1[None], n1[None], m1p[None, :, :M_HEADS, 0],
            mconv1[None, :, SUBLANES - (M_CONV_W - 1):, :], sconv1[None, :, SUBLANES - (S_CONV_W - 1):, :])


def _prompt_group(x, p, wts, experts_f32, *, tt):
    bsz, t, _ = x.shape
    (x1, *states), (wg, wu, wd) = _mixer(x.reshape(bsz * t, D_MODEL), wts, experts_f32, bsz=bsz, t=t, tt=tt)
    wts.update(w_gate=wg, w_up=wu, w_down=wd.reshape(MOE_STEPS, EXPERTS_PER_STEP * D_EXPERT, D_MODEL))
    return _outputs(_moe(x1, p.reshape(bsz * t, PLE_DIM), wts), x.shape, *states)


def _sample_group(x, p, mconv0, sconv0, c0, n0, m0, wts, *, tm):
    bsz, t, _ = x.shape
    xf = x.reshape(bsz * t, D_MODEL)
    m0p = jnp.broadcast_to(jnp.pad(m0, ((0, 0), (0, SUBLANES - M_HEADS)))[:, :, None], (bsz, SUBLANES, LANES))
    xm, v, o, ifp, sb, cx, ga, gb = _in_proj(xf, wts["g_mix"], wts["w_in"], tm)
    x1, *states = _mlstm(xf, xm, v, cx, ifp, o, sb, ga, gb, _pad_tail(mconv0), _pad_tail(sconv0), c0, n0, m0p, wts,
                         bsz=bsz, t=t, tt=t, nb=SAMPLE_SEQS_PER_STEP)
    return _outputs(_moe(x1, p.reshape(bsz * t, PLE_DIM), wts), x.shape, *states)


def kernel(x_prompt, x_sample, p_prompt, p_sample, state_mlstm_C, state_mlstm_n, state_mlstm_m, state_mlstm_conv, state_sconv, g_mix, w_in, w_mconv, w_q, w_k, b_i, b_f, g_head, w_a, w_sconv, w_sout, w_o, g_ffn, w_rg, b_rg, w_re, b_re, w_gate, w_up, w_down, g_ple, w_ple_gate, w_ple_proj, g_final):
    assert g_mix.shape[0] == 1, "single-layer trunk"
    d = D_MODEL
    n_if = 2 * M_HEADS
    w_in0 = lax.optimization_barrier(w_in[0].astype(BF16))
    w_router = jnp.pad(jnp.concatenate([w_rg[0], w_re[0]], axis=1), ((0, 0), (0, LANES - N_GROUPS - N_EXPERTS)))
    wr_hi = w_router.astype(BF16)
    wr_lo = (w_router - wr_hi.astype(F32)).astype(BF16)
    w_r = jnp.concatenate([wr_hi, wr_lo], axis=1)
    wts = {
        "g_mix": g_mix,
        "w_in": jnp.concatenate([w_in0[:, :_GATE_COL],
                                 jnp.pad(w_in0[:, _GATE_COL:_GATE_COL + n_if], ((0, 0), (0, LANES - n_if))),
                                 w_in0[:, _GATE_COL + n_if:]], axis=1),
        "w_mconv": w_mconv[0], "w_sconv": w_sconv[0],
        "w_q": w_q[0].astype(BF16), "w_k": w_k[0].astype(BF16),
        "gate_bias": jnp.pad(jnp.concatenate([b_i[0], b_f[0]])[None, :], ((0, 0), (0, LANES - n_if))),
        "g_head": g_head[0].reshape(1, d),
        "w_a": w_a[0].astype(BF16), "w_sout": w_sout[0].astype(BF16), "w_o": w_o[0].astype(BF16),
        "g_ffn": g_ffn, "w_r": w_r,
        "b_r": jnp.pad(jnp.concatenate([b_rg[0], b_re[0]])[None, :], ((0, 0), (0, LANES - N_GROUPS - N_EXPERTS))),
        "g_ple": g_ple, "w_ple_gate": w_ple_gate[0].astype(BF16), "w_ple_proj": w_ple_proj[0].astype(BF16),
        "g_final": g_final[None, :],
    }
    yp, *st_p = _prompt_group(x_prompt, p_prompt[0], wts, (w_gate[0], w_up[0], w_down[0]), tt=PROMPT_TIME_TILE)
    ys, *st_s = _sample_group(x_sample, p_sample[0], state_mlstm_conv[0], state_sconv[0],
                              state_mlstm_C[0], state_mlstm_n[0], state_mlstm_m[0], wts, tm=SAMPLE_ROW_TILE)
    return (yp, ys, *st_p, *st_s)
```

```python
import functools

import jax
import jax.numpy as jnp
from jax import lax
from jax.experimental import pallas as pl
from jax.experimental.pallas import tpu as pltpu

D_MODEL = 1024
M_HEADS = 4
M_HEAD_DIM = 256
M_CONV_W = 4
S_CONV_W = 3
N_GROUPS = 4
EXPERTS_PER_GROUP = 8
N_EXPERTS = N_GROUPS * EXPERTS_PER_GROUP
D_EXPERT = 256
PLE_DIM = 256
RMS_EPS = 1e-6

LANES = 128
SUBLANES = 8
VMEM_LIMIT = 60 * 1024 * 1024

BF16 = jnp.bfloat16
F32 = jnp.float32


def _rms(x, g):
    return x * lax.rsqrt(jnp.mean(x * x, axis=-1, keepdims=True) + RMS_EPS) * g


def _dot(a, b):
    return jnp.dot(a.astype(BF16), b.astype(BF16), preferred_element_type=F32)


def _sigmoid(x):
    return 0.5 * jnp.tanh(0.5 * x) + 0.5


def _silu(x):
    return x * _sigmoid(x)


def _resident(shape):
    nd = len(shape)
    return pl.BlockSpec(shape, lambda *_: (0,) * nd, pipeline_mode=pl.Buffered(1))


_GATE_COL = 3 * D_MODEL
_REST_COL = _GATE_COL + LANES
IN_COL = {"xm": 0, "v": D_MODEL, "o": 2 * D_MODEL, "sb": _REST_COL, "sc": _REST_COL + D_MODEL,
          "sx": _REST_COL + 2 * D_MODEL, "ga": _REST_COL + 3 * D_MODEL, "gb": _REST_COL + 4 * D_MODEL}


def _projector(h, w_in_ref):
    def proj(name):
        lo, width = (_GATE_COL, LANES) if name == "gates" else (IN_COL[name], D_MODEL)
        return jnp.dot(h, w_in_ref[:, lo:lo + width], preferred_element_type=F32)
    return proj


def _in_proj_kernel(x_ref, g_ref, w_in_ref,
                    xm_ref, v_ref, o_ref, if_ref, sb_ref, cx_ref, ga_ref, gb_ref):
    proj = _projector(_rms(x_ref[...], g_ref[...]).astype(BF16), w_in_ref)
    xm_ref[...] = proj("xm")
    v_ref[...] = proj("v")
    o_ref[...] = proj("o")
    if_ref[...] = proj("gates")
    sb_ref[...] = proj("sb")
    cx_ref[...] = proj("sc") * proj("sx")
    ga_ref[...] = proj("ga")
    gb_ref[...] = proj("gb")


def _in_proj(x, g_mix, w_in, tm):
    n = x.shape[0]
    row = lambda i: (i, 0)
    big = pl.BlockSpec((tm, D_MODEL), row)
    outs = [jax.ShapeDtypeStruct((n, D_MODEL), F32)] * 3 + [jax.ShapeDtypeStruct((n, LANES), F32)] \
        + [jax.ShapeDtypeStruct((n, D_MODEL), F32)] * 4
    return pl.pallas_call(
        _in_proj_kernel,
        grid=(n // tm,),
        in_specs=[big, _resident(g_mix.shape), _resident(w_in.shape)],
        out_specs=[big, big, big, pl.BlockSpec((tm, LANES), row), big, big, big, big],
        out_shape=outs,
        compiler_params=pltpu.CompilerParams(dimension_semantics=("arbitrary",),
                                             vmem_limit_bytes=VMEM_LIMIT),
        name="in_proj",
    )(x, g_mix, w_in)


def _scan_rows(x, op, identity):
    sub = lax.broadcasted_iota(jnp.int32, (SUBLANES, x.shape[1]), 0)
    blocks, carry = [], None
    for i in range(x.shape[0] // SUBLANES):
        blk = x[i * SUBLANES:(i + 1) * SUBLANES, :]
        for shift in (1, 2, 4):
            blk = op(blk, jnp.where(sub >= shift, pltpu.roll(blk, shift, axis=0), identity))
        if carry is not None:
            blk = op(blk, carry)
        carry = jnp.broadcast_to(blk[SUBLANES - 1:SUBLANES, :], blk.shape)
        blocks.append(blk)
    return jnp.concatenate(blocks, axis=0)


def _log_sigmoid(x):
    return jnp.minimum(x, 0.0) - jnp.log1p(jnp.exp(-jnp.abs(x)))


def _seq_init(mconv0_ref, sconv0_ref, c0_ref, n0_ref, m0_ref, xp_sc, cp_sc, c1_ref, n1_ref, m1_ref):
    @pl.when(pl.program_id(1) == 0)
    def _():
        xp_sc[0:SUBLANES, :] = mconv0_ref[...]
        cp_sc[0:SUBLANES, :] = sconv0_ref[...]
        c1_ref[...] = c0_ref[...]
        n1_ref[...] = n0_ref[...]
        m1_ref[...] = m0_ref[...]


def _causal_conv(src_sc, w_ref, width, tt):
    acc = None
    for j in range(width):
        term = src_sc[pl.ds(SUBLANES - (width - 1) + j, tt), :] * w_ref[j:j + 1, :]
        acc = term if acc is None else acc + term
    return acc


def _carry_tail(src_sc, tail_ref, tt):
    tail = src_sc[tt:tt + SUBLANES, :]
    src_sc[0:SUBLANES, :] = tail
    tail_ref[...] = tail


def _head_slices():
    return [slice(h * M_HEAD_DIM, (h + 1) * M_HEAD_DIM) for h in range(M_HEADS)]


def _mlstm_tile(xq, v, ifp, states, wq_ref, wk_ref, bias_ref, ghead_ref, tt, between=()):
    nh = M_HEADS
    hs = _head_slices()
    seq_rows = [slice(j * tt, (j + 1) * tt) for j in range(len(states))]
    g = [ifp[rows, :] + bias_ref[...] for rows in seq_rows]
    b_c = [_scan_rows(_log_sigmoid(gj), jnp.add, 0.0) for gj in g]
    a_c = [gj - pltpu.roll(bj, LANES - nh, axis=1) for gj, bj in zip(g, b_c)]
    amax_c = [_scan_rows(aj, jnp.maximum, -jnp.inf) for aj in a_c]
    a_t = [aj.T for aj in a_c]

    r_idx = lax.broadcasted_iota(jnp.int32, (tt, tt), 0)
    c_idx = lax.broadcasted_iota(jnp.int32, (tt, tt), 1)
    causal = c_idx <= r_idx

    q_all = [_dot(xq[:, hs[h]], wq_ref[h]) * (M_HEAD_DIM ** -0.5) for h in range(nh)]
    k_all = [_dot(xq[:, hs[h]], wk_ref[h]) for h in range(nh)]
    units = [(j, h) for j in range(len(states)) for h in range(nh)]
    ids = range(len(units))
    q = [q_all[h][seq_rows[j], :] for j, h in units]
    k = [k_all[h][seq_rows[j], :] for j, h in units]
    vv = [v[seq_rows[j], hs[h]] for j, h in units]
    c_prev = [states[j][0][h] for j, h in units]
    n_prev = [states[j][1][h:h + 1, :] for j, h in units]
    m_prev = [states[j][2][h:h + 1, 0:1] for j, h in units]
    b_col = [b_c[j][:, nh + h:nh + h + 1] for j, h in units]
    a_col = [a_c[j][:, h:h + 1] for j, h in units]
    a_row = [a_t[j][h:h + 1, :] for j, h in units]
    b_last = [b_c[j][tt - 1:tt, nh + h:nh + h + 1] for j, h in units]
    mm_col = [jnp.maximum(m_prev[i], amax_c[j][:, h:h + 1]) for i, (j, h) in enumerate(units)]
    mm_last = [jnp.maximum(m_prev[i], amax_c[j][tt - 1:tt, h:h + 1]) for i, (j, h) in enumerate(units)]

    extra = []
    pending = list(between)

    def run_one():
        if pending:
            extra.append(pending.pop(0)())

    w_inter = [jnp.exp(m_prev[i] - mm_col[i]) for i in ids]
    qk = [lax.dot_general(q[i].astype(BF16), k[i].astype(BF16), (((1,), (1,)), ((), ())),
                          preferred_element_type=F32) for i in ids]
    run_one()
    s = [jnp.exp(jnp.where(causal, a_row[i] - mm_col[i], -jnp.inf)) * qk[i] for i in ids]
    run_one()
    decay = [jnp.exp(m_prev[i] - mm_last[i]) for i in ids]
    wk = [jnp.exp(a_col[i] - mm_last[i]) * k[i] for i in ids]
    for i, (j, h) in enumerate(units):
        c_ref, n_ref, m_ref = states[j]
        c_ref[h] = decay[i] * c_prev[i] + lax.dot_general(
            wk[i].astype(BF16), vv[i].astype(BF16), (((0,), (0,)), ((), ())), preferred_element_type=F32)
        n_ref[h:h + 1, :] = decay[i] * n_prev[i] + jnp.sum(wk[i], axis=0, keepdims=True)
        m_ref[h:h + 1, :] = jnp.broadcast_to(b_last[i] + mm_last[i], (1, LANES))
    num = [w_inter[i] * _dot(q[i], c_prev[i]) + _dot(s[i], vv[i]) for i in ids]
    den = [w_inter[i] * jnp.sum(q[i] * n_prev[i], axis=-1, keepdims=True)
           + jnp.sum(s[i], axis=-1, keepdims=True) for i in ids]
    run_one()
    out = [num[i] * (1.0 / jnp.maximum(jnp.abs(den[i]), jnp.exp(-(b_col[i] + mm_col[i])))) for i in ids]
    out = [out[i] * lax.rsqrt(jnp.mean(out[i] * out[i], axis=-1, keepdims=True) + RMS_EPS)
           * ghead_ref[:, hs[units[i][1]]] for i in ids]
    while pending:
        run_one()
    return [out[j * nh:(j + 1) * nh] for j in range(len(states))], extra


def _merge(x, hm, u, o, sb, ga, gb, wa_ref, wsout_ref, wo_ref):
    y_a = _dot(_sigmoid(o) * hm, wa_ref[...])
    y_b = _dot(sb * u, wsout_ref[...])
    return x + _dot(_sigmoid(ga) * y_a + _sigmoid(gb) * y_b, wo_ref[...])


def _mlstm_kernel(x_ref, xm_ref, v_ref, cx_ref, if_ref, o_ref, sb_ref, ga_ref, gb_ref,
                  mconv0_ref, sconv0_ref, c0_ref, n0_ref, m0_ref,
                  w_mconv_ref, w_sconv_ref, wq_ref, wk_ref, bias_ref, ghead_ref, wa_ref, wsout_ref, wo_ref,
                  x1_ref, c1_ref, n1_ref, m1_ref, mconv1_ref, sconv1_ref,
                  xp_sc, cp_sc, *, tt, nb):
    xq, u, states = [], [], []
    for j in range(nb):
        rows = slice(j * tt, (j + 1) * tt)
        xp, cp, state = xp_sc.at[j], cp_sc.at[j], (c1_ref.at[j], n1_ref.at[j], m1_ref.at[j])
        _seq_init(mconv0_ref.at[j], sconv0_ref.at[j], c0_ref.at[j], n0_ref.at[j], m0_ref.at[j], xp, cp, *state)
        xp[SUBLANES:SUBLANES + tt, :] = xm_ref[rows, :]
        cp[SUBLANES:SUBLANES + tt, :] = cx_ref[rows, :]
        xq.append(_silu(_causal_conv(xp, w_mconv_ref, M_CONV_W, tt)))
        u.append(_causal_conv(cp, w_sconv_ref, S_CONV_W, tt))
        _carry_tail(xp, mconv1_ref.at[j], tt)
        _carry_tail(cp, sconv1_ref.at[j], tt)
        states.append(state)
    stack = lambda parts: parts[0] if len(parts) == 1 else jnp.concatenate(parts, axis=0)
    out, _ = _mlstm_tile(stack(xq), v_ref[...], if_ref[...], states, wq_ref, wk_ref, bias_ref, ghead_ref, tt)
    hm = stack([jnp.concatenate(heads, axis=1) for heads in out])
    x1_ref[...] = _merge(x_ref[...], hm, stack(u), o_ref[...], sb_ref[...], ga_ref[...], gb_ref[...],
                         wa_ref, wsout_ref, wo_ref)


def _mlstm(x, xm, v, cx, ifp, o, sb, ga, gb, mconv0, sconv0, c0, n0, m0, wts, *, bsz, t, tt, nb):
    nt = t // tt
    assert nb == 1 or nt == 1, "several sequences per step only when a step covers whole sequences"
    tok = lambda b, i: (b * nt + i, 0)
    seq3 = lambda b, i: (b, 0, 0)
    seq4 = lambda b, i: (b, 0, 0, 0)
    big = pl.BlockSpec((nb * tt, D_MODEL), tok)
    tail = pl.BlockSpec((nb, SUBLANES, D_MODEL), seq3)
    c_spec = pl.BlockSpec((nb, M_HEADS, M_HEAD_DIM, M_HEAD_DIM), seq4)
    n_spec = pl.BlockSpec((nb, M_HEADS, M_HEAD_DIM), seq3)
    m_spec = pl.BlockSpec((nb, SUBLANES, LANES), seq3)
    names = ("w_mconv", "w_sconv", "w_q", "w_k", "gate_bias", "g_head", "w_a", "w_sout", "w_o")
    weights = [wts[k] for k in names]
    return pl.pallas_call(
        functools.partial(_mlstm_kernel, tt=tt, nb=nb),
        grid=(bsz // nb, nt),
        in_specs=[big] * 4 + [pl.BlockSpec((nb * tt, LANES), tok)] + [big] * 4
                 + [tail, tail, c_spec, n_spec, m_spec] + [_resident(w.shape) for w in weights],
        out_specs=[big, c_spec, n_spec, m_spec, tail, tail],
        out_shape=[jax.ShapeDtypeStruct(x.shape, F32),
                   jax.ShapeDtypeStruct(c0.shape, F32), jax.ShapeDtypeStruct(n0.shape, F32),
                   jax.ShapeDtypeStruct(m0.shape, F32),
                   jax.ShapeDtypeStruct(mconv0.shape, F32), jax.ShapeDtypeStruct(sconv0.shape, F32)],
        scratch_shapes=[pltpu.VMEM((nb, tt + SUBLANES, D_MODEL), F32),
                        pltpu.VMEM((nb, tt + SUBLANES, D_MODEL), F32)],
        compiler_params=pltpu.CompilerParams(dimension_semantics=("arbitrary", "arbitrary"),
                                             vmem_limit_bytes=VMEM_LIMIT),
        name="mlstm",
    )(x, xm, v, cx, ifp, o, sb, ga, gb, mconv0, sconv0, c0, n0, m0, *weights)


def _mixer_kernel(x_ref, g_ref, w_in_ref, w_mconv_ref, w_sconv_ref, wq_ref, wk_ref,
                  bias_ref, ghead_ref, wa_ref, wsout_ref, wo_ref, wg32_ref, wu32_ref, wd32_ref,
                  x1_ref, c1_ref, n1_ref, m1_ref, mconv1_ref, sconv1_ref, wg16_ref, wu16_ref, wd16_ref,
                  xp_sc, cp_sc, hm_sc, *, tt):
    @pl.when(pl.program_id(1) == 0)
    def _():
        xp_sc[0:SUBLANES, :] = jnp.zeros((SUBLANES, D_MODEL), F32)
        cp_sc[0:SUBLANES, :] = jnp.zeros((SUBLANES, D_MODEL), F32)
        c1_ref[...] = jnp.zeros(c1_ref.shape, F32)
        n1_ref[...] = jnp.zeros(n1_ref.shape, F32)
        m1_ref[...] = jnp.zeros(m1_ref.shape, F32)

    wg16_ref[...] = wg32_ref[...].astype(BF16)
    wu16_ref[...] = wu32_ref[...].astype(BF16)
    wd16_ref[...] = wd32_ref[...].astype(BF16)

    proj = _projector(_rms(x_ref[...], g_ref[...]).astype(BF16), w_in_ref)
    xp_sc[SUBLANES:SUBLANES + tt, :] = proj("xm")
    ifp = proj("gates")
    cp_sc[SUBLANES:SUBLANES + tt, :] = proj("sc") * proj("sx")
    xq = _silu(_causal_conv(xp_sc, w_mconv_ref, M_CONV_W, tt))
    u = _causal_conv(cp_sc, w_sconv_ref, S_CONV_W, tt)
    _carry_tail(xp_sc, mconv1_ref, tt)
    _carry_tail(cp_sc, sconv1_ref, tt)
    v = proj("v")
    hs = _head_slices()
    (out,), (y_b, sig_o, sig_ga, sig_gb) = _mlstm_tile(
        xq, v, ifp, [(c1_ref, n1_ref, m1_ref)], wq_ref, wk_ref, bias_ref, ghead_ref, tt,
        between=(lambda: _dot(proj("sb") * u, wsout_ref[...]),
                 lambda: _sigmoid(proj("o")),
                 lambda: _sigmoid(proj("ga")),
                 lambda: _sigmoid(proj("gb"))))
    for hd, sl in enumerate(hs):
        hm_sc[:, sl] = (sig_o[:, sl] * out[hd]).astype(BF16)
    y_a = jnp.dot(hm_sc[...], wa_ref[...], preferred_element_type=F32)
    merged = sig_ga * y_a + sig_gb * y_b
    x1_ref[...] = x_ref[...] + _dot(merged, wo_ref[...])


def _mixer(x, wts, experts_f32, *, bsz, t, tt):
    nt = t // tt
    steps = bsz * nt
    tok = lambda b, i: (b * nt + i, 0)
    seq3 = lambda b, i: (b, 0, 0)
    seq4 = lambda b, i: (b, 0, 0, 0)
    big = pl.BlockSpec((tt, D_MODEL), tok)
    tail = pl.BlockSpec((None, SUBLANES, D_MODEL), seq3)
    c_spec = pl.BlockSpec((None, M_HEADS, M_HEAD_DIM, M_HEAD_DIM), seq4)
    n_spec = pl.BlockSpec((None, M_HEADS, M_HEAD_DIM), seq3)
    m_spec = pl.BlockSpec((None, SUBLANES, LANES), seq3)
    names = ("g_mix", "w_in", "w_mconv", "w_sconv", "w_q", "w_k", "gate_bias", "g_head",
             "w_a", "w_sout", "w_o")
    weights = [wts[k] for k in names]
    flat = [w.reshape(-1, w.shape[-1]) for w in experts_f32]
    cast_specs = [pl.BlockSpec((w.shape[0] // steps, w.shape[1]), tok) for w in flat]
    outs = pl.pallas_call(
        functools.partial(_mixer_kernel, tt=tt),
        grid=(bsz, nt),
        in_specs=[big] + [_resident(w.shape) for w in weights] + cast_specs,
        out_specs=[big, c_spec, n_spec, m_spec, tail, tail] + cast_specs,
        out_shape=[jax.ShapeDtypeStruct(x.shape, F32),
                   jax.ShapeDtypeStruct((bsz, M_HEADS, M_HEAD_DIM, M_HEAD_DIM), F32),
                   jax.ShapeDtypeStruct((bsz, M_HEADS, M_HEAD_DIM), F32),
                   jax.ShapeDtypeStruct((bsz, SUBLANES, LANES), F32),
                   jax.ShapeDtypeStruct((bsz, SUBLANES, D_MODEL), F32),
                   jax.ShapeDtypeStruct((bsz, SUBLANES, D_MODEL), F32)]
                  + [jax.ShapeDtypeStruct(w.shape, BF16) for w in flat],
        scratch_shapes=[pltpu.VMEM((tt + SUBLANES, D_MODEL), F32), pltpu.VMEM((tt + SUBLANES, D_MODEL), F32),
                        pltpu.VMEM((tt, D_MODEL), BF16)],
        compiler_params=pltpu.CompilerParams(dimension_semantics=("arbitrary", "arbitrary"),
                                             vmem_limit_bytes=VMEM_LIMIT),
        name="mixer",
    )(x, *weights, *flat)
    return outs[:6], [o.reshape(w.shape) for o, w in zip(outs[6:], experts_f32)]


MOE_WINDOW = 1024
MOE_ROW_TILE = 128
MOE_ALIGN = 16
EXPERTS_PER_STEP = EXPERTS_PER_GROUP
MOE_STEPS = N_EXPERTS // EXPERTS_PER_STEP
MOE_CHUNK = 256
MOE_SORT_CHUNK = 1280
MOE_UNSORT_CHUNK = 512
MOE_SORTED_ROWS = -(-(MOE_WINDOW + N_GROUPS * (MOE_ALIGN - 1) + MOE_ROW_TILE) // MOE_CHUNK) * MOE_CHUNK


def _split_bf16(x):
    hi = x.astype(BF16)
    lo = (x - hi.astype(F32)).astype(BF16)
    return hi, lo


CW_LANE_STRIDE = 40


def _pack_split3(cw):
    hi = cw.astype(BF16).astype(F32)
    r1 = cw - hi
    mid = r1.astype(BF16).astype(F32)
    lo = (r1 - mid).astype(BF16).astype(F32)
    return (hi + pltpu.roll(mid, CW_LANE_STRIDE, axis=1) + pltpu.roll(lo, 2 * CW_LANE_STRIDE, axis=1)).astype(BF16)


def _unpack_split3(packed):
    return (packed + pltpu.roll(packed, LANES - CW_LANE_STRIDE, axis=1)
            + pltpu.roll(packed, LANES - 2 * CW_LANE_STRIDE, axis=1))


def _route(logits):
    lane = lax.broadcasted_iota(jnp.int32, logits.shape, 1)
    neg = -jnp.inf
    big = jnp.int32(LANES)
    is_grp = lane < N_GROUPS
    g_max = jnp.max(jnp.where(is_grp, logits, neg), axis=-1, keepdims=True)
    g_sel = jnp.min(jnp.where(is_grp & (logits == g_max), lane, big), axis=-1, keepdims=True)
    p_grp = 1.0 / jnp.sum(jnp.where(is_grp, jnp.exp(logits - g_max), 0.0), axis=-1, keepdims=True)
    lo = N_GROUPS + g_sel * EXPERTS_PER_GROUP
    in_grp = (lane >= lo) & (lane < lo + EXPERTS_PER_GROUP)
    v1 = jnp.max(jnp.where(in_grp, logits, neg), axis=-1, keepdims=True)
    e1 = jnp.min(jnp.where(in_grp & (logits == v1), lane, big), axis=-1, keepdims=True)
    rest = in_grp & (lane != e1)
    v2 = jnp.max(jnp.where(rest, logits, neg), axis=-1, keepdims=True)
    e2 = jnp.min(jnp.where(rest & (logits == v2), lane, big), axis=-1, keepdims=True)
    z = jnp.exp(v2 - v1)
    w1 = p_grp / (1.0 + z)
    w2 = p_grp * z / (1.0 + z)
    return g_sel, jnp.where(lane == e1, w1, jnp.where(lane == e2, w2, 0.0))


def _moe_tail_kernel(x1_ref, p_ref, gffn_ref, wr_ref, br_ref, wg_ref, wu_ref, wd_ref,
                     gple_ref, wpg_ref, wpp_ref, gfin_ref, y_ref,
                     xs_sc, cws_sc, osort_sc, pos_sc, off_sm, nt_sm):
    s = pl.program_id(1)
    w = MOE_WINDOW

    @pl.when(s == 0)
    def _():
        hn = _rms(x1_ref[...], gffn_ref[...])
        hn_hi, hn_lo = _split_bf16(hn)
        hh_hl = jnp.dot(hn_hi, wr_ref[...], preferred_element_type=F32)
        logits = (hh_hl[:, :LANES] + hh_hl[:, LANES:]
                  + jnp.dot(hn_lo, wr_ref[:, :LANES], preferred_element_type=F32)) + br_ref[...]
        g_sel, cw = _route(logits)
        lane = lax.broadcasted_iota(jnp.int32, (w, LANES), 1)
        onehot = jnp.where(lane == g_sel, 1.0, 0.0)
        cum = _scan_rows(onehot, jnp.add, 0.0)
        cnt = cum[w - 1:w, :].astype(jnp.int32)
        cnt_pad = ((cnt + (MOE_ALIGN - 1)) // MOE_ALIGN) * MOE_ALIGN
        lane1 = lax.broadcasted_iota(jnp.int32, (1, LANES), 1)
        off = jnp.zeros((1, LANES), jnp.int32)
        for gi in range(N_GROUPS - 1):
            off = off + jnp.where(lane1 > gi, cnt_pad[:, gi:gi + 1], 0)
        n_tiles = (cnt + (MOE_ROW_TILE - 1)) // MOE_ROW_TILE
        for gi in range(N_GROUPS):
            off_sm[gi] = off[0, gi]
            nt_sm[gi] = n_tiles[0, gi]
        pos = jnp.sum(onehot * (off.astype(F32) + cum - 1.0), axis=-1, keepdims=True)
        pos_b = jnp.broadcast_to(pos, (w, LANES))
        pos_sc[...] = pos_b
        pos_row = pos_b.T[0:1, :].astype(jnp.int32)
        cw_packed = _pack_split3(cw)
        for c in range(MOE_SORTED_ROWS // MOE_SORT_CHUNK):
            rows = lax.broadcasted_iota(jnp.int32, (MOE_SORT_CHUNK, w), 0) + c * MOE_SORT_CHUNK
            sel = jnp.where(rows == pos_row, 1.0, 0.0).astype(BF16)
            sl = slice(c * MOE_SORT_CHUNK, (c + 1) * MOE_SORT_CHUNK)
            xs_sc[sl, :] = jnp.dot(sel, hn_hi, preferred_element_type=F32).astype(BF16)
            cws_sc[sl, :] = _unpack_split3(jnp.dot(sel, cw_packed, preferred_element_type=F32))
        osort_sc[...] = jnp.zeros(osort_sc.shape, BF16)

    row0 = off_sm[s]
    lane_t = lax.broadcasted_iota(jnp.int32, (MOE_ROW_TILE, LANES), 1)
    first_lane = N_GROUPS + s * EXPERTS_PER_STEP

    def tile_body(i, carry):
        r0 = pl.multiple_of(row0 + i * MOE_ROW_TILE, MOE_ALIGN)
        xt = xs_sc[pl.ds(r0, MOE_ROW_TILE), :]
        cwt = cws_sc[pl.ds(r0, MOE_ROW_TILE), :]
        parts = []
        for e in range(EXPERTS_PER_STEP):
            hg = jnp.dot(xt, wg_ref[e], preferred_element_type=F32)
            hu = jnp.dot(xt, wu_ref[e], preferred_element_type=F32)
            col = jnp.sum(jnp.where(lane_t == first_lane + e, cwt, 0.0), axis=-1, keepdims=True)
            parts.append((_silu(hg) * hu * col).astype(BF16))
        he = jnp.concatenate(parts, axis=-1)
        out = jnp.dot(he, wd_ref[...], preferred_element_type=F32)
        osort_sc[pl.ds(r0, MOE_ROW_TILE), :] = out.astype(BF16)
        return carry

    lax.fori_loop(0, nt_sm[s], tile_body, 0)

    @pl.when(s == MOE_STEPS - 1)
    def _():
        osort = osort_sc[...]
        for c in range(w // MOE_UNSORT_CHUNK):
            sl = slice(c * MOE_UNSORT_CHUNK, (c + 1) * MOE_UNSORT_CHUNK)
            pos_col = pos_sc[sl, 0:1].astype(jnp.int32)
            cols = lax.broadcasted_iota(jnp.int32, (MOE_UNSORT_CHUNK, MOE_SORTED_ROWS), 1)
            sel = jnp.where(cols == pos_col, 1.0, 0.0).astype(BF16)
            x2 = x1_ref[sl, :] + jnp.dot(sel, osort, preferred_element_type=F32)
            gate = _sigmoid(_dot(_rms(x2, gple_ref[...]), wpg_ref[...]))
            x3 = x2 + gate * _dot(p_ref[sl, :], wpp_ref[...])
            y_ref[sl, :] = _rms(x3, gfin_ref[...])


def _moe_tail(x1, p, g_ffn, w_r, b_r, wg, wu, wd, g_ple, w_pg, w_pp, g_fin):
    n = x1.shape[0]
    tm = MOE_WINDOW
    row = lambda i, s: (i, 0)
    step = lambda i, s: (s, 0, 0)
    big = pl.BlockSpec((tm, D_MODEL), row)
    sw = EXPERTS_PER_STEP * D_EXPERT
    return pl.pallas_call(
        _moe_tail_kernel,
        grid=(n // tm, MOE_STEPS),
        in_specs=[big, pl.BlockSpec((tm, PLE_DIM), row), _resident(g_ffn.shape), _resident(w_r.shape),
                  _resident(b_r.shape),
                  pl.BlockSpec((EXPERTS_PER_STEP, D_MODEL, D_EXPERT), step),
                  pl.BlockSpec((EXPERTS_PER_STEP, D_MODEL, D_EXPERT), step),
                  pl.BlockSpec((None, sw, D_MODEL), step),
                  _resident(g_ple.shape), _resident(w_pg.shape), _resident(w_pp.shape), _resident(g_fin.shape)],
        out_specs=big,
        out_shape=jax.ShapeDtypeStruct((n, D_MODEL), F32),
        scratch_shapes=[pltpu.VMEM((MOE_SORTED_ROWS, D_MODEL), BF16), pltpu.VMEM((MOE_SORTED_ROWS, LANES), F32),
                        pltpu.VMEM((MOE_SORTED_ROWS, D_MODEL), BF16), pltpu.VMEM((tm, LANES), F32),
                        pltpu.SMEM((N_GROUPS,), jnp.int32), pltpu.SMEM((N_GROUPS,), jnp.int32)],
        compiler_params=pltpu.CompilerParams(dimension_semantics=("arbitrary", "arbitrary"),
                                             vmem_limit_bytes=VMEM_LIMIT),
        name="moe_tail",
    )(x1, p, g_ffn, w_r, b_r, wg, wu, wd, g_ple, w_pg, w_pp, g_fin)


PROMPT_TIME_TILE = 256
SAMPLE_ROW_TILE = 512
SAMPLE_SEQS_PER_STEP = 8


def _pad_tail(buf):
    return jnp.pad(buf, ((0, 0), (SUBLANES - buf.shape[1], 0), (0, 0)))


def _moe(x1, p, wts):
    return _moe_tail(x1, p, wts["g_ffn"], wts["w_r"], wts["b_r"], wts["w_gate"], wts["w_up"], wts["w_down"],
                     wts["g_ple"], wts["w_ple_gate"], wts["w_ple_proj"], wts["g_final"])


def _outputs(y, shape, c1, n1, m1p, mconv1, sconv1):
    return (y.reshape(shape), c1[None], n1[None], m1p[None, :, :M_HEADS, 0],
            mconv1[None, :, SUBLANES - (M_CONV_W - 1):, :], sconv1[None, :, SUBLANES - (S_CONV_W - 1):, :])


def _prompt_group(x, p, wts, experts_f32, *, tt):
    bsz, t, _ = x.shape
    (x1, *states), (wg, wu, wd) = _mixer(x.reshape(bsz * t, D_MODEL), wts, experts_f32, bsz=bsz, t=t, tt=tt)
    wts.update(w_gate=wg, w_up=wu, w_down=wd.reshape(MOE_STEPS, EXPERTS_PER_STEP * D_EXPERT, D_MODEL))
    return _outputs(_moe(x1, p.reshape(bsz * t, PLE_DIM), wts), x.shape, *states)


def _sample_group(x, p, mconv0, sconv0, c0, n0, m0, wts, *, tm):
    bsz, t, _ = x.shape
    xf = x.reshape(bsz * t, D_MODEL)
    m0p = jnp.broadcast_to(jnp.pad(m0, ((0, 0), (0, SUBLANES - M_HEADS)))[:, :, None], (bsz, SUBLANES, LANES))
    xm, v, o, ifp, sb, cx, ga, gb = _in_proj(xf, wts["g_mix"], wts["w_in"], tm)
    x1, *states = _mlstm(xf, xm, v, cx, ifp, o, sb, ga, gb, _pad_tail(mconv0), _pad_tail(sconv0), c0, n0, m0p, wts,
                         bsz=bsz, t=t, tt=t, nb=SAMPLE_SEQS_PER_STEP)
    return _outputs(_moe(x1, p.reshape(bsz * t, PLE_DIM), wts), x.shape, *states)


def kernel(x_prompt, x_sample, p_prompt, p_sample, state_mlstm_C, state_mlstm_n, state_mlstm_m, state_mlstm_conv, state_sconv, g_mix, w_in, w_mconv, w_q, w_k, b_i, b_f, g_head, w_a, w_sconv, w_sout, w_o, g_ffn, w_rg, b_rg, w_re, b_re, w_gate, w_up, w_down, g_ple, w_ple_gate, w_ple_proj, g_final):
    assert g_mix.shape[0] == 1, "single-layer trunk"
    d = D_MODEL
    n_if = 2 * M_HEADS
    w_in0 = lax.optimization_barrier(w_in[0].astype(BF16))
    w_router = jnp.pad(jnp.concatenate([w_rg[0], w_re[0]], axis=1), ((0, 0), (0, LANES - N_GROUPS - N_EXPERTS)))
    wr_hi = w_router.astype(BF16)
    wr_lo = (w_router - wr_hi.astype(F32)).astype(BF16)
    w_r = jnp.concatenate([wr_hi, wr_lo], axis=1)
    wts = {
        "g_mix": g_mix,
        "w_in": jnp.concatenate([w_in0[:, :_GATE_COL],
                                 jnp.pad(w_in0[:, _GATE_COL:_GATE_COL + n_if], ((0, 0), (0, LANES - n_if))),
                                 w_in0[:, _GATE_COL + n_if:]], axis=1),
        "w_mconv": w_mconv[0], "w_sconv": w_sconv[0],
        "w_q": w_q[0].astype(BF16), "w_k": w_k[0].astype(BF16),
        "gate_bias": jnp.pad(jnp.concatenate([b_i[0], b_f[0]])[None, :], ((0, 0), (0, LANES - n_if))),
        "g_head": g_head[0].reshape(1, d),
        "w_a": w_a[0].astype(BF16), "w_sout": w_sout[0].astype(BF16), "w_o": w_o[0].astype(BF16),
        "g_ffn": g_ffn, "w_r": w_r,
        "b_r": jnp.pad(jnp.concatenate([b_rg[0], b_re[0]])[None, :], ((0, 0), (0, LANES - N_GROUPS - N_EXPERTS))),
        "g_ple": g_ple, "w_ple_gate": w_ple_gate[0].astype(BF16), "w_ple_proj": w_ple_proj[0].astype(BF16),
        "g_final": g_final[None, :],
    }
    yp, *st_p = _prompt_group(x_prompt, p_prompt[0], wts, (w_gate[0], w_up[0], w_down[0]), tt=PROMPT_TIME_TILE)
    ys, *st_s = _sample_group(x_sample, p_sample[0], state_mlstm_conv[0], state_sconv[0],
                              state_mlstm_C[0], state_mlstm_n[0], state_mlstm_m[0], wts, tm=SAMPLE_ROW_TILE)
    return (yp, ys, *st_p, *st_s)
```

```python
import functools

import jax
import jax.numpy as jnp
from jax import lax
from jax.experimental import pallas as pl
from jax.experimental.pallas import tpu as pltpu

D_MODEL = 1024
M_HEADS = 4
M_HEAD_DIM = 256
M_CONV_W = 4
S_CONV_W = 3
N_GROUPS = 4
EXPERTS_PER_GROUP = 8
N_EXPERTS = N_GROUPS * EXPERTS_PER_GROUP
D_EXPERT = 256
PLE_DIM = 256
RMS_EPS = 1e-6

LANES = 128
SUBLANES = 8
VMEM_LIMIT = 60 * 1024 * 1024

BF16 = jnp.bfloat16
F32 = jnp.float32


def _rms(x, g):
    return x * lax.rsqrt(jnp.mean(x * x, axis=-1, keepdims=True) + RMS_EPS) * g


def _dot(a, b):
    return jnp.dot(a.astype(BF16), b.astype(BF16), preferred_element_type=F32)


def _sigmoid(x):
    return 0.5 * jnp.tanh(0.5 * x) + 0.5


def _silu(x):
    return x * _sigmoid(x)


def _resident(shape):
    nd = len(shape)
    return pl.BlockSpec(shape, lambda *_: (0,) * nd, pipeline_mode=pl.Buffered(1))


_GATE_COL = 3 * D_MODEL
_REST_COL = _GATE_COL + LANES
IN_COL = {"xm": 0, "v": D_MODEL, "o": 2 * D_MODEL, "sb": _REST_COL, "sc": _REST_COL + D_MODEL,
          "sx": _REST_COL + 2 * D_MODEL, "ga": _REST_COL + 3 * D_MODEL, "gb": _REST_COL + 4 * D_MODEL}


def _projector(h, w_in_ref):
    def proj(name):
        lo, width = (_GATE_COL, LANES) if name == "gates" else (IN_COL[name], D_MODEL)
        return jnp.dot(h, w_in_ref[:, lo:lo + width], preferred_element_type=F32)
    return proj


def _in_proj_kernel(x_ref, g_ref, w_in_ref,
                    xm_ref, v_ref, o_ref, if_ref, sb_ref, cx_ref, ga_ref, gb_ref):
    proj = _projector(_rms(x_ref[...], g_ref[...]).astype(BF16), w_in_ref)
    xm_ref[...] = proj("xm")
    v_ref[...] = proj("v")
    o_ref[...] = proj("o")
    if_ref[...] = proj("gates")
    sb_ref[...] = proj("sb")
    cx_ref[...] = proj("sc") * proj("sx")
    ga_ref[...] = proj("ga")
    gb_ref[...] = proj("gb")


def _in_proj(x, g_mix, w_in, tm):
    n = x.shape[0]
    row = lambda i: (i, 0)
    big = pl.BlockSpec((tm, D_MODEL), row)
    outs = [jax.ShapeDtypeStruct((n, D_MODEL), F32)] * 3 + [jax.ShapeDtypeStruct((n, LANES), F32)] \
        + [jax.ShapeDtypeStruct((n, D_MODEL), F32)] * 4
    return pl.pallas_call(
        _in_proj_kernel,
        grid=(n // tm,),
        in_specs=[big, _resident(g_mix.shape), _resident(w_in.shape)],
        out_specs=[big, big, big, pl.BlockSpec((tm, LANES), row), big, big, big, big],
        out_shape=outs,
        compiler_params=pltpu.CompilerParams(dimension_semantics=("arbitrary",),
                                             vmem_limit_bytes=VMEM_LIMIT),
        name="in_proj",
    )(x, g_mix, w_in)


def _scan_rows(x, op, identity):
    sub = lax.broadcasted_iota(jnp.int32, (SUBLANES, x.shape[1]), 0)
    blocks, carry = [], None
    for i in range(x.shape[0] // SUBLANES):
        blk = x[i * SUBLANES:(i + 1) * SUBLANES, :]
        for shift in (1, 2, 4):
            blk = op(blk, jnp.where(sub >= shift, pltpu.roll(blk, shift, axis=0), identity))
        if carry is not None:
            blk = op(blk, carry)
        carry = jnp.broadcast_to(blk[SUBLANES - 1:SUBLANES, :], blk.shape)
        blocks.append(blk)
    return jnp.concatenate(blocks, axis=0)


def _log_sigmoid(x):
    return jnp.minimum(x, 0.0) - jnp.log1p(jnp.exp(-jnp.abs(x)))


def _seq_init(mconv0_ref, sconv0_ref, c0_ref, n0_ref, m0_ref, xp_sc, cp_sc, c1_ref, n1_ref, m1_ref):
    @pl.when(pl.program_id(1) == 0)
    def _():
        xp_sc[0:SUBLANES, :] = mconv0_ref[...]
        cp_sc[0:SUBLANES, :] = sconv0_ref[...]
        c1_ref[...] = c0_ref[...]
        n1_ref[...] = n0_ref[...]
        m1_ref[...] = m0_ref[...]


def _causal_conv(src_sc, w_ref, width, tt):
    acc = None
    for j in range(width):
        term = src_sc[pl.ds(SUBLANES - (width - 1) + j, tt), :] * w_ref[j:j + 1, :]
        acc = term if acc is None else acc + term
    return acc


def _carry_tail(src_sc, tail_ref, tt):
    tail = src_sc[tt:tt + SUBLANES, :]
    src_sc[0:SUBLANES, :] = tail
    tail_ref[...] = tail


def _head_slices():
    return [slice(h * M_HEAD_DIM, (h + 1) * M_HEAD_DIM) for h in range(M_HEADS)]


def _mlstm_tile(xq, v, ifp, states, wq_ref, wk_ref, bias_ref, ghead_ref, tt, between=()):
    nh = M_HEADS
    hs = _head_slices()
    seq_rows = [slice(j * tt, (j + 1) * tt) for j in range(len(states))]
    g = [ifp[rows, :] + bias_ref[...] for rows in seq_rows]
    b_c = [_scan_rows(_log_sigmoid(gj), jnp.add, 0.0) for gj in g]
    a_c = [gj - pltpu.roll(bj, LANES - nh, axis=1) for gj, bj in zip(g, b_c)]
    amax_c = [_scan_rows(aj, jnp.maximum, -jnp.inf) for aj in a_c]
    a_t = [aj.T for aj in a_c]

    r_idx = lax.broadcasted_iota(jnp.int32, (tt, tt), 0)
    c_idx = lax.broadcasted_iota(jnp.int32, (tt, tt), 1)
    causal = c_idx <= r_idx

    q_all = [_dot(xq[:, hs[h]], wq_ref[h]) * (M_HEAD_DIM ** -0.5) for h in range(nh)]
    k_all = [_dot(xq[:, hs[h]], wk_ref[h]) for h in range(nh)]
    units = [(j, h) for j in range(len(states)) for h in range(nh)]
    ids = range(len(units))
    q = [q_all[h][seq_rows[j], :] for j, h in units]
    k = [k_all[h][seq_rows[j], :] for j, h in units]
    vv = [v[seq_rows[j], hs[h]] for j, h in units]
    c_prev = [states[j][0][h] for j, h in units]
    n_prev = [states[j][1][h:h + 1, :] for j, h in units]
    m_prev = [states[j][2][h:h + 1, 0:1] for j, h in units]
    b_col = [b_c[j][:, nh + h:nh + h + 1] for j, h in units]
    a_col = [a_c[j][:, h:h + 1] for j, h in units]
    a_row = [a_t[j][h:h + 1, :] for j, h in units]
    b_last = [b_c[j][tt - 1:tt, nh + h:nh + h + 1] for j, h in units]
    mm_col = [jnp.maximum(m_prev[i], amax_c[j][:, h:h + 1]) for i, (j, h) in enumerate(units)]
    mm_last = [jnp.maximum(m_prev[i], amax_c[j][tt - 1:tt, h:h + 1]) for i, (j, h) in enumerate(units)]

    extra = []
    pending = list(between)

    def run_one():
        if pending:
            extra.append(pending.pop(0)())

    w_inter = [jnp.exp(m_prev[i] - mm_col[i]) for i in ids]
    qk = [lax.dot_general(q[i].astype(BF16), k[i].astype(BF16), (((1,), (1,)), ((), ())),
                          preferred_element_type=F32) for i in ids]
    run_one()
    s = [jnp.exp(jnp.where(causal, a_row[i] - mm_col[i], -jnp.inf)) * qk[i] for i in ids]
    run_one()
    decay = [jnp.exp(m_prev[i] - mm_last[i]) for i in ids]
    wk = [jnp.exp(a_col[i] - mm_last[i]) * k[i] for i in ids]
    for i, (j, h) in enumerate(units):
        c_ref, n_ref, m_ref = states[j]
        c_ref[h] = decay[i] * c_prev[i] + lax.dot_general(
            wk[i].astype(BF16), vv[i].astype(BF16), (((0,), (0,)), ((), ())), preferred_element_type=F32)
        n_ref[h:h + 1, :] = decay[i] * n_prev[i] + jnp.sum(wk[i], axis=0, keepdims=True)
        m_ref[h:h + 1, :] = jnp.broadcast_to(b_last[i] + mm_last[i], (1, LANES))
    num = [w_inter[i] * _dot(q[i], c_prev[i]) + _dot(s[i], vv[i]) for i in ids]
    den = [w_inter[i] * jnp.sum(q[i] * n_prev[i], axis=-1, keepdims=True)
           + jnp.sum(s[i], axis=-1, keepdims=True) for i in ids]
    run_one()
    out = [num[i] * (1.0 / jnp.maximum(jnp.abs(den[i]), jnp.exp(-(b_col[i] + mm_col[i])))) for i in ids]
    out = [out[i] * lax.rsqrt(jnp.mean(out[i] * out[i], axis=-1, keepdims=True) + RMS_EPS)
           * ghead_ref[:, hs[units[i][1]]] for i in ids]
    while pending:
        run_one()
    return [out[j * nh:(j + 1) * nh] for j in range(len(states))], extra


def _merge(x, hm, u, o, sb, ga, gb, wa_ref, wsout_ref, wo_ref):
    y_a = _dot(_sigmoid(o) * hm, wa_ref[...])
    y_b = _dot(sb * u, wsout_ref[...])
    return x + _dot(_sigmoid(ga) * y_a + _sigmoid(gb) * y_b, wo_ref[...])


def _mlstm_kernel(x_ref, xm_ref, v_ref, cx_ref, if_ref, o_ref, sb_ref, ga_ref, gb_ref,
                  mconv0_ref, sconv0_ref, c0_ref, n0_ref, m0_ref,
                  w_mconv_ref, w_sconv_ref, wq_ref, wk_ref, bias_ref, ghead_ref, wa_ref, wsout_ref, wo_ref,
                  x1_ref, c1_ref, n1_ref, m1_ref, mconv1_ref, sconv1_ref,
                  xp_sc, cp_sc, *, tt, nb):
    xq, u, states = [], [], []
    for j in range(nb):
        rows = slice(j * tt, (j + 1) * tt)
        xp, cp, state = xp_sc.at[j], cp_sc.at[j], (c1_ref.at[j], n1_ref.at[j], m1_ref.at[j])
        _seq_init(mconv0_ref.at[j], sconv0_ref.at[j], c0_ref.at[j], n0_ref.at[j], m0_ref.at[j], xp, cp, *state)
        xp[SUBLANES:SUBLANES + tt, :] = xm_ref[rows, :]
        cp[SUBLANES:SUBLANES + tt, :] = cx_ref[rows, :]
        xq.append(_silu(_causal_conv(xp, w_mconv_ref, M_CONV_W, tt)))
        u.append(_causal_conv(cp, w_sconv_ref, S_CONV_W, tt))
        _carry_tail(xp, mconv1_ref.at[j], tt)
        _carry_tail(cp, sconv1_ref.at[j], tt)
        states.append(state)
    stack = lambda parts: parts[0] if len(parts) == 1 else jnp.concatenate(parts, axis=0)
    out, _ = _mlstm_tile(stack(xq), v_ref[...], if_ref[...], states, wq_ref, wk_ref, bias_ref, ghead_ref, tt)
    hm = stack([jnp.concatenate(heads, axis=1) for heads in out])
    x1_ref[...] = _merge(x_ref[...], hm, stack(u), o_ref[...], sb_ref[...], ga_ref[...], gb_ref[...],
                         wa_ref, wsout_ref, wo_ref)


def _mlstm(x, xm, v, cx, ifp, o, sb, ga, gb, mconv0, sconv0, c0, n0, m0, wts, *, bsz, t, tt, nb):
    nt = t // tt
    assert nb == 1 or nt == 1, "several sequences per step only when a step covers whole sequences"
    tok = lambda b, i: (b * nt + i, 0)
    seq3 = lambda b, i: (b, 0, 0)
    seq4 = lambda b, i: (b, 0, 0, 0)
    big = pl.BlockSpec((nb * tt, D_MODEL), tok)
    tail = pl.BlockSpec((nb, SUBLANES, D_MODEL), seq3)
    c_spec = pl.BlockSpec((nb, M_HEADS, M_HEAD_DIM, M_HEAD_DIM), seq4)
    n_spec = pl.BlockSpec((nb, M_HEADS, M_HEAD_DIM), seq3)
    m_spec = pl.BlockSpec((nb, SUBLANES, LANES), seq3)
    names = ("w_mconv", "w_sconv", "w_q", "w_k", "gate_bias", "g_head", "w_a", "w_sout", "w_o")
    weights = [wts[k] for k in names]
    return pl.pallas_call(
        functools.partial(_mlstm_kernel, tt=tt, nb=nb),
        grid=(bsz // nb, nt),
        in_specs=[big] * 4 + [pl.BlockSpec((nb * tt, LANES), tok)] + [big] * 4
                 + [tail, tail, c_spec, n_spec, m_spec] + [_resident(w.shape) for w in weights],
        out_specs=[big, c_spec, n_spec, m_spec, tail, tail],
        out_shape=[jax.ShapeDtypeStruct(x.shape, F32),
                   jax.ShapeDtypeStruct(c0.shape, F32), jax.ShapeDtypeStruct(n0.shape, F32),
                   jax.ShapeDtypeStruct(m0.shape, F32),
                   jax.ShapeDtypeStruct(mconv0.shape, F32), jax.ShapeDtypeStruct(sconv0.shape, F32)],
        scratch_shapes=[pltpu.VMEM((nb, tt + SUBLANES, D_MODEL), F32),
                        pltpu.VMEM((nb, tt + SUBLANES, D_MODEL), F32)],
        compiler_params=pltpu.CompilerParams(dimension_semantics=("arbitrary", "arbitrary"),
                                             vmem_limit_bytes=VMEM_LIMIT),
        name="mlstm",
    )(x, xm, v, cx, ifp, o, sb, ga, gb, mconv0, sconv0, c0, n0, m0, *weights)


def _mixer_kernel(x_ref, g_ref, w_in_ref, w_mconv_ref, w_sconv_ref, wq_ref, wk_ref,
                  bias_ref, ghead_ref, wa_ref, wsout_ref, wo_ref, wg32_ref, wu32_ref, wd32_ref,
                  x1_ref, c1_ref, n1_ref, m1_ref, mconv1_ref, sconv1_ref, wg16_ref, wu16_ref, wd16_ref,
                  xp_sc, cp_sc, hm_sc, *, tt):
    @pl.when(pl.program_id(1) == 0)
    def _():
        xp_sc[0:SUBLANES, :] = jnp.zeros((SUBLANES, D_MODEL), F32)
        cp_sc[0:SUBLANES, :] = jnp.zeros((SUBLANES, D_MODEL), F32)
        c1_ref[...] = jnp.zeros(c1_ref.shape, F32)
        n1_ref[...] = jnp.zeros(n1_ref.shape, F32)
        m1_ref[...] = jnp.zeros(m1_ref.shape, F32)

    wg16_ref[...] = wg32_ref[...].astype(BF16)
    wu16_ref[...] = wu32_ref[...].astype(BF16)
    wd16_ref[...] = wd32_ref[...].astype(BF16)

    proj = _projector(_rms(x_ref[...], g_ref[...]).astype(BF16), w_in_ref)
    xp_sc[SUBLANES:SUBLANES + tt, :] = proj("xm")
    ifp = proj("gates")
    cp_sc[SUBLANES:SUBLANES + tt, :] = proj("sc") * proj("sx")
    xq = _silu(_causal_conv(xp_sc, w_mconv_ref, M_CONV_W, tt))
    u = _causal_conv(cp_sc, w_sconv_ref, S_CONV_W, tt)
    _carry_tail(xp_sc, mconv1_ref, tt)
    _carry_tail(cp_sc, sconv1_ref, tt)
    v = proj("v")
    hs = _head_slices()
    (out,), (y_b, sig_o, sig_ga, sig_gb) = _mlstm_tile(
        xq, v, ifp, [(c1_ref, n1_ref, m1_ref)], wq_ref, wk_ref, bias_ref, ghead_ref, tt,
        between=(lambda: _dot(proj("sb") * u, wsout_ref[...]),
                 lambda: _sigmoid(proj("o")),
                 lambda: _sigmoid(proj("ga")),
                 lambda: _sigmoid(proj("gb"))))
    for hd, sl in enumerate(hs):
        hm_sc[:, sl] = (sig_o[:, sl] * out[hd]).astype(BF16)
    y_a = jnp.dot(hm_sc[...], wa_ref[...], preferred_element_type=F32)
    merged = sig_ga * y_a + sig_gb * y_b
    x1_ref[...] = x_ref[...] + _dot(merged, wo_ref[...])


def _mixer(x, wts, experts_f32, *, bsz, t, tt):
    nt = t // tt
    steps = bsz * nt
    tok = lambda b, i: (b * nt + i, 0)
    seq3 = lambda b, i: (b, 0, 0)
    seq4 = lambda b, i: (b, 0, 0, 0)
    big = pl.BlockSpec((tt, D_MODEL), tok)
    tail = pl.BlockSpec((None, SUBLANES, D_MODEL), seq3)
    c_spec = pl.BlockSpec((None, M_HEADS, M_HEAD_DIM, M_HEAD_DIM), seq4)
    n_spec = pl.BlockSpec((None, M_HEADS, M_HEAD_DIM), seq3)
    m_spec = pl.BlockSpec((None, SUBLANES, LANES), seq3)
    names = ("g_mix", "w_in", "w_mconv", "w_sconv", "w_q", "w_k", "gate_bias", "g_head",
             "w_a", "w_sout", "w_o")
    weights = [wts[k] for k in names]
    flat = [w.reshape(-1, w.shape[-1]) for w in experts_f32]
    cast_specs = [pl.BlockSpec((w.shape[0] // steps, w.shape[1]), tok) for w in flat]
    outs = pl.pallas_call(
        functools.partial(_mixer_kernel, tt=tt),
        grid=(bsz, nt),
        in_specs=[big] + [_resident(w.shape) for w in weights] + cast_specs,
        out_specs=[big, c_spec, n_spec, m_spec, tail, tail] + cast_specs,
        out_shape=[jax.ShapeDtypeStruct(x.shape, F32),
                   jax.ShapeDtypeStruct((bsz, M_HEADS, M_HEAD_DIM, M_HEAD_DIM), F32),
                   jax.ShapeDtypeStruct((bsz, M_HEADS, M_HEAD_DIM), F32),
                   jax.ShapeDtypeStruct((bsz, SUBLANES, LANES), F32),
                   jax.ShapeDtypeStruct((bsz, SUBLANES, D_MODEL), F32),
                   jax.ShapeDtypeStruct((bsz, SUBLANES, D_MODEL), F32)]
                  + [jax.ShapeDtypeStruct(w.shape, BF16) for w in flat],
        scratch_shapes=[pltpu.VMEM((tt + SUBLANES, D_MODEL), F32), pltpu.VMEM((tt + SUBLANES, D_MODEL), F32),
                        pltpu.VMEM((tt, D_MODEL), BF16)],
        compiler_params=pltpu.CompilerParams(dimension_semantics=("arbitrary", "arbitrary"),
                                             vmem_limit_bytes=VMEM_LIMIT),
        name="mixer",
    )(x, *weights, *flat)
    return outs[:6], [o.reshape(w.shape) for o, w in zip(outs[6:], experts_f32)]


MOE_WINDOW = 1024
MOE_ROW_TILE = 128
MOE_ALIGN = 16
EXPERTS_PER_STEP = EXPERTS_PER_GROUP
MOE_STEPS = N_EXPERTS // EXPERTS_PER_STEP
MOE_CHUNK = 256
MOE_SORT_CHUNK = 1280
MOE_UNSORT_CHUNK = 512
MOE_SORTED_ROWS = -(-(MOE_WINDOW + N_GROUPS * (MOE_ALIGN - 1) + MOE_ROW_TILE) // MOE_CHUNK) * MOE_CHUNK


def _split_bf16(x):
    hi = x.astype(BF16)
    lo = (x - hi.astype(F32)).astype(BF16)
    return hi, lo


CW_LANE_STRIDE = 40


def _pack_split3(cw):
    hi = cw.astype(BF16).astype(F32)
    r1 = cw - hi
    mid = r1.astype(BF16).astype(F32)
    lo = (r1 - mid).astype(BF16).astype(F32)
    return (hi + pltpu.roll(mid, CW_LANE_STRIDE, axis=1) + pltpu.roll(lo, 2 * CW_LANE_STRIDE, axis=1)).astype(BF16)


def _unpack_split3(packed):
    return (packed + pltpu.roll(packed, LANES - CW_LANE_STRIDE, axis=1)
            + pltpu.roll(packed, LANES - 2 * CW_LANE_STRIDE, axis=1))


def _route(logits):
    lane = lax.broadcasted_iota(jnp.int32, logits.shape, 1)
    neg = -jnp.inf
    big = jnp.int32(LANES)
    is_grp = lane < N_GROUPS
    g_max = jnp.max(jnp.where(is_grp, logits, neg), axis=-1, keepdims=True)
    g_sel = jnp.min(jnp.where(is_grp & (logits == g_max), lane, big), axis=-1, keepdims=True)
    p_grp = 1.0 / jnp.sum(jnp.where(is_grp, jnp.exp(logits - g_max), 0.0), axis=-1, keepdims=True)
    lo = N_GROUPS + g_sel * EXPERTS_PER_GROUP
    in_grp = (lane >= lo) & (lane < lo + EXPERTS_PER_GROUP)
    v1 = jnp.max(jnp.where(in_grp, logits, neg), axis=-1, keepdims=True)
    e1 = jnp.min(jnp.where(in_grp & (logits == v1), lane, big), axis=-1, keepdims=True)
    rest = in_grp & (lane != e1)
    v2 = jnp.max(jnp.where(rest, logits, neg), axis=-1, keepdims=True)
    e2 = jnp.min(jnp.where(rest & (logits == v2), lane, big), axis=-1, keepdims=True)
    z = jnp.exp(v2 - v1)
    w1 = p_grp / (1.0 + z)
    w2 = p_grp * z / (1.0 + z)
    return g_sel, jnp.where(lane == e1, w1, jnp.where(lane == e2, w2, 0.0))


def _moe_tail_kernel(x1_ref, p_ref, gffn_ref, wr_ref, br_ref, wg_ref, wu_ref, wd_ref,
                     gple_ref, wpg_ref, wpp_ref, gfin_ref, y_ref,
                     xs_sc, cws_sc, osort_sc, pos_sc, off_sm, nt_sm):
    s = pl.program_id(1)
    w = MOE_WINDOW

    @pl.when(s == 0)
    def _():
        hn = _rms(x1_ref[...], gffn_ref[...])
        hn_hi, hn_lo = _split_bf16(hn)
        hh_hl = jnp.dot(hn_hi, wr_ref[...], preferred_element_type=F32)
        logits = (hh_hl[:, :LANES] + hh_hl[:, LANES:]
                  + jnp.dot(hn_lo, wr_ref[:, :LANES], preferred_element_type=F32)) + br_ref[...]
        g_sel, cw = _route(logits)
        lane = lax.broadcasted_iota(jnp.int32, (w, LANES), 1)
        onehot = jnp.where(lane == g_sel, 1.0, 0.0)
        cum = _scan_rows(onehot, jnp.add, 0.0)
        cnt = cum[w - 1:w, :].astype(jnp.int32)
        cnt_pad = ((cnt + (MOE_ALIGN - 1)) // MOE_ALIGN) * MOE_ALIGN
        lane1 = lax.broadcasted_iota(jnp.int32, (1, LANES), 1)
        off = jnp.zeros((1, LANES), jnp.int32)
        for gi in range(N_GROUPS - 1):
            off = off + jnp.where(lane1 > gi, cnt_pad[:, gi:gi + 1], 0)
        n_tiles = (cnt + (MOE_ROW_TILE - 1)) // MOE_ROW_TILE
        for gi in range(N_GROUPS):
            off_sm[gi] = off[0, gi]
            nt_sm[gi] = n_tiles[0, gi]
        pos = jnp.sum(onehot * (off.astype(F32) + cum - 1.0), axis=-1, keepdims=True)
        pos_b = jnp.broadcast_to(pos, (w, LANES))
        pos_sc[...] = pos_b
        pos_row = pos_b.T[0:1, :].astype(jnp.int32)
        cw_packed = _pack_split3(cw)
        for c in range(MOE_SORTED_ROWS // MOE_SORT_CHUNK):
            rows = lax.broadcasted_iota(jnp.int32, (MOE_SORT_CHUNK, w), 0) + c * MOE_SORT_CHUNK
            sel = jnp.where(rows == pos_row, 1.0, 0.0).astype(BF16)
            sl = slice(c * MOE_SORT_CHUNK, (c + 1) * MOE_SORT_CHUNK)
            xs_sc[sl, :] = jnp.dot(sel, hn_hi, preferred_element_type=F32).astype(BF16)
            cws_sc[sl, :] = _unpack_split3(jnp.dot(sel, cw_packed, preferred_element_type=F32))
        osort_sc[...] = jnp.zeros(osort_sc.shape, BF16)

    row0 = off_sm[s]
    lane_t = lax.broadcasted_iota(jnp.int32, (MOE_ROW_TILE, LANES), 1)
    first_lane = N_GROUPS + s * EXPERTS_PER_STEP

    def tile_body(i, carry):
        r0 = pl.multiple_of(row0 + i * MOE_ROW_TILE, MOE_ALIGN)
        xt = xs_sc[pl.ds(r0, MOE_ROW_TILE), :]
        cwt = cws_sc[pl.ds(r0, MOE_ROW_TILE), :]
        parts = []
        for e in range(EXPERTS_PER_STEP):
            hg = jnp.dot(xt, wg_ref[e], preferred_element_type=F32)
            hu = jnp.dot(xt, wu_ref[e], preferred_element_type=F32)
            col = jnp.sum(jnp.where(lane_t == first_lane + e, cwt, 0.0), axis=-1, keepdims=True)
            parts.append((_silu(hg) * hu * col).astype(BF16))
        he = jnp.concatenate(parts, axis=-1)
        out = jnp.dot(he, wd_ref[...], preferred_element_type=F32)
        osort_sc[pl.ds(r0, MOE_ROW_TILE), :] = out.astype(BF16)
        return carry

    lax.fori_loop(0, nt_sm[s], tile_body, 0)

    @pl.when(s == MOE_STEPS - 1)
    def _():
        osort = osort_sc[...]
        for c in range(w // MOE_UNSORT_CHUNK):
            sl = slice(c * MOE_UNSORT_CHUNK, (c + 1) * MOE_UNSORT_CHUNK)
            pos_col = pos_sc[sl, 0:1].astype(jnp.int32)
            cols = lax.broadcasted_iota(jnp.int32, (MOE_UNSORT_CHUNK, MOE_SORTED_ROWS), 1)
            sel = jnp.where(cols == pos_col, 1.0, 0.0).astype(BF16)
            x2 = x1_ref[sl, :] + jnp.dot(sel, osort, preferred_element_type=F32)
            gate = _sigmoid(_dot(_rms(x2, gple_ref[...]), wpg_ref[...]))
            x3 = x2 + gate * _dot(p_ref[sl, :], wpp_ref[...])
            y_ref[sl, :] = _rms(x3, gfin_ref[...])


def _moe_tail(x1, p, g_ffn, w_r, b_r, wg, wu, wd, g_ple, w_pg, w_pp, g_fin):
    n = x1.shape[0]
    tm = MOE_WINDOW
    row = lambda i, s: (i, 0)
    step = lambda i, s: (s, 0, 0)
    big = pl.BlockSpec((tm, D_MODEL), row)
    sw = EXPERTS_PER_STEP * D_EXPERT
    return pl.pallas_call(
        _moe_tail_kernel,
        grid=(n // tm, MOE_STEPS),
        in_specs=[big, pl.BlockSpec((tm, PLE_DIM), row), _resident(g_ffn.shape), _resident(w_r.shape),
                  _resident(b_r.shape),
                  pl.BlockSpec((EXPERTS_PER_STEP, D_MODEL, D_EXPERT), step),
                  pl.BlockSpec((EXPERTS_PER_STEP, D_MODEL, D_EXPERT), step),
                  pl.BlockSpec((None, sw, D_MODEL), step),
                  _resident(g_ple.shape), _resident(w_pg.shape), _resident(w_pp.shape), _resident(g_fin.shape)],
        out_specs=big,
        out_shape=jax.ShapeDtypeStruct((n, D_MODEL), F32),
        scratch_shapes=[pltpu.VMEM((MOE_SORTED_ROWS, D_MODEL), BF16), pltpu.VMEM((MOE_SORTED_ROWS, LANES), F32),
                        pltpu.VMEM((MOE_SORTED_ROWS, D_MODEL), BF16), pltpu.VMEM((tm, LANES), F32),
                        pltpu.SMEM((N_GROUPS,), jnp.int32), pltpu.SMEM((N_GROUPS,), jnp.int32)],
        compiler_params=pltpu.CompilerParams(dimension_semantics=("arbitrary", "arbitrary"),
                                             vmem_limit_bytes=VMEM_LIMIT),
        name="moe_tail",
    )(x1, p, g_ffn, w_r, b_r, wg, wu, wd, g_ple, w_pg, w_pp, g_fin)


PROMPT_TIME_TILE = 256
SAMPLE_ROW_TILE = 512
SAMPLE_SEQS_PER_STEP = 8


def _pad_tail(buf):
    return jnp.pad(buf, ((0, 0), (SUBLANES - buf.shape[1], 0), (0, 0)))


def _moe(x1, p, wts):
    return _moe_tail(x1, p, wts["g_ffn"], wts["w_r"], wts["b_r"], wts["w_gate"], wts["w_up"], wts["w_down"],
                     wts["g_ple"], wts["w_ple_gate"], wts["w_ple_proj"], wts["g_final"])


def _outputs(y, shape, c1, n1, m1p, mconv1, sconv1):
    return (y.reshape(shape), c1[None], n1[None], m1p[None, :, :M_HEADS, 0],
            mconv1[None, :, SUBLANES - (M_CONV_W - 1):, :], sconv1[None, :, SUBLANES - (S_CONV_W - 1):, :])


def _prompt_group(x, p, wts, experts_f32, *, tt):
    bsz, t, _ = x.shape
    (x1, *states), (wg, wu, wd) = _mixer(x.reshape(bsz * t, D_MODEL), wts, experts_f32, bsz=bsz, t=t, tt=tt)
    wts.update(w_gate=wg, w_up=wu, w_down=wd.reshape(MOE_STEPS, EXPERTS_PER_STEP * D_EXPERT, D_MODEL))
    return _outputs(_moe(x1, p.reshape(bsz * t, PLE_DIM), wts), x.shape, *states)


def _sample_group(x, p, mconv0, sconv0, c0, n0, m0, wts, *, tm):
    bsz, t, _ = x.shape
    xf = x.reshape(bsz * t, D_MODEL)
    m0p = jnp.broadcast_to(jnp.pad(m0, ((0, 0), (0, SUBLANES - M_HEADS)))[:, :, None], (bsz, SUBLANES, LANES))
    xm, v, o, ifp, sb, cx, ga, gb = _in_proj(xf, wts["g_mix"], wts["w_in"], tm)
    x1, *states = _mlstm(xf, xm, v, cx, ifp, o, sb, ga, gb, _pad_tail(mconv0), _pad_tail(sconv0), c0, n0, m0p, wts,
                         bsz=bsz, t=t, tt=t, nb=SAMPLE_SEQS_PER_STEP)
    return _outputs(_moe(x1, p.reshape(bsz * t, PLE_DIM), wts), x.shape, *states)


def kernel(x_prompt, x_sample, p_prompt, p_sample, state_mlstm_C, state_mlstm_n, state_mlstm_m, state_mlstm_conv, state_sconv, g_mix, w_in, w_mconv, w_q, w_k, b_i, b_f, g_head, w_a, w_sconv, w_sout, w_o, g_ffn, w_rg, b_rg, w_re, b_re, w_gate, w_up, w_down, g_ple, w_ple_gate, w_ple_proj, g_final):
    assert g_mix.shape[0] == 1, "single-layer trunk"
    d = D_MODEL
    n_if = 2 * M_HEADS
    w_in0 = w_in[0].astype(BF16)
    w_router = jnp.pad(jnp.concatenate([w_rg[0], w_re[0]], axis=1), ((0, 0), (0, LANES - N_GROUPS - N_EXPERTS)))
    wr_hi = w_router.astype(BF16)
    wr_lo = (w_router - wr_hi.astype(F32)).astype(BF16)
    w_r = jnp.concatenate([wr_hi, wr_lo], axis=1)
    wts = {
        "g_mix": g_mix,
        "w_in": jnp.concatenate([w_in0[:, :_GATE_COL],
                                 jnp.pad(w_in0[:, _GATE_COL:_GATE_COL + n_if], ((0, 0), (0, LANES - n_if))),
                                 w_in0[:, _GATE_COL + n_if:]], axis=1),
        "w_mconv": w_mconv[0], "w_sconv": w_sconv[0],
        "w_q": w_q[0].astype(BF16), "w_k": w_k[0].astype(BF16),
        "gate_bias": jnp.pad(jnp.concatenate([b_i[0], b_f[0]])[None, :], ((0, 0), (0, LANES - n_if))),
        "g_head": g_head[0].reshape(1, d),
        "w_a": w_a[0].astype(BF16), "w_sout": w_sout[0].astype(BF16), "w_o": w_o[0].astype(BF16),
        "g_ffn": g_ffn, "w_r": w_r,
        "b_r": jnp.pad(jnp.concatenate([b_rg[0], b_re[0]])[None, :], ((0, 0), (0, LANES - N_GROUPS - N_EXPERTS))),
        "g_ple": g_ple, "w_ple_gate": w_ple_gate[0].astype(BF16), "w_ple_proj": w_ple_proj[0].astype(BF16),
        "g_final": g_final[None, :],
    }
    yp, *st_p = _prompt_group(x_prompt, p_prompt[0], wts, (w_gate[0], w_up[0], w_down[0]), tt=PROMPT_TIME_TILE)
    ys, *st_s = _sample_group(x_sample, p_sample[0], state_mlstm_conv[0], state_sconv[0],
                              state_mlstm_C[0], state_mlstm_n[0], state_mlstm_m[0], wts, tm=SAMPLE_ROW_TILE)
    return (yp, ys, *st_p, *st_s)
```

```python
import functools

import jax
import jax.numpy as jnp
from jax import lax
from jax.experimental import pallas as pl
from jax.experimental.pallas import tpu as pltpu

D_MODEL = 1024
M_HEADS = 4
M_HEAD_DIM = 256
M_CONV_W = 4
S_CONV_W = 3
N_GROUPS = 4
EXPERTS_PER_GROUP = 8
N_EXPERTS = N_GROUPS * EXPERTS_PER_GROUP
D_EXPERT = 256
PLE_DIM = 256
RMS_EPS = 1e-6

LANES = 128
SUBLANES = 8
VMEM_LIMIT = 60 * 1024 * 1024

BF16 = jnp.bfloat16
F32 = jnp.float32


def _rms(x, g):
    return x * lax.rsqrt(jnp.mean(x * x, axis=-1, keepdims=True) + RMS_EPS) * g


def _dot(a, b):
    return jnp.dot(a.astype(BF16), b.astype(BF16), preferred_element_type=F32)


def _sigmoid(x):
    return 0.5 * jnp.tanh(0.5 * x) + 0.5


def _silu(x):
    return x * _sigmoid(x)


def _resident(shape):
    nd = len(shape)
    return pl.BlockSpec(shape, lambda *_: (0,) * nd, pipeline_mode=pl.Buffered(1))


_GATE_COL = 3 * D_MODEL
_REST_COL = _GATE_COL + LANES
IN_COL = {"xm": 0, "v": D_MODEL, "o": 2 * D_MODEL, "sb": _REST_COL, "sc": _REST_COL + D_MODEL,
          "sx": _REST_COL + 2 * D_MODEL, "ga": _REST_COL + 3 * D_MODEL, "gb": _REST_COL + 4 * D_MODEL}


def _projector(h, w_in_ref):
    def proj(name):
        lo, width = (_GATE_COL, LANES) if name == "gates" else (IN_COL[name], D_MODEL)
        return jnp.dot(h, w_in_ref[:, lo:lo + width], preferred_element_type=F32)
    return proj


def _in_proj_kernel(x_ref, g_ref, w_in_ref,
                    xm_ref, v_ref, o_ref, if_ref, sb_ref, cx_ref, ga_ref, gb_ref):
    proj = _projector(_rms(x_ref[...], g_ref[...]).astype(BF16), w_in_ref)
    xm_ref[...] = proj("xm")
    v_ref[...] = proj("v")
    o_ref[...] = proj("o")
    if_ref[...] = proj("gates")
    sb_ref[...] = proj("sb")
    cx_ref[...] = proj("sc") * proj("sx")
    ga_ref[...] = proj("ga")
    gb_ref[...] = proj("gb")


def _in_proj(x, g_mix, w_in, tm):
    n = x.shape[0]
    row = lambda i: (i, 0)
    big = pl.BlockSpec((tm, D_MODEL), row)
    outs = [jax.ShapeDtypeStruct((n, D_MODEL), F32)] * 3 + [jax.ShapeDtypeStruct((n, LANES), F32)] \
        + [jax.ShapeDtypeStruct((n, D_MODEL), F32)] * 4
    return pl.pallas_call(
        _in_proj_kernel,
        grid=(n // tm,),
        in_specs=[big, _resident(g_mix.shape), _resident(w_in.shape)],
        out_specs=[big, big, big, pl.BlockSpec((tm, LANES), row), big, big, big, big],
        out_shape=outs,
        compiler_params=pltpu.CompilerParams(dimension_semantics=("arbitrary",),
                                             vmem_limit_bytes=VMEM_LIMIT),
        name="in_proj",
    )(x, g_mix, w_in)


def _scan_rows(x, op, identity):
    sub = lax.broadcasted_iota(jnp.int32, (SUBLANES, x.shape[1]), 0)
    blocks, carry = [], None
    for i in range(x.shape[0] // SUBLANES):
        blk = x[i * SUBLANES:(i + 1) * SUBLANES, :]
        for shift in (1, 2, 4):
            blk = op(blk, jnp.where(sub >= shift, pltpu.roll(blk, shift, axis=0), identity))
        if carry is not None:
            blk = op(blk, carry)
        carry = jnp.broadcast_to(blk[SUBLANES - 1:SUBLANES, :], blk.shape)
        blocks.append(blk)
    return jnp.concatenate(blocks, axis=0)


def _log_sigmoid(x):
    return jnp.minimum(x, 0.0) - jnp.log1p(jnp.exp(-jnp.abs(x)))


def _seq_init(mconv0_ref, sconv0_ref, c0_ref, n0_ref, m0_ref, xp_sc, cp_sc, c1_ref, n1_ref, m1_ref):
    @pl.when(pl.program_id(1) == 0)
    def _():
        xp_sc[0:SUBLANES, :] = mconv0_ref[...]
        cp_sc[0:SUBLANES, :] = sconv0_ref[...]
        c1_ref[...] = c0_ref[...]
        n1_ref[...] = n0_ref[...]
        m1_ref[...] = m0_ref[...]


def _causal_conv(src_sc, w_ref, width, tt):
    acc = None
    for j in range(width):
        term = src_sc[pl.ds(SUBLANES - (width - 1) + j, tt), :] * w_ref[j:j + 1, :]
        acc = term if acc is None else acc + term
    return acc


def _carry_tail(src_sc, tail_ref, tt):
    tail = src_sc[tt:tt + SUBLANES, :]
    src_sc[0:SUBLANES, :] = tail
    tail_ref[...] = tail


def _head_slices():
    return [slice(h * M_HEAD_DIM, (h + 1) * M_HEAD_DIM) for h in range(M_HEADS)]


def _mlstm_tile(xq, v, ifp, states, wq_ref, wk_ref, bias_ref, ghead_ref, tt, between=()):
    nh = M_HEADS
    hs = _head_slices()
    seq_rows = [slice(j * tt, (j + 1) * tt) for j in range(len(states))]
    g = [ifp[rows, :] + bias_ref[...] for rows in seq_rows]
    b_c = [_scan_rows(_log_sigmoid(gj), jnp.add, 0.0) for gj in g]
    a_c = [gj - pltpu.roll(bj, LANES - nh, axis=1) for gj, bj in zip(g, b_c)]
    amax_c = [_scan_rows(aj, jnp.maximum, -jnp.inf) for aj in a_c]
    a_t = [aj.T for aj in a_c]

    r_idx = lax.broadcasted_iota(jnp.int32, (tt, tt), 0)
    c_idx = lax.broadcasted_iota(jnp.int32, (tt, tt), 1)
    causal = c_idx <= r_idx

    q_all = [_dot(xq[:, hs[h]], wq_ref[h]) * (M_HEAD_DIM ** -0.5) for h in range(nh)]
    k_all = [_dot(xq[:, hs[h]], wk_ref[h]) for h in range(nh)]
    units = [(j, h) for j in range(len(states)) for h in range(nh)]
    ids = range(len(units))
    q = [q_all[h][seq_rows[j], :] for j, h in units]
    k = [k_all[h][seq_rows[j], :] for j, h in units]
    vv = [v[seq_rows[j], hs[h]] for j, h in units]
    c_prev = [states[j][0][h] for j, h in units]
    n_prev = [states[j][1][h:h + 1, :] for j, h in units]
    m_prev = [states[j][2][h:h + 1, 0:1] for j, h in units]
    b_col = [b_c[j][:, nh + h:nh + h + 1] for j, h in units]
    a_col = [a_c[j][:, h:h + 1] for j, h in units]
    a_row = [a_t[j][h:h + 1, :] for j, h in units]
    b_last = [b_c[j][tt - 1:tt, nh + h:nh + h + 1] for j, h in units]
    mm_col = [jnp.maximum(m_prev[i], amax_c[j][:, h:h + 1]) for i, (j, h) in enumerate(units)]
    mm_last = [jnp.maximum(m_prev[i], amax_c[j][tt - 1:tt, h:h + 1]) for i, (j, h) in enumerate(units)]

    extra = []
    pending = list(between)

    def run_one():
        if pending:
            extra.append(pending.pop(0)())

    w_inter = [jnp.exp(m_prev[i] - mm_col[i]) for i in ids]
    qk = [lax.dot_general(q[i].astype(BF16), k[i].astype(BF16), (((1,), (1,)), ((), ())),
                          preferred_element_type=F32) for i in ids]
    run_one()
    s = [jnp.exp(jnp.where(causal, a_row[i] - mm_col[i], -jnp.inf)) * qk[i] for i in ids]
    run_one()
    decay = [jnp.exp(m_prev[i] - mm_last[i]) for i in ids]
    wk = [jnp.exp(a_col[i] - mm_last[i]) * k[i] for i in ids]
    for i, (j, h) in enumerate(units):
        c_ref, n_ref, m_ref = states[j]
        c_ref[h] = decay[i] * c_prev[i] + lax.dot_general(
            wk[i].astype(BF16), vv[i].astype(BF16), (((0,), (0,)), ((), ())), preferred_element_type=F32)
        n_ref[h:h + 1, :] = decay[i] * n_prev[i] + jnp.sum(wk[i], axis=0, keepdims=True)
        m_ref[h:h + 1, :] = jnp.broadcast_to(b_last[i] + mm_last[i], (1, LANES))
    num = [w_inter[i] * _dot(q[i], c_prev[i]) + _dot(s[i], vv[i]) for i in ids]
    den = [w_inter[i] * jnp.sum(q[i] * n_prev[i], axis=-1, keepdims=True)
           + jnp.sum(s[i], axis=-1, keepdims=True) for i in ids]
    run_one()
    out = [num[i] * (1.0 / jnp.maximum(jnp.abs(den[i]), jnp.exp(-(b_col[i] + mm_col[i])))) for i in ids]
    out = [out[i] * lax.rsqrt(jnp.mean(out[i] * out[i], axis=-1, keepdims=True) + RMS_EPS)
           * ghead_ref[:, hs[units[i][1]]] for i in ids]
    while pending:
        run_one()
    return [out[j * nh:(j + 1) * nh] for j in range(len(states))], extra


def _merge(x, hm, u, o, sb, ga, gb, wa_ref, wsout_ref, wo_ref):
    y_a = _dot(_sigmoid(o) * hm, wa_ref[...])
    y_b = _dot(sb * u, wsout_ref[...])
    return x + _dot(_sigmoid(ga) * y_a + _sigmoid(gb) * y_b, wo_ref[...])


def _mlstm_kernel(x_ref, xm_ref, v_ref, cx_ref, if_ref, o_ref, sb_ref, ga_ref, gb_ref,
                  mconv0_ref, sconv0_ref, c0_ref, n0_ref, m0_ref,
                  w_mconv_ref, w_sconv_ref, wq_ref, wk_ref, bias_ref, ghead_ref, wa_ref, wsout_ref, wo_ref,
                  x1_ref, c1_ref, n1_ref, m1_ref, mconv1_ref, sconv1_ref,
                  xp_sc, cp_sc, *, tt, nb):
    xq, u, states = [], [], []
    for j in range(nb):
        rows = slice(j * tt, (j + 1) * tt)
        xp, cp, state = xp_sc.at[j], cp_sc.at[j], (c1_ref.at[j], n1_ref.at[j], m1_ref.at[j])
        _seq_init(mconv0_ref.at[j], sconv0_ref.at[j], c0_ref.at[j], n0_ref.at[j], m0_ref.at[j], xp, cp, *state)
        xp[SUBLANES:SUBLANES + tt, :] = xm_ref[rows, :]
        cp[SUBLANES:SUBLANES + tt, :] = cx_ref[rows, :]
        xq.append(_silu(_causal_conv(xp, w_mconv_ref, M_CONV_W, tt)))
        u.append(_causal_conv(cp, w_sconv_ref, S_CONV_W, tt))
        _carry_tail(xp, mconv1_ref.at[j], tt)
        _carry_tail(cp, sconv1_ref.at[j], tt)
        states.append(state)
    stack = lambda parts: parts[0] if len(parts) == 1 else jnp.concatenate(parts, axis=0)
    out, _ = _mlstm_tile(stack(xq), v_ref[...], if_ref[...], states, wq_ref, wk_ref, bias_ref, ghead_ref, tt)
    hm = stack([jnp.concatenate(heads, axis=1) for heads in out])
    x1_ref[...] = _merge(x_ref[...], hm, stack(u), o_ref[...], sb_ref[...], ga_ref[...], gb_ref[...],
                         wa_ref, wsout_ref, wo_ref)


def _mlstm(x, xm, v, cx, ifp, o, sb, ga, gb, mconv0, sconv0, c0, n0, m0, wts, *, bsz, t, tt, nb):
    nt = t // tt
    assert nb == 1 or nt == 1, "several sequences per step only when a step covers whole sequences"
    tok = lambda b, i: (b * nt + i, 0)
    seq3 = lambda b, i: (b, 0, 0)
    seq4 = lambda b, i: (b, 0, 0, 0)
    big = pl.BlockSpec((nb * tt, D_MODEL), tok)
    tail = pl.BlockSpec((nb, SUBLANES, D_MODEL), seq3)
    c_spec = pl.BlockSpec((nb, M_HEADS, M_HEAD_DIM, M_HEAD_DIM), seq4)
    n_spec = pl.BlockSpec((nb, M_HEADS, M_HEAD_DIM), seq3)
    m_spec = pl.BlockSpec((nb, SUBLANES, LANES), seq3)
    names = ("w_mconv", "w_sconv", "w_q", "w_k", "gate_bias", "g_head", "w_a", "w_sout", "w_o")
    weights = [wts[k] for k in names]
    return pl.pallas_call(
        functools.partial(_mlstm_kernel, tt=tt, nb=nb),
        grid=(bsz // nb, nt),
        in_specs=[big] * 4 + [pl.BlockSpec((nb * tt, LANES), tok)] + [big] * 4
                 + [tail, tail, c_spec, n_spec, m_spec] + [_resident(w.shape) for w in weights],
        out_specs=[big, c_spec, n_spec, m_spec, tail, tail],
        out_shape=[jax.ShapeDtypeStruct(x.shape, F32),
                   jax.ShapeDtypeStruct(c0.shape, F32), jax.ShapeDtypeStruct(n0.shape, F32),
                   jax.ShapeDtypeStruct(m0.shape, F32),
                   jax.ShapeDtypeStruct(mconv0.shape, F32), jax.ShapeDtypeStruct(sconv0.shape, F32)],
        scratch_shapes=[pltpu.VMEM((nb, tt + SUBLANES, D_MODEL), F32),
                        pltpu.VMEM((nb, tt + SUBLANES, D_MODEL), F32)],
        compiler_params=pltpu.CompilerParams(dimension_semantics=("arbitrary", "arbitrary"),
                                             vmem_limit_bytes=VMEM_LIMIT),
        name="mlstm",
    )(x, xm, v, cx, ifp, o, sb, ga, gb, mconv0, sconv0, c0, n0, m0, *weights)


def _mixer_kernel(x_ref, g_ref, w_in_ref, w_mconv_ref, w_sconv_ref, wq_ref, wk_ref,
                  bias_ref, ghead_ref, wa_ref, wsout_ref, wo_ref, wg32_ref, wu32_ref, wd32_ref,
                  x1_ref, c1_ref, n1_ref, m1_ref, mconv1_ref, sconv1_ref, wg16_ref, wu16_ref, wd16_ref,
                  xp_sc, cp_sc, *, tt):
    @pl.when(pl.program_id(1) == 0)
    def _():
        xp_sc[0:SUBLANES, :] = jnp.zeros((SUBLANES, D_MODEL), F32)
        cp_sc[0:SUBLANES, :] = jnp.zeros((SUBLANES, D_MODEL), F32)
        c1_ref[...] = jnp.zeros(c1_ref.shape, F32)
        n1_ref[...] = jnp.zeros(n1_ref.shape, F32)
        m1_ref[...] = jnp.zeros(m1_ref.shape, F32)

    wg16_ref[...] = wg32_ref[...].astype(BF16)
    wu16_ref[...] = wu32_ref[...].astype(BF16)
    wd16_ref[...] = wd32_ref[...].astype(BF16)

    proj = _projector(_rms(x_ref[...], g_ref[...]).astype(BF16), w_in_ref)
    xp_sc[SUBLANES:SUBLANES + tt, :] = proj("xm")
    ifp = proj("gates")
    cp_sc[SUBLANES:SUBLANES + tt, :] = proj("sc") * proj("sx")
    xq = _silu(_causal_conv(xp_sc, w_mconv_ref, M_CONV_W, tt))
    u = _causal_conv(cp_sc, w_sconv_ref, S_CONV_W, tt)
    _carry_tail(xp_sc, mconv1_ref, tt)
    _carry_tail(cp_sc, sconv1_ref, tt)
    v = proj("v")
    hs = _head_slices()
    (out,), (y_b, sig_o, sig_ga, sig_gb) = _mlstm_tile(
        xq, v, ifp, [(c1_ref, n1_ref, m1_ref)], wq_ref, wk_ref, bias_ref, ghead_ref, tt,
        between=(lambda: _dot(proj("sb") * u, wsout_ref[...]),
                 lambda: _sigmoid(proj("o")),
                 lambda: _sigmoid(proj("ga")),
                 lambda: _sigmoid(proj("gb"))))
    hm = jnp.concatenate([(sig_o[:, sl] * out[hd]).astype(BF16) for hd, sl in enumerate(hs)], axis=1)
    y_a = jnp.dot(hm, wa_ref[...], preferred_element_type=F32)
    merged = sig_ga * y_a + sig_gb * y_b
    x1_ref[...] = x_ref[...] + _dot(merged, wo_ref[...])


def _mixer(x, wts, experts_f32, *, bsz, t, tt):
    nt = t // tt
    steps = bsz * nt
    tok = lambda b, i: (b * nt + i, 0)
    seq3 = lambda b, i: (b, 0, 0)
    seq4 = lambda b, i: (b, 0, 0, 0)
    big = pl.BlockSpec((tt, D_MODEL), tok)
    tail = pl.BlockSpec((None, SUBLANES, D_MODEL), seq3)
    c_spec = pl.BlockSpec((None, M_HEADS, M_HEAD_DIM, M_HEAD_DIM), seq4)
    n_spec = pl.BlockSpec((None, M_HEADS, M_HEAD_DIM), seq3)
    m_spec = pl.BlockSpec((None, SUBLANES, LANES), seq3)
    names = ("g_mix", "w_in", "w_mconv", "w_sconv", "w_q", "w_k", "gate_bias", "g_head",
             "w_a", "w_sout", "w_o")
    weights = [wts[k] for k in names]
    flat = [w.reshape(-1, w.shape[-1]) for w in experts_f32]
    cast_specs = [pl.BlockSpec((w.shape[0] // steps, w.shape[1]), tok) for w in flat]
    outs = pl.pallas_call(
        functools.partial(_mixer_kernel, tt=tt),
        grid=(bsz, nt),
        in_specs=[big] + [_resident(w.shape) for w in weights] + cast_specs,
        out_specs=[big, c_spec, n_spec, m_spec, tail, tail] + cast_specs,
        out_shape=[jax.ShapeDtypeStruct(x.shape, F32),
                   jax.ShapeDtypeStruct((bsz, M_HEADS, M_HEAD_DIM, M_HEAD_DIM), F32),
                   jax.ShapeDtypeStruct((bsz, M_HEADS, M_HEAD_DIM), F32),
                   jax.ShapeDtypeStruct((bsz, SUBLANES, LANES), F32),
                   jax.ShapeDtypeStruct((bsz, SUBLANES, D_MODEL), F32),
                   jax.ShapeDtypeStruct((bsz, SUBLANES, D_MODEL), F32)]
                  + [jax.ShapeDtypeStruct(w.shape, BF16) for w in flat],
        scratch_shapes=[pltpu.VMEM((tt + SUBLANES, D_MODEL), F32), pltpu.VMEM((tt + SUBLANES, D_MODEL), F32)],
        compiler_params=pltpu.CompilerParams(dimension_semantics=("arbitrary", "arbitrary"),
                                             vmem_limit_bytes=VMEM_LIMIT),
        name="mixer",
    )(x, *weights, *flat)
    return outs[:6], [o.reshape(w.shape) for o, w in zip(outs[6:], experts_f32)]


MOE_WINDOW = 1024
MOE_ROW_TILE = 128
MOE_ALIGN = 16
EXPERTS_PER_STEP = EXPERTS_PER_GROUP
MOE_STEPS = N_EXPERTS // EXPERTS_PER_STEP
MOE_CHUNK = 256
MOE_SORT_CHUNK = 1280
MOE_UNSORT_CHUNK = 512
MOE_SORTED_ROWS = -(-(MOE_WINDOW + N_GROUPS * (MOE_ALIGN - 1) + MOE_ROW_TILE) // MOE_CHUNK) * MOE_CHUNK


def _split_bf16(x):
    hi = x.astype(BF16)
    lo = (x - hi.astype(F32)).astype(BF16)
    return hi, lo


CW_LANE_STRIDE = 40


def _pack_split3(cw):
    hi = cw.astype(BF16).astype(F32)
    r1 = cw - hi
    mid = r1.astype(BF16).astype(F32)
    lo = (r1 - mid).astype(BF16).astype(F32)
    return (hi + pltpu.roll(mid, CW_LANE_STRIDE, axis=1) + pltpu.roll(lo, 2 * CW_LANE_STRIDE, axis=1)).astype(BF16)


def _unpack_split3(packed):
    return (packed + pltpu.roll(packed, LANES - CW_LANE_STRIDE, axis=1)
            + pltpu.roll(packed, LANES - 2 * CW_LANE_STRIDE, axis=1))


def _route(logits):
    lane = lax.broadcasted_iota(jnp.int32, logits.shape, 1)
    neg = -jnp.inf
    big = jnp.int32(LANES)
    is_grp = lane < N_GROUPS
    g_max = jnp.max(jnp.where(is_grp, logits, neg), axis=-1, keepdims=True)
    g_sel = jnp.min(jnp.where(is_grp & (logits == g_max), lane, big), axis=-1, keepdims=True)
    p_grp = 1.0 / jnp.sum(jnp.where(is_grp, jnp.exp(logits - g_max), 0.0), axis=-1, keepdims=True)
    lo = N_GROUPS + g_sel * EXPERTS_PER_GROUP
    in_grp = (lane >= lo) & (lane < lo + EXPERTS_PER_GROUP)
    v1 = jnp.max(jnp.where(in_grp, logits, neg), axis=-1, keepdims=True)
    e1 = jnp.min(jnp.where(in_grp & (logits == v1), lane, big), axis=-1, keepdims=True)
    rest = in_grp & (lane != e1)
    v2 = jnp.max(jnp.where(rest, logits, neg), axis=-1, keepdims=True)
    e2 = jnp.min(jnp.where(rest & (logits == v2), lane, big), axis=-1, keepdims=True)
    z = jnp.exp(v2 - v1)
    w1 = p_grp / (1.0 + z)
    w2 = p_grp * z / (1.0 + z)
    return g_sel, jnp.where(lane == e1, w1, jnp.where(lane == e2, w2, 0.0))


def _moe_tail_kernel(x1_ref, p_ref, gffn_ref, wr_ref, br_ref, wg_ref, wu_ref, wd_ref,
                     gple_ref, wpg_ref, wpp_ref, gfin_ref, y_ref,
                     xs_sc, cws_sc, osort_sc, pos_sc, off_sm, nt_sm):
    s = pl.program_id(1)
    w = MOE_WINDOW

    @pl.when(s == 0)
    def _():
        hn = _rms(x1_ref[...], gffn_ref[...])
        hn_hi, hn_lo = _split_bf16(hn)
        hh_hl = jnp.dot(hn_hi, wr_ref[...], preferred_element_type=F32)
        logits = (hh_hl[:, :LANES] + hh_hl[:, LANES:]
                  + jnp.dot(hn_lo, wr_ref[:, :LANES], preferred_element_type=F32)) + br_ref[...]
        g_sel, cw = _route(logits)
        lane = lax.broadcasted_iota(jnp.int32, (w, LANES), 1)
        onehot = jnp.where(lane == g_sel, 1.0, 0.0)
        cum = _scan_rows(onehot, jnp.add, 0.0)
        cnt = cum[w - 1:w, :].astype(jnp.int32)
        cnt_pad = ((cnt + (MOE_ALIGN - 1)) // MOE_ALIGN) * MOE_ALIGN
        lane1 = lax.broadcasted_iota(jnp.int32, (1, LANES), 1)
        off = jnp.zeros((1, LANES), jnp.int32)
        for gi in range(N_GROUPS - 1):
            off = off + jnp.where(lane1 > gi, cnt_pad[:, gi:gi + 1], 0)
        n_tiles = (cnt + (MOE_ROW_TILE - 1)) // MOE_ROW_TILE
        for gi in range(N_GROUPS):
            off_sm[gi] = off[0, gi]
            nt_sm[gi] = n_tiles[0, gi]
        pos = jnp.sum(onehot * (off.astype(F32) + cum - 1.0), axis=-1, keepdims=True)
        pos_b = jnp.broadcast_to(pos, (w, LANES))
        pos_sc[...] = pos_b
        pos_row = pos_b.T[0:1, :].astype(jnp.int32)
        cw_packed = _pack_split3(cw)
        for c in range(MOE_SORTED_ROWS // MOE_SORT_CHUNK):
            rows = lax.broadcasted_iota(jnp.int32, (MOE_SORT_CHUNK, w), 0) + c * MOE_SORT_CHUNK
            sel = jnp.where(rows == pos_row, 1.0, 0.0).astype(BF16)
            sl = slice(c * MOE_SORT_CHUNK, (c + 1) * MOE_SORT_CHUNK)
            xs_sc[sl, :] = jnp.dot(sel, hn_hi, preferred_element_type=F32).astype(BF16)
            cws_sc[sl, :] = _unpack_split3(jnp.dot(sel, cw_packed, preferred_element_type=F32))
        osort_sc[...] = jnp.zeros(osort_sc.shape, BF16)

    row0 = off_sm[s]
    lane_t = lax.broadcasted_iota(jnp.int32, (MOE_ROW_TILE, LANES), 1)
    first_lane = N_GROUPS + s * EXPERTS_PER_STEP

    def tile_body(i, carry):
        r0 = pl.multiple_of(row0 + i * MOE_ROW_TILE, MOE_ALIGN)
        xt = xs_sc[pl.ds(r0, MOE_ROW_TILE), :]
        cwt = cws_sc[pl.ds(r0, MOE_ROW_TILE), :]
        parts = []
        for e in range(EXPERTS_PER_STEP):
            hg = jnp.dot(xt, wg_ref[e], preferred_element_type=F32)
            hu = jnp.dot(xt, wu_ref[e], preferred_element_type=F32)
            col = jnp.sum(jnp.where(lane_t == first_lane + e, cwt, 0.0), axis=-1, keepdims=True)
            parts.append((_silu(hg) * hu * col).astype(BF16))
        he = jnp.concatenate(parts, axis=-1)
        out = jnp.dot(he, wd_ref[...], preferred_element_type=F32)
        osort_sc[pl.ds(r0, MOE_ROW_TILE), :] = out.astype(BF16)
        return carry

    lax.fori_loop(0, nt_sm[s], tile_body, 0)

    @pl.when(s == MOE_STEPS - 1)
    def _():
        osort = osort_sc[...]
        for c in range(w // MOE_UNSORT_CHUNK):
            sl = slice(c * MOE_UNSORT_CHUNK, (c + 1) * MOE_UNSORT_CHUNK)
            pos_col = pos_sc[sl, 0:1].astype(jnp.int32)
            cols = lax.broadcasted_iota(jnp.int32, (MOE_UNSORT_CHUNK, MOE_SORTED_ROWS), 1)
            sel = jnp.where(cols == pos_col, 1.0, 0.0).astype(BF16)
            x2 = x1_ref[sl, :] + jnp.dot(sel, osort, preferred_element_type=F32)
            gate = _sigmoid(_dot(_rms(x2, gple_ref[...]), wpg_ref[...]))
            x3 = x2 + gate * _dot(p_ref[sl, :], wpp_ref[...])
            y_ref[sl, :] = _rms(x3, gfin_ref[...])


def _moe_tail(x1, p, g_ffn, w_r, b_r, wg, wu, wd, g_ple, w_pg, w_pp, g_fin):
    n = x1.shape[0]
    tm = MOE_WINDOW
    row = lambda i, s: (i, 0)
    step = lambda i, s: (s, 0, 0)
    big = pl.BlockSpec((tm, D_MODEL), row)
    sw = EXPERTS_PER_STEP * D_EXPERT
    return pl.pallas_call(
        _moe_tail_kernel,
        grid=(n // tm, MOE_STEPS),
        in_specs=[big, pl.BlockSpec((tm, PLE_DIM), row), _resident(g_ffn.shape), _resident(w_r.shape),
                  _resident(b_r.shape),
                  pl.BlockSpec((EXPERTS_PER_STEP, D_MODEL, D_EXPERT), step),
                  pl.BlockSpec((EXPERTS_PER_STEP, D_MODEL, D_EXPERT), step),
                  pl.BlockSpec((None, sw, D_MODEL), step),
                  _resident(g_ple.shape), _resident(w_pg.shape), _resident(w_pp.shape), _resident(g_fin.shape)],
        out_specs=big,
        out_shape=jax.ShapeDtypeStruct((n, D_MODEL), F32),
        scratch_shapes=[pltpu.VMEM((MOE_SORTED_ROWS, D_MODEL), BF16), pltpu.VMEM((MOE_SORTED_ROWS, LANES), F32),
                        pltpu.VMEM((MOE_SORTED_ROWS, D_MODEL), BF16), pltpu.VMEM((tm, LANES), F32),
                        pltpu.SMEM((N_GROUPS,), jnp.int32), pltpu.SMEM((N_GROUPS,), jnp.int32)],
        compiler_params=pltpu.CompilerParams(dimension_semantics=("arbitrary", "arbitrary"),
                                             vmem_limit_bytes=VMEM_LIMIT),
        name="moe_tail",
    )(x1, p, g_ffn, w_r, b_r, wg, wu, wd, g_ple, w_pg, w_pp, g_fin)


PROMPT_TIME_TILE = 256
SAMPLE_ROW_TILE = 512
SAMPLE_SEQS_PER_STEP = 8


def _pad_tail(buf):
    return jnp.pad(buf, ((0, 0), (SUBLANES - buf.shape[1], 0), (0, 0)))


def _moe(x1, p, wts):
    return _moe_tail(x1, p, wts["g_ffn"], wts["w_r"], wts["b_r"], wts["w_gate"], wts["w_up"], wts["w_down"],
                     wts["g_ple"], wts["w_ple_gate"], wts["w_ple_proj"], wts["g_final"])


def _outputs(y, shape, c1, n1, m1p, mconv1, sconv1):
    return (y.reshape(shape), c1[None], n1[None], m1p[None, :, :M_HEADS, 0],
            mconv1[None, :, SUBLANES - (M_CONV_W - 1):, :], sconv1[None, :, SUBLANES - (S_CONV_W - 1):, :])


def _prompt_group(x, p, wts, experts_f32, *, tt):
    bsz, t, _ = x.shape
    (x1, *states), (wg, wu, wd) = _mixer(x.reshape(bsz * t, D_MODEL), wts, experts_f32, bsz=bsz, t=t, tt=tt)
    wts.update(w_gate=wg, w_up=wu, w_down=wd.reshape(MOE_STEPS, EXPERTS_PER_STEP * D_EXPERT, D_MODEL))
    return _outputs(_moe(x1, p.reshape(bsz * t, PLE_DIM), wts), x.shape, *states)


def _sample_group(x, p, mconv0, sconv0, c0, n0, m0, wts, *, tm):
    bsz, t, _ = x.shape
    xf = x.reshape(bsz * t, D_MODEL)
    m0p = jnp.broadcast_to(jnp.pad(m0, ((0, 0), (0, SUBLANES - M_HEADS)))[:, :, None], (bsz, SUBLANES, LANES))
    xm, v, o, ifp, sb, cx, ga, gb = _in_proj(xf, wts["g_mix"], wts["w_in"], tm)
    x1, *states = _mlstm(xf, xm, v, cx, ifp, o, sb, ga, gb, _pad_tail(mconv0), _pad_tail(sconv0), c0, n0, m0p, wts,
                         bsz=bsz, t=t, tt=t, nb=SAMPLE_SEQS_PER_STEP)
    return _outputs(_moe(x1, p.reshape(bsz * t, PLE_DIM), wts), x.shape, *states)


def kernel(x_prompt, x_sample, p_prompt, p_sample, state_mlstm_C, state_mlstm_n, state_mlstm_m, state_mlstm_conv, state_sconv, g_mix, w_in, w_mconv, w_q, w_k, b_i, b_f, g_head, w_a, w_sconv, w_sout, w_o, g_ffn, w_rg, b_rg, w_re, b_re, w_gate, w_up, w_down, g_ple, w_ple_gate, w_ple_proj, g_final):
    assert g_mix.shape[0] == 1, "single-layer trunk"
    d = D_MODEL
    n_if = 2 * M_HEADS
    w_in0 = w_in[0].astype(BF16)
    w_router = jnp.pad(jnp.concatenate([w_rg[0], w_re[0]], axis=1), ((0, 0), (0, LANES - N_GROUPS - N_EXPERTS)))
    wr_hi = w_router.astype(BF16)
    wr_lo = (w_router - wr_hi.astype(F32)).astype(BF16)
    w_r = jnp.concatenate([wr_hi, wr_lo], axis=1)
    wts = {
        "g_mix": g_mix,
        "w_in": jnp.concatenate([w_in0[:, :_GATE_COL],
                                 jnp.pad(w_in0[:, _GATE_COL:_GATE_COL + n_if], ((0, 0), (0, LANES - n_if))),
                                 w_in0[:, _GATE_COL + n_if:]], axis=1),
        "w_mconv": w_mconv[0], "w_sconv": w_sconv[0],
        "w_q": w_q[0].astype(BF16), "w_k": w_k[0].astype(BF16),
        "gate_bias": jnp.pad(jnp.concatenate([b_i[0], b_f[0]])[None, :], ((0, 0), (0, LANES - n_if))),
        "g_head": g_head[0].reshape(1, d),
        "w_a": w_a[0].astype(BF16), "w_sout": w_sout[0].astype(BF16), "w_o": w_o[0].astype(BF16),
        "g_ffn": g_ffn, "w_r": w_r,
        "b_r": jnp.pad(jnp.concatenate([b_rg[0], b_re[0]])[None, :], ((0, 0), (0, LANES - N_GROUPS - N_EXPERTS))),
        "g_ple": g_ple, "w_ple_gate": w_ple_gate[0].astype(BF16), "w_ple_proj": w_ple_proj[0].astype(BF16),
        "g_final": g_final[None, :],
    }
    yp, *st_p = _prompt_group(x_prompt, p_prompt[0], wts, (w_gate[0], w_up[0], w_down[0]), tt=PROMPT_TIME_TILE)
    ys, *st_s = _sample_group(x_sample, p_sample[0], state_mlstm_conv[0], state_sconv[0],
                              state_mlstm_C[0], state_mlstm_n[0], state_mlstm_m[0], wts, tm=SAMPLE_ROW_TILE)
    return (yp, ys, *st_p, *st_s)
```
